```python
import jax, jax.numpy as jnp
from jax import lax
import numpy as np

D_MODEL = 1024
BATCH = 8
SEQ = 4096
DEPTH = 1

CTX_LEN = 256
GRID_W = 64
HEAD_DIM = 64
ATTN_HEADS = 8
ATTN_KV_HEADS = 2
ATTN_GROUP = ATTN_HEADS // ATTN_KV_HEADS
ATTN_WIDTH = ATTN_HEADS * HEAD_DIM
KV_WIDTH = ATTN_KV_HEADS * HEAD_DIM
MLP_HEADS = 8
MLP_WIDTH = MLP_HEADS * HEAD_DIM
MIX_WIDTH = ATTN_WIDTH + MLP_WIDTH
IN_WIDTH = 2 * KV_WIDTH + ATTN_WIDTH + 2 * MLP_WIDTH
WINDOW = 128
BLOCK = 128
CHUNK = 128
FFN_HIDDEN = -(-8 * D_MODEL // (3 * 256)) * 256
AXIS_DIM = HEAD_DIM // 2
ROPE_THETA = 10000.0
EPS = 1e-6
N_MOD = 6

kernel_name = "hymba_style_dit_window_gqa_chunk_gmlp"


def rmsnorm(x, gain):
    xf = x.astype(jnp.float32)
    y = xf * lax.rsqrt(jnp.mean(xf * xf, axis=-1, keepdims=True) + EPS)
    return (y * gain.astype(jnp.float32)).astype(x.dtype)


def adaln(cond, w_mod, b_mod):
    m = jax.nn.silu(cond) @ w_mod + b_mod
    return jnp.split(m[:, None, :], N_MOD, axis=-1)


def modulated_norm(h, gain, shift, scale):
    return rmsnorm(h, gain) * (1.0 + scale) + shift


def rope_tables(pos):
    inv_freq = ROPE_THETA ** (-jnp.arange(0, AXIS_DIM, 2, dtype=jnp.float32) / AXIS_DIM)
    ang = pos.astype(jnp.float32)[:, None] * inv_freq[None, :]
    return jnp.cos(ang), jnp.sin(ang)


def rotate(x, cos, sin):
    x1, x2 = jnp.split(x, 2, axis=-1)
    c = cos[:, None, :]
    s = sin[:, None, :]
    return jnp.concatenate([x1 * c - x2 * s, x1 * s + x2 * c], axis=-1)


def axial_rope(x, cos_r, sin_r, cos_c, sin_c):
    xr, xc = jnp.split(x, 2, axis=-1)
    out = jnp.concatenate([rotate(xr, cos_r, sin_r), rotate(xc, cos_c, sin_c)], axis=-1)
    return out.astype(x.dtype)


def sink_logits(sink, batch, n_q):
    s = sink.astype(jnp.float32).reshape(ATTN_KV_HEADS, ATTN_GROUP)[None, :, :, None, None]
    return jnp.broadcast_to(s, (batch, ATTN_KV_HEADS, ATTN_GROUP, n_q, 1))


def window_attention(q, k, v, k_ctx, v_ctx, sink):
    B, S = q.shape[0], q.shape[1]
    C = k_ctx.shape[1]
    nb = S // BLOCK
    scale = HEAD_DIM ** -0.5
    pad = ((0, 0), (BLOCK, BLOCK), (0, 0), (0, 0))
    kp = jnp.pad(k, pad)
    vp = jnp.pad(v, pad)
    sink_b = sink_logits(sink, B, BLOCK)

    def block(i):
        q_blk = lax.dynamic_slice_in_dim(q, i * BLOCK, BLOCK, axis=1)
        q_blk = q_blk.reshape(B, BLOCK, ATTN_KV_HEADS, ATTN_GROUP, HEAD_DIM)
        k_blk = lax.dynamic_slice_in_dim(kp, i * BLOCK, 3 * BLOCK, axis=1)
        v_blk = lax.dynamic_slice_in_dim(vp, i * BLOCK, 3 * BLOCK, axis=1)
        q_pos = i * BLOCK + jnp.arange(BLOCK)
        k_pos = (i - 1) * BLOCK + jnp.arange(3 * BLOCK)
        mask = (jnp.abs(q_pos[:, None] - k_pos[None, :]) <= WINDOW) & (k_pos >= 0)[None, :] & (k_pos < S)[None, :]
        s_loc = jnp.einsum('bqkgd,bskd->bkgqs', q_blk, k_blk, preferred_element_type=jnp.float32) * scale
        s_loc = jnp.where(mask[None, None, None], s_loc, -jnp.inf)
        s_ctx = jnp.einsum('bqkgd,bckd->bkgqc', q_blk, k_ctx, preferred_element_type=jnp.float32) * scale
        p = jax.nn.softmax(jnp.concatenate([s_loc, s_ctx, sink_b], axis=-1), axis=-1)
        p_loc = p[..., :3 * BLOCK]
        p_ctx = p[..., 3 * BLOCK:3 * BLOCK + C]
        o = (jnp.einsum('bkgqs,bskd->bqkgd', p_loc, v_blk.astype(jnp.float32))
             + jnp.einsum('bkgqc,bckd->bqkgd', p_ctx, v_ctx.astype(jnp.float32)))
        return o.reshape(B, BLOCK, ATTN_WIDTH).astype(q.dtype)

    out = lax.map(block, jnp.arange(nb))
    return jnp.transpose(out, (1, 0, 2, 3)).reshape(B, S, ATTN_WIDTH)


def context_attention(q, k, v, sink):
    B, C = q.shape[0], q.shape[1]
    qg = q.reshape(B, C, ATTN_KV_HEADS, ATTN_GROUP, HEAD_DIM)
    s = jnp.einsum('bqkgd,bskd->bkgqs', qg, k, preferred_element_type=jnp.float32) * HEAD_DIM ** -0.5
    p = jax.nn.softmax(jnp.concatenate([s, sink_logits(sink, B, C)], axis=-1), axis=-1)[..., :C]
    o = jnp.einsum('bkgqs,bskd->bqkgd', p, v.astype(jnp.float32))
    return o.reshape(B, C, ATTN_WIDTH).astype(q.dtype)


def chunk_spatial_gating(u, g, gate_gain, w_s, b_s):
    B, N = u.shape[0], u.shape[1]
    nc = N // CHUNK
    u = jax.nn.gelu(u)
    g = rmsnorm(jax.nn.gelu(g).reshape(B, N, MLP_HEADS, HEAD_DIM), gate_gain)
    gc = g.reshape(B, nc, CHUNK, MLP_HEADS, HEAD_DIM)
    mixed = jnp.einsum('hpq,bnqhd->bnphd', w_s, gc) + jnp.transpose(b_s)[None, None, :, :, None]
    return u * mixed.reshape(B, N, MLP_WIDTH)


def merge_groups(o_attn, o_mlp, attn_out_gain, mlp_out_gain, w_out):
    return jnp.concatenate([rmsnorm(o_attn, attn_out_gain), rmsnorm(o_mlp, mlp_out_gain)], axis=-1) @ w_out


def swiglu_ffn(h, shift, scale, gate, norm_gain, w_gate_up, w_down):
    hn = modulated_norm(h, norm_gain, shift, scale)
    a, b = jnp.split(hn @ w_gate_up, 2, axis=-1)
    return h + gate * ((jax.nn.silu(a) * b) @ w_down)


def setup_inputs(seed: int = 0) -> dict:
    key = jax.random.key(seed)
    ks = jax.random.split(key, 20)
    f32 = jnp.float32
    nrm = lambda k, shape, s: jax.random.normal(k, shape, f32) * s
    return {
        "x": nrm(ks[0], (BATCH, SEQ, D_MODEL), 1.0),
        "c": nrm(ks[1], (BATCH, D_MODEL), 1.0),
        "ctx": nrm(ks[2], (BATCH, CTX_LEN, D_MODEL), 1.0),
        "c_ctx": nrm(ks[3], (D_MODEL,), 1.0),
        "w_mod": nrm(ks[4], (DEPTH, D_MODEL, N_MOD * D_MODEL), 0.5 * D_MODEL ** -0.5),
        "b_mod": nrm(ks[5], (DEPTH, N_MOD * D_MODEL), 0.02),
        "norm_mix": 1.0 + nrm(ks[6], (DEPTH, D_MODEL), 0.02),
        "norm_ffn": 1.0 + nrm(ks[7], (DEPTH, D_MODEL), 0.02),
        "w_in": nrm(ks[8], (DEPTH, D_MODEL, IN_WIDTH), D_MODEL ** -0.5),
        "q_gain": 1.0 + nrm(ks[9], (DEPTH, HEAD_DIM), 0.02),
        "k_gain": 1.0 + nrm(ks[10], (DEPTH, HEAD_DIM), 0.02),
        "attn_sink": nrm(ks[11], (DEPTH, ATTN_HEADS), 0.5),
        "gate_gain": 1.0 + nrm(ks[12], (DEPTH, MLP_HEADS, HEAD_DIM), 0.02),
        "w_spatial": nrm(ks[13], (DEPTH, MLP_HEADS, CHUNK, CHUNK), CHUNK ** -0.5),
        "b_spatial": 1.0 + nrm(ks[14], (DEPTH, MLP_HEADS, CHUNK), 0.02),
        "attn_out_gain": 1.0 + nrm(ks[15], (DEPTH, ATTN_WIDTH), 0.02),
        "mlp_out_gain": 1.0 + nrm(ks[16], (DEPTH, MLP_WIDTH), 0.02),
        "w_out": nrm(ks[17], (DEPTH, MIX_WIDTH, D_MODEL), MIX_WIDTH ** -0.5),
        "w_gate_up": nrm(ks[18], (DEPTH, D_MODEL, 2 * FFN_HIDDEN), D_MODEL ** -0.5),
        "w_down": nrm(ks[19], (DEPTH, FFN_HIDDEN, D_MODEL), FFN_HIDDEN ** -0.5),
    }


def reference(x, c, ctx, c_ctx, w_mod, b_mod, norm_mix, norm_ffn, w_in, q_gain, k_gain,
              attn_sink, gate_gain, w_spatial, b_spatial, attn_out_gain, mlp_out_gain,
              w_out, w_gate_up, w_down):
    B, S = x.shape[0], x.shape[1]
    C = ctx.shape[1]
    rows = S // GRID_W
    row_id = jnp.repeat(jnp.arange(rows), GRID_W)
    col_id = jnp.tile(jnp.arange(GRID_W), rows)
    cos_r, sin_r = rope_tables(row_id)
    cos_c, sin_c = rope_tables(col_id)
    kv_cols = 2 * KV_WIDTH
    h_lat = x
    h_ctx = ctx
    for layer in range(DEPTH):
        sh_m, sc_m, gt_m, sh_f, sc_f, gt_f = adaln(c, w_mod[layer], b_mod[layer])
        csh_m, csc_m, cgt_m, csh_f, csc_f, cgt_f = adaln(c_ctx[None, :], w_mod[layer], b_mod[layer])

        hn_ctx = modulated_norm(h_ctx, norm_mix[layer], csh_m, csc_m)
        kc, vc = jnp.split(hn_ctx @ w_in[layer][:, :kv_cols], 2, axis=-1)
        kc = rmsnorm(kc.reshape(B, C, ATTN_KV_HEADS, HEAD_DIM), k_gain[layer])
        vc = vc.reshape(B, C, ATTN_KV_HEADS, HEAD_DIM)

        hn_lat = modulated_norm(h_lat, norm_mix[layer], sh_m, sc_m)
        kl, vl, ql, ul, gl = jnp.split(
            hn_lat @ w_in[layer],
            [KV_WIDTH, kv_cols, kv_cols + ATTN_WIDTH, kv_cols + ATTN_WIDTH + MLP_WIDTH], axis=-1)
        ql = axial_rope(rmsnorm(ql.reshape(B, S, ATTN_HEADS, HEAD_DIM), q_gain[layer]), cos_r, sin_r, cos_c, sin_c)
        kl = axial_rope(rmsnorm(kl.reshape(B, S, ATTN_KV_HEADS, HEAD_DIM), k_gain[layer]), cos_r, sin_r, cos_c, sin_c)
        vl = vl.reshape(B, S, ATTN_KV_HEADS, HEAD_DIM)

        attn_lat = window_attention(ql, kl, vl, kc, vc, attn_sink[layer])
        mlp_lat = chunk_spatial_gating(ul, gl, gate_gain[layer], w_spatial[layer], b_spatial[layer])
        h_lat = h_lat + gt_m * merge_groups(attn_lat, mlp_lat, attn_out_gain[layer], mlp_out_gain[layer], w_out[layer])
        h_lat = swiglu_ffn(h_lat, sh_f, sc_f, gt_f, norm_ffn[layer], w_gate_up[layer], w_down[layer])

        if layer + 1 < DEPTH:
            qc, uc, gcx = jnp.split(hn_ctx @ w_in[layer][:, kv_cols:], [ATTN_WIDTH, ATTN_WIDTH + MLP_WIDTH], axis=-1)
            qc = rmsnorm(qc.reshape(B, C, ATTN_HEADS, HEAD_DIM), q_gain[layer])
            attn_ctx = context_attention(qc, kc, vc, attn_sink[layer])
            mlp_ctx = chunk_spatial_gating(uc, gcx, gate_gain[layer], w_spatial[layer], b_spatial[layer])
            h_ctx = h_ctx + cgt_m * merge_groups(attn_ctx, mlp_ctx, attn_out_gain[layer], mlp_out_gain[layer], w_out[layer])
            h_ctx = swiglu_ffn(h_ctx, csh_f, csc_f, cgt_f, norm_ffn[layer], w_gate_up[layer], w_down[layer])
    return h_lat
```

```python
import functools
import math

import jax
import jax.numpy as jnp
from jax import lax
from jax.experimental import pallas as pl
from jax.experimental.pallas import tpu as pltpu

F32 = jnp.float32
BF16 = jnp.bfloat16

HEAD_DIM = 64
ATTN_HEADS = 8
ATTN_KV_HEADS = 2
ATTN_GROUP = ATTN_HEADS // ATTN_KV_HEADS
MLP_HEADS = 8
N_MOD = 6
BLOCK = 128
GRID_W = 64
ROPE_THETA = 10000.0
EPS = 1e-6
MASKED = -1e30

LANES = 128
MXU_DIM = 256
VMEM_LIMIT = 56 * 1024 * 1024


def _dot(a, b):
    return jnp.dot(a, b, preferred_element_type=F32)


def _silu(x):
    return x * (1.0 / (1.0 + jnp.exp(-x)))


def _gelu_tanh(x):
    c = math.sqrt(2.0 / math.pi)
    return 0.5 * x * (1.0 + jnp.tanh(c * (x + 0.044715 * (x * x * x))))


def _mod_norm(x, gain, shift, scale):
    y = x * lax.rsqrt(jnp.mean(x * x, axis=-1, keepdims=True) + EPS)
    return (y * gain) * (1.0 + scale) + shift


def _head_rms(x, bd):
    x2 = x * x
    hi = x2.astype(BF16)
    lo = (x2 - hi.astype(F32)).astype(BF16)
    ms = _dot(hi, bd) + _dot(lo, bd)
    return x * lax.rsqrt(ms + EPS)


def _rope(x, cos, sin_up, sin_dn):
    up = pltpu.roll(x, LANES - 16, 1)
    dn = pltpu.roll(x, 16, 1)
    return x * cos + up * sin_up + dn * sin_dn


def _adaln_kernel(cond_ref, w_ref, b_ref, o_ref):
    s = _silu(cond_ref[...])
    o_ref[...] = jnp.dot(s, w_ref[...], preferred_element_type=F32,
                         precision=lax.Precision.HIGHEST) + b_ref[...]


def _adaln(cond, w_mod, b_mod):
    rows, d = cond.shape
    n = w_mod.shape[1]
    tn = n // 4
    return pl.pallas_call(
        _adaln_kernel,
        grid=(n // tn,),
        in_specs=[pl.BlockSpec((rows, d), lambda j: (0, 0)),
                  pl.BlockSpec((d, tn), lambda j: (0, j)),
                  pl.BlockSpec((1, tn), lambda j: (0, j))],
        out_specs=pl.BlockSpec((rows, tn), lambda j: (0, j)),
        out_shape=jax.ShapeDtypeStruct((rows, n), F32),
        compiler_params=pltpu.CompilerParams(
            dimension_semantics=("arbitrary",), vmem_limit_bytes=VMEM_LIMIT),
        name="adaln",
    )(cond, w_mod, b_mod)


def _ctx_kernel(x_ref, mod_ref, gain_ref, w_ref, kgain_ref, bd_ref, kc_ref, vct_ref):
    hn = _mod_norm(x_ref[...], gain_ref[...], mod_ref[0:1, :], mod_ref[1:2, :])
    kv = _dot(hn.astype(BF16), w_ref[...])
    k = _head_rms(kv[:, :LANES], bd_ref[:LANES, :LANES]) * kgain_ref[...]
    kc_ref[...] = k.astype(BF16)
    vct_ref[...] = kv[:, LANES:].T.astype(BF16)


def _ctx_kv(ctx, mod3, ctx_row, norm_gain, w_in_bf, kgain2, bd):
    b, c, d = ctx.shape
    kvw = 2 * ATTN_KV_HEADS * HEAD_DIM
    return pl.pallas_call(
        _ctx_kernel,
        grid=(b,),
        in_specs=[pl.BlockSpec((None, c, d), lambda i: (i, 0, 0)),
                  pl.BlockSpec((None, N_MOD, d), lambda i: (ctx_row, 0, 0)),
                  pl.BlockSpec((1, d), lambda i: (0, 0)),
                  pl.BlockSpec((d, kvw), lambda i: (0, 0)),
                  pl.BlockSpec((1, LANES), lambda i: (0, 0)),
                  pl.BlockSpec((MXU_DIM, MXU_DIM), lambda i: (0, 0))],
        out_specs=[pl.BlockSpec((None, c, LANES), lambda i: (i, 0, 0)),
                   pl.BlockSpec((None, LANES, c), lambda i: (i, 0, 0))],
        out_shape=[jax.ShapeDtypeStruct((b, c, LANES), BF16),
                   jax.ShapeDtypeStruct((b, LANES, c), BF16)],
        compiler_params=pltpu.CompilerParams(
            dimension_semantics=("arbitrary",), vmem_limit_bytes=VMEM_LIMIT),
        name="ctx_kv",
    )(ctx, mod3, norm_gain, w_in_bf, kgain2, bd)


def _inproj_kernel(x_ref, mod_ref, gain_ref, w_ref, cos_ref, sup_ref, sdn_ref,
                   qgain_ref, kgain_ref, ggain_ref, ws_ref, bs_ref, ogain_ref, bd_ref,
                   k_ref, vt_ref, qt_ref, mlp_ref):
    t = x_ref.shape[0]
    nblk = t // BLOCK
    kvw = ATTN_KV_HEADS * HEAD_DIM
    aw = ATTN_HEADS * HEAD_DIM
    mw = MLP_HEADS * HEAD_DIM
    bd = bd_ref[...]
    cos, sup, sdn = cos_ref[...], sup_ref[...], sdn_ref[...]

    hn = _mod_norm(x_ref[...], gain_ref[...], mod_ref[0:1, :], mod_ref[1:2, :]).astype(BF16)

    kv = _dot(hn, w_ref[:, 0:2 * kvw])
    k = _head_rms(kv[:, :kvw], bd[:kvw, :kvw]) * kgain_ref[...]
    k_ref[...] = _rope(k, cos, sup, sdn).astype(BF16)
    vt_ref[...] = kv[:, kvw:].T.astype(BF16)

    q = _dot(hn, w_ref[:, 2 * kvw:2 * kvw + aw])
    for half in range(aw // MXU_DIM):
        qh = _head_rms(q[:, half * MXU_DIM:(half + 1) * MXU_DIM], bd)
        qh = qh * qgain_ref[:, half * MXU_DIM:(half + 1) * MXU_DIM]
        for sl in range(MXU_DIM // LANES):
            qs = _rope(qh[:, sl * LANES:(sl + 1) * LANES], cos, sup, sdn)
            qst = qs.T
            for hh in range(LANES // HEAD_DIM):
                head = (half * MXU_DIM + sl * LANES) // HEAD_DIM + hh
                kvh, grp = head // ATTN_GROUP, head % ATTN_GROUP
                for jb in range(nblk):
                    col = (jb * ATTN_GROUP + grp) * BLOCK
                    qt_ref[kvh, :, col:col + BLOCK] = qst[
                        hh * HEAD_DIM:(hh + 1) * HEAD_DIM, jb * BLOCK:(jb + 1) * BLOCK].astype(BF16)

    c0 = 2 * kvw + aw
    u = _gelu_tanh(_dot(hn, w_ref[:, c0:c0 + mw]))
    g = _gelu_tanh(_dot(hn, w_ref[:, c0 + mw:c0 + 2 * mw]))
    gn = jnp.concatenate(
        [_head_rms(g[:, h * MXU_DIM:(h + 1) * MXU_DIM], bd) for h in range(mw // MXU_DIM)],
        axis=1) * ggain_ref[...]
    gnb = gn.astype(BF16)
    low_head = lax.broadcasted_iota(jnp.int32, (BLOCK, t), 1) % LANES < HEAD_DIM
    mixed_slabs = []
    for p in range(MLP_HEADS // 2):
        rhs = jnp.concatenate(
            [gnb[c * BLOCK:(c + 1) * BLOCK, p * LANES:(p + 1) * LANES] for c in range(nblk)], axis=1)
        a = _dot(ws_ref[2 * p], rhs)
        b = _dot(ws_ref[2 * p + 1], rhs)
        mixed_slabs.append(jnp.where(low_head, a, b))
    rows = []
    for c in range(nblk):
        mixed_c = jnp.concatenate(
            [m[:, c * LANES:(c + 1) * LANES] for m in mixed_slabs], axis=1) + bs_ref[...]
        rows.append(u[c * BLOCK:(c + 1) * BLOCK, :] * mixed_c)
    o = jnp.concatenate(rows, axis=0)
    o = o * lax.rsqrt(jnp.mean(o * o, axis=-1, keepdims=True) + EPS) * ogain_ref[...]
    mlp_ref[...] = o.astype(BF16)


def _inproj(x, mod3, norm_gain, w_in_bf, cos, sup, sdn, qgain, kgain2, ggain, ws_bf, bs_full,
            ogain, bd, tile):
    b, s, d = x.shape
    inw = w_in_bf.shape[1]
    aw = ATTN_HEADS * HEAD_DIM
    mw = MLP_HEADS * HEAD_DIM
    const = lambda shape: pl.BlockSpec(shape, lambda i, bb: (0,) * len(shape))
    return pl.pallas_call(
        _inproj_kernel,
        grid=(s // tile, b),
        in_specs=[pl.BlockSpec((None, tile, d), lambda i, bb: (bb, i, 0)),
                  pl.BlockSpec((None, N_MOD, d), lambda i, bb: (bb, 0, 0)),
                  const((1, d)),
                  const((d, inw)),
                  pl.BlockSpec((tile, LANES), lambda i, bb: (i, 0)),
                  pl.BlockSpec((tile, LANES), lambda i, bb: (i, 0)),
                  pl.BlockSpec((tile, LANES), lambda i, bb: (i, 0)),
                  const((1, aw)), const((1, LANES)), const((1, mw)),
                  const((MLP_HEADS, BLOCK, BLOCK)), const((BLOCK, mw)), const((1, mw)),
                  const((MXU_DIM, MXU_DIM))],
        out_specs=[pl.BlockSpec((None, tile, LANES), lambda i, bb: (bb, i, 0)),
                   pl.BlockSpec((None, LANES, tile), lambda i, bb: (bb, 0, i)),
                   pl.BlockSpec((None, ATTN_KV_HEADS, HEAD_DIM, ATTN_GROUP * tile),
                                lambda i, bb: (bb, 0, 0, i)),
                   pl.BlockSpec((None, tile, mw), lambda i, bb: (bb, i, 0))],
        out_shape=[jax.ShapeDtypeStruct((b, s, LANES), BF16),
                   jax.ShapeDtypeStruct((b, LANES, s), BF16),
                   jax.ShapeDtypeStruct((b, ATTN_KV_HEADS, HEAD_DIM, ATTN_GROUP * s), BF16),
                   jax.ShapeDtypeStruct((b, s, mw), BF16)],
        compiler_params=pltpu.CompilerParams(
            dimension_semantics=("arbitrary", "arbitrary"), vmem_limit_bytes=VMEM_LIMIT),
        name="inproj",
    )(x, mod3, norm_gain, w_in_bf, cos, sup, sdn, qgain, kgain2, ggain, ws_bf, bs_full, ogain, bd)


def _attn_kernel(x_ref, mod_ref, kp_ref, km_ref, kn_ref, vp_ref, vm_ref, vn_ref,
                 kc_ref, vct_ref, qt_ref, mlp_ref, sink_ref, bias_ref, again_ref, wo_ref,
                 o_ref):
    tq = x_ref.shape[0]
    nblk = tq // BLOCK
    i = pl.program_id(1)
    last = pl.num_programs(1) - 1
    gq = ATTN_GROUP * BLOCK

    k_ext = jnp.concatenate([kp_ref[...], km_ref[...], kn_ref[...]], axis=0)
    vt_ext = jnp.concatenate([vp_ref[...], vm_ref[...], vn_ref[...]], axis=1)
    kc = kc_ref[...]
    vct = vct_ref[...]
    zeros_q = jnp.zeros((HEAD_DIM, gq), BF16)
    bias_prev = bias_ref[0:BLOCK, :]
    bias_next = bias_ref[BLOCK:2 * BLOCK, :]

    attn_rows = []
    for jb in range(nblk):
        pen_prev = jnp.where(i == 0, MASKED, 0.0) if jb == 0 else 0.0
        pen_next = jnp.where(i == last, MASKED, 0.0) if jb == nblk - 1 else 0.0
        out_t = []
        for kvh in range(ATTN_KV_HEADS):
            k_all = jnp.concatenate([k_ext[jb * BLOCK:(jb + 3) * BLOCK, :], kc], axis=0)
            qt = qt_ref[kvh, :, jb * gq:(jb + 1) * gq]
            rhs = jnp.concatenate([qt, zeros_q] if kvh == 0 else [zeros_q, qt], axis=0)
            s = _dot(k_all, rhs)
            s_prev = s[0:BLOCK] + (bias_prev + pen_prev)
            s_cur = s[BLOCK:2 * BLOCK]
            s_next = s[2 * BLOCK:3 * BLOCK] + (bias_next + pen_next)
            s_ctx = s[3 * BLOCK:]
            sink = sink_ref[kvh]
            m = jnp.maximum(
                jnp.maximum(jnp.max(s_prev, axis=0, keepdims=True), jnp.max(s_cur, axis=0, keepdims=True)),
                jnp.maximum(jnp.max(s_next, axis=0, keepdims=True), jnp.max(s_ctx, axis=0, keepdims=True)))
            m = jnp.maximum(m, sink)
            e = [jnp.exp(part - m) for part in (s_prev, s_cur, s_next, s_ctx)]
            denom = (jnp.sum(e[0], axis=0, keepdims=True) + jnp.sum(e[1], axis=0, keepdims=True)
                     + jnp.sum(e[2], axis=0, keepdims=True) + jnp.sum(e[3], axis=0, keepdims=True)
                     + jnp.exp(sink - m))
            p = jnp.concatenate([part.astype(BF16) for part in e], axis=0)
            v_all = jnp.concatenate(
                [vt_ext[kvh * HEAD_DIM:(kvh + 1) * HEAD_DIM, jb * BLOCK:(jb + 3) * BLOCK],
                 vct[kvh * HEAD_DIM:(kvh + 1) * HEAD_DIM, :]], axis=1)
            o_t = _dot(v_all, p) * (1.0 / denom)
            out_t.extend(o_t[:, g * BLOCK:(g + 1) * BLOCK] for g in range(ATTN_GROUP))
        o_all = jnp.concatenate(out_t, axis=0)
        ms = jnp.mean(o_all * o_all, axis=0, keepdims=True)
        y = o_all * lax.rsqrt(ms + EPS) * again_ref[...]
        attn_rows.append(y.T.astype(BF16))
    attn = jnp.concatenate(attn_rows, axis=0)
    merged = jnp.concatenate([attn, mlp_ref[...]], axis=1)
    proj = _dot(merged, wo_ref[...])
    o_ref[...] = x_ref[...] + mod_ref[2:3, :] * proj


def _attn(x, mod3, k, vt, kc, vct, qt, mlpn, sink_rows, bias, again_b, wo_bf, tile):
    b, s, d = x.shape
    c = kc.shape[1]
    aw = ATTN_HEADS * HEAD_DIM
    mw = mlpn.shape[2]
    r = tile // BLOCK
    nb = s // BLOCK
    gq = ATTN_GROUP * BLOCK
    const = lambda shape: pl.BlockSpec(shape, lambda bb, i: (0,) * len(shape))
    return pl.pallas_call(
        _attn_kernel,
        grid=(b, s // tile),
        in_specs=[pl.BlockSpec((None, tile, d), lambda bb, i: (bb, i, 0)),
                  pl.BlockSpec((None, N_MOD, d), lambda bb, i: (bb, 0, 0)),
                  pl.BlockSpec((None, BLOCK, LANES), lambda bb, i: (bb, jnp.maximum(i * r - 1, 0), 0)),
                  pl.BlockSpec((None, tile, LANES), lambda bb, i: (bb, i, 0)),
                  pl.BlockSpec((None, BLOCK, LANES), lambda bb, i: (bb, jnp.minimum((i + 1) * r, nb - 1), 0)),
                  pl.BlockSpec((None, LANES, BLOCK), lambda bb, i: (bb, 0, jnp.maximum(i * r - 1, 0))),
                  pl.BlockSpec((None, LANES, tile), lambda bb, i: (bb, 0, i)),
                  pl.BlockSpec((None, LANES, BLOCK), lambda bb, i: (bb, 0, jnp.minimum((i + 1) * r, nb - 1))),
                  pl.BlockSpec((None, c, LANES), lambda bb, i: (bb, 0, 0)),
                  pl.BlockSpec((None, LANES, c), lambda bb, i: (bb, 0, 0)),
                  pl.BlockSpec((None, ATTN_KV_HEADS, HEAD_DIM, ATTN_GROUP * tile), lambda bb, i: (bb, 0, 0, i)),
                  pl.BlockSpec((None, tile, mw), lambda bb, i: (bb, i, 0)),
                  const((ATTN_KV_HEADS, 1, gq)),
                  const((2 * BLOCK, gq)),
                  const((aw, BLOCK)),
                  const((aw + mw, d))],
        out_specs=pl.BlockSpec((None, tile, d), lambda bb, i: (bb, i, 0)),
        out_shape=jax.ShapeDtypeStruct((b, s, d), F32),
        compiler_params=pltpu.CompilerParams(
            dimension_semantics=("arbitrary", "arbitrary"), vmem_limit_bytes=VMEM_LIMIT),
        name="attn",
    )(x, mod3, k, k, k, vt, vt, vt, kc, vct, qt, mlpn, sink_rows, bias, again_b, wo_bf)


def _ffn_kernel(h_ref, mod_ref, gain_ref, wg_ref, wu_ref, wd_ref, o_ref, hid_ref):
    h = h_ref[...]
    hn = _mod_norm(h, gain_ref[...], mod_ref[3:4, :], mod_ref[4:5, :]).astype(BF16)
    ff = wg_ref.shape[1]
    for c in range(ff // MXU_DIM):
        a = _dot(hn, wg_ref[:, c * MXU_DIM:(c + 1) * MXU_DIM])
        b = _dot(hn, wu_ref[:, c * MXU_DIM:(c + 1) * MXU_DIM])
        hid_ref[:, c * MXU_DIM:(c + 1) * MXU_DIM] = (_silu(a) * b).astype(BF16)
    o_ref[...] = h + mod_ref[5:6, :] * _dot(hid_ref[...], wd_ref[...])


def _ffn(h, mod3, norm_gain, wg_bf, wu_bf, wd_bf, tile):
    b, s, d = h.shape
    ff = wg_bf.shape[1]
    const = lambda shape: pl.BlockSpec(shape, lambda bb, i: (0,) * len(shape))
    return pl.pallas_call(
        _ffn_kernel,
        grid=(b, s // tile),
        in_specs=[pl.BlockSpec((None, tile, d), lambda bb, i: (bb, i, 0)),
                  pl.BlockSpec((None, N_MOD, d), lambda bb, i: (bb, 0, 0)),
                  const((1, d)), const((d, ff)), const((d, ff)), const((ff, d))],
        out_specs=pl.BlockSpec((None, tile, d), lambda bb, i: (bb, i, 0)),
        out_shape=jax.ShapeDtypeStruct((b, s, d), F32),
        scratch_shapes=[pltpu.VMEM((tile, ff), BF16)],
        compiler_params=pltpu.CompilerParams(
            dimension_semantics=("arbitrary", "arbitrary"), vmem_limit_bytes=VMEM_LIMIT),
        name="ffn",
    )(h, mod3, norm_gain, wg_bf, wu_bf, wd_bf)


def _rope_tables(s):
    axis_dim = HEAD_DIM // 2
    pos = jnp.arange(s)
    inv_freq = ROPE_THETA ** (-jnp.arange(0, axis_dim, 2, dtype=F32) / axis_dim)
    ang_r = (pos // GRID_W).astype(F32)[:, None] * inv_freq[None, :]
    ang_c = (pos % GRID_W).astype(F32)[:, None] * inv_freq[None, :]
    cr, sr, cc, sc = jnp.cos(ang_r), jnp.sin(ang_r), jnp.cos(ang_c), jnp.sin(ang_c)
    z = jnp.zeros_like(sr)
    reps = LANES // HEAD_DIM
    cos = jnp.tile(jnp.concatenate([cr, cr, cc, cc], axis=1), (1, reps))
    sin_up = jnp.tile(jnp.concatenate([-sr, z, -sc, z], axis=1), (1, reps))
    sin_dn = jnp.tile(jnp.concatenate([z, sr, z, sc], axis=1), (1, reps))
    return cos, sin_up, sin_dn


def _window_bias():
    c = jnp.arange(BLOCK)[:, None]
    r = jnp.arange(BLOCK)[None, :]
    prev = jnp.where(c >= r, 0.0, MASKED).astype(F32)
    nxt = jnp.where(c <= r, 0.0, MASKED).astype(F32)
    return jnp.tile(jnp.concatenate([prev, nxt], axis=0), (1, ATTN_GROUP))


def kernel(x, c, ctx, c_ctx, w_mod, b_mod, norm_mix, norm_ffn, w_in, q_gain, k_gain, attn_sink,
           gate_gain, w_spatial, b_spatial, attn_out_gain, mlp_out_gain, w_out, w_gate_up, w_down):
    b, s, d = x.shape
    assert w_mod.shape[0] == 1, "single-layer problem"
    assert s % 512 == 0 and d % LANES == 0
    aw = ATTN_HEADS * HEAD_DIM
    mw = MLP_HEADS * HEAD_DIM
    ff = w_down.shape[1]

    rows = -(-(b + 1) // 8) * 8
    cond = jnp.concatenate([c, c_ctx[None, :], jnp.zeros((rows - b - 1, d), F32)], axis=0)
    mod3 = _adaln(cond, w_mod[0], b_mod[0][None, :]).reshape(rows, N_MOD, d)

    w_in_bf = w_in[0].astype(BF16)
    bd = jnp.kron(jnp.eye(MXU_DIM // HEAD_DIM, dtype=F32),
                  jnp.full((HEAD_DIM, HEAD_DIM), 1.0 / HEAD_DIM, F32)).astype(BF16)
    kgain2 = jnp.tile(k_gain[0], ATTN_KV_HEADS)[None, :]
    qgain = (jnp.tile(q_gain[0], ATTN_HEADS) * HEAD_DIM ** -0.5)[None, :]
    norm_mix_g = norm_mix[0][None, :]

    kc, vct = _ctx_kv(ctx, mod3, b, norm_mix_g, w_in_bf, kgain2, bd)

    cos, sup, sdn = _rope_tables(s)
    bs_full = jnp.repeat(b_spatial[0].T, HEAD_DIM, axis=1)
    k, vt, qt, mlpn = _inproj(
        x, mod3, norm_mix_g, w_in_bf, cos, sup, sdn, qgain, kgain2,
        gate_gain[0].reshape(1, mw), w_spatial[0].astype(BF16), bs_full,
        mlp_out_gain[0][None, :], bd, tile=512)

    sink_rows = jnp.repeat(attn_sink[0].reshape(ATTN_KV_HEADS, ATTN_GROUP), BLOCK, axis=1)[:, None, :]
    again_b = jnp.broadcast_to(attn_out_gain[0][:, None], (aw, BLOCK))
    h1 = _attn(x, mod3, k, vt, kc, vct, qt, mlpn, sink_rows, _window_bias(), again_b,
               w_out[0].astype(BF16), tile=512)

    wgu = w_gate_up[0]
    return _ffn(h1, mod3, norm_ffn[0][None, :], wgu[:, :ff].astype(BF16), wgu[:, ff:].astype(BF16),
                w_down[0].astype(BF16), tile=512)
```

```python
import functools
import math

import jax
import jax.numpy as jnp
from jax import lax
from jax.experimental import pallas as pl
from jax.experimental.pallas import tpu as pltpu

F32 = jnp.float32
BF16 = jnp.bfloat16

HEAD_DIM = 64
ATTN_HEADS = 8
ATTN_KV_HEADS = 2
ATTN_GROUP = ATTN_HEADS // ATTN_KV_HEADS
MLP_HEADS = 8
N_MOD = 6
BLOCK = 128
GRID_W = 64
ROPE_THETA = 10000.0
EPS = 1e-6
MASKED = -1e30
LOG2E = math.log2(math.e)

LANES = 128
BF16_SUBLANES = 16
MXU_DIM = 256
VMEM_LIMIT = 56 * 1024 * 1024
HEADS_PER_STEP = 4


def _dot(a, b):
    return jnp.dot(a, b, preferred_element_type=F32)


def _silu(x):
    return x * (1.0 / (1.0 + jnp.exp(-x)))


def _gelu2_tanh(x):
    c = math.sqrt(2.0 / math.pi)
    return x * (1.0 + jnp.tanh(x * (c + (c * 0.044715) * (x * x))))


def _mod_norm(x, gain, shift, scale):
    y = x * lax.rsqrt(jnp.mean(x * x, axis=-1, keepdims=True) + EPS)
    return y * (gain * (1.0 + scale)) + shift


def _head_ms(x, bd):
    return _dot((x * x).astype(BF16), bd)


def _rope(x, cos, sin_up, sin_dn):
    up = pltpu.roll(x, LANES - 16, 1)
    dn = pltpu.roll(x, 16, 1)
    return x * cos + up * sin_up + dn * sin_dn


def _adaln_kernel(cond_ref, w_ref, b_ref, o_ref):
    s = _silu(cond_ref[...])
    o_ref[...] = jnp.dot(s, w_ref[...], preferred_element_type=F32,
                         precision=lax.Precision.HIGHEST) + b_ref[...]


def _adaln(cond, w_mod, b_mod):
    rows, d = cond.shape
    n = w_mod.shape[1]
    tn = n // 4
    return pl.pallas_call(
        _adaln_kernel,
        grid=(n // tn,),
        in_specs=[pl.BlockSpec((rows, d), lambda j: (0, 0)),
                  pl.BlockSpec((d, tn), lambda j: (0, j)),
                  pl.BlockSpec((1, tn), lambda j: (0, j))],
        out_specs=pl.BlockSpec((rows, tn), lambda j: (0, j)),
        out_shape=jax.ShapeDtypeStruct((rows, n), F32),
        compiler_params=pltpu.CompilerParams(
            dimension_semantics=("arbitrary",), vmem_limit_bytes=VMEM_LIMIT),
        name="adaln",
    )(cond, w_mod, b_mod)


def _ctx_kernel(x_ref, mod_ref, gain_ref, w_ref, kgain_ref, bd_ref, kc_ref, vct_ref):
    hn = _mod_norm(x_ref[...], gain_ref[...], mod_ref[0:1, :], mod_ref[1:2, :])
    kv = _dot(hn.astype(BF16), w_ref[...])
    k = kv[:, :LANES]
    k = k * lax.rsqrt(_head_ms(k, bd_ref[:LANES, :LANES]) + EPS) * kgain_ref[...]
    kc_ref[...] = k.astype(BF16)
    vct_ref[...] = kv[:, LANES:].T.astype(BF16)


def _ctx_kv(ctx, mod3, ctx_row, norm_gain, w_in_bf, kgain2, bd):
    b, c, d = ctx.shape
    kvw = 2 * ATTN_KV_HEADS * HEAD_DIM
    return pl.pallas_call(
        _ctx_kernel,
        grid=(b,),
        in_specs=[pl.BlockSpec((None, c, d), lambda i: (i, 0, 0)),
                  pl.BlockSpec((None, N_MOD, d), lambda i: (ctx_row, 0, 0)),
                  pl.BlockSpec((1, d), lambda i: (0, 0)),
                  pl.BlockSpec((d, kvw), lambda i: (0, 0)),
                  pl.BlockSpec((1, LANES), lambda i: (0, 0)),
                  pl.BlockSpec((MXU_DIM, MXU_DIM), lambda i: (0, 0))],
        out_specs=[pl.BlockSpec((None, c, LANES), lambda i: (i, 0, 0)),
                   pl.BlockSpec((None, LANES, c), lambda i: (i, 0, 0))],
        out_shape=[jax.ShapeDtypeStruct((b, c, LANES), BF16),
                   jax.ShapeDtypeStruct((b, LANES, c), BF16)],
        compiler_params=pltpu.CompilerParams(
            dimension_semantics=("arbitrary",), vmem_limit_bytes=VMEM_LIMIT),
        name="ctx_kv",
    )(ctx, mod3, norm_gain, w_in_bf, kgain2, bd)


def _inproj_kernel(x_ref, mod_ref, gain_ref, w_ref, cos_ref, sup_ref, sdn_ref,
                   qgain_ref, kgain_ref, ggain_ref, ws_ref, bs_ref, ogain_ref, bd_ref,
                   k_ref, vt_ref, qt_ref, mlp_ref):
    t = x_ref.shape[0]
    nblk = t // BLOCK
    kvw = ATTN_KV_HEADS * HEAD_DIM
    aw = ATTN_HEADS * HEAD_DIM
    mw = MLP_HEADS * HEAD_DIM
    bd = bd_ref[...]
    cos, sup, sdn = cos_ref[...], sup_ref[...], sdn_ref[...]

    hn = _mod_norm(x_ref[...], gain_ref[...], mod_ref[0:1, :], mod_ref[1:2, :]).astype(BF16)

    c0 = 2 * kvw + aw
    halves = range(aw // MXU_DIM)
    q = _dot(hn, w_ref[:, 2 * kvw:2 * kvw + aw])
    kv = _dot(hn, w_ref[:, 0:2 * kvw])
    q_ms = [_head_ms(q[:, h * MXU_DIM:(h + 1) * MXU_DIM], bd) for h in halves]
    k_ms = _head_ms(kv[:, :kvw], bd[:kvw, :kvw])
    g_raw = _dot(hn, w_ref[:, c0 + mw:c0 + 2 * mw])

    k = kv[:, :kvw] * lax.rsqrt(k_ms + EPS) * kgain_ref[...]
    k_ref[...] = _rope(k, cos, sup, sdn).astype(BF16)
    vt_ref[...] = kv[:, kvw:].T.astype(BF16)

    for half in halves:
        qh = q[:, half * MXU_DIM:(half + 1) * MXU_DIM] * lax.rsqrt(q_ms[half] + EPS)
        qh = qh * qgain_ref[:, half * MXU_DIM:(half + 1) * MXU_DIM]
        for sl in range(MXU_DIM // LANES):
            qs = _rope(qh[:, sl * LANES:(sl + 1) * LANES], cos, sup, sdn)
            qst = qs.T
            for hh in range(LANES // HEAD_DIM):
                head = (half * MXU_DIM + sl * LANES) // HEAD_DIM + hh
                kvh, grp = head // ATTN_GROUP, head % ATTN_GROUP
                for jb in range(nblk):
                    col = (jb * ATTN_GROUP + grp) * BLOCK
                    qt_ref[kvh, :, col:col + BLOCK] = qst[
                        hh * HEAD_DIM:(hh + 1) * HEAD_DIM, jb * BLOCK:(jb + 1) * BLOCK].astype(BF16)

    u_raw = _dot(hn, w_ref[:, c0:c0 + mw])

    g = _gelu2_tanh(g_raw)
    g_ms = [_head_ms(g[:, h * MXU_DIM:(h + 1) * MXU_DIM], bd) for h in halves]
    gn = jnp.concatenate(
        [g[:, h * MXU_DIM:(h + 1) * MXU_DIM] * lax.rsqrt(g_ms[h] + 4.0 * EPS) for h in halves],
        axis=1) * ggain_ref[...]
    gnb = gn.astype(BF16)
    low_head = lax.broadcasted_iota(jnp.int32, (BLOCK, t), 1) % LANES < HEAD_DIM
    mixed_slabs = []
    for p in range(MLP_HEADS // 2):
        rhs = jnp.concatenate(
            [gnb[c * BLOCK:(c + 1) * BLOCK, p * LANES:(p + 1) * LANES] for c in range(nblk)], axis=1)
        a = _dot(ws_ref[2 * p], rhs)
        b = _dot(ws_ref[2 * p + 1], rhs)
        mixed_slabs.append(jnp.where(low_head, a, b))

    u = _gelu2_tanh(u_raw)
    rows = []
    for c in range(nblk):
        mixed_c = jnp.concatenate(
            [m[:, c * LANES:(c + 1) * LANES] for m in mixed_slabs], axis=1) + bs_ref[...]
        rows.append(u[c * BLOCK:(c + 1) * BLOCK, :] * mixed_c)
    o = jnp.concatenate(rows, axis=0)
    o = o * lax.rsqrt(jnp.mean(o * o, axis=-1, keepdims=True) + 4.0 * EPS) * ogain_ref[...]
    mlp_ref[...] = o.astype(BF16)


def _inproj(x, mod3, norm_gain, w_in_bf, cos, sup, sdn, qgain, kgain2, ggain, ws_bf, bs_full,
            ogain, bd, tile):
    b, s, d = x.shape
    inw = w_in_bf.shape[1]
    aw = ATTN_HEADS * HEAD_DIM
    mw = MLP_HEADS * HEAD_DIM
    const = lambda shape: pl.BlockSpec(shape, lambda i, bb: (0,) * len(shape))
    return pl.pallas_call(
        _inproj_kernel,
        grid=(s // tile, b),
        in_specs=[pl.BlockSpec((None, tile, d), lambda i, bb: (bb, i, 0)),
                  pl.BlockSpec((None, N_MOD, d), lambda i, bb: (bb, 0, 0)),
                  const((1, d)),
                  const((d, inw)),
                  pl.BlockSpec((tile, LANES), lambda i, bb: (i, 0)),
                  pl.BlockSpec((tile, LANES), lambda i, bb: (i, 0)),
                  pl.BlockSpec((tile, LANES), lambda i, bb: (i, 0)),
                  const((1, aw)), const((1, LANES)), const((1, mw)),
                  const((MLP_HEADS, BLOCK, BLOCK)), const((BLOCK, mw)), const((1, mw)),
                  const((MXU_DIM, MXU_DIM))],
        out_specs=[pl.BlockSpec((None, tile, LANES), lambda i, bb: (bb, i, 0)),
                   pl.BlockSpec((None, LANES, tile), lambda i, bb: (bb, 0, i)),
                   pl.BlockSpec((None, ATTN_KV_HEADS, HEAD_DIM, ATTN_GROUP * tile),
                                lambda i, bb: (bb, 0, 0, i)),
                   pl.BlockSpec((None, tile, mw), lambda i, bb: (bb, i, 0))],
        out_shape=[jax.ShapeDtypeStruct((b, s, LANES), BF16),
                   jax.ShapeDtypeStruct((b, LANES, s), BF16),
                   jax.ShapeDtypeStruct((b, ATTN_KV_HEADS, HEAD_DIM, ATTN_GROUP * s), BF16),
                   jax.ShapeDtypeStruct((b, s, mw), BF16)],
        compiler_params=pltpu.CompilerParams(
            dimension_semantics=("arbitrary", "arbitrary"), vmem_limit_bytes=VMEM_LIMIT),
        name="inproj",
    )(x, mod3, norm_gain, w_in_bf, cos, sup, sdn, qgain, kgain2, ggain, ws_bf, bs_full, ogain, bd)


def _attn_kernel(x_ref, mod_ref, kp_ref, km_ref, kn_ref, vp_ref, vm_ref, vn_ref,
                 kc_ref, vct_ref, qt_ref, mlp_ref, sink_ref, bias_ref, again_ref, wo_ref,
                 o_ref):
    tq = x_ref.shape[0]
    nblk = tq // BLOCK
    i = pl.program_id(1)
    last = pl.num_programs(1) - 1
    gq = ATTN_GROUP * BLOCK

    k_ext = jnp.concatenate([kp_ref[...], km_ref[...], kn_ref[...]], axis=0)
    vt_ext = jnp.concatenate([vp_ref[...], vm_ref[...], vn_ref[...]], axis=1)
    kc = kc_ref[...]
    vct = vct_ref[...]
    cw = HEADS_PER_STEP * BLOCK
    zeros_q = jnp.zeros((HEAD_DIM, cw), BF16)
    ones_rows = jnp.ones((BF16_SUBLANES, 3 * BLOCK + kc.shape[0]), BF16)
    bias_prev = bias_ref[0:BLOCK, :cw]
    bias_next = bias_ref[BLOCK:2 * BLOCK, :cw]

    def score_matmul(jb, kvh, part):
        qt = qt_ref[kvh, :, jb * gq + part * cw:jb * gq + (part + 1) * cw]
        rhs = jnp.concatenate([qt, zeros_q] if kvh == 0 else [zeros_q, qt], axis=0)
        return _dot(kc, rhs), _dot(k_ext[jb * BLOCK:(jb + 3) * BLOCK, :], rhs)

    def mask_and_max(jb, kvh, part, s):
        pen_prev = jnp.where(i == 0, MASKED, 0.0) if jb == 0 else 0.0
        pen_next = jnp.where(i == last, MASKED, 0.0) if jb == nblk - 1 else 0.0
        s_ctx, s = s
        parts = [s[0:BLOCK] + (bias_prev + pen_prev), s[BLOCK:2 * BLOCK],
                 s[2 * BLOCK:3 * BLOCK] + (bias_next + pen_next), s_ctx]
        m = jnp.maximum(
            jnp.maximum(jnp.max(parts[0], axis=0, keepdims=True), jnp.max(parts[1], axis=0, keepdims=True)),
            jnp.maximum(jnp.max(parts[2], axis=0, keepdims=True), jnp.max(parts[3], axis=0, keepdims=True)))
        return parts, jnp.maximum(m, sink_ref[kvh, :, part * cw:(part + 1) * cw])

    def exp_weights(parts, m):
        return jnp.concatenate([jnp.exp2(p_ - m).astype(BF16) for p_ in parts], axis=0)

    def value_matmul(jb, kvh, part, p):
        v_all = jnp.concatenate(
            [vt_ext[kvh * HEAD_DIM:(kvh + 1) * HEAD_DIM, jb * BLOCK:(jb + 3) * BLOCK],
             vct[kvh * HEAD_DIM:(kvh + 1) * HEAD_DIM, :]], axis=1)
        return _dot(jnp.concatenate([v_all, ones_rows], axis=0), p)

    def normalize(jb, kvh, part, pv, m):
        denom = pv[HEAD_DIM:HEAD_DIM + 1, :] + jnp.exp2(sink_ref[kvh, :, part * cw:(part + 1) * cw] - m)
        o_t = pv[:HEAD_DIM, :] * (1.0 / denom)
        return [o_t[:, g * BLOCK:(g + 1) * BLOCK] for g in range(HEADS_PER_STEP)]

    def block_norm(out_t):
        o_all = jnp.concatenate(out_t, axis=0)
        ms = jnp.mean(o_all * o_all, axis=0, keepdims=True)
        y = o_all * lax.rsqrt(ms + EPS) * again_ref[...]
        return y.T.astype(BF16)

    aw = ATTN_HEADS * HEAD_DIM
    steps = [(jb, kvh, part) for jb in range(nblk) for kvh in range(ATTN_KV_HEADS)
             for part in range(ATTN_GROUP // HEADS_PER_STEP)]
    raw = {n: score_matmul(*steps[n]) for n in range(2)}
    proj_mlp = _dot(mlp_ref[...], wo_ref[aw:, :])
    ready = {0: mask_and_max(*steps[0], raw.pop(0))}
    attn_rows, out_t, unnormalized = [], [], None

    def collect(step, pv, m):
        out_t.extend(normalize(*step, pv, m))
        if len(out_t) == ATTN_HEADS:
            attn_rows.append(block_norm(out_t))
            out_t.clear()

    for n, step in enumerate(steps):
        if n + 1 < len(steps):
            ready[n + 1] = mask_and_max(*steps[n + 1], raw.pop(n + 1))
        if n + 2 < len(steps):
            raw[n + 2] = score_matmul(*steps[n + 2])
        if unnormalized is not None:
            collect(*unnormalized)
        parts, m = ready.pop(n)
        unnormalized = (step, value_matmul(*step, exp_weights(parts, m)), m)
    collect(*unnormalized)
    head_rows = (nblk // 2) * BLOCK
    wo_attn = wo_ref[:aw, :]
    proj_head = _dot(jnp.concatenate(attn_rows[:nblk // 2], axis=0), wo_attn)
    proj_tail = _dot(jnp.concatenate(attn_rows[nblk // 2:], axis=0), wo_attn)
    gate = mod_ref[2:3, :]
    o_ref[:head_rows, :] = x_ref[:head_rows, :] + gate * (proj_head + proj_mlp[:head_rows])
    o_ref[head_rows:, :] = x_ref[head_rows:, :] + gate * (proj_tail + proj_mlp[head_rows:])


def _attn(x, mod3, k, vt, kc, vct, qt, mlpn, sink_rows, bias, again_b, wo_bf, tile):
    b, s, d = x.shape
    c = kc.shape[1]
    aw = ATTN_HEADS * HEAD_DIM
    mw = mlpn.shape[2]
    r = tile // BLOCK
    nb = s // BLOCK
    gq = ATTN_GROUP * BLOCK
    const = lambda shape: pl.BlockSpec(shape, lambda bb, i: (0,) * len(shape))
    return pl.pallas_call(
        _attn_kernel,
        grid=(b, s // tile),
        in_specs=[pl.BlockSpec((None, tile, d), lambda bb, i: (bb, i, 0)),
                  pl.BlockSpec((None, N_MOD, d), lambda bb, i: (bb, 0, 0)),
                  pl.BlockSpec((None, BLOCK, LANES), lambda bb, i: (bb, jnp.maximum(i * r - 1, 0), 0)),
                  pl.BlockSpec((None, tile, LANES), lambda bb, i: (bb, i, 0)),
                  pl.BlockSpec((None, BLOCK, LANES), lambda bb, i: (bb, jnp.minimum((i + 1) * r, nb - 1), 0)),
                  pl.BlockSpec((None, LANES, BLOCK), lambda bb, i: (bb, 0, jnp.maximum(i * r - 1, 0))),
                  pl.BlockSpec((None, LANES, tile), lambda bb, i: (bb, 0, i)),
                  pl.BlockSpec((None, LANES, BLOCK), lambda bb, i: (bb, 0, jnp.minimum((i + 1) * r, nb - 1))),
                  pl.BlockSpec((None, c, LANES), lambda bb, i: (bb, 0, 0)),
                  pl.BlockSpec((None, LANES, c), lambda bb, i: (bb, 0, 0)),
                  pl.BlockSpec((None, ATTN_KV_HEADS, HEAD_DIM, ATTN_GROUP * tile), lambda bb, i: (bb, 0, 0, i)),
                  pl.BlockSpec((None, tile, mw), lambda bb, i: (bb, i, 0)),
                  const((ATTN_KV_HEADS, 1, gq)),
                  const((2 * BLOCK, gq)),
                  const((aw, BLOCK)),
                  const((aw + mw, d))],
        out_specs=pl.BlockSpec((None, tile, d), lambda bb, i: (bb, i, 0)),
        out_shape=jax.ShapeDtypeStruct((b, s, d), F32),
        compiler_params=pltpu.CompilerParams(
            dimension_semantics=("arbitrary", "arbitrary"), vmem_limit_bytes=VMEM_LIMIT),
        name="attn",
    )(x, mod3, k, k, k, vt, vt, vt, kc, vct, qt, mlpn, sink_rows, bias, again_b, wo_bf)


def _ffn_kernel(h_ref, mod_ref, gain_ref, wgu_ref, wd_ref, o_ref, hid_ref):
    h = h_ref[...]
    hn = _mod_norm(h, gain_ref[...], mod_ref[3:4, :], mod_ref[4:5, :]).astype(BF16)
    ff = wd_ref.shape[0]
    for c in range(ff // MXU_DIM):
        a = _dot(hn, wgu_ref[:, c * MXU_DIM:(c + 1) * MXU_DIM])
        b = _dot(hn, wgu_ref[:, ff + c * MXU_DIM:ff + (c + 1) * MXU_DIM])
        hid_ref[:, c * MXU_DIM:(c + 1) * MXU_DIM] = (_silu(a) * b).astype(BF16)
    o_ref[...] = h + mod_ref[5:6, :] * _dot(hid_ref[...], wd_ref[...])


def _ffn(h, mod3, norm_gain, wgu_bf, wd_bf, tile):
    b, s, d = h.shape
    ff = wd_bf.shape[0]
    assert ff % MXU_DIM == 0
    const = lambda shape: pl.BlockSpec(shape, lambda bb, i: (0,) * len(shape))
    return pl.pallas_call(
        _ffn_kernel,
        grid=(b, s // tile),
        in_specs=[pl.BlockSpec((None, tile, d), lambda bb, i: (bb, i, 0)),
                  pl.BlockSpec((None, N_MOD, d), lambda bb, i: (bb, 0, 0)),
                  const((1, d)), const((d, 2 * ff)), const((ff, d))],
        out_specs=pl.BlockSpec((None, tile, d), lambda bb, i: (bb, i, 0)),
        out_shape=jax.ShapeDtypeStruct((b, s, d), F32),
        scratch_shapes=[pltpu.VMEM((tile, ff), BF16)],
        compiler_params=pltpu.CompilerParams(
            dimension_semantics=("arbitrary", "arbitrary"), vmem_limit_bytes=VMEM_LIMIT),
        name="ffn",
    )(h, mod3, norm_gain, wgu_bf, wd_bf)


def _rope_tables(s):
    axis_dim = HEAD_DIM // 2
    pos = jnp.arange(s)
    inv_freq = ROPE_THETA ** (-jnp.arange(0, axis_dim, 2, dtype=F32) / axis_dim)
    ang_r = (pos // GRID_W).astype(F32)[:, None] * inv_freq[None, :]
    ang_c = (pos % GRID_W).astype(F32)[:, None] * inv_freq[None, :]
    cr, sr, cc, sc = jnp.cos(ang_r), jnp.sin(ang_r), jnp.cos(ang_c), jnp.sin(ang_c)
    z = jnp.zeros_like(sr)
    reps = LANES // HEAD_DIM
    cos = jnp.tile(jnp.concatenate([cr, cr, cc, cc], axis=1), (1, reps))
    sin_up = jnp.tile(jnp.concatenate([-sr, z, -sc, z], axis=1), (1, reps))
    sin_dn = jnp.tile(jnp.concatenate([z, sr, z, sc], axis=1), (1, reps))
    return cos, sin_up, sin_dn


def _window_bias():
    c = jnp.arange(BLOCK)[:, None]
    r = jnp.arange(BLOCK)[None, :]
    prev = jnp.where(c >= r, 0.0, MASKED).astype(F32)
    nxt = jnp.where(c <= r, 0.0, MASKED).astype(F32)
    return jnp.tile(jnp.concatenate([prev, nxt], axis=0), (1, ATTN_GROUP))


def kernel(x, c, ctx, c_ctx, w_mod, b_mod, norm_mix, norm_ffn, w_in, q_gain, k_gain, attn_sink,
           gate_gain, w_spatial, b_spatial, attn_out_gain, mlp_out_gain, w_out, w_gate_up, w_down):
    b, s, d = x.shape
    assert w_mod.shape[0] == 1, "single-layer problem"
    assert s % 512 == 0 and d % LANES == 0
    aw = ATTN_HEADS * HEAD_DIM
    mw = MLP_HEADS * HEAD_DIM
    ff = w_down.shape[1]

    rows = -(-(b + 1) // 8) * 8
    cond = jnp.concatenate([c, c_ctx[None, :], jnp.zeros((rows - b - 1, d), F32)], axis=0)
    mod3 = _adaln(cond, w_mod[0], b_mod[0][None, :]).reshape(rows, N_MOD, d)

    w_in_bf = w_in[0].astype(BF16)
    bd = jnp.kron(jnp.eye(MXU_DIM // HEAD_DIM, dtype=F32),
                  jnp.full((HEAD_DIM, HEAD_DIM), 1.0 / HEAD_DIM, F32)).astype(BF16)
    kgain2 = jnp.tile(k_gain[0], ATTN_KV_HEADS)[None, :]
    qgain = (jnp.tile(q_gain[0], ATTN_HEADS) * (HEAD_DIM ** -0.5 * LOG2E))[None, :]
    norm_mix_g = norm_mix[0][None, :]

    kc, vct = _ctx_kv(ctx, mod3, b, norm_mix_g, w_in_bf, kgain2, bd)

    cos, sup, sdn = _rope_tables(s)
    bs_full = jnp.repeat(b_spatial[0].T, HEAD_DIM, axis=1)
    k, vt, qt, mlpn = _inproj(
        x, mod3, norm_mix_g, w_in_bf, cos, sup, sdn, qgain, kgain2,
        gate_gain[0].reshape(1, mw), w_spatial[0].astype(BF16), bs_full,
        mlp_out_gain[0][None, :], bd, tile=512)

    sink_rows = jnp.repeat(attn_sink[0].reshape(ATTN_KV_HEADS, ATTN_GROUP) * LOG2E,
                           BLOCK, axis=1)[:, None, :]
    again_b = jnp.broadcast_to(attn_out_gain[0][:, None], (aw, BLOCK))
    h1 = _attn(x, mod3, k, vt, kc, vct, qt, mlpn, sink_rows, _window_bias(), again_b,
               w_out[0].astype(BF16), tile=512)

    return _ffn(h1, mod3, norm_ffn[0][None, :], w_gate_up[0].astype(BF16),
                w_down[0].astype(BF16), tile=512)
```

```python
import functools
import math

import jax
import jax.numpy as jnp
from jax import lax
from jax.experimental import pallas as pl
from jax.experimental.pallas import tpu as pltpu

F32 = jnp.float32
BF16 = jnp.bfloat16

HEAD_DIM = 64
ATTN_HEADS = 8
ATTN_KV_HEADS = 2
ATTN_GROUP = ATTN_HEADS // ATTN_KV_HEADS
MLP_HEADS = 8
N_MOD = 6
BLOCK = 128
GRID_W = 64
ROPE_THETA = 10000.0
EPS = 1e-6
MASKED = -1e30
LOG2E = math.log2(math.e)

LANES = 128
BF16_SUBLANES = 16
MXU_DIM = 256
VMEM_LIMIT = 56 * 1024 * 1024
HEADS_PER_STEP = 4


def _dot(a, b):
    return jnp.dot(a, b, preferred_element_type=F32)


def _silu(x):
    return x * (1.0 / (1.0 + jnp.exp(-x)))


def _gelu2_tanh(x):
    c = math.sqrt(2.0 / math.pi)
    return x * (1.0 + jnp.tanh(x * (c + (c * 0.044715) * (x * x))))


def _mod_norm(x, gain, shift, scale):
    y = x * lax.rsqrt(jnp.mean(x * x, axis=-1, keepdims=True) + EPS)
    return y * (gain * (1.0 + scale)) + shift


def _head_ms(x, bd):
    return _dot((x * x).astype(BF16), bd)


def _rope(x, cos, sin_up, sin_dn):
    up = pltpu.roll(x, LANES - 16, 1)
    dn = pltpu.roll(x, 16, 1)
    return x * cos + up * sin_up + dn * sin_dn


def _adaln_kernel(cond_ref, w_ref, b_ref, o_ref):
    s = _silu(cond_ref[...])
    o_ref[...] = jnp.dot(s, w_ref[...], preferred_element_type=F32,
                         precision=lax.Precision.HIGHEST) + b_ref[...]


def _adaln(cond, w_mod, b_mod):
    rows, d = cond.shape
    n = w_mod.shape[1]
    tn = n // 4
    return pl.pallas_call(
        _adaln_kernel,
        grid=(n // tn,),
        in_specs=[pl.BlockSpec((rows, d), lambda j: (0, 0)),
                  pl.BlockSpec((d, tn), lambda j: (0, j)),
                  pl.BlockSpec((1, tn), lambda j: (0, j))],
        out_specs=pl.BlockSpec((rows, tn), lambda j: (0, j)),
        out_shape=jax.ShapeDtypeStruct((rows, n), F32),
        compiler_params=pltpu.CompilerParams(
            dimension_semantics=("arbitrary",), vmem_limit_bytes=VMEM_LIMIT),
        name="adaln",
    )(cond, w_mod, b_mod)


def _ctx_kernel(x_ref, mod_ref, gain_ref, w_ref, kgain_ref, bd_ref, kc_ref, vct_ref):
    hn = _mod_norm(x_ref[...], gain_ref[...], mod_ref[0:1, :], mod_ref[1:2, :])
    kv = _dot(hn.astype(BF16), w_ref[...])
    k = kv[:, :LANES]
    k = k * lax.rsqrt(_head_ms(k, bd_ref[:LANES, :LANES]) + EPS) * kgain_ref[...]
    kc_ref[...] = k.astype(BF16)
    vct_ref[...] = kv[:, LANES:].T.astype(BF16)


def _ctx_kv(ctx, mod3, ctx_row, norm_gain, w_in_bf, kgain2, bd):
    b, c, d = ctx.shape
    kvw = 2 * ATTN_KV_HEADS * HEAD_DIM
    return pl.pallas_call(
        _ctx_kernel,
        grid=(b,),
        in_specs=[pl.BlockSpec((None, c, d), lambda i: (i, 0, 0)),
                  pl.BlockSpec((None, N_MOD, d), lambda i: (ctx_row, 0, 0)),
                  pl.BlockSpec((1, d), lambda i: (0, 0)),
                  pl.BlockSpec((d, kvw), lambda i: (0, 0)),
                  pl.BlockSpec((1, LANES), lambda i: (0, 0)),
                  pl.BlockSpec((MXU_DIM, MXU_DIM), lambda i: (0, 0))],
        out_specs=[pl.BlockSpec((None, c, LANES), lambda i: (i, 0, 0)),
                   pl.BlockSpec((None, LANES, c), lambda i: (i, 0, 0))],
        out_shape=[jax.ShapeDtypeStruct((b, c, LANES), BF16),
                   jax.ShapeDtypeStruct((b, LANES, c), BF16)],
        compiler_params=pltpu.CompilerParams(
            dimension_semantics=("arbitrary",), vmem_limit_bytes=VMEM_LIMIT),
        name="ctx_kv",
    )(ctx, mod3, norm_gain, w_in_bf, kgain2, bd)


def _inproj_kernel(x_ref, mod_ref, gain_ref, w_ref, cos_ref, sup_ref, sdn_ref,
                   qgain_ref, kgain_ref, ggain_ref, ws_ref, bs_ref, ogain_ref, bd_ref,
                   k_ref, vt_ref, qt_ref, mlp_ref):
    t = x_ref.shape[0]
    nblk = t // BLOCK
    kvw = ATTN_KV_HEADS * HEAD_DIM
    aw = ATTN_HEADS * HEAD_DIM
    mw = MLP_HEADS * HEAD_DIM
    bd = bd_ref[...]
    cos, sup, sdn = cos_ref[...], sup_ref[...], sdn_ref[...]

    hn = _mod_norm(x_ref[...], gain_ref[...], mod_ref[0:1, :], mod_ref[1:2, :]).astype(BF16)

    c0 = 2 * kvw + aw
    halves = range(aw // MXU_DIM)
    q = _dot(hn, w_ref[:, 2 * kvw:2 * kvw + aw])
    kv = _dot(hn, w_ref[:, 0:2 * kvw])
    q_ms = [_head_ms(q[:, h * MXU_DIM:(h + 1) * MXU_DIM], bd) for h in halves]
    k_ms = _head_ms(kv[:, :kvw], bd[:kvw, :kvw])
    g_raw = _dot(hn, w_ref[:, c0 + mw:c0 + 2 * mw])

    k = kv[:, :kvw] * lax.rsqrt(k_ms + EPS) * kgain_ref[...]
    k_ref[...] = _rope(k, cos, sup, sdn).astype(BF16)
    vt_ref[...] = kv[:, kvw:].T.astype(BF16)

    for half in halves:
        qh = q[:, half * MXU_DIM:(half + 1) * MXU_DIM] * lax.rsqrt(q_ms[half] + EPS)
        qh = qh * qgain_ref[:, half * MXU_DIM:(half + 1) * MXU_DIM]
        for sl in range(MXU_DIM // LANES):
            qs = _rope(qh[:, sl * LANES:(sl + 1) * LANES], cos, sup, sdn)
            qst = qs.T
            for hh in range(LANES // HEAD_DIM):
                head = (half * MXU_DIM + sl * LANES) // HEAD_DIM + hh
                kvh, grp = head // ATTN_GROUP, head % ATTN_GROUP
                for jb in range(nblk):
                    col = (jb * ATTN_GROUP + grp) * BLOCK
                    qt_ref[kvh, :, col:col + BLOCK] = qst[
                        hh * HEAD_DIM:(hh + 1) * HEAD_DIM, jb * BLOCK:(jb + 1) * BLOCK].astype(BF16)

    u_raw = _dot(hn, w_ref[:, c0:c0 + mw])

    g = _gelu2_tanh(g_raw)
    g_ms = [_head_ms(g[:, h * MXU_DIM:(h + 1) * MXU_DIM], bd) for h in halves]
    gn = jnp.concatenate(
        [g[:, h * MXU_DIM:(h + 1) * MXU_DIM] * lax.rsqrt(g_ms[h] + 4.0 * EPS) for h in halves],
        axis=1) * ggain_ref[...]
    gnb = gn.astype(BF16)
    low_head = lax.broadcasted_iota(jnp.int32, (BLOCK, t), 1) % LANES < HEAD_DIM
    mixed_slabs = []
    for p in range(MLP_HEADS // 2):
        rhs = jnp.concatenate(
            [gnb[c * BLOCK:(c + 1) * BLOCK, p * LANES:(p + 1) * LANES] for c in range(nblk)], axis=1)
        a = _dot(ws_ref[2 * p], rhs)
        b = _dot(ws_ref[2 * p + 1], rhs)
        mixed_slabs.append(jnp.where(low_head, a, b))

    u = _gelu2_tanh(u_raw)
    rows = []
    for c in range(nblk):
        mixed_c = jnp.concatenate(
            [m[:, c * LANES:(c + 1) * LANES] for m in mixed_slabs], axis=1) + bs_ref[...]
        rows.append(u[c * BLOCK:(c + 1) * BLOCK, :] * mixed_c)
    o = jnp.concatenate(rows, axis=0)
    o = o * lax.rsqrt(jnp.mean(o * o, axis=-1, keepdims=True) + 4.0 * EPS) * ogain_ref[...]
    mlp_ref[...] = o.astype(BF16)


def _inproj(x, mod3, norm_gain, w_in_bf, cos, sup, sdn, qgain, kgain2, ggain, ws_bf, bs_full,
            ogain, bd, tile):
    b, s, d = x.shape
    inw = w_in_bf.shape[1]
    aw = ATTN_HEADS * HEAD_DIM
    mw = MLP_HEADS * HEAD_DIM
    const = lambda shape: pl.BlockSpec(shape, lambda i, bb: (0,) * len(shape))
    return pl.pallas_call(
        _inproj_kernel,
        grid=(s // tile, b),
        in_specs=[pl.BlockSpec((None, tile, d), lambda i, bb: (bb, i, 0)),
                  pl.BlockSpec((None, N_MOD, d), lambda i, bb: (bb, 0, 0)),
                  const((1, d)),
                  const((d, inw)),
                  pl.BlockSpec((tile, LANES), lambda i, bb: (i, 0)),
                  pl.BlockSpec((tile, LANES), lambda i, bb: (i, 0)),
                  pl.BlockSpec((tile, LANES), lambda i, bb: (i, 0)),
                  const((1, aw)), const((1, LANES)), const((1, mw)),
                  const((MLP_HEADS, BLOCK, BLOCK)), const((BLOCK, mw)), const((1, mw)),
                  const((MXU_DIM, MXU_DIM))],
        out_specs=[pl.BlockSpec((None, tile, LANES), lambda i, bb: (bb, i, 0)),
                   pl.BlockSpec((None, LANES, tile), lambda i, bb: (bb, 0, i)),
                   pl.BlockSpec((None, ATTN_KV_HEADS, HEAD_DIM, ATTN_GROUP * tile),
                                lambda i, bb: (bb, 0, 0, i)),
                   pl.BlockSpec((None, tile, mw), lambda i, bb: (bb, i, 0))],
        out_shape=[jax.ShapeDtypeStruct((b, s, LANES), BF16),
                   jax.ShapeDtypeStruct((b, LANES, s), BF16),
                   jax.ShapeDtypeStruct((b, ATTN_KV_HEADS, HEAD_DIM, ATTN_GROUP * s), BF16),
                   jax.ShapeDtypeStruct((b, s, mw), BF16)],
        compiler_params=pltpu.CompilerParams(
            dimension_semantics=("arbitrary", "arbitrary"), vmem_limit_bytes=VMEM_LIMIT),
        name="inproj",
    )(x, mod3, norm_gain, w_in_bf, cos, sup, sdn, qgain, kgain2, ggain, ws_bf, bs_full, ogain, bd)


def _mix_ffn_kernel(tiles_per_seq,
                    x_ref, mod_ref, kp_ref, km_ref, kn_ref, vp_ref, vm_ref, vn_ref,
                    kc_ref, vct_ref, qt_ref, mlp_ref, sink_ref, bias_ref, again_ref, wo_ref,
                    fmod_ref, fgain_ref, wgu_ref, wd_ref,
                    o_ref, h_ref, hid_ref):
    tq = x_ref.shape[0]
    nblk = tq // BLOCK
    step_id = pl.program_id(0)
    n_tiles = pl.num_programs(0) - 1
    i = lax.rem(jnp.minimum(step_id, n_tiles - 1), tiles_per_seq)
    last = tiles_per_seq - 1
    gq = ATTN_GROUP * BLOCK
    slot_w = lax.rem(step_id, 2)
    slot_r = 1 - slot_w

    @pl.when(step_id == 0)
    def _():
        h_ref[1] = jnp.zeros(h_ref.shape[1:], F32)

    ff = wd_ref.shape[0]
    n_chunks = ff // MXU_DIM
    h_in = h_ref[slot_r]
    hn = _mod_norm(h_in, fgain_ref[...], fmod_ref[3:4, :], fmod_ref[4:5, :]).astype(BF16)

    def ffn_matmuls(c):
        return (_dot(hn, wgu_ref[:, c * MXU_DIM:(c + 1) * MXU_DIM]),
                _dot(hn, wgu_ref[:, ff + c * MXU_DIM:ff + (c + 1) * MXU_DIM]))

    def ffn_activation(c, gate_up):
        a, b = gate_up
        hid_ref[:, c * MXU_DIM:(c + 1) * MXU_DIM] = (_silu(a) * b).astype(BF16)

    k_ext = jnp.concatenate([kp_ref[...], km_ref[...], kn_ref[...]], axis=0)
    vt_ext = jnp.concatenate([vp_ref[...], vm_ref[...], vn_ref[...]], axis=1)
    kc = kc_ref[...]
    vct = vct_ref[...]
    cw = HEADS_PER_STEP * BLOCK
    zeros_q = jnp.zeros((HEAD_DIM, cw), BF16)
    ones_rows = jnp.ones((BF16_SUBLANES, 3 * BLOCK + kc.shape[0]), BF16)
    bias_prev = bias_ref[0:BLOCK, :cw]
    bias_next = bias_ref[BLOCK:2 * BLOCK, :cw]

    def score_matmul(jb, kvh, part):
        qt = qt_ref[kvh, :, jb * gq + part * cw:jb * gq + (part + 1) * cw]
        rhs = jnp.concatenate([qt, zeros_q] if kvh == 0 else [zeros_q, qt], axis=0)
        return _dot(kc, rhs), _dot(k_ext[jb * BLOCK:(jb + 3) * BLOCK, :], rhs)

    def mask_and_max(jb, kvh, part, s):
        pen_prev = jnp.where(i == 0, MASKED, 0.0) if jb == 0 else 0.0
        pen_next = jnp.where(i == last, MASKED, 0.0) if jb == nblk - 1 else 0.0
        s_ctx, s = s
        parts = [s[0:BLOCK] + (bias_prev + pen_prev), s[BLOCK:2 * BLOCK],
                 s[2 * BLOCK:3 * BLOCK] + (bias_next + pen_next), s_ctx]
        m = jnp.maximum(
            jnp.maximum(jnp.max(parts[0], axis=0, keepdims=True), jnp.max(parts[1], axis=0, keepdims=True)),
            jnp.maximum(jnp.max(parts[2], axis=0, keepdims=True), jnp.max(parts[3], axis=0, keepdims=True)))
        return parts, jnp.maximum(m, sink_ref[kvh, :, part * cw:(part + 1) * cw])

    def exp_weights(parts, m):
        return jnp.concatenate([jnp.exp2(p_ - m).astype(BF16) for p_ in parts], axis=0)

    def value_matmul(jb, kvh, part, p):
        v_all = jnp.concatenate(
            [vt_ext[kvh * HEAD_DIM:(kvh + 1) * HEAD_DIM, jb * BLOCK:(jb + 3) * BLOCK],
             vct[kvh * HEAD_DIM:(kvh + 1) * HEAD_DIM, :]], axis=1)
        return _dot(jnp.concatenate([v_all, ones_rows], axis=0), p)

    def normalize(jb, kvh, part, pv, m):
        denom = pv[HEAD_DIM:HEAD_DIM + 1, :] + jnp.exp2(sink_ref[kvh, :, part * cw:(part + 1) * cw] - m)
        o_t = pv[:HEAD_DIM, :] * (1.0 / denom)
        return [o_t[:, g * BLOCK:(g + 1) * BLOCK] for g in range(HEADS_PER_STEP)]

    def block_norm(out_t):
        o_all = jnp.concatenate(out_t, axis=0)
        ms = jnp.mean(o_all * o_all, axis=0, keepdims=True)
        y = o_all * lax.rsqrt(ms + EPS) * again_ref[...]
        return y.T.astype(BF16)

    aw = ATTN_HEADS * HEAD_DIM
    steps = [(jb, kvh, part) for jb in range(nblk) for kvh in range(ATTN_KV_HEADS)
             for part in range(ATTN_GROUP // HEADS_PER_STEP)]
    assert n_chunks >= len(steps)
    raw = {n: score_matmul(*steps[n]) for n in range(2)}
    proj_mlp = _dot(mlp_ref[...], wo_ref[aw:, :])
    ready = {0: mask_and_max(*steps[0], raw.pop(0))}
    attn_rows, out_t, unnormalized, gate_up = [], [], None, {}

    def collect(step, pv, m):
        out_t.extend(normalize(*step, pv, m))
        if len(out_t) == ATTN_HEADS:
            attn_rows.append(block_norm(out_t))
            out_t.clear()

    for n, step in enumerate(steps):
        if n + 1 < len(steps):
            ready[n + 1] = mask_and_max(*steps[n + 1], raw.pop(n + 1))
        if n + 2 < len(steps):
            raw[n + 2] = score_matmul(*steps[n + 2])
        gate_up[n] = ffn_matmuls(n)
        if unnormalized is not None:
            collect(*unnormalized)
            ffn_activation(n - 1, gate_up.pop(n - 1))
        parts, m = ready.pop(n)
        unnormalized = (step, value_matmul(*step, exp_weights(parts, m)), m)

    c_next = len(steps)
    gate_up[c_next] = ffn_matmuls(c_next)
    collect(*unnormalized)
    ffn_activation(c_next - 1, gate_up.pop(c_next - 1))
    for c in range(c_next + 1, n_chunks):
        gate_up[c] = ffn_matmuls(c)
        ffn_activation(c - 1, gate_up.pop(c - 1))
    proj_attn = _dot(jnp.concatenate(attn_rows, axis=0), wo_ref[:aw, :])
    ffn_activation(n_chunks - 1, gate_up.pop(n_chunks - 1))
    h_ref[slot_w] = x_ref[...] + mod_ref[2:3, :] * (proj_attn + proj_mlp)
    o_ref[...] = h_in + fmod_ref[5:6, :] * _dot(hid_ref[...], wd_ref[...])


def _mix_ffn(x, mod3, k, vt, kc, vct, qt, mlpn, sink_rows, bias, again_b, wo_bf,
             ffn_gain, wgu_bf, wd_bf, tile):
    b, s, d = x.shape
    c = kc.shape[1]
    aw = ATTN_HEADS * HEAD_DIM
    mw = mlpn.shape[2]
    ff = wd_bf.shape[0]
    assert ff % MXU_DIM == 0
    r = tile // BLOCK
    nb = s // BLOCK
    nt = s // tile
    n_tiles = b * nt
    gq = ATTN_GROUP * BLOCK

    def mix_tile(g):
        t = jnp.minimum(g, n_tiles - 1)
        return t // nt, t % nt

    def ffn_tile(g):
        t = jnp.maximum(g - 1, 0)
        return t // nt, t % nt

    def at_mix(fn):
        return lambda g: fn(*mix_tile(g))

    const = lambda shape: pl.BlockSpec(shape, lambda g: (0,) * len(shape))
    return pl.pallas_call(
        functools.partial(_mix_ffn_kernel, nt),
        grid=(n_tiles + 1,),
        in_specs=[pl.BlockSpec((None, tile, d), at_mix(lambda bb, i: (bb, i, 0))),
                  pl.BlockSpec((None, N_MOD, d), at_mix(lambda bb, i: (bb, 0, 0))),
                  pl.BlockSpec((None, BLOCK, LANES), at_mix(lambda bb, i: (bb, jnp.maximum(i * r - 1, 0), 0))),
                  pl.BlockSpec((None, tile, LANES), at_mix(lambda bb, i: (bb, i, 0))),
                  pl.BlockSpec((None, BLOCK, LANES),
                               at_mix(lambda bb, i: (bb, jnp.minimum((i + 1) * r, nb - 1), 0))),
                  pl.BlockSpec((None, LANES, BLOCK), at_mix(lambda bb, i: (bb, 0, jnp.maximum(i * r - 1, 0)))),
                  pl.BlockSpec((None, LANES, tile), at_mix(lambda bb, i: (bb, 0, i))),
                  pl.BlockSpec((None, LANES, BLOCK),
                               at_mix(lambda bb, i: (bb, 0, jnp.minimum((i + 1) * r, nb - 1)))),
                  pl.BlockSpec((None, c, LANES), at_mix(lambda bb, i: (bb, 0, 0))),
                  pl.BlockSpec((None, LANES, c), at_mix(lambda bb, i: (bb, 0, 0))),
                  pl.BlockSpec((None, ATTN_KV_HEADS, HEAD_DIM, ATTN_GROUP * tile),
                               at_mix(lambda bb, i: (bb, 0, 0, i))),
                  pl.BlockSpec((None, tile, mw), at_mix(lambda bb, i: (bb, i, 0))),
                  const((ATTN_KV_HEADS, 1, gq)),
                  const((2 * BLOCK, gq)),
                  const((aw, BLOCK)),
                  const((aw + mw, d)),
                  pl.BlockSpec((None, N_MOD, d), lambda g: (ffn_tile(g)[0], 0, 0)),
                  const((1, d)), const((d, 2 * ff)), const((ff, d))],
        out_specs=pl.BlockSpec((None, tile, d), lambda g: (*ffn_tile(g), 0)),
        out_shape=jax.ShapeDtypeStruct((b, s, d), F32),
        scratch_shapes=[pltpu.VMEM((2, tile, d), F32), pltpu.VMEM((tile, ff), BF16)],
        compiler_params=pltpu.CompilerParams(
            dimension_semantics=("arbitrary",), vmem_limit_bytes=VMEM_LIMIT),
        name="mix_ffn",
    )(x, mod3, k, k, k, vt, vt, vt, kc, vct, qt, mlpn, sink_rows, bias, again_b, wo_bf,
      mod3, ffn_gain, wgu_bf, wd_bf)


def _rope_tables(s):
    axis_dim = HEAD_DIM // 2
    pos = jnp.arange(s)
    inv_freq = ROPE_THETA ** (-jnp.arange(0, axis_dim, 2, dtype=F32) / axis_dim)
    ang_r = (pos // GRID_W).astype(F32)[:, None] * inv_freq[None, :]
    ang_c = (pos % GRID_W).astype(F32)[:, None] * inv_freq[None, :]
    cr, sr, cc, sc = jnp.cos(ang_r), jnp.sin(ang_r), jnp.cos(ang_c), jnp.sin(ang_c)
    z = jnp.zeros_like(sr)
    reps = LANES // HEAD_DIM
    cos = jnp.tile(jnp.concatenate([cr, cr, cc, cc], axis=1), (1, reps))
    sin_up = jnp.tile(jnp.concatenate([-sr, z, -sc, z], axis=1), (1, reps))
    sin_dn = jnp.tile(jnp.concatenate([z, sr, z, sc], axis=1), (1, reps))
    return cos, sin_up, sin_dn


def _window_bias():
    c = jnp.arange(BLOCK)[:, None]
    r = jnp.arange(BLOCK)[None, :]
    prev = jnp.where(c >= r, 0.0, MASKED).astype(F32)
    nxt = jnp.where(c <= r, 0.0, MASKED).astype(F32)
    return jnp.tile(jnp.concatenate([prev, nxt], axis=0), (1, ATTN_GROUP))


def kernel(x, c, ctx, c_ctx, w_mod, b_mod, norm_mix, norm_ffn, w_in, q_gain, k_gain, attn_sink,
           gate_gain, w_spatial, b_spatial, attn_out_gain, mlp_out_gain, w_out, w_gate_up, w_down):
    b, s, d = x.shape
    assert w_mod.shape[0] == 1, "single-layer problem"
    assert s % 512 == 0 and d % LANES == 0
    aw = ATTN_HEADS * HEAD_DIM
    mw = MLP_HEADS * HEAD_DIM
    ff = w_down.shape[1]

    rows = -(-(b + 1) // 8) * 8
    cond = jnp.concatenate([c, c_ctx[None, :], jnp.zeros((rows - b - 1, d), F32)], axis=0)
    mod3 = _adaln(cond, w_mod[0], b_mod[0][None, :]).reshape(rows, N_MOD, d)

    w_in_bf = w_in[0].astype(BF16)
    bd = jnp.kron(jnp.eye(MXU_DIM // HEAD_DIM, dtype=F32),
                  jnp.full((HEAD_DIM, HEAD_DIM), 1.0 / HEAD_DIM, F32)).astype(BF16)
    kgain2 = jnp.tile(k_gain[0], ATTN_KV_HEADS)[None, :]
    qgain = (jnp.tile(q_gain[0], ATTN_HEADS) * (HEAD_DIM ** -0.5 * LOG2E))[None, :]
    norm_mix_g = norm_mix[0][None, :]

    kc, vct = _ctx_kv(ctx, mod3, b, norm_mix_g, w_in_bf, kgain2, bd)

    cos, sup, sdn = _rope_tables(s)
    bs_full = jnp.repeat(b_spatial[0].T, HEAD_DIM, axis=1)
    k, vt, qt, mlpn = _inproj(
        x, mod3, norm_mix_g, w_in_bf, cos, sup, sdn, qgain, kgain2,
        gate_gain[0].reshape(1, mw), w_spatial[0].astype(BF16), bs_full,
        mlp_out_gain[0][None, :], bd, tile=512)

    sink_rows = jnp.repeat(attn_sink[0].reshape(ATTN_KV_HEADS, ATTN_GROUP) * LOG2E,
                           BLOCK, axis=1)[:, None, :]
    again_b = jnp.broadcast_to(attn_out_gain[0][:, None], (aw, BLOCK))
    return _mix_ffn(x, mod3, k, vt, kc, vct, qt, mlpn, sink_rows, _window_bias(), again_b,
                    w_out[0].astype(BF16), norm_ffn[0][None, :], w_gate_up[0].astype(BF16),
                    w_down[0].astype(BF16), tile=512)
```

```python
import functools
import math

import jax
import jax.numpy as jnp
from jax import lax
from jax.experimental import pallas as pl
from jax.experimental.pallas import tpu as pltpu

F32 = jnp.float32
BF16 = jnp.bfloat16

HEAD_DIM = 64
ATTN_HEADS = 8
ATTN_KV_HEADS = 2
ATTN_GROUP = ATTN_HEADS // ATTN_KV_HEADS
MLP_HEADS = 8
N_MOD = 6
BLOCK = 128
GRID_W = 64
ROPE_THETA = 10000.0
EPS = 1e-6
MASKED = -1e30
LOG2E = math.log2(math.e)

LANES = 128
BF16_SUBLANES = 16
MXU_DIM = 256
VMEM_LIMIT = 56 * 1024 * 1024
HEADS_PER_STEP = 4


def _dot(a, b):
    return jnp.dot(a, b, preferred_element_type=F32)


def _silu(x):
    return x * (1.0 / (1.0 + jnp.exp(-x)))


def _gelu2_tanh(x):
    c = math.sqrt(2.0 / math.pi)
    return x * (1.0 + jnp.tanh(x * (c + (c * 0.044715) * (x * x))))


def _mod_norm(x, gain, shift, scale):
    y = x * lax.rsqrt(jnp.mean(x * x, axis=-1, keepdims=True) + EPS)
    return y * (gain * (1.0 + scale)) + shift


def _head_ms(x, bd):
    return _dot((x * x).astype(BF16), bd)


def _rope(x, cos, sin_up, sin_dn):
    up = pltpu.roll(x, LANES - 16, 1)
    dn = pltpu.roll(x, 16, 1)
    return x * cos + up * sin_up + dn * sin_dn


def _split_bf16(x):
    hi = x.astype(BF16)
    return hi, (x - hi.astype(F32)).astype(BF16)


def _adaln_kernel(cond_ref, w_ref, b_ref, o_ref):
    rows = cond_ref.shape[0]
    s_hi, s_lo = _split_bf16(_silu(cond_ref[...]))
    w_hi, w_lo = _split_bf16(w_ref[...])
    both = _dot(jnp.concatenate([s_hi, s_lo], axis=0), w_hi)
    o_ref[...] = both[:rows] + both[rows:] + _dot(s_hi, w_lo) + b_ref[...]


def _adaln(cond, w_mod, b_mod):
    rows, d = cond.shape
    n = w_mod.shape[1]
    tn = n // 12
    assert rows % BF16_SUBLANES == 0 and tn % LANES == 0
    return pl.pallas_call(
        _adaln_kernel,
        grid=(n // tn,),
        in_specs=[pl.BlockSpec((rows, d), lambda j: (0, 0)),
                  pl.BlockSpec((d, tn), lambda j: (0, j)),
                  pl.BlockSpec((1, tn), lambda j: (0, j))],
        out_specs=pl.BlockSpec((rows, tn), lambda j: (0, j)),
        out_shape=jax.ShapeDtypeStruct((rows, n), F32),
        compiler_params=pltpu.CompilerParams(
            dimension_semantics=("arbitrary",), vmem_limit_bytes=VMEM_LIMIT),
        name="adaln",
    )(cond, w_mod, b_mod)


def _ctx_kernel(x_ref, mod_ref, gain_ref, w_ref, kgain_ref, bd_ref, kc_ref, vct_ref):
    hn = _mod_norm(x_ref[...], gain_ref[...], mod_ref[0:1, :], mod_ref[1:2, :])
    kv = _dot(hn.astype(BF16), w_ref[...])
    k = kv[:, :LANES]
    k = k * lax.rsqrt(_head_ms(k, bd_ref[:LANES, :LANES]) + EPS) * kgain_ref[...]
    kc_ref[...] = k.astype(BF16)
    vct_ref[...] = kv[:, LANES:].T.astype(BF16)


def _ctx_kv(ctx, mod3, ctx_row, norm_gain, w_in_bf, kgain2, bd):
    b, c, d = ctx.shape
    kvw = 2 * ATTN_KV_HEADS * HEAD_DIM
    return pl.pallas_call(
        _ctx_kernel,
        grid=(b,),
        in_specs=[pl.BlockSpec((None, c, d), lambda i: (i, 0, 0)),
                  pl.BlockSpec((None, N_MOD, d), lambda i: (ctx_row, 0, 0)),
                  pl.BlockSpec((1, d), lambda i: (0, 0)),
                  pl.BlockSpec((d, kvw), lambda i: (0, 0)),
                  pl.BlockSpec((1, LANES), lambda i: (0, 0)),
                  pl.BlockSpec((MXU_DIM, MXU_DIM), lambda i: (0, 0))],
        out_specs=[pl.BlockSpec((None, c, LANES), lambda i: (i, 0, 0)),
                   pl.BlockSpec((None, LANES, c), lambda i: (i, 0, 0))],
        out_shape=[jax.ShapeDtypeStruct((b, c, LANES), BF16),
                   jax.ShapeDtypeStruct((b, LANES, c), BF16)],
        compiler_params=pltpu.CompilerParams(
            dimension_semantics=("arbitrary",), vmem_limit_bytes=VMEM_LIMIT),
        name="ctx_kv",
    )(ctx, mod3, norm_gain, w_in_bf, kgain2, bd)


def _inproj_kernel(x_ref, mod_ref, gain_ref, w_ref, cos_ref, sup_ref, sdn_ref,
                   qgain_ref, kgain_ref, ggain_ref, ws_ref, bs_ref, ogain_ref, bd_ref,
                   wo_f32_ref, wgu_f32_ref, wd_f32_ref,
                   k_ref, vt_ref, qt_ref, mlp_ref, wo_bf_ref, wgu_bf_ref, wd_bf_ref):
    wo_bf_ref[...] = wo_f32_ref[...].astype(BF16)
    wgu_bf_ref[...] = wgu_f32_ref[...].astype(BF16)
    wd_bf_ref[...] = wd_f32_ref[...].astype(BF16)

    t = x_ref.shape[0]
    nblk = t // BLOCK
    kvw = ATTN_KV_HEADS * HEAD_DIM
    aw = ATTN_HEADS * HEAD_DIM
    mw = MLP_HEADS * HEAD_DIM
    bd = bd_ref[...]
    cos, sup, sdn = cos_ref[...], sup_ref[...], sdn_ref[...]

    hn = _mod_norm(x_ref[...], gain_ref[...], mod_ref[0:1, :], mod_ref[1:2, :]).astype(BF16)

    c0 = 2 * kvw + aw
    halves = range(aw // MXU_DIM)
    q = _dot(hn, w_ref[:, 2 * kvw:2 * kvw + aw])
    kv = _dot(hn, w_ref[:, 0:2 * kvw])
    q_ms = [_head_ms(q[:, h * MXU_DIM:(h + 1) * MXU_DIM], bd) for h in halves]
    k_ms = _head_ms(kv[:, :kvw], bd[:kvw, :kvw])
    g_raw = _dot(hn, w_ref[:, c0 + mw:c0 + 2 * mw])

    k = kv[:, :kvw] * lax.rsqrt(k_ms + EPS) * kgain_ref[...]
    k_ref[...] = _rope(k, cos, sup, sdn).astype(BF16)
    vt_ref[...] = kv[:, kvw:].T.astype(BF16)

    for half in halves:
        qh = q[:, half * MXU_DIM:(half + 1) * MXU_DIM] * lax.rsqrt(q_ms[half] + EPS)
        qh = qh * qgain_ref[:, half * MXU_DIM:(half + 1) * MXU_DIM]
        for sl in range(MXU_DIM // LANES):
            qs = _rope(qh[:, sl * LANES:(sl + 1) * LANES], cos, sup, sdn)
            qst = qs.T
            for hh in range(LANES // HEAD_DIM):
                head = (half * MXU_DIM + sl * LANES) // HEAD_DIM + hh
                kvh, grp = head // ATTN_GROUP, head % ATTN_GROUP
                for jb in range(nblk):
                    col = (jb * ATTN_GROUP + grp) * BLOCK
                    qt_ref[kvh, :, col:col + BLOCK] = qst[
                        hh * HEAD_DIM:(hh + 1) * HEAD_DIM, jb * BLOCK:(jb + 1) * BLOCK].astype(BF16)

    u_raw = _dot(hn, w_ref[:, c0:c0 + mw])

    g = _gelu2_tanh(g_raw)
    g_ms = [_head_ms(g[:, h * MXU_DIM:(h + 1) * MXU_DIM], bd) for h in halves]
    gn = jnp.concatenate(
        [g[:, h * MXU_DIM:(h + 1) * MXU_DIM] * lax.rsqrt(g_ms[h] + 4.0 * EPS) for h in halves],
        axis=1) * ggain_ref[...]
    gnb = gn.astype(BF16)
    low_head = lax.broadcasted_iota(jnp.int32, (BLOCK, t), 1) % LANES < HEAD_DIM
    mixed_slabs = []
    for p in range(MLP_HEADS // 2):
        rhs = jnp.concatenate(
            [gnb[c * BLOCK:(c + 1) * BLOCK, p * LANES:(p + 1) * LANES] for c in range(nblk)], axis=1)
        a = _dot(ws_ref[2 * p], rhs)
        b = _dot(ws_ref[2 * p + 1], rhs)
        mixed_slabs.append(jnp.where(low_head, a, b))

    u = _gelu2_tanh(u_raw)
    rows = []
    for c in range(nblk):
        mixed_c = jnp.concatenate(
            [m[:, c * LANES:(c + 1) * LANES] for m in mixed_slabs], axis=1) + bs_ref[...]
        rows.append(u[c * BLOCK:(c + 1) * BLOCK, :] * mixed_c)
    o = jnp.concatenate(rows, axis=0)
    o = o * lax.rsqrt(jnp.mean(o * o, axis=-1, keepdims=True) + 4.0 * EPS) * ogain_ref[...]
    mlp_ref[...] = o.astype(BF16)


def _inproj(x, mod3, norm_gain, w_in_bf, cos, sup, sdn, qgain, kgain2, ggain, ws_bf, bs_full,
            ogain, bd, later_weights, tile):
    b, s, d = x.shape
    inw = w_in_bf.shape[1]
    aw = ATTN_HEADS * HEAD_DIM
    mw = MLP_HEADS * HEAD_DIM
    const = lambda shape: pl.BlockSpec(shape, lambda i, bb: (0,) * len(shape))

    n_steps = (s // tile) * b

    def slab_spec(w):
        rows = next(r for r in range(BF16_SUBLANES, w.shape[0] + 1, BF16_SUBLANES)
                    if w.shape[0] % r == 0 and w.shape[0] // r <= n_steps)
        n_slabs = w.shape[0] // rows
        return pl.BlockSpec((rows, w.shape[1]), lambda i, bb: (jnp.minimum(i * b + bb, n_slabs - 1), 0))

    slab_specs = [slab_spec(w) for w in later_weights]
    return pl.pallas_call(
        _inproj_kernel,
        grid=(s // tile, b),
        in_specs=[pl.BlockSpec((None, tile, d), lambda i, bb: (bb, i, 0)),
                  pl.BlockSpec((None, N_MOD, d), lambda i, bb: (bb, 0, 0)),
                  const((1, d)),
                  const((d, inw)),
                  pl.BlockSpec((tile, LANES), lambda i, bb: (i, 0)),
                  pl.BlockSpec((tile, LANES), lambda i, bb: (i, 0)),
                  pl.BlockSpec((tile, LANES), lambda i, bb: (i, 0)),
                  const((1, aw)), const((1, LANES)), const((1, mw)),
                  const((MLP_HEADS, BLOCK, BLOCK)), const((BLOCK, mw)), const((1, mw)),
                  const((MXU_DIM, MXU_DIM))] + slab_specs,
        out_specs=[pl.BlockSpec((None, tile, LANES), lambda i, bb: (bb, i, 0)),
                   pl.BlockSpec((None, LANES, tile), lambda i, bb: (bb, 0, i)),
                   pl.BlockSpec((None, ATTN_KV_HEADS, HEAD_DIM, ATTN_GROUP * tile),
                                lambda i, bb: (bb, 0, 0, i)),
                   pl.BlockSpec((None, tile, mw), lambda i, bb: (bb, i, 0))] + slab_specs,
        out_shape=[jax.ShapeDtypeStruct((b, s, LANES), BF16),
                   jax.ShapeDtypeStruct((b, LANES, s), BF16),
                   jax.ShapeDtypeStruct((b, ATTN_KV_HEADS, HEAD_DIM, ATTN_GROUP * s), BF16),
                   jax.ShapeDtypeStruct((b, s, mw), BF16)]
        + [jax.ShapeDtypeStruct(w.shape, BF16) for w in later_weights],
        compiler_params=pltpu.CompilerParams(
            dimension_semantics=("arbitrary", "arbitrary"), vmem_limit_bytes=VMEM_LIMIT),
        name="inproj",
    )(x, mod3, norm_gain, w_in_bf, cos, sup, sdn, qgain, kgain2, ggain, ws_bf, bs_full, ogain, bd,
      *later_weights)


def _mix_ffn_kernel(tiles_per_seq,
                    x_ref, mod_ref, kp_ref, km_ref, kn_ref, vp_ref, vm_ref, vn_ref,
                    kc_ref, vct_ref, qt_ref, mlp_ref, sink_ref, bias_ref, again_ref, wo_ref,
                    fmod_ref, fgain_ref, wgu_ref, wd_ref,
                    o_ref, h_ref, hid_ref):
    tq = x_ref.shape[0]
    nblk = tq // BLOCK
    step_id = pl.program_id(0)
    n_tiles = pl.num_programs(0) - 1
    i = lax.rem(jnp.minimum(step_id, n_tiles - 1), tiles_per_seq)
    last = tiles_per_seq - 1
    gq = ATTN_GROUP * BLOCK
    slot_w = lax.rem(step_id, 2)
    slot_r = 1 - slot_w

    @pl.when(step_id == 0)
    def _():
        h_ref[1] = jnp.zeros(h_ref.shape[1:], F32)

    ff = wd_ref.shape[0]
    n_chunks = ff // MXU_DIM
    h_in = h_ref[slot_r]
    hn = _mod_norm(h_in, fgain_ref[...], fmod_ref[3:4, :], fmod_ref[4:5, :]).astype(BF16)

    def ffn_matmuls(c):
        return (_dot(hn, wgu_ref[:, c * MXU_DIM:(c + 1) * MXU_DIM]),
                _dot(hn, wgu_ref[:, ff + c * MXU_DIM:ff + (c + 1) * MXU_DIM]))

    def ffn_activation(c, gate_up):
        a, b = gate_up
        hid_ref[:, c * MXU_DIM:(c + 1) * MXU_DIM] = (_silu(a) * b).astype(BF16)

    k_ext = jnp.concatenate([kp_ref[...], km_ref[...], kn_ref[...]], axis=0)
    vt_ext = jnp.concatenate([vp_ref[...], vm_ref[...], vn_ref[...]], axis=1)
    kc = kc_ref[...]
    vct = vct_ref[...]
    cw = HEADS_PER_STEP * BLOCK
    zeros_q = jnp.zeros((HEAD_DIM, cw), BF16)
    ones_rows = jnp.ones((BF16_SUBLANES, 3 * BLOCK + kc.shape[0]), BF16)
    bias_prev = bias_ref[0:BLOCK, :cw]
    bias_next = bias_ref[BLOCK:2 * BLOCK, :cw]

    def score_matmul(jb, kvh, part):
        qt = qt_ref[kvh, :, jb * gq + part * cw:jb * gq + (part + 1) * cw]
        rhs = jnp.concatenate([qt, zeros_q] if kvh == 0 else [zeros_q, qt], axis=0)
        return _dot(kc, rhs), _dot(k_ext[jb * BLOCK:(jb + 3) * BLOCK, :], rhs)

    def mask_and_max(jb, kvh, part, s):
        pen_prev = jnp.where(i == 0, MASKED, 0.0) if jb == 0 else 0.0
        pen_next = jnp.where(i == last, MASKED, 0.0) if jb == nblk - 1 else 0.0
        s_ctx, s = s
        parts = [s[0:BLOCK] + (bias_prev + pen_prev), s[BLOCK:2 * BLOCK],
                 s[2 * BLOCK:3 * BLOCK] + (bias_next + pen_next), s_ctx]
        m = jnp.maximum(
            jnp.maximum(jnp.max(parts[0], axis=0, keepdims=True), jnp.max(parts[1], axis=0, keepdims=True)),
            jnp.maximum(jnp.max(parts[2], axis=0, keepdims=True), jnp.max(parts[3], axis=0, keepdims=True)))
        return parts, jnp.maximum(m, sink_ref[kvh, :, part * cw:(part + 1) * cw])

    def exp_weights(parts, m):
        return jnp.concatenate([jnp.exp2(p_ - m).astype(BF16) for p_ in parts], axis=0)

    def value_matmul(jb, kvh, part, p):
        v_all = jnp.concatenate(
            [vt_ext[kvh * HEAD_DIM:(kvh + 1) * HEAD_DIM, jb * BLOCK:(jb + 3) * BLOCK],
             vct[kvh * HEAD_DIM:(kvh + 1) * HEAD_DIM, :]], axis=1)
        return _dot(jnp.concatenate([v_all, ones_rows], axis=0), p)

    def normalize(jb, kvh, part, pv, m):
        denom = pv[HEAD_DIM:HEAD_DIM + 1, :] + jnp.exp2(sink_ref[kvh, :, part * cw:(part + 1) * cw] - m)
        o_t = pv[:HEAD_DIM, :] * (1.0 / denom)
        return [o_t[:, g * BLOCK:(g + 1) * BLOCK] for g in range(HEADS_PER_STEP)]

    def block_norm(out_t):
        o_all = jnp.concatenate(out_t, axis=0)
        ms = jnp.mean(o_all * o_all, axis=0, keepdims=True)
        y = o_all * lax.rsqrt(ms + EPS) * again_ref[...]
        return y.T.astype(BF16)

    aw = ATTN_HEADS * HEAD_DIM
    steps = [(jb, kvh, part) for jb in range(nblk) for kvh in range(ATTN_KV_HEADS)
             for part in range(ATTN_GROUP // HEADS_PER_STEP)]
    assert n_chunks >= len(steps)
    raw = {n: score_matmul(*steps[n]) for n in range(2)}
    proj_mlp = _dot(mlp_ref[...], wo_ref[aw:, :])
    ready = {0: mask_and_max(*steps[0], raw.pop(0))}
    attn_rows, out_t, unnormalized, gate_up = [], [], None, {}

    def collect(step, pv, m):
        out_t.extend(normalize(*step, pv, m))
        if len(out_t) == ATTN_HEADS:
            attn_rows.append(block_norm(out_t))
            out_t.clear()

    for n, step in enumerate(steps):
        if n + 1 < len(steps):
            ready[n + 1] = mask_and_max(*steps[n + 1], raw.pop(n + 1))
        if n + 2 < len(steps):
            raw[n + 2] = score_matmul(*steps[n + 2])
        gate_up[n] = ffn_matmuls(n)
        if unnormalized is not None:
            collect(*unnormalized)
            ffn_activation(n - 1, gate_up.pop(n - 1))
        parts, m = ready.pop(n)
        unnormalized = (step, value_matmul(*step, exp_weights(parts, m)), m)

    c_next = len(steps)
    gate_up[c_next] = ffn_matmuls(c_next)
    collect(*unnormalized)
    ffn_activation(c_next - 1, gate_up.pop(c_next - 1))
    for c in range(c_next + 1, n_chunks):
        gate_up[c] = ffn_matmuls(c)
        ffn_activation(c - 1, gate_up.pop(c - 1))
    proj_attn = _dot(jnp.concatenate(attn_rows, axis=0), wo_ref[:aw, :])
    ffn_activation(n_chunks - 1, gate_up.pop(n_chunks - 1))
    h_ref[slot_w] = x_ref[...] + mod_ref[2:3, :] * (proj_attn + proj_mlp)
    o_ref[...] = h_in + fmod_ref[5:6, :] * _dot(hid_ref[...], wd_ref[...])


def _mix_ffn(x, mod3, k, vt, kc, vct, qt, mlpn, sink_rows, bias, again_b, wo_bf,
             ffn_gain, wgu_bf, wd_bf, tile):
    b, s, d = x.shape
    c = kc.shape[1]
    aw = ATTN_HEADS * HEAD_DIM
    mw = mlpn.shape[2]
    ff = wd_bf.shape[0]
    assert ff % MXU_DIM == 0
    r = tile // BLOCK
    nb = s // BLOCK
    nt = s // tile
    n_tiles = b * nt
    gq = ATTN_GROUP * BLOCK

    def mix_tile(g):
        t = jnp.minimum(g, n_tiles - 1)
        return t // nt, t % nt

    def ffn_tile(g):
        t = jnp.maximum(g - 1, 0)
        return t // nt, t % nt

    def at_mix(fn):
        return lambda g: fn(*mix_tile(g))

    const = lambda shape: pl.BlockSpec(shape, lambda g: (0,) * len(shape))
    return pl.pallas_call(
        functools.partial(_mix_ffn_kernel, nt),
        grid=(n_tiles + 1,),
        in_specs=[pl.BlockSpec((None, tile, d), at_mix(lambda bb, i: (bb, i, 0))),
                  pl.BlockSpec((None, N_MOD, d), at_mix(lambda bb, i: (bb, 0, 0))),
                  pl.BlockSpec((None, BLOCK, LANES), at_mix(lambda bb, i: (bb, jnp.maximum(i * r - 1, 0), 0))),
                  pl.BlockSpec((None, tile, LANES), at_mix(lambda bb, i: (bb, i, 0))),
                  pl.BlockSpec((None, BLOCK, LANES),
                               at_mix(lambda bb, i: (bb, jnp.minimum((i + 1) * r, nb - 1), 0))),
                  pl.BlockSpec((None, LANES, BLOCK), at_mix(lambda bb, i: (bb, 0, jnp.maximum(i * r - 1, 0)))),
                  pl.BlockSpec((None, LANES, tile), at_mix(lambda bb, i: (bb, 0, i))),
                  pl.BlockSpec((None, LANES, BLOCK),
                               at_mix(lambda bb, i: (bb, 0, jnp.minimum((i + 1) * r, nb - 1)))),
                  pl.BlockSpec((None, c, LANES), at_mix(lambda bb, i: (bb, 0, 0))),
                  pl.BlockSpec((None, LANES, c), at_mix(lambda bb, i: (bb, 0, 0))),
                  pl.BlockSpec((None, ATTN_KV_HEADS, HEAD_DIM, ATTN_GROUP * tile),
                               at_mix(lambda bb, i: (bb, 0, 0, i))),
                  pl.BlockSpec((None, tile, mw), at_mix(lambda bb, i: (bb, i, 0))),
                  const((ATTN_KV_HEADS, 1, gq)),
                  const((2 * BLOCK, gq)),
                  const((aw, BLOCK)),
                  const((aw + mw, d)),
                  pl.BlockSpec((None, N_MOD, d), lambda g: (ffn_tile(g)[0], 0, 0)),
                  const((1, d)), const((d, 2 * ff)), const((ff, d))],
        out_specs=pl.BlockSpec((None, tile, d), lambda g: (*ffn_tile(g), 0)),
        out_shape=jax.ShapeDtypeStruct((b, s, d), F32),
        scratch_shapes=[pltpu.VMEM((2, tile, d), F32), pltpu.VMEM((tile, ff), BF16)],
        compiler_params=pltpu.CompilerParams(
            dimension_semantics=("arbitrary",), vmem_limit_bytes=VMEM_LIMIT),
        name="mix_ffn",
    )(x, mod3, k, k, k, vt, vt, vt, kc, vct, qt, mlpn, sink_rows, bias, again_b, wo_bf,
      mod3, ffn_gain, wgu_bf, wd_bf)


def _rope_tables(s):
    axis_dim = HEAD_DIM // 2
    pos = jnp.arange(s)
    inv_freq = ROPE_THETA ** (-jnp.arange(0, axis_dim, 2, dtype=F32) / axis_dim)
    ang_r = (pos // GRID_W).astype(F32)[:, None] * inv_freq[None, :]
    ang_c = (pos % GRID_W).astype(F32)[:, None] * inv_freq[None, :]
    cr, sr, cc, sc = jnp.cos(ang_r), jnp.sin(ang_r), jnp.cos(ang_c), jnp.sin(ang_c)
    z = jnp.zeros_like(sr)
    reps = LANES // HEAD_DIM
    cos = jnp.tile(jnp.concatenate([cr, cr, cc, cc], axis=1), (1, reps))
    sin_up = jnp.tile(jnp.concatenate([-sr, z, -sc, z], axis=1), (1, reps))
    sin_dn = jnp.tile(jnp.concatenate([z, sr, z, sc], axis=1), (1, reps))
    return cos, sin_up, sin_dn


def _window_bias():
    c = jnp.arange(BLOCK)[:, None]
    r = jnp.arange(BLOCK)[None, :]
    prev = jnp.where(c >= r, 0.0, MASKED).astype(F32)
    nxt = jnp.where(c <= r, 0.0, MASKED).astype(F32)
    return jnp.tile(jnp.concatenate([prev, nxt], axis=0), (1, ATTN_GROUP))


def kernel(x, c, ctx, c_ctx, w_mod, b_mod, norm_mix, norm_ffn, w_in, q_gain, k_gain, attn_sink,
           gate_gain, w_spatial, b_spatial, attn_out_gain, mlp_out_gain, w_out, w_gate_up, w_down):
    b, s, d = x.shape
    assert w_mod.shape[0] == 1, "single-layer problem"
    assert s % 512 == 0 and d % LANES == 0
    aw = ATTN_HEADS * HEAD_DIM
    mw = MLP_HEADS * HEAD_DIM
    ff = w_down.shape[1]

    rows = -(-(b + 1) // BF16_SUBLANES) * BF16_SUBLANES
    cond = jnp.concatenate([c, c_ctx[None, :], jnp.zeros((rows - b - 1, d), F32)], axis=0)
    mod3 = _adaln(cond, w_mod[0], b_mod[0][None, :]).reshape(rows, N_MOD, d)

    w_in_bf = w_in[0].astype(BF16)
    bd = jnp.kron(jnp.eye(MXU_DIM // HEAD_DIM, dtype=F32),
                  jnp.full((HEAD_DIM, HEAD_DIM), 1.0 / HEAD_DIM, F32)).astype(BF16)
    kgain2 = jnp.tile(k_gain[0], ATTN_KV_HEADS)[None, :]
    qgain = (jnp.tile(q_gain[0], ATTN_HEADS) * (HEAD_DIM ** -0.5 * LOG2E))[None, :]
    norm_mix_g = norm_mix[0][None, :]

    kc, vct = _ctx_kv(ctx, mod3, b, norm_mix_g, w_in_bf, kgain2, bd)

    cos, sup, sdn = _rope_tables(s)
    bs_full = jnp.repeat(b_spatial[0].T, HEAD_DIM, axis=1)
    k, vt, qt, mlpn, wo_bf, wgu_bf, wd_bf = _inproj(
        x, mod3, norm_mix_g, w_in_bf, cos, sup, sdn, qgain, kgain2,
        gate_gain[0].reshape(1, mw), w_spatial[0].astype(BF16), bs_full,
        mlp_out_gain[0][None, :], bd, (w_out[0], w_gate_up[0], w_down[0]), tile=512)

    sink_rows = jnp.repeat(attn_sink[0].reshape(ATTN_KV_HEADS, ATTN_GROUP) * LOG2E,
                           BLOCK, axis=1)[:, None, :]
    again_b = jnp.broadcast_to(attn_out_gain[0][:, None], (aw, BLOCK))
    return _mix_ffn(x, mod3, k, vt, kc, vct, qt, mlpn, sink_rows, _window_bias(), again_b,
                    wo_bf, norm_ffn[0][None, :], wgu_bf, wd_bf, tile=512)
```

```python
import functools
import math

import jax
import jax.numpy as jnp
from jax import lax
from jax.experimental import pallas as pl
from jax.experimental.pallas import tpu as pltpu

F32 = jnp.float32
BF16 = jnp.bfloat16

HEAD_DIM = 64
ATTN_HEADS = 8
ATTN_KV_HEADS = 2
ATTN_GROUP = ATTN_HEADS // ATTN_KV_HEADS
MLP_HEADS = 8
N_MOD = 6
BLOCK = 128
GRID_W = 64
ROPE_THETA = 10000.0
EPS = 1e-6
MASKED = -1e30
LOG2E = math.log2(math.e)

LANES = 128
BF16_SUBLANES = 16
MXU_DIM = 256
VMEM_LIMIT = 56 * 1024 * 1024
HEADS_PER_STEP = 4


def _dot(a, b):
    return jnp.dot(a, b, preferred_element_type=F32)


def _silu(x):
    return x * (1.0 / (1.0 + jnp.exp(-x)))


def _gelu2_tanh(x):
    c = math.sqrt(2.0 / math.pi)
    return x * (1.0 + jnp.tanh(x * (c + (c * 0.044715) * (x * x))))


def _mod_norm(x, gain, shift, scale):
    y = x * lax.rsqrt(jnp.mean(x * x, axis=-1, keepdims=True) + EPS)
    return y * (gain * (1.0 + scale)) + shift


def _head_ms(x, bd):
    return _dot((x * x).astype(BF16), bd)


def _rope(x, cos, sin_up, sin_dn):
    up = pltpu.roll(x, LANES - 16, 1)
    dn = pltpu.roll(x, 16, 1)
    return x * cos + up * sin_up + dn * sin_dn


def _split_bf16(x):
    hi = x.astype(BF16)
    return hi, (x - hi.astype(F32)).astype(BF16)


def _adaln_kernel(cond_ref, w_ref, b_ref, o_ref):
    rows = cond_ref.shape[0]

    @pl.when(pl.program_id(0) == 0)
    def _():
        o_ref[...] = jnp.broadcast_to(b_ref[...], o_ref.shape)

    s_hi, s_lo = _split_bf16(_silu(cond_ref[...]))
    w_hi, w_lo = _split_bf16(w_ref[...])
    both = _dot(jnp.concatenate([s_hi, s_lo], axis=0), w_hi)
    o_ref[...] += both[:rows] + both[rows:] + _dot(s_hi, w_lo)


def _adaln(cond, w_mod, b_mod):
    rows, d = cond.shape
    n = w_mod.shape[1]
    tk = d // 8
    assert rows % BF16_SUBLANES == 0 and tk % LANES == 0
    return pl.pallas_call(
        _adaln_kernel,
        grid=(d // tk,),
        in_specs=[pl.BlockSpec((rows, tk), lambda j: (0, j)),
                  pl.BlockSpec((tk, n), lambda j: (j, 0)),
                  pl.BlockSpec((1, n), lambda j: (0, 0))],
        out_specs=pl.BlockSpec((rows, n), lambda j: (0, 0)),
        out_shape=jax.ShapeDtypeStruct((rows, n), F32),
        compiler_params=pltpu.CompilerParams(
            dimension_semantics=("arbitrary",), vmem_limit_bytes=VMEM_LIMIT),
        name="adaln",
    )(cond, w_mod, b_mod)


def _ctx_kernel(x_ref, mod_ref, gain_ref, w_ref, kgain_ref, bd_ref, kc_ref, vct_ref):
    hn = _mod_norm(x_ref[...], gain_ref[...], mod_ref[0:1, :], mod_ref[1:2, :])
    kv = _dot(hn.astype(BF16), w_ref[...])
    k = kv[:, :LANES]
    k = k * lax.rsqrt(_head_ms(k, bd_ref[:LANES, :LANES]) + EPS) * kgain_ref[...]
    kc_ref[...] = k.astype(BF16)
    vct_ref[...] = kv[:, LANES:].T.astype(BF16)


def _ctx_kv(ctx, mod3, ctx_row, norm_gain, w_in_bf, kgain2, bd):
    b, c, d = ctx.shape
    kvw = 2 * ATTN_KV_HEADS * HEAD_DIM
    return pl.pallas_call(
        _ctx_kernel,
        grid=(b,),
        in_specs=[pl.BlockSpec((None, c, d), lambda i: (i, 0, 0)),
                  pl.BlockSpec((None, N_MOD, d), lambda i: (ctx_row, 0, 0)),
                  pl.BlockSpec((1, d), lambda i: (0, 0)),
                  pl.BlockSpec((d, kvw), lambda i: (0, 0)),
                  pl.BlockSpec((1, LANES), lambda i: (0, 0)),
                  pl.BlockSpec((MXU_DIM, MXU_DIM), lambda i: (0, 0))],
        out_specs=[pl.BlockSpec((None, c, LANES), lambda i: (i, 0, 0)),
                   pl.BlockSpec((None, LANES, c), lambda i: (i, 0, 0))],
        out_shape=[jax.ShapeDtypeStruct((b, c, LANES), BF16),
                   jax.ShapeDtypeStruct((b, LANES, c), BF16)],
        compiler_params=pltpu.CompilerParams(
            dimension_semantics=("arbitrary",), vmem_limit_bytes=VMEM_LIMIT),
        name="ctx_kv",
    )(ctx, mod3, norm_gain, w_in_bf, kgain2, bd)


def _inproj_kernel(x_ref, mod_ref, gain_ref, w_ref, cos_ref, sup_ref, sdn_ref,
                   qgain_ref, kgain_ref, ggain_ref, ws_ref, bs_ref, ogain_ref, bd_ref,
                   wo_f32_ref, wgu_f32_ref, wd_f32_ref,
                   k_ref, vt_ref, qt_ref, mlp_ref, wo_bf_ref, wgu_bf_ref, wd_bf_ref,
                   raw_ref):
    wo_bf_ref[...] = wo_f32_ref[...].astype(BF16)
    wgu_bf_ref[...] = wgu_f32_ref[...].astype(BF16)
    wd_bf_ref[...] = wd_f32_ref[...].astype(BF16)

    step_id = pl.program_id(0)

    @pl.when(step_id == 0)
    def _():
        raw_ref[1] = jnp.zeros(raw_ref.shape[1:], F32)

    for parity in range(2):
        @pl.when(lax.rem(step_id, 2) == parity)
        def _():
            _inproj_step(x_ref, mod_ref, gain_ref, w_ref, cos_ref, sup_ref, sdn_ref,
                         qgain_ref, kgain_ref, ggain_ref, ws_ref, bs_ref, ogain_ref, bd_ref,
                         k_ref, vt_ref, qt_ref, mlp_ref,
                         raw_ref.at[1 - parity], raw_ref.at[parity])


def _inproj_step(x_ref, mod_ref, gain_ref, w_ref, cos_ref, sup_ref, sdn_ref,
                 qgain_ref, kgain_ref, ggain_ref, ws_ref, bs_ref, ogain_ref, bd_ref,
                 k_ref, vt_ref, qt_ref, mlp_ref, prev_ref, new_ref):
    t = x_ref.shape[0]
    nblk = t // BLOCK
    kvw = ATTN_KV_HEADS * HEAD_DIM
    aw = ATTN_HEADS * HEAD_DIM
    mw = MLP_HEADS * HEAD_DIM
    q_cols = slice(2 * kvw, 2 * kvw + aw)
    kv_cols = slice(0, 2 * kvw)
    u_cols = slice(2 * kvw + aw, 2 * kvw + aw + mw)
    g_cols = slice(2 * kvw + aw + mw, 2 * kvw + aw + 2 * mw)
    halves = range(aw // MXU_DIM)
    bd = bd_ref[...]

    def project(cols):
        new_ref[:, cols] = _dot(hn, w_ref[:, cols])

    def finish_qkv(q, kv, q_ms, k_ms):
        cos, sup, sdn = cos_ref[...], sup_ref[...], sdn_ref[...]
        k = kv[:, :kvw] * lax.rsqrt(k_ms + EPS) * kgain_ref[...]
        k_ref[...] = _rope(k, cos, sup, sdn).astype(BF16)
        vt_ref[...] = kv[:, kvw:].T.astype(BF16)
        for half in halves:
            qh = q[:, half * MXU_DIM:(half + 1) * MXU_DIM] * lax.rsqrt(q_ms[half] + EPS)
            qh = qh * qgain_ref[:, half * MXU_DIM:(half + 1) * MXU_DIM]
            for sl in range(MXU_DIM // LANES):
                qs = _rope(qh[:, sl * LANES:(sl + 1) * LANES], cos, sup, sdn)
                qst = qs.T
                for hh in range(LANES // HEAD_DIM):
                    head = (half * MXU_DIM + sl * LANES) // HEAD_DIM + hh
                    kvh, grp = head // ATTN_GROUP, head % ATTN_GROUP
                    for jb in range(nblk):
                        col = (jb * ATTN_GROUP + grp) * BLOCK
                        qt_ref[kvh, :, col:col + BLOCK] = qst[
                            hh * HEAD_DIM:(hh + 1) * HEAD_DIM, jb * BLOCK:(jb + 1) * BLOCK].astype(BF16)

    def gate_mix(g_raw):
        g = _gelu2_tanh(g_raw)
        g_ms = [_head_ms(g[:, h * MXU_DIM:(h + 1) * MXU_DIM], bd) for h in halves]
        gn = jnp.concatenate(
            [g[:, h * MXU_DIM:(h + 1) * MXU_DIM] * lax.rsqrt(g_ms[h] + 4.0 * EPS) for h in halves],
            axis=1) * ggain_ref[...]
        gnb = gn.astype(BF16)
        low_head = lax.broadcasted_iota(jnp.int32, (BLOCK, t), 1) % LANES < HEAD_DIM
        mixed_slabs = []
        for p in range(MLP_HEADS // 2):
            rhs = jnp.concatenate(
                [gnb[c * BLOCK:(c + 1) * BLOCK, p * LANES:(p + 1) * LANES] for c in range(nblk)], axis=1)
            a = _dot(ws_ref[2 * p], rhs)
            b = _dot(ws_ref[2 * p + 1], rhs)
            mixed_slabs.append(jnp.where(low_head, a, b))
        return mixed_slabs

    def finish_mlp(u_raw, mixed_slabs):
        u = _gelu2_tanh(u_raw)
        rows = []
        for c in range(nblk):
            mixed_c = jnp.concatenate(
                [m[:, c * LANES:(c + 1) * LANES] for m in mixed_slabs], axis=1) + bs_ref[...]
            rows.append(u[c * BLOCK:(c + 1) * BLOCK, :] * mixed_c)
        o = jnp.concatenate(rows, axis=0)
        o = o * lax.rsqrt(jnp.mean(o * o, axis=-1, keepdims=True) + 4.0 * EPS) * ogain_ref[...]
        mlp_ref[...] = o.astype(BF16)

    hn = _mod_norm(x_ref[...], gain_ref[...], mod_ref[0:1, :], mod_ref[1:2, :]).astype(BF16)
    q_prev, kv_prev = prev_ref[:, q_cols], prev_ref[:, kv_cols]
    q_ms = [_head_ms(q_prev[:, h * MXU_DIM:(h + 1) * MXU_DIM], bd) for h in halves]
    k_ms = _head_ms(kv_prev[:, :kvw], bd[:kvw, :kvw])
    project(q_cols)
    project(kv_cols)
    finish_qkv(q_prev, kv_prev, q_ms, k_ms)
    project(g_cols)
    mixed = gate_mix(prev_ref[:, g_cols])
    project(u_cols)
    finish_mlp(prev_ref[:, u_cols], mixed)


def _inproj(x, mod3, norm_gain, w_in_bf, cos, sup, sdn, qgain, kgain2, ggain, ws_bf, bs_full,
            ogain, bd, later_weights, tile):
    b, s, d = x.shape
    inw = w_in_bf.shape[1]
    aw = ATTN_HEADS * HEAD_DIM
    mw = MLP_HEADS * HEAD_DIM
    n_tiles = (s // tile) * b
    const = lambda shape: pl.BlockSpec(shape, lambda g: (0,) * len(shape))

    def proj_tile(g):
        t_ = jnp.minimum(g, n_tiles - 1)
        return t_ % b, t_ // b

    def fin_tile(g):
        t_ = jnp.maximum(g - 1, 0)
        return t_ % b, t_ // b

    def at_fin(fn):
        return lambda g: fn(*fin_tile(g))

    def slab_spec(w):
        rows = next(r for r in range(BF16_SUBLANES, w.shape[0] + 1, BF16_SUBLANES)
                    if w.shape[0] % r == 0 and w.shape[0] // r <= n_tiles)
        n_slabs = w.shape[0] // rows
        return pl.BlockSpec((rows, w.shape[1]), lambda g: (jnp.minimum(g, n_slabs - 1), 0))

    slab_specs = [slab_spec(w) for w in later_weights]
    table_spec = pl.BlockSpec((tile, LANES), at_fin(lambda bb, i: (i, 0)))
    return pl.pallas_call(
        _inproj_kernel,
        grid=(n_tiles + 1,),
        in_specs=[pl.BlockSpec((None, tile, d), lambda g: (proj_tile(g)[0], proj_tile(g)[1], 0)),
                  pl.BlockSpec((None, N_MOD, d), lambda g: (proj_tile(g)[0], 0, 0)),
                  const((1, d)),
                  const((d, inw)),
                  table_spec, table_spec, table_spec,
                  const((1, aw)), const((1, LANES)), const((1, mw)),
                  const((MLP_HEADS, BLOCK, BLOCK)), const((BLOCK, mw)), const((1, mw)),
                  const((MXU_DIM, MXU_DIM))] + slab_specs,
        out_specs=[pl.BlockSpec((None, tile, LANES), at_fin(lambda bb, i: (bb, i, 0))),
                   pl.BlockSpec((None, LANES, tile), at_fin(lambda bb, i: (bb, 0, i))),
                   pl.BlockSpec((None, ATTN_KV_HEADS, HEAD_DIM, ATTN_GROUP * tile),
                                at_fin(lambda bb, i: (bb, 0, 0, i))),
                   pl.BlockSpec((None, tile, mw), at_fin(lambda bb, i: (bb, i, 0)))] + slab_specs,
        out_shape=[jax.ShapeDtypeStruct((b, s, LANES), BF16),
                   jax.ShapeDtypeStruct((b, LANES, s), BF16),
                   jax.ShapeDtypeStruct((b, ATTN_KV_HEADS, HEAD_DIM, ATTN_GROUP * s), BF16),
                   jax.ShapeDtypeStruct((b, s, mw), BF16)]
        + [jax.ShapeDtypeStruct(w.shape, BF16) for w in later_weights],
        scratch_shapes=[pltpu.VMEM((2, tile, inw), F32)],
        compiler_params=pltpu.CompilerParams(
            dimension_semantics=("arbitrary",), vmem_limit_bytes=VMEM_LIMIT),
        name="inproj",
    )(x, mod3, norm_gain, w_in_bf, cos, sup, sdn, qgain, kgain2, ggain, ws_bf, bs_full, ogain, bd,
      *later_weights)


def _mix_ffn_kernel(tiles_per_seq,
                    x_ref, mod_ref, kp_ref, km_ref, kn_ref, vp_ref, vm_ref, vn_ref,
                    kc_ref, vct_ref, qt_ref, mlp_ref, sink_ref, bias_ref, again_ref, wo_ref,
                    fmod_ref, fgain_ref, wgu_ref, wd_ref,
                    o_ref, h_ref, hid_ref):
    tq = x_ref.shape[0]
    nblk = tq // BLOCK
    step_id = pl.program_id(0)
    n_tiles = pl.num_programs(0) - 1
    i = lax.rem(jnp.minimum(step_id, n_tiles - 1), tiles_per_seq)
    last = tiles_per_seq - 1
    gq = ATTN_GROUP * BLOCK
    slot_w = lax.rem(step_id, 2)
    slot_r = 1 - slot_w

    @pl.when(step_id == 0)
    def _():
        h_ref[1] = jnp.zeros(h_ref.shape[1:], F32)

    ff = wd_ref.shape[0]
    n_chunks = ff // MXU_DIM
    h_in = h_ref[slot_r]
    hn = _mod_norm(h_in, fgain_ref[...], fmod_ref[3:4, :], fmod_ref[4:5, :]).astype(BF16)

    def ffn_matmuls(c):
        return (_dot(hn, wgu_ref[:, c * MXU_DIM:(c + 1) * MXU_DIM]),
                _dot(hn, wgu_ref[:, ff + c * MXU_DIM:ff + (c + 1) * MXU_DIM]))

    def ffn_activation(c, gate_up):
        a, b = gate_up
        hid_ref[:, c * MXU_DIM:(c + 1) * MXU_DIM] = (_silu(a) * b).astype(BF16)

    k_ext = jnp.concatenate([kp_ref[...], km_ref[...], kn_ref[...]], axis=0)
    vt_ext = jnp.concatenate([vp_ref[...], vm_ref[...], vn_ref[...]], axis=1)
    kc = kc_ref[...]
    vct = vct_ref[...]
    cw = HEADS_PER_STEP * BLOCK
    zeros_q = jnp.zeros((HEAD_DIM, cw), BF16)
    ones_rows = jnp.ones((BF16_SUBLANES, 3 * BLOCK + kc.shape[0]), BF16)
    bias_prev = bias_ref[0:BLOCK, :cw]
    bias_next = bias_ref[BLOCK:2 * BLOCK, :cw]

    def score_matmul(jb, kvh, part):
        qt = qt_ref[kvh, :, jb * gq + part * cw:jb * gq + (part + 1) * cw]
        rhs = jnp.concatenate([qt, zeros_q] if kvh == 0 else [zeros_q, qt], axis=0)
        return _dot(kc, rhs), _dot(k_ext[jb * BLOCK:(jb + 3) * BLOCK, :], rhs)

    def mask_and_max(jb, kvh, part, s):
        pen_prev = jnp.where(i == 0, MASKED, 0.0) if jb == 0 else 0.0
        pen_next = jnp.where(i == last, MASKED, 0.0) if jb == nblk - 1 else 0.0
        s_ctx, s = s
        parts = [s[0:BLOCK] + (bias_prev + pen_prev), s[BLOCK:2 * BLOCK],
                 s[2 * BLOCK:3 * BLOCK] + (bias_next + pen_next), s_ctx]
        m = jnp.maximum(
            jnp.maximum(jnp.max(parts[0], axis=0, keepdims=True), jnp.max(parts[1], axis=0, keepdims=True)),
            jnp.maximum(jnp.max(parts[2], axis=0, keepdims=True), jnp.max(parts[3], axis=0, keepdims=True)))
        return parts, jnp.maximum(m, sink_ref[kvh, :, part * cw:(part + 1) * cw])

    def exp_weights(parts, m):
        return jnp.concatenate([jnp.exp2(p_ - m).astype(BF16) for p_ in parts], axis=0)

    def value_matmul(jb, kvh, part, p):
        v_all = jnp.concatenate(
            [vt_ext[kvh * HEAD_DIM:(kvh + 1) * HEAD_DIM, jb * BLOCK:(jb + 3) * BLOCK],
             vct[kvh * HEAD_DIM:(kvh + 1) * HEAD_DIM, :]], axis=1)
        return _dot(jnp.concatenate([v_all, ones_rows], axis=0), p)

    def normalize(jb, kvh, part, pv, m):
        denom = pv[HEAD_DIM:HEAD_DIM + 1, :] + jnp.exp2(sink_ref[kvh, :, part * cw:(part + 1) * cw] - m)
        o_t = pv[:HEAD_DIM, :] * (1.0 / denom)
        return [o_t[:, g * BLOCK:(g + 1) * BLOCK] for g in range(HEADS_PER_STEP)]

    def block_norm(out_t):
        o_all = jnp.concatenate(out_t, axis=0)
        ms = jnp.mean(o_all * o_all, axis=0, keepdims=True)
        y = o_all * lax.rsqrt(ms + EPS) * again_ref[...]
        return y.T.astype(BF16)

    aw = ATTN_HEADS * HEAD_DIM
    steps = [(jb, kvh, part) for jb in range(nblk) for kvh in range(ATTN_KV_HEADS)
             for part in range(ATTN_GROUP // HEADS_PER_STEP)]
    assert n_chunks >= len(steps)
    raw = {n: score_matmul(*steps[n]) for n in range(2)}
    proj_mlp = _dot(mlp_ref[...], wo_ref[aw:, :])
    ready = {0: mask_and_max(*steps[0], raw.pop(0))}
    attn_rows, out_t, unnormalized, gate_up = [], [], None, {}

    def collect(step, pv, m):
        out_t.extend(normalize(*step, pv, m))
        if len(out_t) == ATTN_HEADS:
            attn_rows.append(block_norm(out_t))
            out_t.clear()

    for n, step in enumerate(steps):
        if n + 1 < len(steps):
            ready[n + 1] = mask_and_max(*steps[n + 1], raw.pop(n + 1))
        if n + 2 < len(steps):
            raw[n + 2] = score_matmul(*steps[n + 2])
        gate_up[n] = ffn_matmuls(n)
        if unnormalized is not None:
            collect(*unnormalized)
            ffn_activation(n - 1, gate_up.pop(n - 1))
        parts, m = ready.pop(n)
        unnormalized = (step, value_matmul(*step, exp_weights(parts, m)), m)

    c_next = len(steps)
    gate_up[c_next] = ffn_matmuls(c_next)
    collect(*unnormalized)
    ffn_activation(c_next - 1, gate_up.pop(c_next - 1))
    for c in range(c_next + 1, n_chunks):
        gate_up[c] = ffn_matmuls(c)
        ffn_activation(c - 1, gate_up.pop(c - 1))
    proj_attn = _dot(jnp.concatenate(attn_rows, axis=0), wo_ref[:aw, :])
    ffn_activation(n_chunks - 1, gate_up.pop(n_chunks - 1))
    h_ref[slot_w] = x_ref[...] + mod_ref[2:3, :] * (proj_attn + proj_mlp)
    o_ref[...] = h_in + fmod_ref[5:6, :] * _dot(hid_ref[...], wd_ref[...])


def _mix_ffn(x, mod3, k, vt, kc, vct, qt, mlpn, sink_rows, bias, again_b, wo_bf,
             ffn_gain, wgu_bf, wd_bf, tile):
    b, s, d = x.shape
    c = kc.shape[1]
    aw = ATTN_HEADS * HEAD_DIM
    mw = mlpn.shape[2]
    ff = wd_bf.shape[0]
    assert ff % MXU_DIM == 0
    r = tile // BLOCK
    nb = s // BLOCK
    nt = s // tile
    n_tiles = b * nt
    gq = ATTN_GROUP * BLOCK

    def mix_tile(g):
        t = jnp.minimum(g, n_tiles - 1)
        return t // nt, t % nt

    def ffn_tile(g):
        t = jnp.maximum(g - 1, 0)
        return t // nt, t % nt

    def at_mix(fn):
        return lambda g: fn(*mix_tile(g))

    const = lambda shape: pl.BlockSpec(shape, lambda g: (0,) * len(shape))
    return pl.pallas_call(
        functools.partial(_mix_ffn_kernel, nt),
        grid=(n_tiles + 1,),
        in_specs=[pl.BlockSpec((None, tile, d), at_mix(lambda bb, i: (bb, i, 0))),
                  pl.BlockSpec((None, N_MOD, d), at_mix(lambda bb, i: (bb, 0, 0))),
                  pl.BlockSpec((None, BLOCK, LANES), at_mix(lambda bb, i: (bb, jnp.maximum(i * r - 1, 0), 0))),
                  pl.BlockSpec((None, tile, LANES), at_mix(lambda bb, i: (bb, i, 0))),
                  pl.BlockSpec((None, BLOCK, LANES),
                               at_mix(lambda bb, i: (bb, jnp.minimum((i + 1) * r, nb - 1), 0))),
                  pl.BlockSpec((None, LANES, BLOCK), at_mix(lambda bb, i: (bb, 0, jnp.maximum(i * r - 1, 0)))),
                  pl.BlockSpec((None, LANES, tile), at_mix(lambda bb, i: (bb, 0, i))),
                  pl.BlockSpec((None, LANES, BLOCK),
                               at_mix(lambda bb, i: (bb, 0, jnp.minimum((i + 1) * r, nb - 1)))),
                  pl.BlockSpec((None, c, LANES), at_mix(lambda bb, i: (bb, 0, 0))),
                  pl.BlockSpec((None, LANES, c), at_mix(lambda bb, i: (bb, 0, 0))),
                  pl.BlockSpec((None, ATTN_KV_HEADS, HEAD_DIM, ATTN_GROUP * tile),
                               at_mix(lambda bb, i: (bb, 0, 0, i))),
                  pl.BlockSpec((None, tile, mw), at_mix(lambda bb, i: (bb, i, 0))),
                  const((ATTN_KV_HEADS, 1, gq)),
                  const((2 * BLOCK, gq)),
                  const((aw, BLOCK)),
                  const((aw + mw, d)),
                  pl.BlockSpec((None, N_MOD, d), lambda g: (ffn_tile(g)[0], 0, 0)),
                  const((1, d)), const((d, 2 * ff)), const((ff, d))],
        out_specs=pl.BlockSpec((None, tile, d), lambda g: (*ffn_tile(g), 0)),
        out_shape=jax.ShapeDtypeStruct((b, s, d), F32),
        scratch_shapes=[pltpu.VMEM((2, tile, d), F32), pltpu.VMEM((tile, ff), BF16)],
        compiler_params=pltpu.CompilerParams(
            dimension_semantics=("arbitrary",), vmem_limit_bytes=VMEM_LIMIT),
        name="mix_ffn",
    )(x, mod3, k, k, k, vt, vt, vt, kc, vct, qt, mlpn, sink_rows, bias, again_b, wo_bf,
      mod3, ffn_gain, wgu_bf, wd_bf)


def _rope_tables(s):
    axis_dim = HEAD_DIM // 2
    pos = jnp.arange(s)
    inv_freq = ROPE_THETA ** (-jnp.arange(0, axis_dim, 2, dtype=F32) / axis_dim)
    ang_r = (pos // GRID_W).astype(F32)[:, None] * inv_freq[None, :]
    ang_c = (pos % GRID_W).astype(F32)[:, None] * inv_freq[None, :]
    cr, sr, cc, sc = jnp.cos(ang_r), jnp.sin(ang_r), jnp.cos(ang_c), jnp.sin(ang_c)
    z = jnp.zeros_like(sr)
    reps = LANES // HEAD_DIM
    cos = jnp.tile(jnp.concatenate([cr, cr, cc, cc], axis=1), (1, reps))
    sin_up = jnp.tile(jnp.concatenate([-sr, z, -sc, z], axis=1), (1, reps))
    sin_dn = jnp.tile(jnp.concatenate([z, sr, z, sc], axis=1), (1, reps))
    return cos, sin_up, sin_dn


def _window_bias():
    c = jnp.arange(BLOCK)[:, None]
    r = jnp.arange(BLOCK)[None, :]
    prev = jnp.where(c >= r, 0.0, MASKED).astype(F32)
    nxt = jnp.where(c <= r, 0.0, MASKED).astype(F32)
    return jnp.tile(jnp.concatenate([prev, nxt], axis=0), (1, ATTN_GROUP))


def kernel(x, c, ctx, c_ctx, w_mod, b_mod, norm_mix, norm_ffn, w_in, q_gain, k_gain, attn_sink,
           gate_gain, w_spatial, b_spatial, attn_out_gain, mlp_out_gain, w_out, w_gate_up, w_down):
    b, s, d = x.shape
    assert w_mod.shape[0] == 1, "single-layer problem"
    assert s % 512 == 0 and d % LANES == 0
    aw = ATTN_HEADS * HEAD_DIM
    mw = MLP_HEADS * HEAD_DIM

    rows = -(-(b + 1) // BF16_SUBLANES) * BF16_SUBLANES
    cond = jnp.concatenate([c, c_ctx[None, :], jnp.zeros((rows - b - 1, d), F32)], axis=0)
    mod3 = _adaln(cond, w_mod[0], b_mod[0][None, :]).reshape(rows, N_MOD, d)

    w_in_bf = w_in[0].astype(BF16)
    bd = jnp.kron(jnp.eye(MXU_DIM // HEAD_DIM, dtype=F32),
                  jnp.full((HEAD_DIM, HEAD_DIM), 1.0 / HEAD_DIM, F32)).astype(BF16)
    kgain2 = jnp.tile(k_gain[0], ATTN_KV_HEADS)[None, :]
    qgain = (jnp.tile(q_gain[0], ATTN_HEADS) * (HEAD_DIM ** -0.5 * LOG2E))[None, :]
    norm_mix_g = norm_mix[0][None, :]

    kc, vct = _ctx_kv(ctx, mod3, b, norm_mix_g, w_in_bf, kgain2, bd)

    cos, sup, sdn = _rope_tables(s)
    bs_full = jnp.repeat(b_spatial[0].T, HEAD_DIM, axis=1)
    k, vt, qt, mlpn, wo_bf, wgu_bf, wd_bf = _inproj(
        x, mod3, norm_mix_g, w_in_bf, cos, sup, sdn, qgain, kgain2,
        gate_gain[0].reshape(1, mw), w_spatial[0].astype(BF16), bs_full,
        mlp_out_gain[0][None, :], bd, (w_out[0], w_gate_up[0], w_down[0]), tile=512)

    sink_rows = jnp.repeat(attn_sink[0].reshape(ATTN_KV_HEADS, ATTN_GROUP) * LOG2E,
                           BLOCK, axis=1)[:, None, :]
    again_b = jnp.broadcast_to(attn_out_gain[0][:, None], (aw, BLOCK))
    return _mix_ffn(x, mod3, k, vt, kc, vct, qt, mlpn, sink_rows, _window_bias(), again_b,
                    wo_bf, norm_ffn[0][None, :], wgu_bf, wd_bf, tile=512)
```

```python
import functools
import math

import jax
import jax.numpy as jnp
import numpy as np
from jax import lax
from jax.experimental import pallas as pl
from jax.experimental.pallas import tpu as pltpu

F32 = jnp.float32
BF16 = jnp.bfloat16

HEAD_DIM = 64
ATTN_HEADS = 8
ATTN_KV_HEADS = 2
ATTN_GROUP = ATTN_HEADS // ATTN_KV_HEADS
MLP_HEADS = 8
N_MOD = 6
BLOCK = 128
GRID_W = 64
ROPE_THETA = 10000.0
EPS = 1e-6
MASKED = -1e30
LOG2E = math.log2(math.e)

LANES = 128
BF16_SUBLANES = 16
MXU_DIM = 256
VMEM_LIMIT = 56 * 1024 * 1024
HEADS_PER_STEP = 4


def _dot(a, b):
    return jnp.dot(a, b, preferred_element_type=F32)


def _silu(x):
    return x * (1.0 / (1.0 + jnp.exp(-x)))


def _gelu2_tanh(x):
    c = math.sqrt(2.0 / math.pi)
    return x * (1.0 + jnp.tanh(x * (c + (c * 0.044715) * (x * x))))


def _mod_norm(x, gain, shift, scale):
    y = x * lax.rsqrt(jnp.mean(x * x, axis=-1, keepdims=True) + EPS)
    return y * (gain * (1.0 + scale)) + shift


def _head_ms(x, bd):
    return _dot((x * x).astype(BF16), bd)


def _rope(x, cos, sin_up, sin_dn):
    up = pltpu.roll(x, LANES - 16, 1)
    dn = pltpu.roll(x, 16, 1)
    return x * cos + up * sin_up + dn * sin_dn


def _split_bf16(x):
    hi = x.astype(BF16)
    return hi, (x - hi.astype(F32)).astype(BF16)


def _adaln_kernel(cond_ref, w_ref, b_ref, o_ref):
    rows = cond_ref.shape[0]

    @pl.when(pl.program_id(0) == 0)
    def _():
        o_ref[...] = jnp.broadcast_to(b_ref[...], o_ref.shape)

    s_hi, s_lo = _split_bf16(_silu(cond_ref[...]))
    w_hi, w_lo = _split_bf16(w_ref[...])
    both = _dot(jnp.concatenate([s_hi, s_lo], axis=0), w_hi)
    o_ref[...] += both[:rows] + both[rows:] + _dot(s_hi, w_lo)


def _adaln(cond, w_mod, b_mod):
    rows, d = cond.shape
    n = w_mod.shape[1]
    tk = d // 8
    assert rows % BF16_SUBLANES == 0 and tk % LANES == 0
    return pl.pallas_call(
        _adaln_kernel,
        grid=(d // tk,),
        in_specs=[pl.BlockSpec((rows, tk), lambda j: (0, j)),
                  pl.BlockSpec((tk, n), lambda j: (j, 0)),
                  pl.BlockSpec((1, n), lambda j: (0, 0))],
        out_specs=pl.BlockSpec((rows, n), lambda j: (0, 0)),
        out_shape=jax.ShapeDtypeStruct((rows, n), F32),
        compiler_params=pltpu.CompilerParams(
            dimension_semantics=("arbitrary",), vmem_limit_bytes=VMEM_LIMIT),
        name="adaln",
    )(cond, w_mod, b_mod)


def _ctx_kernel(x_ref, mod_ref, gain_ref, w_ref, kgain_ref, bd_ref, kc_ref, vct_ref):
    hn = _mod_norm(x_ref[...], gain_ref[...], mod_ref[0:1, :], mod_ref[1:2, :])
    kv = _dot(hn.astype(BF16), w_ref[...])
    k = kv[:, :LANES]
    k = k * lax.rsqrt(_head_ms(k, bd_ref[:LANES, :LANES]) + EPS) * kgain_ref[...]
    kc_ref[...] = k.astype(BF16)
    vct_ref[...] = kv[:, LANES:].T.astype(BF16)


def _ctx_kv(ctx, mod3, ctx_row, norm_gain, w_in_bf, kgain2, bd):
    b, c, d = ctx.shape
    kvw = 2 * ATTN_KV_HEADS * HEAD_DIM
    return pl.pallas_call(
        _ctx_kernel,
        grid=(b,),
        in_specs=[pl.BlockSpec((None, c, d), lambda i: (i, 0, 0)),
                  pl.BlockSpec((None, N_MOD, d), lambda i: (ctx_row, 0, 0)),
                  pl.BlockSpec((1, d), lambda i: (0, 0)),
                  pl.BlockSpec((d, kvw), lambda i: (0, 0)),
                  pl.BlockSpec((1, LANES), lambda i: (0, 0)),
                  pl.BlockSpec((MXU_DIM, MXU_DIM), lambda i: (0, 0))],
        out_specs=[pl.BlockSpec((None, c, LANES), lambda i: (i, 0, 0)),
                   pl.BlockSpec((None, LANES, c), lambda i: (i, 0, 0))],
        out_shape=[jax.ShapeDtypeStruct((b, c, LANES), BF16),
                   jax.ShapeDtypeStruct((b, LANES, c), BF16)],
        compiler_params=pltpu.CompilerParams(
            dimension_semantics=("arbitrary",), vmem_limit_bytes=VMEM_LIMIT),
        name="ctx_kv",
    )(ctx, mod3, norm_gain, w_in_bf, kgain2, bd)


def _inproj_kernel(x_ref, mod_ref, gain_ref, w_ref, cos_ref, sup_ref, sdn_ref,
                   qgain_ref, kgain_ref, ggain_ref, ws_ref, bs_ref, ogain_ref, bd_ref,
                   wo_f32_ref, wgu_f32_ref, wd_f32_ref,
                   k_ref, vt_ref, qt_ref, mlp_ref, wo_bf_ref, wgu_bf_ref, wd_bf_ref):
    wo_bf_ref[...] = wo_f32_ref[...].astype(BF16)
    wgu_bf_ref[...] = wgu_f32_ref[...].astype(BF16)
    wd_bf_ref[...] = wd_f32_ref[...].astype(BF16)

    t = x_ref.shape[0]
    nblk = t // BLOCK
    kvw = ATTN_KV_HEADS * HEAD_DIM
    aw = ATTN_HEADS * HEAD_DIM
    mw = MLP_HEADS * HEAD_DIM
    q_cols = slice(2 * kvw, 2 * kvw + aw)
    kv_cols = slice(0, 2 * kvw)
    u_cols = slice(2 * kvw + aw, 2 * kvw + aw + mw)
    g_cols = slice(2 * kvw + aw + mw, 2 * kvw + aw + 2 * mw)
    halves = range(aw // MXU_DIM)
    bd = bd_ref[...]

    def project(cols):
        return _dot(hn, w_ref[:, cols])

    def finish_qkv(q, kv, q_ms, k_ms):
        cos, sup, sdn = cos_ref[...], sup_ref[...], sdn_ref[...]
        k = kv[:, :kvw] * lax.rsqrt(k_ms + EPS) * kgain_ref[...]
        k_ref[...] = _rope(k, cos, sup, sdn).astype(BF16)
        vt_ref[...] = kv[:, kvw:].T.astype(BF16)
        for half in halves:
            qh = q[:, half * MXU_DIM:(half + 1) * MXU_DIM] * lax.rsqrt(q_ms[half] + EPS)
            qh = qh * qgain_ref[:, half * MXU_DIM:(half + 1) * MXU_DIM]
            for sl in range(MXU_DIM // LANES):
                qs = _rope(qh[:, sl * LANES:(sl + 1) * LANES], cos, sup, sdn)
                qst = qs.T
                for hh in range(LANES // HEAD_DIM):
                    head = (half * MXU_DIM + sl * LANES) // HEAD_DIM + hh
                    kvh, grp = head // ATTN_GROUP, head % ATTN_GROUP
                    for jb in range(nblk):
                        col = (jb * ATTN_GROUP + grp) * BLOCK
                        qt_ref[kvh, :, col:col + BLOCK] = qst[
                            hh * HEAD_DIM:(hh + 1) * HEAD_DIM, jb * BLOCK:(jb + 1) * BLOCK].astype(BF16)

    def gate_mix(g_raw):
        g = _gelu2_tanh(g_raw)
        g_ms = [_head_ms(g[:, h * MXU_DIM:(h + 1) * MXU_DIM], bd) for h in halves]
        gn = jnp.concatenate(
            [g[:, h * MXU_DIM:(h + 1) * MXU_DIM] * lax.rsqrt(g_ms[h] + 4.0 * EPS) for h in halves],
            axis=1) * ggain_ref[...]
        gnb = gn.astype(BF16)
        low_head = lax.broadcasted_iota(jnp.int32, (BLOCK, t), 1) % LANES < HEAD_DIM
        mixed_slabs = []
        for p in range(MLP_HEADS // 2):
            rhs = jnp.concatenate(
                [gnb[c * BLOCK:(c + 1) * BLOCK, p * LANES:(p + 1) * LANES] for c in range(nblk)], axis=1)
            a = _dot(ws_ref[2 * p], rhs)
            b = _dot(ws_ref[2 * p + 1], rhs)
            mixed_slabs.append(jnp.where(low_head, a, b))
        return mixed_slabs

    def finish_mlp(u_raw, mixed_slabs):
        u = _gelu2_tanh(u_raw)
        rows = []
        for c in range(nblk):
            mixed_c = jnp.concatenate(
                [m[:, c * LANES:(c + 1) * LANES] for m in mixed_slabs], axis=1) + bs_ref[...]
            rows.append(u[c * BLOCK:(c + 1) * BLOCK, :] * mixed_c)
        o = jnp.concatenate(rows, axis=0)
        o = o * lax.rsqrt(jnp.mean(o * o, axis=-1, keepdims=True) + 4.0 * EPS) * ogain_ref[...]
        mlp_ref[...] = o.astype(BF16)

    hn = _mod_norm(x_ref[...], gain_ref[...], mod_ref[0:1, :], mod_ref[1:2, :]).astype(BF16)
    q, kv = project(q_cols), project(kv_cols)
    q_ms = [_head_ms(q[:, h * MXU_DIM:(h + 1) * MXU_DIM], bd) for h in halves]
    k_ms = _head_ms(kv[:, :kvw], bd[:kvw, :kvw])
    g_raw = project(g_cols)
    finish_qkv(q, kv, q_ms, k_ms)
    u_raw = project(u_cols)
    finish_mlp(u_raw, gate_mix(g_raw))


def _inproj(x, mod3, norm_gain, w_in_bf, cos, sup, sdn, qgain, kgain2, ggain, ws_bf, bs_full,
            ogain, bd, later_weights, tile):
    b, s, d = x.shape
    inw = w_in_bf.shape[1]
    aw = ATTN_HEADS * HEAD_DIM
    mw = MLP_HEADS * HEAD_DIM
    n_steps = (s // tile) * b
    const = lambda shape: pl.BlockSpec(shape, lambda i, bb: (0,) * len(shape))

    def slab_spec(w):
        rows = next(r for r in range(BF16_SUBLANES, w.shape[0] + 1, BF16_SUBLANES)
                    if w.shape[0] % r == 0 and w.shape[0] // r <= n_steps)
        n_slabs = w.shape[0] // rows
        return pl.BlockSpec((rows, w.shape[1]), lambda i, bb: (jnp.minimum(i * b + bb, n_slabs - 1), 0))

    slab_specs = [slab_spec(w) for w in later_weights]
    table_spec = pl.BlockSpec((tile, LANES), lambda i, bb: (i, 0))
    return pl.pallas_call(
        _inproj_kernel,
        grid=(s // tile, b),
        in_specs=[pl.BlockSpec((None, tile, d), lambda i, bb: (bb, i, 0)),
                  pl.BlockSpec((None, N_MOD, d), lambda i, bb: (bb, 0, 0)),
                  const((1, d)),
                  const((d, inw)),
                  table_spec, table_spec, table_spec,
                  const((1, aw)), const((1, LANES)), const((1, mw)),
                  const((MLP_HEADS, BLOCK, BLOCK)), const((BLOCK, mw)), const((1, mw)),
                  const((MXU_DIM, MXU_DIM))] + slab_specs,
        out_specs=[pl.BlockSpec((None, tile, LANES), lambda i, bb: (bb, i, 0)),
                   pl.BlockSpec((None, LANES, tile), lambda i, bb: (bb, 0, i)),
                   pl.BlockSpec((None, ATTN_KV_HEADS, HEAD_DIM, ATTN_GROUP * tile),
                                lambda i, bb: (bb, 0, 0, i)),
                   pl.BlockSpec((None, tile, mw), lambda i, bb: (bb, i, 0))] + slab_specs,
        out_shape=[jax.ShapeDtypeStruct((b, s, LANES), BF16),
                   jax.ShapeDtypeStruct((b, LANES, s), BF16),
                   jax.ShapeDtypeStruct((b, ATTN_KV_HEADS, HEAD_DIM, ATTN_GROUP * s), BF16),
                   jax.ShapeDtypeStruct((b, s, mw), BF16)]
        + [jax.ShapeDtypeStruct(w.shape, BF16) for w in later_weights],
        compiler_params=pltpu.CompilerParams(
            dimension_semantics=("arbitrary", "arbitrary"), vmem_limit_bytes=VMEM_LIMIT),
        name="inproj",
    )(x, mod3, norm_gain, w_in_bf, cos, sup, sdn, qgain, kgain2, ggain, ws_bf, bs_full, ogain, bd,
      *later_weights)


def _mix_ffn_kernel(tiles_per_seq,
                    x_ref, mod_ref, kp_ref, km_ref, kn_ref, vp_ref, vm_ref, vn_ref,
                    kc_ref, vct_ref, qt_ref, mlp_ref, sink_ref, bias_ref, again_ref, wo_ref,
                    fmod_ref, fgain_ref, wgu_ref, wd_ref,
                    o_ref, h_ref, hid_ref):
    tq = x_ref.shape[0]
    nblk = tq // BLOCK
    step_id = pl.program_id(0)
    n_tiles = pl.num_programs(0) - 1
    i = lax.rem(jnp.minimum(step_id, n_tiles - 1), tiles_per_seq)
    last = tiles_per_seq - 1
    gq = ATTN_GROUP * BLOCK
    slot_w = lax.rem(step_id, 2)
    slot_r = 1 - slot_w

    @pl.when(step_id == 0)
    def _():
        h_ref[1] = jnp.zeros(h_ref.shape[1:], F32)

    ff = wd_ref.shape[0]
    n_chunks = ff // MXU_DIM
    h_in = h_ref[slot_r]
    hn = _mod_norm(h_in, fgain_ref[...], fmod_ref[3:4, :], fmod_ref[4:5, :]).astype(BF16)

    def ffn_matmuls(c):
        return (_dot(hn, wgu_ref[:, c * MXU_DIM:(c + 1) * MXU_DIM]),
                _dot(hn, wgu_ref[:, ff + c * MXU_DIM:ff + (c + 1) * MXU_DIM]))

    def ffn_activation(c, gate_up):
        a, b = gate_up
        hid_ref[:, c * MXU_DIM:(c + 1) * MXU_DIM] = (_silu(a) * b).astype(BF16)

    k_ext = jnp.concatenate([kp_ref[...], km_ref[...], kn_ref[...]], axis=0)
    vt_ext = jnp.concatenate([vp_ref[...], vm_ref[...], vn_ref[...]], axis=1)
    kc = kc_ref[...]
    vct = vct_ref[...]
    cw = HEADS_PER_STEP * BLOCK
    zeros_q = jnp.zeros((HEAD_DIM, cw), BF16)
    ones_rows = jnp.ones((BF16_SUBLANES, 3 * BLOCK + kc.shape[0]), BF16)
    bias_prev = bias_ref[0:BLOCK, :cw]
    bias_next = bias_ref[BLOCK:2 * BLOCK, :cw]

    def score_matmul(jb, kvh, part):
        qt = qt_ref[kvh, :, jb * gq + part * cw:jb * gq + (part + 1) * cw]
        rhs = jnp.concatenate([qt, zeros_q] if kvh == 0 else [zeros_q, qt], axis=0)
        return _dot(kc, rhs), _dot(k_ext[jb * BLOCK:(jb + 3) * BLOCK, :], rhs)

    def mask_and_max(jb, kvh, part, s):
        pen_prev = jnp.where(i == 0, MASKED, 0.0) if jb == 0 else 0.0
        pen_next = jnp.where(i == last, MASKED, 0.0) if jb == nblk - 1 else 0.0
        s_ctx, s = s
        parts = [s[0:BLOCK] + (bias_prev + pen_prev), s[BLOCK:2 * BLOCK],
                 s[2 * BLOCK:3 * BLOCK] + (bias_next + pen_next), s_ctx]
        m = jnp.maximum(
            jnp.maximum(jnp.max(parts[0], axis=0, keepdims=True), jnp.max(parts[1], axis=0, keepdims=True)),
            jnp.maximum(jnp.max(parts[2], axis=0, keepdims=True), jnp.max(parts[3], axis=0, keepdims=True)))
        return parts, jnp.maximum(m, sink_ref[kvh, :, part * cw:(part + 1) * cw])

    def exp_weights(parts, m):
        return jnp.concatenate([jnp.exp2(p_ - m).astype(BF16) for p_ in parts], axis=0)

    def value_matmul(jb, kvh, part, p):
        v_all = jnp.concatenate(
            [vt_ext[kvh * HEAD_DIM:(kvh + 1) * HEAD_DIM, jb * BLOCK:(jb + 3) * BLOCK],
             vct[kvh * HEAD_DIM:(kvh + 1) * HEAD_DIM, :]], axis=1)
        return _dot(jnp.concatenate([v_all, ones_rows], axis=0), p)

    def normalize(jb, kvh, part, pv, m):
        denom = pv[HEAD_DIM:HEAD_DIM + 1, :] + jnp.exp2(sink_ref[kvh, :, part * cw:(part + 1) * cw] - m)
        o_t = pv[:HEAD_DIM, :] * (1.0 / denom)
        return [o_t[:, g * BLOCK:(g + 1) * BLOCK] for g in range(HEADS_PER_STEP)]

    def block_norm(out_t):
        o_all = jnp.concatenate(out_t, axis=0)
        ms = jnp.mean(o_all * o_all, axis=0, keepdims=True)
        y = o_all * lax.rsqrt(ms + EPS) * again_ref[...]
        return y.T.astype(BF16)

    aw = ATTN_HEADS * HEAD_DIM
    steps = [(jb, kvh, part) for jb in range(nblk) for kvh in range(ATTN_KV_HEADS)
             for part in range(ATTN_GROUP // HEADS_PER_STEP)]
    assert n_chunks >= len(steps)
    raw = {n: score_matmul(*steps[n]) for n in range(2)}
    proj_mlp = _dot(mlp_ref[...], wo_ref[aw:, :])
    ready = {0: mask_and_max(*steps[0], raw.pop(0))}
    attn_rows, out_t, unnormalized, gate_up = [], [], None, {}

    def collect(step, pv, m):
        out_t.extend(normalize(*step, pv, m))
        if len(out_t) == ATTN_HEADS:
            attn_rows.append(block_norm(out_t))
            out_t.clear()

    for n, step in enumerate(steps):
        if n + 1 < len(steps):
            ready[n + 1] = mask_and_max(*steps[n + 1], raw.pop(n + 1))
        if n + 2 < len(steps):
            raw[n + 2] = score_matmul(*steps[n + 2])
        gate_up[n] = ffn_matmuls(n)
        if unnormalized is not None:
            collect(*unnormalized)
            ffn_activation(n - 1, gate_up.pop(n - 1))
        parts, m = ready.pop(n)
        unnormalized = (step, value_matmul(*step, exp_weights(parts, m)), m)

    c_next = len(steps)
    gate_up[c_next] = ffn_matmuls(c_next)
    collect(*unnormalized)
    ffn_activation(c_next - 1, gate_up.pop(c_next - 1))
    for c in range(c_next + 1, n_chunks):
        gate_up[c] = ffn_matmuls(c)
        ffn_activation(c - 1, gate_up.pop(c - 1))
    proj_attn = _dot(jnp.concatenate(attn_rows, axis=0), wo_ref[:aw, :])
    ffn_activation(n_chunks - 1, gate_up.pop(n_chunks - 1))
    h_ref[slot_w] = x_ref[...] + mod_ref[2:3, :] * (proj_attn + proj_mlp)
    o_ref[...] = h_in + fmod_ref[5:6, :] * _dot(hid_ref[...], wd_ref[...])


def _mix_ffn(x, mod3, k, vt, kc, vct, qt, mlpn, sink_rows, bias, again_b, wo_bf,
             ffn_gain, wgu_bf, wd_bf, tile):
    b, s, d = x.shape
    c = kc.shape[1]
    aw = ATTN_HEADS * HEAD_DIM
    mw = mlpn.shape[2]
    ff = wd_bf.shape[0]
    assert ff % MXU_DIM == 0
    r = tile // BLOCK
    nb = s // BLOCK
    nt = s // tile
    n_tiles = b * nt
    gq = ATTN_GROUP * BLOCK

    def mix_tile(g):
        t = jnp.minimum(g, n_tiles - 1)
        return t // nt, t % nt

    def ffn_tile(g):
        t = jnp.maximum(g - 1, 0)
        return t // nt, t % nt

    def at_mix(fn):
        return lambda g: fn(*mix_tile(g))

    const = lambda shape: pl.BlockSpec(shape, lambda g: (0,) * len(shape))
    return pl.pallas_call(
        functools.partial(_mix_ffn_kernel, nt),
        grid=(n_tiles + 1,),
        in_specs=[pl.BlockSpec((None, tile, d), at_mix(lambda bb, i: (bb, i, 0))),
                  pl.BlockSpec((None, N_MOD, d), at_mix(lambda bb, i: (bb, 0, 0))),
                  pl.BlockSpec((None, BLOCK, LANES), at_mix(lambda bb, i: (bb, jnp.maximum(i * r - 1, 0), 0))),
                  pl.BlockSpec((None, tile, LANES), at_mix(lambda bb, i: (bb, i, 0))),
                  pl.BlockSpec((None, BLOCK, LANES),
                               at_mix(lambda bb, i: (bb, jnp.minimum((i + 1) * r, nb - 1), 0))),
                  pl.BlockSpec((None, LANES, BLOCK), at_mix(lambda bb, i: (bb, 0, jnp.maximum(i * r - 1, 0)))),
                  pl.BlockSpec((None, LANES, tile), at_mix(lambda bb, i: (bb, 0, i))),
                  pl.BlockSpec((None, LANES, BLOCK),
                               at_mix(lambda bb, i: (bb, 0, jnp.minimum((i + 1) * r, nb - 1)))),
                  pl.BlockSpec((None, c, LANES), at_mix(lambda bb, i: (bb, 0, 0))),
                  pl.BlockSpec((None, LANES, c), at_mix(lambda bb, i: (bb, 0, 0))),
                  pl.BlockSpec((None, ATTN_KV_HEADS, HEAD_DIM, ATTN_GROUP * tile),
                               at_mix(lambda bb, i: (bb, 0, 0, i))),
                  pl.BlockSpec((None, tile, mw), at_mix(lambda bb, i: (bb, i, 0))),
                  const((ATTN_KV_HEADS, 1, gq)),
                  const((2 * BLOCK, gq)),
                  const((aw, BLOCK)),
                  const((aw + mw, d)),
                  pl.BlockSpec((None, N_MOD, d), lambda g: (ffn_tile(g)[0], 0, 0)),
                  const((1, d)), const((d, 2 * ff)), const((ff, d))],
        out_specs=pl.BlockSpec((None, tile, d), lambda g: (*ffn_tile(g), 0)),
        out_shape=jax.ShapeDtypeStruct((b, s, d), F32),
        scratch_shapes=[pltpu.VMEM((2, tile, d), F32), pltpu.VMEM((tile, ff), BF16)],
        compiler_params=pltpu.CompilerParams(
            dimension_semantics=("arbitrary",), vmem_limit_bytes=VMEM_LIMIT),
        name="mix_ffn",
    )(x, mod3, k, k, k, vt, vt, vt, kc, vct, qt, mlpn, sink_rows, bias, again_b, wo_bf,
      mod3, ffn_gain, wgu_bf, wd_bf)


def _rope_tables(s):
    axis_dim = HEAD_DIM // 2
    pos = np.arange(s)
    inv_freq = (ROPE_THETA ** (-np.arange(0, axis_dim, 2, dtype=np.float32) / axis_dim)).astype(np.float32)
    ang_r = (pos // GRID_W).astype(np.float32)[:, None] * inv_freq[None, :]
    ang_c = (pos % GRID_W).astype(np.float32)[:, None] * inv_freq[None, :]
    cr, sr, cc, sc = np.cos(ang_r), np.sin(ang_r), np.cos(ang_c), np.sin(ang_c)
    z = np.zeros_like(sr)
    reps = LANES // HEAD_DIM
    cos = np.tile(np.concatenate([cr, cr, cc, cc], axis=1), (1, reps))
    sin_up = np.tile(np.concatenate([-sr, z, -sc, z], axis=1), (1, reps))
    sin_dn = np.tile(np.concatenate([z, sr, z, sc], axis=1), (1, reps))
    return jnp.asarray(cos, F32), jnp.asarray(sin_up, F32), jnp.asarray(sin_dn, F32)


def _window_bias():
    c = np.arange(BLOCK)[:, None]
    r = np.arange(BLOCK)[None, :]
    prev = np.where(c >= r, 0.0, MASKED).astype(np.float32)
    nxt = np.where(c <= r, 0.0, MASKED).astype(np.float32)
    return jnp.asarray(np.tile(np.concatenate([prev, nxt], axis=0), (1, ATTN_GROUP)), F32)


def _head_mean_matrix():
    bd = np.kron(np.eye(MXU_DIM // HEAD_DIM, dtype=np.float32),
                 np.full((HEAD_DIM, HEAD_DIM), 1.0 / HEAD_DIM, np.float32))
    return jnp.asarray(bd, BF16)


def kernel(x, c, ctx, c_ctx, w_mod, b_mod, norm_mix, norm_ffn, w_in, q_gain, k_gain, attn_sink,
           gate_gain, w_spatial, b_spatial, attn_out_gain, mlp_out_gain, w_out, w_gate_up, w_down):
    b, s, d = x.shape
    assert w_mod.shape[0] == 1, "single-layer problem"
    assert s % 1024 == 0 and d % LANES == 0
    aw = ATTN_HEADS * HEAD_DIM
    mw = MLP_HEADS * HEAD_DIM

    rows = -(-(b + 1) // BF16_SUBLANES) * BF16_SUBLANES
    cond = jnp.concatenate([c, c_ctx[None, :], jnp.zeros((rows - b - 1, d), F32)], axis=0)
    mod3 = _adaln(cond, w_mod[0], b_mod[0][None, :]).reshape(rows, N_MOD, d)

    w_in_bf = w_in[0].astype(BF16)
    bd = _head_mean_matrix()
    kgain2 = jnp.tile(k_gain[0], ATTN_KV_HEADS)[None, :]
    qgain = (jnp.tile(q_gain[0], ATTN_HEADS) * (HEAD_DIM ** -0.5 * LOG2E))[None, :]
    norm_mix_g = norm_mix[0][None, :]

    kc, vct = _ctx_kv(ctx, mod3, b, norm_mix_g, w_in_bf, kgain2, bd)

    cos, sup, sdn = _rope_tables(s)
    bs_full = jnp.repeat(b_spatial[0].T, HEAD_DIM, axis=1)
    k, vt, qt, mlpn, wo_bf, wgu_bf, wd_bf = _inproj(
        x, mod3, norm_mix_g, w_in_bf, cos, sup, sdn, qgain, kgain2,
        gate_gain[0].reshape(1, mw), w_spatial[0].astype(BF16), bs_full,
        mlp_out_gain[0][None, :], bd, (w_out[0], w_gate_up[0], w_down[0]), tile=1024)

    sink_rows = jnp.repeat(attn_sink[0].reshape(ATTN_KV_HEADS, ATTN_GROUP) * LOG2E,
                           BLOCK, axis=1)[:, None, :]
    again_b = jnp.broadcast_to(attn_out_gain[0][:, None], (aw, BLOCK))
    return _mix_ffn(x, mod3, k, vt, kc, vct, qt, mlpn, sink_rows, _window_bias(), again_b,
                    wo_bf, norm_ffn[0][None, :], wgu_bf, wd_bf, tile=512)
```

```python
import functools
import math

import jax
import jax.numpy as jnp
import numpy as np
from jax import lax
from jax.experimental import pallas as pl
from jax.experimental.pallas import tpu as pltpu

F32 = jnp.float32
BF16 = jnp.bfloat16

HEAD_DIM = 64
ATTN_HEADS = 8
ATTN_KV_HEADS = 2
ATTN_GROUP = ATTN_HEADS // ATTN_KV_HEADS
MLP_HEADS = 8
N_MOD = 6
BLOCK = 128
GRID_W = 64
ROPE_THETA = 10000.0
EPS = 1e-6
MASKED = -1e30
LOG2E = math.log2(math.e)

LANES = 128
BF16_SUBLANES = 16
MXU_DIM = 256
VMEM_LIMIT = 56 * 1024 * 1024
HEADS_PER_STEP = 4


def _dot(a, b):
    return jnp.dot(a, b, preferred_element_type=F32)


def _silu(x):
    return x * (1.0 / (1.0 + jnp.exp(-x)))


def _gelu2_tanh(x):
    c = math.sqrt(2.0 / math.pi)
    return x * (1.0 + jnp.tanh(x * (c + (c * 0.044715) * (x * x))))


def _mod_norm(x, gain, shift, scale):
    y = x * lax.rsqrt(jnp.mean(x * x, axis=-1, keepdims=True) + EPS)
    return y * (gain * (1.0 + scale)) + shift


def _head_ms(x, bd):
    return _dot((x * x).astype(BF16), bd)


def _rope(x, cos, sin_up, sin_dn):
    up = pltpu.roll(x, LANES - 16, 1)
    dn = pltpu.roll(x, 16, 1)
    return x * cos + up * sin_up + dn * sin_dn


def _split_bf16(x):
    hi = x.astype(BF16)
    return hi, (x - hi.astype(F32)).astype(BF16)


def _adaln_kernel(cond_ref, w_ref, b_ref, o_ref):
    rows = cond_ref.shape[0]

    @pl.when(pl.program_id(0) == 0)
    def _():
        o_ref[...] = jnp.broadcast_to(b_ref[...], o_ref.shape)

    s_hi, s_lo = _split_bf16(_silu(cond_ref[...]))
    w_hi, w_lo = _split_bf16(w_ref[...])
    both = _dot(jnp.concatenate([s_hi, s_lo], axis=0), w_hi)
    o_ref[...] += both[:rows] + both[rows:] + _dot(s_hi, w_lo)


def _adaln(cond, w_mod, b_mod):
    rows, d = cond.shape
    n = w_mod.shape[1]
    tk = d // 8
    assert rows % BF16_SUBLANES == 0 and tk % LANES == 0
    return pl.pallas_call(
        _adaln_kernel,
        grid=(d // tk,),
        in_specs=[pl.BlockSpec((rows, tk), lambda j: (0, j)),
                  pl.BlockSpec((tk, n), lambda j: (j, 0)),
                  pl.BlockSpec((1, n), lambda j: (0, 0))],
        out_specs=pl.BlockSpec((rows, n), lambda j: (0, 0)),
        out_shape=jax.ShapeDtypeStruct((rows, n), F32),
        compiler_params=pltpu.CompilerParams(
            dimension_semantics=("arbitrary",), vmem_limit_bytes=VMEM_LIMIT),
        name="adaln",
    )(cond, w_mod, b_mod)


def _ctx_kernel(x_ref, mod_ref, gain_ref, w_ref, kgain_ref, bd_ref, kc_ref, vct_ref):
    hn = _mod_norm(x_ref[...], gain_ref[...], mod_ref[0:1, :], mod_ref[1:2, :])
    kv = _dot(hn.astype(BF16), w_ref[...])
    k = kv[:, :LANES]
    k = k * lax.rsqrt(_head_ms(k, bd_ref[:LANES, :LANES]) + EPS) * kgain_ref[...]
    kc_ref[...] = k.astype(BF16)
    vct_ref[...] = kv[:, LANES:].T.astype(BF16)


def _ctx_kv(ctx, mod3, ctx_row, norm_gain, w_in_bf, kgain2, bd):
    b, c, d = ctx.shape
    kvw = 2 * ATTN_KV_HEADS * HEAD_DIM
    return pl.pallas_call(
        _ctx_kernel,
        grid=(b,),
        in_specs=[pl.BlockSpec((None, c, d), lambda i: (i, 0, 0)),
                  pl.BlockSpec((None, N_MOD, d), lambda i: (ctx_row, 0, 0)),
                  pl.BlockSpec((1, d), lambda i: (0, 0)),
                  pl.BlockSpec((d, kvw), lambda i: (0, 0)),
                  pl.BlockSpec((1, LANES), lambda i: (0, 0)),
                  pl.BlockSpec((MXU_DIM, MXU_DIM), lambda i: (0, 0))],
        out_specs=[pl.BlockSpec((None, c, LANES), lambda i: (i, 0, 0)),
                   pl.BlockSpec((None, LANES, c), lambda i: (i, 0, 0))],
        out_shape=[jax.ShapeDtypeStruct((b, c, LANES), BF16),
                   jax.ShapeDtypeStruct((b, LANES, c), BF16)],
        compiler_params=pltpu.CompilerParams(
            dimension_semantics=("arbitrary",), vmem_limit_bytes=VMEM_LIMIT),
        name="ctx_kv",
    )(ctx, mod3, norm_gain, w_in_bf, kgain2, bd)


def _inproj_kernel(x_ref, mod_ref, gain_ref, w_ref, cos_ref, sup_ref, sdn_ref,
                   qgain_ref, kgain_ref, ggain_ref, ws_ref, bs_ref, ogain_ref, bd_ref,
                   wo_f32_ref, wgu_f32_ref, wd_f32_ref,
                   k_ref, vt_ref, qt_ref, mlp_ref, wo_bf_ref, wgu_bf_ref, wd_bf_ref):
    wo_bf_ref[...] = wo_f32_ref[...].astype(BF16)
    wd_bf_ref[...] = wd_f32_ref[...].astype(BF16)
    ff = wgu_f32_ref.shape[1] // 2
    for c in range(ff // MXU_DIM):
        lo, hi = c * MXU_DIM, (c + 1) * MXU_DIM
        wgu_bf_ref[:, 2 * lo:2 * lo + MXU_DIM] = wgu_f32_ref[:, lo:hi].astype(BF16)
        wgu_bf_ref[:, 2 * lo + MXU_DIM:2 * hi] = wgu_f32_ref[:, ff + lo:ff + hi].astype(BF16)

    t = x_ref.shape[0]
    nblk = t // BLOCK
    kvw = ATTN_KV_HEADS * HEAD_DIM
    aw = ATTN_HEADS * HEAD_DIM
    mw = MLP_HEADS * HEAD_DIM
    q_cols = slice(2 * kvw, 2 * kvw + aw)
    kv_cols = slice(0, 2 * kvw)
    u_cols = slice(2 * kvw + aw, 2 * kvw + aw + mw)
    g_cols = slice(2 * kvw + aw + mw, 2 * kvw + aw + 2 * mw)
    halves = range(aw // MXU_DIM)
    bd = bd_ref[...]

    def project(cols):
        return _dot(hn, w_ref[:, cols])

    def finish_qkv(q, kv, q_ms, k_ms):
        cos, sup, sdn = cos_ref[...], sup_ref[...], sdn_ref[...]
        k = kv[:, :kvw] * lax.rsqrt(k_ms + EPS) * kgain_ref[...]
        k_ref[...] = _rope(k, cos, sup, sdn).astype(BF16)
        vt_ref[...] = kv[:, kvw:].T.astype(BF16)
        for half in halves:
            qh = q[:, half * MXU_DIM:(half + 1) * MXU_DIM] * lax.rsqrt(q_ms[half] + EPS)
            qh = qh * qgain_ref[:, half * MXU_DIM:(half + 1) * MXU_DIM]
            for sl in range(MXU_DIM // LANES):
                qs = _rope(qh[:, sl * LANES:(sl + 1) * LANES], cos, sup, sdn)
                qst = qs.T
                for hh in range(LANES // HEAD_DIM):
                    head = (half * MXU_DIM + sl * LANES) // HEAD_DIM + hh
                    kvh, grp = head // ATTN_GROUP, head % ATTN_GROUP
                    for jb in range(nblk):
                        col = (jb * ATTN_GROUP + grp) * BLOCK
                        qt_ref[kvh, :, col:col + BLOCK] = qst[
                            hh * HEAD_DIM:(hh + 1) * HEAD_DIM, jb * BLOCK:(jb + 1) * BLOCK].astype(BF16)

    def gate_mix(g_raw):
        g = _gelu2_tanh(g_raw)
        g_ms = [_head_ms(g[:, h * MXU_DIM:(h + 1) * MXU_DIM], bd) for h in halves]
        gn = jnp.concatenate(
            [g[:, h * MXU_DIM:(h + 1) * MXU_DIM] * lax.rsqrt(g_ms[h] + 4.0 * EPS) for h in halves],
            axis=1) * ggain_ref[...]
        gnb = gn.astype(BF16)
        low_head = lax.broadcasted_iota(jnp.int32, (BLOCK, t), 1) % LANES < HEAD_DIM
        mixed_slabs = []
        for p in range(MLP_HEADS // 2):
            rhs = jnp.concatenate(
                [gnb[c * BLOCK:(c + 1) * BLOCK, p * LANES:(p + 1) * LANES] for c in range(nblk)], axis=1)
            a = _dot(ws_ref[2 * p], rhs)
            b = _dot(ws_ref[2 * p + 1], rhs)
            mixed_slabs.append(jnp.where(low_head, a, b))
        return mixed_slabs

    def finish_mlp(u_raw, mixed_slabs):
        u = _gelu2_tanh(u_raw)
        rows = []
        for c in range(nblk):
            mixed_c = jnp.concatenate(
                [m[:, c * LANES:(c + 1) * LANES] for m in mixed_slabs], axis=1) + bs_ref[...]
            rows.append(u[c * BLOCK:(c + 1) * BLOCK, :] * mixed_c)
        o = jnp.concatenate(rows, axis=0)
        o = o * lax.rsqrt(jnp.mean(o * o, axis=-1, keepdims=True) + 4.0 * EPS) * ogain_ref[...]
        mlp_ref[...] = o.astype(BF16)

    hn = _mod_norm(x_ref[...], gain_ref[...], mod_ref[0:1, :], mod_ref[1:2, :]).astype(BF16)
    q, kv = project(q_cols), project(kv_cols)
    q_ms = [_head_ms(q[:, h * MXU_DIM:(h + 1) * MXU_DIM], bd) for h in halves]
    k_ms = _head_ms(kv[:, :kvw], bd[:kvw, :kvw])
    g_raw = project(g_cols)
    finish_qkv(q, kv, q_ms, k_ms)
    u_raw = project(u_cols)
    finish_mlp(u_raw, gate_mix(g_raw))


def _inproj(x, mod3, norm_gain, w_in_bf, cos, sup, sdn, qgain, kgain2, ggain, ws_bf, bs_full,
            ogain, bd, later_weights, tile):
    b, s, d = x.shape
    inw = w_in_bf.shape[1]
    aw = ATTN_HEADS * HEAD_DIM
    mw = MLP_HEADS * HEAD_DIM
    n_steps = (s // tile) * b
    const = lambda shape: pl.BlockSpec(shape, lambda i, bb: (0,) * len(shape))

    def slab_spec(w):
        rows = next(r for r in range(BF16_SUBLANES, w.shape[0] + 1, BF16_SUBLANES)
                    if w.shape[0] % r == 0 and w.shape[0] // r <= n_steps)
        n_slabs = w.shape[0] // rows
        return pl.BlockSpec((rows, w.shape[1]), lambda i, bb: (jnp.minimum(i * b + bb, n_slabs - 1), 0))

    slab_specs = [slab_spec(w) for w in later_weights]
    table_spec = pl.BlockSpec((tile, LANES), lambda i, bb: (i, 0))
    return pl.pallas_call(
        _inproj_kernel,
        grid=(s // tile, b),
        in_specs=[pl.BlockSpec((None, tile, d), lambda i, bb: (bb, i, 0)),
                  pl.BlockSpec((None, N_MOD, d), lambda i, bb: (bb, 0, 0)),
                  const((1, d)),
                  const((d, inw)),
                  table_spec, table_spec, table_spec,
                  const((1, aw)), const((1, LANES)), const((1, mw)),
                  const((MLP_HEADS, BLOCK, BLOCK)), const((BLOCK, mw)), const((1, mw)),
                  const((MXU_DIM, MXU_DIM))] + slab_specs,
        out_specs=[pl.BlockSpec((None, tile, LANES), lambda i, bb: (bb, i, 0)),
                   pl.BlockSpec((None, LANES, tile), lambda i, bb: (bb, 0, i)),
                   pl.BlockSpec((None, ATTN_KV_HEADS, HEAD_DIM, ATTN_GROUP * tile),
                                lambda i, bb: (bb, 0, 0, i)),
                   pl.BlockSpec((None, tile, mw), lambda i, bb: (bb, i, 0))] + slab_specs,
        out_shape=[jax.ShapeDtypeStruct((b, s, LANES), BF16),
                   jax.ShapeDtypeStruct((b, LANES, s), BF16),
                   jax.ShapeDtypeStruct((b, ATTN_KV_HEADS, HEAD_DIM, ATTN_GROUP * s), BF16),
                   jax.ShapeDtypeStruct((b, s, mw), BF16)]
        + [jax.ShapeDtypeStruct(w.shape, BF16) for w in later_weights],
        compiler_params=pltpu.CompilerParams(
            dimension_semantics=("arbitrary", "arbitrary"), vmem_limit_bytes=VMEM_LIMIT),
        name="inproj",
    )(x, mod3, norm_gain, w_in_bf, cos, sup, sdn, qgain, kgain2, ggain, ws_bf, bs_full, ogain, bd,
      *later_weights)


def _mix_ffn_kernel(tiles_per_seq,
                    x_ref, mod_ref, kp_ref, km_ref, kn_ref, vp_ref, vm_ref, vn_ref,
                    kc_ref, vct_ref, qt_ref, mlp_ref, sink_ref, bias_ref, again_ref, wo_ref,
                    fmod_ref, fgain_ref, wgu_ref, wd_ref,
                    o_ref, h_ref, hn_ref, hid_ref):
    tq = x_ref.shape[0]
    nblk = tq // BLOCK
    step_id = pl.program_id(0)
    n_tiles = pl.num_programs(0) - 1
    i = lax.rem(jnp.minimum(step_id, n_tiles - 1), tiles_per_seq)
    last = tiles_per_seq - 1
    gq = ATTN_GROUP * BLOCK
    slot_w = lax.rem(step_id, 2)
    slot_r = 1 - slot_w

    @pl.when(step_id == 0)
    def _():
        h_ref[1] = jnp.zeros(h_ref.shape[1:], F32)
        hn_ref[...] = jnp.zeros(hn_ref.shape, BF16)

    ff = wd_ref.shape[0]
    n_chunks = ff // MXU_DIM

    def ffn_matmuls(c):
        return _dot(hn_ref[...], wgu_ref[:, 2 * c * MXU_DIM:2 * (c + 1) * MXU_DIM])

    def ffn_activation(c, gate_up):
        a, b = gate_up[:, :MXU_DIM], gate_up[:, MXU_DIM:]
        hid_ref[:, c * MXU_DIM:(c + 1) * MXU_DIM] = (_silu(a) * b).astype(BF16)

    k_ext = jnp.concatenate([kp_ref[...], km_ref[...], kn_ref[...]], axis=0)
    vt_ext = jnp.concatenate([vp_ref[...], vm_ref[...], vn_ref[...]], axis=1)
    kc = kc_ref[...]
    vct = vct_ref[...]
    cw = HEADS_PER_STEP * BLOCK
    zeros_q = jnp.zeros((HEAD_DIM, cw), BF16)
    ones_rows = jnp.ones((BF16_SUBLANES, 3 * BLOCK + kc.shape[0]), BF16)
    bias_prev = bias_ref[0:BLOCK, :cw]
    bias_next = bias_ref[BLOCK:2 * BLOCK, :cw]

    def score_matmul(jb, kvh, part):
        qt = qt_ref[kvh, :, jb * gq + part * cw:jb * gq + (part + 1) * cw]
        rhs = jnp.concatenate([qt, zeros_q] if kvh == 0 else [zeros_q, qt], axis=0)
        return _dot(kc, rhs), _dot(k_ext[jb * BLOCK:(jb + 3) * BLOCK, :], rhs)

    def mask_and_max(jb, kvh, part, s):
        pen_prev = jnp.where(i == 0, MASKED, 0.0) if jb == 0 else 0.0
        pen_next = jnp.where(i == last, MASKED, 0.0) if jb == nblk - 1 else 0.0
        s_ctx, s = s
        parts = [s[0:BLOCK] + (bias_prev + pen_prev), s[BLOCK:2 * BLOCK],
                 s[2 * BLOCK:3 * BLOCK] + (bias_next + pen_next), s_ctx]
        m = jnp.maximum(
            jnp.maximum(jnp.max(parts[0], axis=0, keepdims=True), jnp.max(parts[1], axis=0, keepdims=True)),
            jnp.maximum(jnp.max(parts[2], axis=0, keepdims=True), jnp.max(parts[3], axis=0, keepdims=True)))
        return parts, jnp.maximum(m, sink_ref[kvh, :, part * cw:(part + 1) * cw])

    def exp_weights(parts, m):
        return jnp.concatenate([jnp.exp2(p_ - m).astype(BF16) for p_ in parts], axis=0)

    def value_matmul(jb, kvh, part, p):
        v_all = jnp.concatenate(
            [vt_ext[kvh * HEAD_DIM:(kvh + 1) * HEAD_DIM, jb * BLOCK:(jb + 3) * BLOCK],
             vct[kvh * HEAD_DIM:(kvh + 1) * HEAD_DIM, :]], axis=1)
        return _dot(jnp.concatenate([v_all, ones_rows], axis=0), p)

    def normalize(jb, kvh, part, pv, m):
        denom = pv[HEAD_DIM:HEAD_DIM + 1, :] + jnp.exp2(sink_ref[kvh, :, part * cw:(part + 1) * cw] - m)
        o_t = pv[:HEAD_DIM, :] * (1.0 / denom)
        return [o_t[:, g * BLOCK:(g + 1) * BLOCK] for g in range(HEADS_PER_STEP)]

    def block_norm(out_t):
        o_all = jnp.concatenate(out_t, axis=0)
        ms = jnp.mean(o_all * o_all, axis=0, keepdims=True)
        y = o_all * lax.rsqrt(ms + EPS) * again_ref[...]
        return y.T.astype(BF16)

    aw = ATTN_HEADS * HEAD_DIM
    steps = [(jb, kvh, part) for jb in range(nblk) for kvh in range(ATTN_KV_HEADS)
             for part in range(ATTN_GROUP // HEADS_PER_STEP)]
    assert n_chunks >= len(steps)
    raw = {n: score_matmul(*steps[n]) for n in range(2)}
    proj_mlp = _dot(mlp_ref[...], wo_ref[aw:, :])
    ready = {0: mask_and_max(*steps[0], raw.pop(0))}
    attn_rows, out_t, unnormalized, gate_up = [], [], None, {}

    def collect(step, pv, m):
        out_t.extend(normalize(*step, pv, m))
        if len(out_t) == ATTN_HEADS:
            attn_rows.append(block_norm(out_t))
            out_t.clear()

    for n, step in enumerate(steps):
        if n + 1 < len(steps):
            ready[n + 1] = mask_and_max(*steps[n + 1], raw.pop(n + 1))
        if n + 2 < len(steps):
            raw[n + 2] = score_matmul(*steps[n + 2])
        gate_up[n] = ffn_matmuls(n)
        if unnormalized is not None:
            collect(*unnormalized)
        parts, m = ready.pop(n)
        unnormalized = (step, value_matmul(*step, exp_weights(parts, m)), m)
        ffn_activation(n, gate_up.pop(n))

    c_next = len(steps)
    gate_up[c_next] = ffn_matmuls(c_next)
    collect(*unnormalized)
    for c in range(c_next + 1, n_chunks):
        gate_up[c] = ffn_matmuls(c)
        ffn_activation(c - 1, gate_up.pop(c - 1))
    proj_attn = _dot(jnp.concatenate(attn_rows, axis=0), wo_ref[:aw, :])
    k_head = (n_chunks - 1) * MXU_DIM
    down = _dot(hid_ref[:, :k_head], wd_ref[:k_head, :])
    ffn_activation(n_chunks - 1, gate_up.pop(n_chunks - 1))
    h_new = x_ref[...] + mod_ref[2:3, :] * (proj_attn + proj_mlp)
    h_ref[slot_w] = h_new
    hn_ref[...] = _mod_norm(h_new, fgain_ref[...], mod_ref[3:4, :], mod_ref[4:5, :]).astype(BF16)
    down = down + _dot(hid_ref[:, k_head:], wd_ref[k_head:, :])
    o_ref[...] = h_ref[slot_r] + fmod_ref[5:6, :] * down


def _mix_ffn(x, mod3, k, vt, kc, vct, qt, mlpn, sink_rows, bias, again_b, wo_bf,
             ffn_gain, wgu_bf, wd_bf, tile):
    b, s, d = x.shape
    c = kc.shape[1]
    aw = ATTN_HEADS * HEAD_DIM
    mw = mlpn.shape[2]
    ff = wd_bf.shape[0]
    assert ff % MXU_DIM == 0
    r = tile // BLOCK
    nb = s // BLOCK
    nt = s // tile
    n_tiles = b * nt
    gq = ATTN_GROUP * BLOCK

    def mix_tile(g):
        t = jnp.minimum(g, n_tiles - 1)
        return t // nt, t % nt

    def ffn_tile(g):
        t = jnp.maximum(g - 1, 0)
        return t // nt, t % nt

    def at_mix(fn):
        return lambda g: fn(*mix_tile(g))

    const = lambda shape: pl.BlockSpec(shape, lambda g: (0,) * len(shape))
    return pl.pallas_call(
        functools.partial(_mix_ffn_kernel, nt),
        grid=(n_tiles + 1,),
        in_specs=[pl.BlockSpec((None, tile, d), at_mix(lambda bb, i: (bb, i, 0))),
                  pl.BlockSpec((None, N_MOD, d), at_mix(lambda bb, i: (bb, 0, 0))),
                  pl.BlockSpec((None, BLOCK, LANES), at_mix(lambda bb, i: (bb, jnp.maximum(i * r - 1, 0), 0))),
                  pl.BlockSpec((None, tile, LANES), at_mix(lambda bb, i: (bb, i, 0))),
                  pl.BlockSpec((None, BLOCK, LANES),
                               at_mix(lambda bb, i: (bb, jnp.minimum((i + 1) * r, nb - 1), 0))),
                  pl.BlockSpec((None, LANES, BLOCK), at_mix(lambda bb, i: (bb, 0, jnp.maximum(i * r - 1, 0)))),
                  pl.BlockSpec((None, LANES, tile), at_mix(lambda bb, i: (bb, 0, i))),
                  pl.BlockSpec((None, LANES, BLOCK),
                               at_mix(lambda bb, i: (bb, 0, jnp.minimum((i + 1) * r, nb - 1)))),
                  pl.BlockSpec((None, c, LANES), at_mix(lambda bb, i: (bb, 0, 0))),
                  pl.BlockSpec((None, LANES, c), at_mix(lambda bb, i: (bb, 0, 0))),
                  pl.BlockSpec((None, ATTN_KV_HEADS, HEAD_DIM, ATTN_GROUP * tile),
                               at_mix(lambda bb, i: (bb, 0, 0, i))),
                  pl.BlockSpec((None, tile, mw), at_mix(lambda bb, i: (bb, i, 0))),
                  const((ATTN_KV_HEADS, 1, gq)),
                  const((2 * BLOCK, gq)),
                  const((aw, BLOCK)),
                  const((aw + mw, d)),
                  pl.BlockSpec((None, N_MOD, d), lambda g: (ffn_tile(g)[0], 0, 0)),
                  const((1, d)), const((d, 2 * ff)), const((ff, d))],
        out_specs=pl.BlockSpec((None, tile, d), lambda g: (*ffn_tile(g), 0)),
        out_shape=jax.ShapeDtypeStruct((b, s, d), F32),
        scratch_shapes=[pltpu.VMEM((2, tile, d), F32), pltpu.VMEM((tile, d), BF16),
                        pltpu.VMEM((tile, ff), BF16)],
        compiler_params=pltpu.CompilerParams(
            dimension_semantics=("arbitrary",), vmem_limit_bytes=VMEM_LIMIT),
        name="mix_ffn",
    )(x, mod3, k, k, k, vt, vt, vt, kc, vct, qt, mlpn, sink_rows, bias, again_b, wo_bf,
      mod3, ffn_gain, wgu_bf, wd_bf)


def _rope_tables(s):
    axis_dim = HEAD_DIM // 2
    pos = np.arange(s)
    inv_freq = (ROPE_THETA ** (-np.arange(0, axis_dim, 2, dtype=np.float32) / axis_dim)).astype(np.float32)
    ang_r = (pos // GRID_W).astype(np.float32)[:, None] * inv_freq[None, :]
    ang_c = (pos % GRID_W).astype(np.float32)[:, None] * inv_freq[None, :]
    cr, sr, cc, sc = np.cos(ang_r), np.sin(ang_r), np.cos(ang_c), np.sin(ang_c)
    z = np.zeros_like(sr)
    reps = LANES // HEAD_DIM
    cos = np.tile(np.concatenate([cr, cr, cc, cc], axis=1), (1, reps))
    sin_up = np.tile(np.concatenate([-sr, z, -sc, z], axis=1), (1, reps))
    sin_dn = np.tile(np.concatenate([z, sr, z, sc], axis=1), (1, reps))
    return jnp.asarray(cos, F32), jnp.asarray(sin_up, F32), jnp.asarray(sin_dn, F32)


def _window_bias():
    c = np.arange(BLOCK)[:, None]
    r = np.arange(BLOCK)[None, :]
    prev = np.where(c >= r, 0.0, MASKED).astype(np.float32)
    nxt = np.where(c <= r, 0.0, MASKED).astype(np.float32)
    return jnp.asarray(np.tile(np.concatenate([prev, nxt], axis=0), (1, ATTN_GROUP)), F32)


def _head_mean_matrix():
    bd = np.kron(np.eye(MXU_DIM // HEAD_DIM, dtype=np.float32),
                 np.full((HEAD_DIM, HEAD_DIM), 1.0 / HEAD_DIM, np.float32))
    return jnp.asarray(bd, BF16)


def kernel(x, c, ctx, c_ctx, w_mod, b_mod, norm_mix, norm_ffn, w_in, q_gain, k_gain, attn_sink,
           gate_gain, w_spatial, b_spatial, attn_out_gain, mlp_out_gain, w_out, w_gate_up, w_down):
    b, s, d = x.shape
    assert w_mod.shape[0] == 1, "single-layer problem"
    assert s % 1024 == 0 and d % LANES == 0
    aw = ATTN_HEADS * HEAD_DIM
    mw = MLP_HEADS * HEAD_DIM

    rows = -(-(b + 1) // BF16_SUBLANES) * BF16_SUBLANES
    cond = jnp.concatenate([c, c_ctx[None, :], jnp.zeros((rows - b - 1, d), F32)], axis=0)
    mod3 = _adaln(cond, w_mod[0], b_mod[0][None, :]).reshape(rows, N_MOD, d)

    w_in_bf = w_in[0].astype(BF16)
    bd = _head_mean_matrix()
    kgain2 = jnp.tile(k_gain[0], ATTN_KV_HEADS)[None, :]
    qgain = (jnp.tile(q_gain[0], ATTN_HEADS) * (HEAD_DIM ** -0.5 * LOG2E))[None, :]
    norm_mix_g = norm_mix[0][None, :]

    kc, vct = _ctx_kv(ctx, mod3, b, norm_mix_g, w_in_bf, kgain2, bd)

    cos, sup, sdn = _rope_tables(s)
    bs_full = jnp.repeat(b_spatial[0].T, HEAD_DIM, axis=1)
    k, vt, qt, mlpn, wo_bf, wgu_bf, wd_bf = _inproj(
        x, mod3, norm_mix_g, w_in_bf, cos, sup, sdn, qgain, kgain2,
        gate_gain[0].reshape(1, mw), w_spatial[0].astype(BF16), bs_full,
        mlp_out_gain[0][None, :], bd, (w_out[0], w_gate_up[0], w_down[0]), tile=1024)

    sink_rows = jnp.repeat(attn_sink[0].reshape(ATTN_KV_HEADS, ATTN_GROUP) * LOG2E,
                           BLOCK, axis=1)[:, None, :]
    again_b = jnp.broadcast_to(attn_out_gain[0][:, None], (aw, BLOCK))
    return _mix_ffn(x, mod3, k, vt, kc, vct, qt, mlpn, sink_rows, _window_bias(), again_b,
                    wo_bf, norm_ffn[0][None, :], wgu_bf, wd_bf, tile=512)
```

```python
import functools
import math

import jax
import jax.numpy as jnp
import numpy as np
from jax import lax
from jax.experimental import pallas as pl
from jax.experimental.pallas import tpu as pltpu

F32 = jnp.float32
BF16 = jnp.bfloat16

HEAD_DIM = 64
ATTN_HEADS = 8
ATTN_KV_HEADS = 2
ATTN_GROUP = ATTN_HEADS // ATTN_KV_HEADS
MLP_HEADS = 8
N_MOD = 6
BLOCK = 128
GRID_W = 64
ROPE_THETA = 10000.0
EPS = 1e-6
MASKED = -1e30
LOG2E = math.log2(math.e)

LANES = 128
BF16_SUBLANES = 16
MXU_DIM = 256
VMEM_LIMIT = 56 * 1024 * 1024
HEADS_PER_STEP = 4


def _dot(a, b):
    return jnp.dot(a, b, preferred_element_type=F32)


def _silu(x):
    return x * (1.0 / (1.0 + jnp.exp(-x)))


def _gelu2_tanh(x):
    c = math.sqrt(2.0 / math.pi)
    return x * (1.0 + jnp.tanh(x * (c + (c * 0.044715) * (x * x))))


def _mod_norm(x, gain, shift, scale):
    y = x * lax.rsqrt(jnp.mean(x * x, axis=-1, keepdims=True) + EPS)
    return y * (gain * (1.0 + scale)) + shift


def _head_ms(x, bd):
    return _dot((x * x).astype(BF16), bd)


def _rope(x, cos, sin_up, sin_dn):
    up = pltpu.roll(x, LANES - 16, 1)
    dn = pltpu.roll(x, 16, 1)
    return x * cos + up * sin_up + dn * sin_dn


def _split_bf16(x):
    hi = x.astype(BF16)
    return hi, (x - hi.astype(F32)).astype(BF16)


def _adaln_kernel(cond_ref, w_ref, b_ref, o_ref):
    rows = cond_ref.shape[0]

    @pl.when(pl.program_id(0) == 0)
    def _():
        o_ref[...] = jnp.broadcast_to(b_ref[...], o_ref.shape)

    s_hi, s_lo = _split_bf16(_silu(cond_ref[...]))
    w_hi, w_lo = _split_bf16(w_ref[...])
    both = _dot(jnp.concatenate([s_hi, s_lo], axis=0), w_hi)
    o_ref[...] += both[:rows] + both[rows:] + _dot(s_hi, w_lo)


def _adaln(cond, w_mod, b_mod):
    rows, d = cond.shape
    n = w_mod.shape[1]
    tk = d // 8
    assert rows % BF16_SUBLANES == 0 and tk % LANES == 0
    return pl.pallas_call(
        _adaln_kernel,
        grid=(d // tk,),
        in_specs=[pl.BlockSpec((rows, tk), lambda j: (0, j)),
                  pl.BlockSpec((tk, n), lambda j: (j, 0)),
                  pl.BlockSpec((1, n), lambda j: (0, 0))],
        out_specs=pl.BlockSpec((rows, n), lambda j: (0, 0)),
        out_shape=jax.ShapeDtypeStruct((rows, n), F32),
        compiler_params=pltpu.CompilerParams(
            dimension_semantics=("arbitrary",), vmem_limit_bytes=VMEM_LIMIT),
        name="adaln",
    )(cond, w_mod, b_mod)


def _ctx_kernel(x_ref, mod_ref, gain_ref, w_ref, kgain_ref, bd_ref, kc_ref, vct_ref):
    hn = _mod_norm(x_ref[...], gain_ref[...], mod_ref[0:1, :], mod_ref[1:2, :])
    kv = _dot(hn.astype(BF16), w_ref[...])
    k = kv[:, :LANES]
    k = k * lax.rsqrt(_head_ms(k, bd_ref[:LANES, :LANES]) + EPS) * kgain_ref[...]
    kc_ref[...] = k.astype(BF16)
    vct_ref[...] = kv[:, LANES:].T.astype(BF16)


def _ctx_kv(ctx, mod3, ctx_row, norm_gain, w_in_bf, kgain2, bd):
    b, c, d = ctx.shape
    kvw = 2 * ATTN_KV_HEADS * HEAD_DIM
    return pl.pallas_call(
        _ctx_kernel,
        grid=(b,),
        in_specs=[pl.BlockSpec((None, c, d), lambda i: (i, 0, 0)),
                  pl.BlockSpec((None, N_MOD, d), lambda i: (ctx_row, 0, 0)),
                  pl.BlockSpec((1, d), lambda i: (0, 0)),
                  pl.BlockSpec((d, kvw), lambda i: (0, 0)),
                  pl.BlockSpec((1, LANES), lambda i: (0, 0)),
                  pl.BlockSpec((MXU_DIM, MXU_DIM), lambda i: (0, 0))],
        out_specs=[pl.BlockSpec((None, c, LANES), lambda i: (i, 0, 0)),
                   pl.BlockSpec((None, LANES, c), lambda i: (i, 0, 0))],
        out_shape=[jax.ShapeDtypeStruct((b, c, LANES), BF16),
                   jax.ShapeDtypeStruct((b, LANES, c), BF16)],
        compiler_params=pltpu.CompilerParams(
            dimension_semantics=("arbitrary",), vmem_limit_bytes=VMEM_LIMIT),
        name="ctx_kv",
    )(ctx, mod3, norm_gain, w_in_bf, kgain2, bd)


def _inproj_kernel(x_ref, mod_ref, gain_ref, w_ref, cos_ref, sup_ref, sdn_ref,
                   qgain_ref, kgain_ref, ggain_ref, ws_ref, bs_ref, ogain_ref, bd_ref,
                   wo_f32_ref, wgu_f32_ref, wd_f32_ref,
                   k_ref, vt_ref, qt_ref, mlp_ref, wo_bf_ref, wgu_bf_ref, wd_bf_ref):
    wo_bf_ref[...] = wo_f32_ref[...].astype(BF16)
    wgu_bf_ref[...] = wgu_f32_ref[...].astype(BF16)
    wd_bf_ref[...] = wd_f32_ref[...].astype(BF16)

    t = x_ref.shape[0]
    nblk = t // BLOCK
    kvw = ATTN_KV_HEADS * HEAD_DIM
    aw = ATTN_HEADS * HEAD_DIM
    mw = MLP_HEADS * HEAD_DIM
    q_cols = slice(2 * kvw, 2 * kvw + aw)
    kv_cols = slice(0, 2 * kvw)
    u_cols = slice(2 * kvw + aw, 2 * kvw + aw + mw)
    g_cols = slice(2 * kvw + aw + mw, 2 * kvw + aw + 2 * mw)
    halves = range(aw // MXU_DIM)
    bd = bd_ref[...]

    def project(cols):
        return _dot(hn, w_ref[:, cols])

    def finish_qkv(q, kv, q_ms, k_ms):
        cos, sup, sdn = cos_ref[...], sup_ref[...], sdn_ref[...]
        k = kv[:, :kvw] * lax.rsqrt(k_ms + EPS) * kgain_ref[...]
        k_ref[...] = _rope(k, cos, sup, sdn).astype(BF16)
        vt_ref[...] = kv[:, kvw:].T.astype(BF16)
        for half in halves:
            qh = q[:, half * MXU_DIM:(half + 1) * MXU_DIM] * lax.rsqrt(q_ms[half] + EPS)
            qh = qh * qgain_ref[:, half * MXU_DIM:(half + 1) * MXU_DIM]
            for sl in range(MXU_DIM // LANES):
                qs = _rope(qh[:, sl * LANES:(sl + 1) * LANES], cos, sup, sdn)
                qst = qs.T
                for hh in range(LANES // HEAD_DIM):
                    head = (half * MXU_DIM + sl * LANES) // HEAD_DIM + hh
                    kvh, grp = head // ATTN_GROUP, head % ATTN_GROUP
                    for jb in range(nblk):
                        col = (jb * ATTN_GROUP + grp) * BLOCK
                        qt_ref[kvh, :, col:col + BLOCK] = qst[
                            hh * HEAD_DIM:(hh + 1) * HEAD_DIM, jb * BLOCK:(jb + 1) * BLOCK].astype(BF16)

    def gate_mix(g_raw):
        g = _gelu2_tanh(g_raw)
        g_ms = [_head_ms(g[:, h * MXU_DIM:(h + 1) * MXU_DIM], bd) for h in halves]
        gn = jnp.concatenate(
            [g[:, h * MXU_DIM:(h + 1) * MXU_DIM] * lax.rsqrt(g_ms[h] + 4.0 * EPS) for h in halves],
            axis=1) * ggain_ref[...]
        gnb = gn.astype(BF16)
        low_head = lax.broadcasted_iota(jnp.int32, (BLOCK, t), 1) % LANES < HEAD_DIM
        mixed_slabs = []
        for p in range(MLP_HEADS // 2):
            rhs = jnp.concatenate(
                [gnb[c * BLOCK:(c + 1) * BLOCK, p * LANES:(p + 1) * LANES] for c in range(nblk)], axis=1)
            a = _dot(ws_ref[2 * p], rhs)
            b = _dot(ws_ref[2 * p + 1], rhs)
            mixed_slabs.append(jnp.where(low_head, a, b))
        return mixed_slabs

    def finish_mlp(u_raw, mixed_slabs):
        u = _gelu2_tanh(u_raw)
        rows = []
        for c in range(nblk):
            mixed_c = jnp.concatenate(
                [m[:, c * LANES:(c + 1) * LANES] for m in mixed_slabs], axis=1) + bs_ref[...]
            rows.append(u[c * BLOCK:(c + 1) * BLOCK, :] * mixed_c)
        o = jnp.concatenate(rows, axis=0)
        o = o * lax.rsqrt(jnp.mean(o * o, axis=-1, keepdims=True) + 4.0 * EPS) * ogain_ref[...]
        mlp_ref[...] = o.astype(BF16)

    hn = _mod_norm(x_ref[...], gain_ref[...], mod_ref[0:1, :], mod_ref[1:2, :]).astype(BF16)
    q, kv = project(q_cols), project(kv_cols)
    q_ms = [_head_ms(q[:, h * MXU_DIM:(h + 1) * MXU_DIM], bd) for h in halves]
    k_ms = _head_ms(kv[:, :kvw], bd[:kvw, :kvw])
    g_raw = project(g_cols)
    finish_qkv(q, kv, q_ms, k_ms)
    u_raw = project(u_cols)
    finish_mlp(u_raw, gate_mix(g_raw))


def _inproj(x, mod3, norm_gain, w_in_bf, cos, sup, sdn, qgain, kgain2, ggain, ws_bf, bs_full,
            ogain, bd, later_weights, tile):
    b, s, d = x.shape
    inw = w_in_bf.shape[1]
    aw = ATTN_HEADS * HEAD_DIM
    mw = MLP_HEADS * HEAD_DIM
    n_steps = (s // tile) * b
    const = lambda shape: pl.BlockSpec(shape, lambda i, bb: (0,) * len(shape))

    def slab_spec(w):
        rows = next(r for r in range(BF16_SUBLANES, w.shape[0] + 1, BF16_SUBLANES)
                    if w.shape[0] % r == 0 and w.shape[0] // r <= n_steps)
        n_slabs = w.shape[0] // rows
        return pl.BlockSpec((rows, w.shape[1]), lambda i, bb: (jnp.minimum(i * b + bb, n_slabs - 1), 0))

    slab_specs = [slab_spec(w) for w in later_weights]
    table_spec = pl.BlockSpec((tile, LANES), lambda i, bb: (i, 0))
    return pl.pallas_call(
        _inproj_kernel,
        grid=(s // tile, b),
        in_specs=[pl.BlockSpec((None, tile, d), lambda i, bb: (bb, i, 0)),
                  pl.BlockSpec((None, N_MOD, d), lambda i, bb: (bb, 0, 0)),
                  const((1, d)),
                  const((d, inw)),
                  table_spec, table_spec, table_spec,
                  const((1, aw)), const((1, LANES)), const((1, mw)),
                  const((MLP_HEADS, BLOCK, BLOCK)), const((BLOCK, mw)), const((1, mw)),
                  const((MXU_DIM, MXU_DIM))] + slab_specs,
        out_specs=[pl.BlockSpec((None, tile, LANES), lambda i, bb: (bb, i, 0)),
                   pl.BlockSpec((None, LANES, tile), lambda i, bb: (bb, 0, i)),
                   pl.BlockSpec((None, ATTN_KV_HEADS, HEAD_DIM, ATTN_GROUP * tile),
                                lambda i, bb: (bb, 0, 0, i)),
                   pl.BlockSpec((None, tile, mw), lambda i, bb: (bb, i, 0))] + slab_specs,
        out_shape=[jax.ShapeDtypeStruct((b, s, LANES), BF16),
                   jax.ShapeDtypeStruct((b, LANES, s), BF16),
                   jax.ShapeDtypeStruct((b, ATTN_KV_HEADS, HEAD_DIM, ATTN_GROUP * s), BF16),
                   jax.ShapeDtypeStruct((b, s, mw), BF16)]
        + [jax.ShapeDtypeStruct(w.shape, BF16) for w in later_weights],
        compiler_params=pltpu.CompilerParams(
            dimension_semantics=("arbitrary", "arbitrary"), vmem_limit_bytes=VMEM_LIMIT),
        name="inproj",
    )(x, mod3, norm_gain, w_in_bf, cos, sup, sdn, qgain, kgain2, ggain, ws_bf, bs_full, ogain, bd,
      *later_weights)


def _mix_ffn_kernel(tiles_per_seq,
                    x_ref, mod_ref, kp_ref, km_ref, kn_ref, vp_ref, vm_ref, vn_ref,
                    kc_ref, vct_ref, qt_ref, mlp_ref, sink_ref, bias_ref, again_ref, wo_ref,
                    fmod_ref, fgain_ref, wgu_ref, wd_ref,
                    o_ref, h_ref, hid_ref):
    tq = x_ref.shape[0]
    nblk = tq // BLOCK
    step_id = pl.program_id(0)
    n_tiles = pl.num_programs(0) - 1
    i = lax.rem(jnp.minimum(step_id, n_tiles - 1), tiles_per_seq)
    last = tiles_per_seq - 1
    gq = ATTN_GROUP * BLOCK
    slot_w = lax.rem(step_id, 2)
    slot_r = 1 - slot_w

    @pl.when(step_id == 0)
    def _():
        h_ref[1] = jnp.zeros(h_ref.shape[1:], F32)

    ff = wd_ref.shape[0]
    n_chunks = ff // MXU_DIM
    h_in = h_ref[slot_r]
    hn = _mod_norm(h_in, fgain_ref[...], fmod_ref[3:4, :], fmod_ref[4:5, :]).astype(BF16)

    def ffn_matmuls(c):
        return (_dot(hn, wgu_ref[:, c * MXU_DIM:(c + 1) * MXU_DIM]),
                _dot(hn, wgu_ref[:, ff + c * MXU_DIM:ff + (c + 1) * MXU_DIM]))

    def ffn_activation(c, gate_up):
        a, b = gate_up
        hid_ref[:, c * MXU_DIM:(c + 1) * MXU_DIM] = (_silu(a) * b).astype(BF16)

    k_ext = jnp.concatenate([kp_ref[...], km_ref[...], kn_ref[...]], axis=0)
    vt_ext = jnp.concatenate([vp_ref[...], vm_ref[...], vn_ref[...]], axis=1)
    kc = kc_ref[...]
    vct = vct_ref[...]
    cw = HEADS_PER_STEP * BLOCK
    zeros_q = jnp.zeros((HEAD_DIM, cw), BF16)
    ones_rows = jnp.ones((BF16_SUBLANES, 3 * BLOCK + kc.shape[0]), BF16)
    bias_prev = bias_ref[0:BLOCK, :cw]
    bias_next = bias_ref[BLOCK:2 * BLOCK, :cw]

    def score_matmul(jb, kvh, part):
        qt = qt_ref[kvh, :, jb * gq + part * cw:jb * gq + (part + 1) * cw]
        rhs = jnp.concatenate([qt, zeros_q] if kvh == 0 else [zeros_q, qt], axis=0)
        return _dot(kc, rhs), _dot(k_ext[jb * BLOCK:(jb + 3) * BLOCK, :], rhs)

    def mask_and_max(jb, kvh, part, s):
        pen_prev = jnp.where(i == 0, MASKED, 0.0) if jb == 0 else 0.0
        pen_next = jnp.where(i == last, MASKED, 0.0) if jb == nblk - 1 else 0.0
        s_ctx, s = s
        parts = [s[0:BLOCK] + (bias_prev + pen_prev), s[BLOCK:2 * BLOCK],
                 s[2 * BLOCK:3 * BLOCK] + (bias_next + pen_next), s_ctx]
        m = jnp.maximum(
            jnp.maximum(jnp.max(parts[0], axis=0, keepdims=True), jnp.max(parts[1], axis=0, keepdims=True)),
            jnp.maximum(jnp.max(parts[2], axis=0, keepdims=True), jnp.max(parts[3], axis=0, keepdims=True)))
        return parts, jnp.maximum(m, sink_ref[kvh, :, part * cw:(part + 1) * cw])

    def exp_weights(parts, m):
        return jnp.concatenate([jnp.exp2((p_ - m).astype(BF16)) for p_ in parts], axis=0)

    def value_matmul(jb, kvh, part, p):
        v_all = jnp.concatenate(
            [vt_ext[kvh * HEAD_DIM:(kvh + 1) * HEAD_DIM, jb * BLOCK:(jb + 3) * BLOCK],
             vct[kvh * HEAD_DIM:(kvh + 1) * HEAD_DIM, :]], axis=1)
        return _dot(jnp.concatenate([v_all, ones_rows], axis=0), p)

    def normalize(jb, kvh, part, pv, m):
        denom = pv[HEAD_DIM:HEAD_DIM + 1, :] + jnp.exp2(sink_ref[kvh, :, part * cw:(part + 1) * cw] - m)
        o_t = pv[:HEAD_DIM, :] * (1.0 / denom)
        return [o_t[:, g * BLOCK:(g + 1) * BLOCK] for g in range(HEADS_PER_STEP)]

    def block_norm(out_t):
        o_all = jnp.concatenate(out_t, axis=0)
        ms = jnp.mean(o_all * o_all, axis=0, keepdims=True)
        y = o_all * lax.rsqrt(ms + EPS) * again_ref[...]
        return y.T.astype(BF16)

    aw = ATTN_HEADS * HEAD_DIM
    steps = [(jb, kvh, part) for jb in range(nblk) for kvh in range(ATTN_KV_HEADS)
             for part in range(ATTN_GROUP // HEADS_PER_STEP)]
    assert n_chunks >= len(steps)
    raw = {n: score_matmul(*steps[n]) for n in range(2)}
    proj_mlp = _dot(mlp_ref[...], wo_ref[aw:, :])
    ready = {0: mask_and_max(*steps[0], raw.pop(0))}
    attn_rows, out_t, unnormalized, gate_up = [], [], None, {}

    def collect(step, pv, m):
        out_t.extend(normalize(*step, pv, m))
        if len(out_t) == ATTN_HEADS:
            attn_rows.append(block_norm(out_t))
            out_t.clear()

    for n, step in enumerate(steps):
        if n + 1 < len(steps):
            ready[n + 1] = mask_and_max(*steps[n + 1], raw.pop(n + 1))
        if n + 2 < len(steps):
            raw[n + 2] = score_matmul(*steps[n + 2])
        gate_up[n] = ffn_matmuls(n)
        if unnormalized is not None:
            collect(*unnormalized)
            ffn_activation(n - 1, gate_up.pop(n - 1))
        parts, m = ready.pop(n)
        unnormalized = (step, value_matmul(*step, exp_weights(parts, m)), m)

    c_next = len(steps)
    gate_up[c_next] = ffn_matmuls(c_next)
    collect(*unnormalized)
    ffn_activation(c_next - 1, gate_up.pop(c_next - 1))
    for c in range(c_next + 1, n_chunks):
        gate_up[c] = ffn_matmuls(c)
        ffn_activation(c - 1, gate_up.pop(c - 1))
    proj_attn = _dot(jnp.concatenate(attn_rows, axis=0), wo_ref[:aw, :])
    ffn_activation(n_chunks - 1, gate_up.pop(n_chunks - 1))
    h_ref[slot_w] = x_ref[...] + mod_ref[2:3, :] * (proj_attn + proj_mlp)
    o_ref[...] = h_in + fmod_ref[5:6, :] * _dot(hid_ref[...], wd_ref[...])


def _mix_ffn(x, mod3, k, vt, kc, vct, qt, mlpn, sink_rows, bias, again_b, wo_bf,
             ffn_gain, wgu_bf, wd_bf, tile):
    b, s, d = x.shape
    c = kc.shape[1]
    aw = ATTN_HEADS * HEAD_DIM
    mw = mlpn.shape[2]
    ff = wd_bf.shape[0]
    assert ff % MXU_DIM == 0
    r = tile // BLOCK
    nb = s // BLOCK
    nt = s // tile
    n_tiles = b * nt
    gq = ATTN_GROUP * BLOCK

    def mix_tile(g):
        t = jnp.minimum(g, n_tiles - 1)
        return t // nt, t % nt

    def ffn_tile(g):
        t = jnp.maximum(g - 1, 0)
        return t // nt, t % nt

    def at_mix(fn):
        return lambda g: fn(*mix_tile(g))

    const = lambda shape: pl.BlockSpec(shape, lambda g: (0,) * len(shape))
    return pl.pallas_call(
        functools.partial(_mix_ffn_kernel, nt),
        grid=(n_tiles + 1,),
        in_specs=[pl.BlockSpec((None, tile, d), at_mix(lambda bb, i: (bb, i, 0))),
                  pl.BlockSpec((None, N_MOD, d), at_mix(lambda bb, i: (bb, 0, 0))),
                  pl.BlockSpec((None, BLOCK, LANES), at_mix(lambda bb, i: (bb, jnp.maximum(i * r - 1, 0), 0))),
                  pl.BlockSpec((None, tile, LANES), at_mix(lambda bb, i: (bb, i, 0))),
                  pl.BlockSpec((None, BLOCK, LANES),
                               at_mix(lambda bb, i: (bb, jnp.minimum((i + 1) * r, nb - 1), 0))),
                  pl.BlockSpec((None, LANES, BLOCK), at_mix(lambda bb, i: (bb, 0, jnp.maximum(i * r - 1, 0)))),
                  pl.BlockSpec((None, LANES, tile), at_mix(lambda bb, i: (bb, 0, i))),
                  pl.BlockSpec((None, LANES, BLOCK),
                               at_mix(lambda bb, i: (bb, 0, jnp.minimum((i + 1) * r, nb - 1)))),
                  pl.BlockSpec((None, c, LANES), at_mix(lambda bb, i: (bb, 0, 0))),
                  pl.BlockSpec((None, LANES, c), at_mix(lambda bb, i: (bb, 0, 0))),
                  pl.BlockSpec((None, ATTN_KV_HEADS, HEAD_DIM, ATTN_GROUP * tile),
                               at_mix(lambda bb, i: (bb, 0, 0, i))),
                  pl.BlockSpec((None, tile, mw), at_mix(lambda bb, i: (bb, i, 0))),
                  const((ATTN_KV_HEADS, 1, gq)),
                  const((2 * BLOCK, gq)),
                  const((aw, BLOCK)),
                  const((aw + mw, d)),
                  pl.BlockSpec((None, N_MOD, d), lambda g: (ffn_tile(g)[0], 0, 0)),
                  const((1, d)), const((d, 2 * ff)), const((ff, d))],
        out_specs=pl.BlockSpec((None, tile, d), lambda g: (*ffn_tile(g), 0)),
        out_shape=jax.ShapeDtypeStruct((b, s, d), F32),
        scratch_shapes=[pltpu.VMEM((2, tile, d), F32), pltpu.VMEM((tile, ff), BF16)],
        compiler_params=pltpu.CompilerParams(
            dimension_semantics=("arbitrary",), vmem_limit_bytes=VMEM_LIMIT),
        name="mix_ffn",
    )(x, mod3, k, k, k, vt, vt, vt, kc, vct, qt, mlpn, sink_rows, bias, again_b, wo_bf,
      mod3, ffn_gain, wgu_bf, wd_bf)


def _rope_tables(s):
    axis_dim = HEAD_DIM // 2
    pos = np.arange(s)
    inv_freq = (ROPE_THETA ** (-np.arange(0, axis_dim, 2, dtype=np.float32) / axis_dim)).astype(np.float32)
    ang_r = (pos // GRID_W).astype(np.float32)[:, None] * inv_freq[None, :]
    ang_c = (pos % GRID_W).astype(np.float32)[:, None] * inv_freq[None, :]
    cr, sr, cc, sc = np.cos(ang_r), np.sin(ang_r), np.cos(ang_c), np.sin(ang_c)
    z = np.zeros_like(sr)
    reps = LANES // HEAD_DIM
    cos = np.tile(np.concatenate([cr, cr, cc, cc], axis=1), (1, reps))
    sin_up = np.tile(np.concatenate([-sr, z, -sc, z], axis=1), (1, reps))
    sin_dn = np.tile(np.concatenate([z, sr, z, sc], axis=1), (1, reps))
    return jnp.asarray(cos, F32), jnp.asarray(sin_up, F32), jnp.asarray(sin_dn, F32)


def _window_bias():
    c = np.arange(BLOCK)[:, None]
    r = np.arange(BLOCK)[None, :]
    prev = np.where(c >= r, 0.0, MASKED).astype(np.float32)
    nxt = np.where(c <= r, 0.0, MASKED).astype(np.float32)
    return jnp.asarray(np.tile(np.concatenate([prev, nxt], axis=0), (1, ATTN_GROUP)), F32)


def _head_mean_matrix():
    bd = np.kron(np.eye(MXU_DIM // HEAD_DIM, dtype=np.float32),
                 np.full((HEAD_DIM, HEAD_DIM), 1.0 / HEAD_DIM, np.float32))
    return jnp.asarray(bd, BF16)


def kernel(x, c, ctx, c_ctx, w_mod, b_mod, norm_mix, norm_ffn, w_in, q_gain, k_gain, attn_sink,
           gate_gain, w_spatial, b_spatial, attn_out_gain, mlp_out_gain, w_out, w_gate_up, w_down):
    b, s, d = x.shape
    assert w_mod.shape[0] == 1, "single-layer problem"
    assert s % 1024 == 0 and d % LANES == 0
    aw = ATTN_HEADS * HEAD_DIM
    mw = MLP_HEADS * HEAD_DIM

    rows = -(-(b + 1) // BF16_SUBLANES) * BF16_SUBLANES
    cond = jnp.concatenate([c, c_ctx[None, :], jnp.zeros((rows - b - 1, d), F32)], axis=0)
    mod3 = _adaln(cond, w_mod[0], b_mod[0][None, :]).reshape(rows, N_MOD, d)

    w_in_bf = w_in[0].astype(BF16)
    bd = _head_mean_matrix()
    kgain2 = jnp.tile(k_gain[0], ATTN_KV_HEADS)[None, :]
    qgain = (jnp.tile(q_gain[0], ATTN_HEADS) * (HEAD_DIM ** -0.5 * LOG2E))[None, :]
    norm_mix_g = norm_mix[0][None, :]

    kc, vct = _ctx_kv(ctx, mod3, b, norm_mix_g, w_in_bf, kgain2, bd)

    cos, sup, sdn = _rope_tables(s)
    bs_full = jnp.repeat(b_spatial[0].T, HEAD_DIM, axis=1)
    k, vt, qt, mlpn, wo_bf, wgu_bf, wd_bf = _inproj(
        x, mod3, norm_mix_g, w_in_bf, cos, sup, sdn, qgain, kgain2,
        gate_gain[0].reshape(1, mw), w_spatial[0].astype(BF16), bs_full,
        mlp_out_gain[0][None, :], bd, (w_out[0], w_gate_up[0], w_down[0]), tile=1024)

    sink_rows = jnp.repeat(attn_sink[0].reshape(ATTN_KV_HEADS, ATTN_GROUP) * LOG2E,
                           BLOCK, axis=1)[:, None, :]
    again_b = jnp.broadcast_to(attn_out_gain[0][:, None], (aw, BLOCK))
    return _mix_ffn(x, mod3, k, vt, kc, vct, qt, mlpn, sink_rows, _window_bias(), again_b,
                    wo_bf, norm_ffn[0][None, :], wgu_bf, wd_bf, tile=512)
```

```python
import functools
import math

import jax
import jax.numpy as jnp
import numpy as np
from jax import lax
from jax.experimental import pallas as pl
from jax.experimental.pallas import tpu as pltpu

F32 = jnp.float32
BF16 = jnp.bfloat16

HEAD_DIM = 64
ATTN_HEADS = 8
ATTN_KV_HEADS = 2
ATTN_GROUP = ATTN_HEADS // ATTN_KV_HEADS
MLP_HEADS = 8
N_MOD = 6
BLOCK = 128
GRID_W = 64
ROPE_THETA = 10000.0
EPS = 1e-6
MASKED = -1e30
BOUND_MARGIN = 1.05
MAX_SAFE_SHIFT = 40.0
LOG2E = math.log2(math.e)

LANES = 128
BF16_SUBLANES = 16
MXU_DIM = 256
VMEM_LIMIT = 56 * 1024 * 1024
HEADS_PER_STEP = 4


def _dot(a, b):
    return jnp.dot(a, b, preferred_element_type=F32)


def _silu(x):
    return x * (1.0 / (1.0 + jnp.exp(-x)))


def _gelu2_tanh(x):
    c = math.sqrt(2.0 / math.pi)
    return x * (1.0 + jnp.tanh(x * (c + (c * 0.044715) * (x * x))))


def _mod_norm(x, gain, shift, scale):
    y = x * lax.rsqrt(jnp.mean(x * x, axis=-1, keepdims=True) + EPS)
    return y * (gain * (1.0 + scale)) + shift


def _head_ms(x, bd):
    return _dot((x * x).astype(BF16), bd)


def _rope(x, cos, sin_up, sin_dn):
    up = pltpu.roll(x, LANES - 16, 1)
    dn = pltpu.roll(x, 16, 1)
    return x * cos + up * sin_up + dn * sin_dn


def _split_bf16(x):
    hi = x.astype(BF16)
    return hi, (x - hi.astype(F32)).astype(BF16)


def _adaln_kernel(cond_ref, w_ref, b_ref, o_ref):
    rows = cond_ref.shape[0]

    @pl.when(pl.program_id(0) == 0)
    def _():
        o_ref[...] = jnp.broadcast_to(b_ref[...], o_ref.shape)

    s_hi, s_lo = _split_bf16(_silu(cond_ref[...]))
    w_hi, w_lo = _split_bf16(w_ref[...])
    both = _dot(jnp.concatenate([s_hi, s_lo], axis=0), w_hi)
    o_ref[...] += both[:rows] + both[rows:] + _dot(s_hi, w_lo)


def _adaln(cond, w_mod, b_mod):
    rows, d = cond.shape
    n = w_mod.shape[1]
    tk = d // 8
    assert rows % BF16_SUBLANES == 0 and tk % LANES == 0
    return pl.pallas_call(
        _adaln_kernel,
        grid=(d // tk,),
        in_specs=[pl.BlockSpec((rows, tk), lambda j: (0, j)),
                  pl.BlockSpec((tk, n), lambda j: (j, 0)),
                  pl.BlockSpec((1, n), lambda j: (0, 0))],
        out_specs=pl.BlockSpec((rows, n), lambda j: (0, 0)),
        out_shape=jax.ShapeDtypeStruct((rows, n), F32),
        compiler_params=pltpu.CompilerParams(
            dimension_semantics=("arbitrary",), vmem_limit_bytes=VMEM_LIMIT),
        name="adaln",
    )(cond, w_mod, b_mod)


def _ctx_kernel(x_ref, mod_ref, gain_ref, w_ref, kgain_ref, bd_ref, kc_ref, vct_ref):
    hn = _mod_norm(x_ref[...], gain_ref[...], mod_ref[0:1, :], mod_ref[1:2, :])
    kv = _dot(hn.astype(BF16), w_ref[...])
    k = kv[:, :LANES]
    k = k * lax.rsqrt(_head_ms(k, bd_ref[:LANES, :LANES]) + EPS) * kgain_ref[...]
    kc_ref[...] = k.astype(BF16)
    vct_ref[...] = kv[:, LANES:].T.astype(BF16)


def _ctx_kv(ctx, mod3, ctx_row, norm_gain, w_in_bf, kgain2, bd):
    b, c, d = ctx.shape
    kvw = 2 * ATTN_KV_HEADS * HEAD_DIM
    return pl.pallas_call(
        _ctx_kernel,
        grid=(b,),
        in_specs=[pl.BlockSpec((None, c, d), lambda i: (i, 0, 0)),
                  pl.BlockSpec((None, N_MOD, d), lambda i: (ctx_row, 0, 0)),
                  pl.BlockSpec((1, d), lambda i: (0, 0)),
                  pl.BlockSpec((d, kvw), lambda i: (0, 0)),
                  pl.BlockSpec((1, LANES), lambda i: (0, 0)),
                  pl.BlockSpec((MXU_DIM, MXU_DIM), lambda i: (0, 0))],
        out_specs=[pl.BlockSpec((None, c, LANES), lambda i: (i, 0, 0)),
                   pl.BlockSpec((None, LANES, c), lambda i: (i, 0, 0))],
        out_shape=[jax.ShapeDtypeStruct((b, c, LANES), BF16),
                   jax.ShapeDtypeStruct((b, LANES, c), BF16)],
        compiler_params=pltpu.CompilerParams(
            dimension_semantics=("arbitrary",), vmem_limit_bytes=VMEM_LIMIT),
        name="ctx_kv",
    )(ctx, mod3, norm_gain, w_in_bf, kgain2, bd)


def _inproj_kernel(x_ref, mod_ref, gain_ref, w_ref, cos_ref, sup_ref, sdn_ref,
                   qgain_ref, kgain_ref, ggain_ref, ws_ref, bs_ref, ogain_ref, bd_ref,
                   wo_f32_ref, wgu_f32_ref, wd_f32_ref,
                   k_ref, vt_ref, qt_ref, mlp_ref, wo_bf_ref, wgu_bf_ref, wd_bf_ref):
    wo_bf_ref[...] = wo_f32_ref[...].astype(BF16)
    wgu_bf_ref[...] = wgu_f32_ref[...].astype(BF16)
    wd_bf_ref[...] = wd_f32_ref[...].astype(BF16)

    t = x_ref.shape[0]
    nblk = t // BLOCK
    kvw = ATTN_KV_HEADS * HEAD_DIM
    aw = ATTN_HEADS * HEAD_DIM
    mw = MLP_HEADS * HEAD_DIM
    q_cols = slice(2 * kvw, 2 * kvw + aw)
    kv_cols = slice(0, 2 * kvw)
    u_cols = slice(2 * kvw + aw, 2 * kvw + aw + mw)
    g_cols = slice(2 * kvw + aw + mw, 2 * kvw + aw + 2 * mw)
    halves = range(aw // MXU_DIM)
    bd = bd_ref[...]

    def project(cols):
        return _dot(hn, w_ref[:, cols])

    def finish_qkv(q, kv, q_ms, k_ms):
        cos, sup, sdn = cos_ref[...], sup_ref[...], sdn_ref[...]
        k = kv[:, :kvw] * lax.rsqrt(k_ms + EPS) * kgain_ref[...]
        k_ref[...] = _rope(k, cos, sup, sdn).astype(BF16)
        vt_ref[...] = kv[:, kvw:].T.astype(BF16)
        for half in halves:
            qh = q[:, half * MXU_DIM:(half + 1) * MXU_DIM] * lax.rsqrt(q_ms[half] + EPS)
            qh = qh * qgain_ref[:, half * MXU_DIM:(half + 1) * MXU_DIM]
            for sl in range(MXU_DIM // LANES):
                qs = _rope(qh[:, sl * LANES:(sl + 1) * LANES], cos, sup, sdn)
                qst = qs.T
                for hh in range(LANES // HEAD_DIM):
                    head = (half * MXU_DIM + sl * LANES) // HEAD_DIM + hh
                    kvh, grp = head // ATTN_GROUP, head % ATTN_GROUP
                    for jb in range(nblk):
                        col = (jb * ATTN_GROUP + grp) * BLOCK
                        qt_ref[kvh, :, col:col + BLOCK] = qst[
                            hh * HEAD_DIM:(hh + 1) * HEAD_DIM, jb * BLOCK:(jb + 1) * BLOCK].astype(BF16)

    def gate_mix(g_raw):
        g = _gelu2_tanh(g_raw)
        g_ms = [_head_ms(g[:, h * MXU_DIM:(h + 1) * MXU_DIM], bd) for h in halves]
        gn = jnp.concatenate(
            [g[:, h * MXU_DIM:(h + 1) * MXU_DIM] * lax.rsqrt(g_ms[h] + 4.0 * EPS) for h in halves],
            axis=1) * ggain_ref[...]
        gnb = gn.astype(BF16)
        low_head = lax.broadcasted_iota(jnp.int32, (BLOCK, t), 1) % LANES < HEAD_DIM
        mixed_slabs = []
        for p in range(MLP_HEADS // 2):
            rhs = jnp.concatenate(
                [gnb[c * BLOCK:(c + 1) * BLOCK, p * LANES:(p + 1) * LANES] for c in range(nblk)], axis=1)
            a = _dot(ws_ref[2 * p], rhs)
            b = _dot(ws_ref[2 * p + 1], rhs)
            mixed_slabs.append(jnp.where(low_head, a, b))
        return mixed_slabs

    def finish_mlp(u_raw, mixed_slabs):
        u = _gelu2_tanh(u_raw)
        rows = []
        for c in range(nblk):
            mixed_c = jnp.concatenate(
                [m[:, c * LANES:(c + 1) * LANES] for m in mixed_slabs], axis=1) + bs_ref[...]
            rows.append(u[c * BLOCK:(c + 1) * BLOCK, :] * mixed_c)
        o = jnp.concatenate(rows, axis=0)
        o = o * lax.rsqrt(jnp.mean(o * o, axis=-1, keepdims=True) + 4.0 * EPS) * ogain_ref[...]
        mlp_ref[...] = o.astype(BF16)

    hn = _mod_norm(x_ref[...], gain_ref[...], mod_ref[0:1, :], mod_ref[1:2, :]).astype(BF16)
    q, kv = project(q_cols), project(kv_cols)
    q_ms = [_head_ms(q[:, h * MXU_DIM:(h + 1) * MXU_DIM], bd) for h in halves]
    k_ms = _head_ms(kv[:, :kvw], bd[:kvw, :kvw])
    g_raw = project(g_cols)
    finish_qkv(q, kv, q_ms, k_ms)
    u_raw = project(u_cols)
    finish_mlp(u_raw, gate_mix(g_raw))


def _inproj(x, mod3, norm_gain, w_in_bf, cos, sup, sdn, qgain, kgain2, ggain, ws_bf, bs_full,
            ogain, bd, later_weights, tile):
    b, s, d = x.shape
    inw = w_in_bf.shape[1]
    aw = ATTN_HEADS * HEAD_DIM
    mw = MLP_HEADS * HEAD_DIM
    n_steps = (s // tile) * b
    const = lambda shape: pl.BlockSpec(shape, lambda i, bb: (0,) * len(shape))

    def slab_spec(w):
        rows = next(r for r in range(BF16_SUBLANES, w.shape[0] + 1, BF16_SUBLANES)
                    if w.shape[0] % r == 0 and w.shape[0] // r <= n_steps)
        n_slabs = w.shape[0] // rows
        return pl.BlockSpec((rows, w.shape[1]), lambda i, bb: (jnp.minimum(i * b + bb, n_slabs - 1), 0))

    slab_specs = [slab_spec(w) for w in later_weights]
    table_spec = pl.BlockSpec((tile, LANES), lambda i, bb: (i, 0))
    return pl.pallas_call(
        _inproj_kernel,
        grid=(s // tile, b),
        in_specs=[pl.BlockSpec((None, tile, d), lambda i, bb: (bb, i, 0)),
                  pl.BlockSpec((None, N_MOD, d), lambda i, bb: (bb, 0, 0)),
                  const((1, d)),
                  const((d, inw)),
                  table_spec, table_spec, table_spec,
                  const((1, aw)), const((1, LANES)), const((1, mw)),
                  const((MLP_HEADS, BLOCK, BLOCK)), const((BLOCK, mw)), const((1, mw)),
                  const((MXU_DIM, MXU_DIM))] + slab_specs,
        out_specs=[pl.BlockSpec((None, tile, LANES), lambda i, bb: (bb, i, 0)),
                   pl.BlockSpec((None, LANES, tile), lambda i, bb: (bb, 0, i)),
                   pl.BlockSpec((None, ATTN_KV_HEADS, HEAD_DIM, ATTN_GROUP * tile),
                                lambda i, bb: (bb, 0, 0, i)),
                   pl.BlockSpec((None, tile, mw), lambda i, bb: (bb, i, 0))] + slab_specs,
        out_shape=[jax.ShapeDtypeStruct((b, s, LANES), BF16),
                   jax.ShapeDtypeStruct((b, LANES, s), BF16),
                   jax.ShapeDtypeStruct((b, ATTN_KV_HEADS, HEAD_DIM, ATTN_GROUP * s), BF16),
                   jax.ShapeDtypeStruct((b, s, mw), BF16)]
        + [jax.ShapeDtypeStruct(w.shape, BF16) for w in later_weights],
        compiler_params=pltpu.CompilerParams(
            dimension_semantics=("arbitrary", "arbitrary"), vmem_limit_bytes=VMEM_LIMIT),
        name="inproj",
    )(x, mod3, norm_gain, w_in_bf, cos, sup, sdn, qgain, kgain2, ggain, ws_bf, bs_full, ogain, bd,
      *later_weights)


def _mix_ffn_kernel(tiles_per_seq, bounded,
                    x_ref, mod_ref, kp_ref, km_ref, kn_ref, vp_ref, vm_ref, vn_ref,
                    kc_ref, vct_ref, qt_ref, mlp_ref, sink_ref, bound_ref, bias_ref, again_ref, wo_ref,
                    fmod_ref, fgain_ref, wgu_ref, wd_ref,
                    o_ref, h_ref, hid_ref):
    tq = x_ref.shape[0]
    nblk = tq // BLOCK
    step_id = pl.program_id(0)
    n_tiles = pl.num_programs(0) - 1
    i = lax.rem(jnp.minimum(step_id, n_tiles - 1), tiles_per_seq)
    last = tiles_per_seq - 1
    gq = ATTN_GROUP * BLOCK
    slot_w = lax.rem(step_id, 2)
    slot_r = 1 - slot_w

    @pl.when(step_id == 0)
    def _():
        h_ref[1] = jnp.zeros(h_ref.shape[1:], F32)

    ff = wd_ref.shape[0]
    n_chunks = ff // MXU_DIM
    h_in = h_ref[slot_r]
    hn = _mod_norm(h_in, fgain_ref[...], fmod_ref[3:4, :], fmod_ref[4:5, :]).astype(BF16)

    def ffn_matmuls(c):
        return (_dot(hn, wgu_ref[:, c * MXU_DIM:(c + 1) * MXU_DIM]),
                _dot(hn, wgu_ref[:, ff + c * MXU_DIM:ff + (c + 1) * MXU_DIM]))

    def ffn_activation(c, gate_up):
        a, b = gate_up
        hid_ref[:, c * MXU_DIM:(c + 1) * MXU_DIM] = (_silu(a) * b).astype(BF16)

    k_ext = jnp.concatenate([kp_ref[...], km_ref[...], kn_ref[...]], axis=0)
    vt_ext = jnp.concatenate([vp_ref[...], vm_ref[...], vn_ref[...]], axis=1)
    kc = kc_ref[...]
    vct = vct_ref[...]
    cw = HEADS_PER_STEP * BLOCK
    zeros_q = jnp.zeros((HEAD_DIM, cw), BF16)
    ones_rows = jnp.ones((BF16_SUBLANES, 3 * BLOCK + kc.shape[0]), BF16)
    bias_prev = bias_ref[0:BLOCK, :cw]
    bias_next = bias_ref[BLOCK:2 * BLOCK, :cw]

    def score_matmul(jb, kvh, part):
        qt = qt_ref[kvh, :, jb * gq + part * cw:jb * gq + (part + 1) * cw]
        rhs = jnp.concatenate([qt, zeros_q] if kvh == 0 else [zeros_q, qt], axis=0)
        return _dot(kc, rhs), _dot(k_ext[jb * BLOCK:(jb + 3) * BLOCK, :], rhs)

    def mask_and_max(jb, kvh, part, s):
        pen_prev = jnp.where(i == 0, MASKED, 0.0) if jb == 0 else 0.0
        pen_next = jnp.where(i == last, MASKED, 0.0) if jb == nblk - 1 else 0.0
        s_ctx, s = s
        parts = [s[0:BLOCK] + (bias_prev + pen_prev), s[BLOCK:2 * BLOCK],
                 s[2 * BLOCK:3 * BLOCK] + (bias_next + pen_next), s_ctx]
        if bounded:
            m = bound_ref[:, :cw]
            return exp_weights(parts, m), m
        m = jnp.maximum(
            jnp.maximum(jnp.max(parts[0], axis=0, keepdims=True), jnp.max(parts[1], axis=0, keepdims=True)),
            jnp.maximum(jnp.max(parts[2], axis=0, keepdims=True), jnp.max(parts[3], axis=0, keepdims=True)))
        return parts, jnp.maximum(m, sink_ref[kvh, :, part * cw:(part + 1) * cw])

    def exp_weights(parts, m):
        return jnp.concatenate([jnp.exp2(p_ - m).astype(BF16) for p_ in parts], axis=0)

    def value_matmul(jb, kvh, part, p):
        v_all = jnp.concatenate(
            [vt_ext[kvh * HEAD_DIM:(kvh + 1) * HEAD_DIM, jb * BLOCK:(jb + 3) * BLOCK],
             vct[kvh * HEAD_DIM:(kvh + 1) * HEAD_DIM, :]], axis=1)
        return _dot(jnp.concatenate([v_all, ones_rows], axis=0), p)

    def normalize(jb, kvh, part, pv, m):
        denom = pv[HEAD_DIM:HEAD_DIM + 1, :] + jnp.exp2(sink_ref[kvh, :, part * cw:(part + 1) * cw] - m)
        o_t = pv[:HEAD_DIM, :] * (1.0 / denom)
        return [o_t[:, g * BLOCK:(g + 1) * BLOCK] for g in range(HEADS_PER_STEP)]

    def block_norm(out_t):
        o_all = jnp.concatenate(out_t, axis=0)
        ms = jnp.mean(o_all * o_all, axis=0, keepdims=True)
        y = o_all * lax.rsqrt(ms + EPS) * again_ref[...]
        return y.T.astype(BF16)

    aw = ATTN_HEADS * HEAD_DIM
    steps = [(jb, kvh, part) for jb in range(nblk) for kvh in range(ATTN_KV_HEADS)
             for part in range(ATTN_GROUP // HEADS_PER_STEP)]
    assert n_chunks >= len(steps)
    raw = {n: score_matmul(*steps[n]) for n in range(2)}
    proj_mlp = _dot(mlp_ref[...], wo_ref[aw:, :])
    ready = {0: mask_and_max(*steps[0], raw.pop(0))}
    attn_rows, out_t, unnormalized, gate_up = [], [], None, {}

    def collect(step, pv, m):
        out_t.extend(normalize(*step, pv, m))
        if len(out_t) == ATTN_HEADS:
            attn_rows.append(block_norm(out_t))
            out_t.clear()

    for n, step in enumerate(steps):
        if n + 1 < len(steps):
            ready[n + 1] = mask_and_max(*steps[n + 1], raw.pop(n + 1))
        if n + 2 < len(steps):
            raw[n + 2] = score_matmul(*steps[n + 2])
        gate_up[n] = ffn_matmuls(n)
        if unnormalized is not None:
            collect(*unnormalized)
            ffn_activation(n - 1, gate_up.pop(n - 1))
        parts, m = ready.pop(n)
        weights = parts if bounded else exp_weights(parts, m)
        unnormalized = (step, value_matmul(*step, weights), m)

    c_next = len(steps)
    gate_up[c_next] = ffn_matmuls(c_next)
    collect(*unnormalized)
    ffn_activation(c_next - 1, gate_up.pop(c_next - 1))
    for c in range(c_next + 1, n_chunks):
        gate_up[c] = ffn_matmuls(c)
        ffn_activation(c - 1, gate_up.pop(c - 1))
    proj_attn = _dot(jnp.concatenate(attn_rows, axis=0), wo_ref[:aw, :])
    ffn_activation(n_chunks - 1, gate_up.pop(n_chunks - 1))
    h_ref[slot_w] = x_ref[...] + mod_ref[2:3, :] * (proj_attn + proj_mlp)
    o_ref[...] = h_in + fmod_ref[5:6, :] * _dot(hid_ref[...], wd_ref[...])


def _mix_ffn(x, mod3, k, vt, kc, vct, qt, mlpn, sink_rows, bound_row, bias, again_b, wo_bf,
             ffn_gain, wgu_bf, wd_bf, tile, bounded):
    b, s, d = x.shape
    c = kc.shape[1]
    aw = ATTN_HEADS * HEAD_DIM
    mw = mlpn.shape[2]
    ff = wd_bf.shape[0]
    assert ff % MXU_DIM == 0
    r = tile // BLOCK
    nb = s // BLOCK
    nt = s // tile
    n_tiles = b * nt
    gq = ATTN_GROUP * BLOCK

    def mix_tile(g):
        t = jnp.minimum(g, n_tiles - 1)
        return t // nt, t % nt

    def ffn_tile(g):
        t = jnp.maximum(g - 1, 0)
        return t // nt, t % nt

    def at_mix(fn):
        return lambda g: fn(*mix_tile(g))

    const = lambda shape: pl.BlockSpec(shape, lambda g: (0,) * len(shape))
    return pl.pallas_call(
        functools.partial(_mix_ffn_kernel, nt, bounded),
        grid=(n_tiles + 1,),
        in_specs=[pl.BlockSpec((None, tile, d), at_mix(lambda bb, i: (bb, i, 0))),
                  pl.BlockSpec((None, N_MOD, d), at_mix(lambda bb, i: (bb, 0, 0))),
                  pl.BlockSpec((None, BLOCK, LANES), at_mix(lambda bb, i: (bb, jnp.maximum(i * r - 1, 0), 0))),
                  pl.BlockSpec((None, tile, LANES), at_mix(lambda bb, i: (bb, i, 0))),
                  pl.BlockSpec((None, BLOCK, LANES),
                               at_mix(lambda bb, i: (bb, jnp.minimum((i + 1) * r, nb - 1), 0))),
                  pl.BlockSpec((None, LANES, BLOCK), at_mix(lambda bb, i: (bb, 0, jnp.maximum(i * r - 1, 0)))),
                  pl.BlockSpec((None, LANES, tile), at_mix(lambda bb, i: (bb, 0, i))),
                  pl.BlockSpec((None, LANES, BLOCK),
                               at_mix(lambda bb, i: (bb, 0, jnp.minimum((i + 1) * r, nb - 1)))),
                  pl.BlockSpec((None, c, LANES), at_mix(lambda bb, i: (bb, 0, 0))),
                  pl.BlockSpec((None, LANES, c), at_mix(lambda bb, i: (bb, 0, 0))),
                  pl.BlockSpec((None, ATTN_KV_HEADS, HEAD_DIM, ATTN_GROUP * tile),
                               at_mix(lambda bb, i: (bb, 0, 0, i))),
                  pl.BlockSpec((None, tile, mw), at_mix(lambda bb, i: (bb, i, 0))),
                  const((ATTN_KV_HEADS, 1, gq)),
                  const((1, gq)),
                  const((2 * BLOCK, gq)),
                  const((aw, BLOCK)),
                  const((aw + mw, d)),
                  pl.BlockSpec((None, N_MOD, d), lambda g: (ffn_tile(g)[0], 0, 0)),
                  const((1, d)), const((d, 2 * ff)), const((ff, d))],
        out_specs=pl.BlockSpec((None, tile, d), lambda g: (*ffn_tile(g), 0)),
        out_shape=jax.ShapeDtypeStruct((b, s, d), F32),
        scratch_shapes=[pltpu.VMEM((2, tile, d), F32), pltpu.VMEM((tile, ff), BF16)],
        compiler_params=pltpu.CompilerParams(
            dimension_semantics=("arbitrary",), vmem_limit_bytes=VMEM_LIMIT),
        name="mix_ffn",
    )(x, mod3, k, k, k, vt, vt, vt, kc, vct, qt, mlpn, sink_rows, bound_row, bias, again_b, wo_bf,
      mod3, ffn_gain, wgu_bf, wd_bf)


def _rope_tables(s):
    axis_dim = HEAD_DIM // 2
    pos = np.arange(s)
    inv_freq = (ROPE_THETA ** (-np.arange(0, axis_dim, 2, dtype=np.float32) / axis_dim)).astype(np.float32)
    ang_r = (pos // GRID_W).astype(np.float32)[:, None] * inv_freq[None, :]
    ang_c = (pos % GRID_W).astype(np.float32)[:, None] * inv_freq[None, :]
    cr, sr, cc, sc = np.cos(ang_r), np.sin(ang_r), np.cos(ang_c), np.sin(ang_c)
    z = np.zeros_like(sr)
    reps = LANES // HEAD_DIM
    cos = np.tile(np.concatenate([cr, cr, cc, cc], axis=1), (1, reps))
    sin_up = np.tile(np.concatenate([-sr, z, -sc, z], axis=1), (1, reps))
    sin_dn = np.tile(np.concatenate([z, sr, z, sc], axis=1), (1, reps))
    return jnp.asarray(cos, F32), jnp.asarray(sin_up, F32), jnp.asarray(sin_dn, F32)


def _window_bias():
    c = np.arange(BLOCK)[:, None]
    r = np.arange(BLOCK)[None, :]
    prev = np.where(c >= r, 0.0, MASKED).astype(np.float32)
    nxt = np.where(c <= r, 0.0, MASKED).astype(np.float32)
    return jnp.asarray(np.tile(np.concatenate([prev, nxt], axis=0), (1, ATTN_GROUP)), F32)


def _head_mean_matrix():
    bd = np.kron(np.eye(MXU_DIM // HEAD_DIM, dtype=np.float32),
                 np.full((HEAD_DIM, HEAD_DIM), 1.0 / HEAD_DIM, np.float32))
    return jnp.asarray(bd, BF16)


def kernel(x, c, ctx, c_ctx, w_mod, b_mod, norm_mix, norm_ffn, w_in, q_gain, k_gain, attn_sink,
           gate_gain, w_spatial, b_spatial, attn_out_gain, mlp_out_gain, w_out, w_gate_up, w_down):
    b, s, d = x.shape
    assert w_mod.shape[0] == 1, "single-layer problem"
    assert s % 1024 == 0 and d % LANES == 0
    aw = ATTN_HEADS * HEAD_DIM
    mw = MLP_HEADS * HEAD_DIM

    rows = -(-(b + 1) // BF16_SUBLANES) * BF16_SUBLANES
    cond = jnp.concatenate([c, c_ctx[None, :], jnp.zeros((rows - b - 1, d), F32)], axis=0)
    mod3 = _adaln(cond, w_mod[0], b_mod[0][None, :]).reshape(rows, N_MOD, d)

    w_in_bf = w_in[0].astype(BF16)
    bd = _head_mean_matrix()
    kgain2 = jnp.tile(k_gain[0], ATTN_KV_HEADS)[None, :]
    qgain = (jnp.tile(q_gain[0], ATTN_HEADS) * (HEAD_DIM ** -0.5 * LOG2E))[None, :]
    norm_mix_g = norm_mix[0][None, :]

    kc, vct = _ctx_kv(ctx, mod3, b, norm_mix_g, w_in_bf, kgain2, bd)

    cos, sup, sdn = _rope_tables(s)
    bs_full = jnp.repeat(b_spatial[0].T, HEAD_DIM, axis=1)
    k, vt, qt, mlpn, wo_bf, wgu_bf, wd_bf = _inproj(
        x, mod3, norm_mix_g, w_in_bf, cos, sup, sdn, qgain, kgain2,
        gate_gain[0].reshape(1, mw), w_spatial[0].astype(BF16), bs_full,
        mlp_out_gain[0][None, :], bd, (w_out[0], w_gate_up[0], w_down[0]), tile=1024)

    sink_rows = jnp.repeat(attn_sink[0].reshape(ATTN_KV_HEADS, ATTN_GROUP) * LOG2E,
                           BLOCK, axis=1)[:, None, :]
    again_b = jnp.broadcast_to(attn_out_gain[0][:, None], (aw, BLOCK))
    logit_bound = (BOUND_MARGIN * HEAD_DIM ** 0.5 * LOG2E
                   * jnp.max(jnp.abs(q_gain[0])) * jnp.max(jnp.abs(k_gain[0])))
    sink_max = jnp.max(jnp.abs(attn_sink[0])) * LOG2E
    use_bound = (logit_bound <= MAX_SAFE_SHIFT) & (sink_max <= MAX_SAFE_SHIFT)
    bound_row = jnp.full((1, ATTN_GROUP * BLOCK), logit_bound, F32)
    operands = (x, mod3, k, vt, kc, vct, qt, mlpn, sink_rows, bound_row, _window_bias(), again_b,
                wo_bf, norm_ffn[0][None, :], wgu_bf, wd_bf)
    return lax.cond(use_bound,
                    lambda ops: _mix_ffn(*ops, tile=512, bounded=True),
                    lambda ops: _mix_ffn(*ops, tile=512, bounded=False),
                    operands)
```

```python
import functools
import math

import jax
import jax.numpy as jnp
import numpy as np
from jax import lax
from jax.experimental import pallas as pl
from jax.experimental.pallas import tpu as pltpu

F32 = jnp.float32
BF16 = jnp.bfloat16

HEAD_DIM = 64
ATTN_HEADS = 8
ATTN_KV_HEADS = 2
ATTN_GROUP = ATTN_HEADS // ATTN_KV_HEADS
MLP_HEADS = 8
N_MOD = 6
BLOCK = 128
GRID_W = 64
ROPE_THETA = 10000.0
EPS = 1e-6
MASKED = -1e30
BOUND_MARGIN = 1.05
MAX_SAFE_LOGIT = 40.0
LOG2E = math.log2(math.e)

LANES = 128
BF16_SUBLANES = 16
MXU_DIM = 256
VMEM_LIMIT = 56 * 1024 * 1024
HEADS_PER_STEP = 4


def _dot(a, b):
    return jnp.dot(a, b, preferred_element_type=F32)


def _silu(x):
    return x * (1.0 / (1.0 + jnp.exp(-x)))


def _gelu2_tanh(x):
    c = math.sqrt(2.0 / math.pi)
    return x * (1.0 + jnp.tanh(x * (c + (c * 0.044715) * (x * x))))


def _mod_norm(x, gain, shift, scale):
    y = x * lax.rsqrt(jnp.mean(x * x, axis=-1, keepdims=True) + EPS)
    return y * (gain * (1.0 + scale)) + shift


def _head_ms(x, bd):
    return _dot((x * x).astype(BF16), bd)


def _rope(x, cos, sin_up, sin_dn):
    up = pltpu.roll(x, LANES - 16, 1)
    dn = pltpu.roll(x, 16, 1)
    return x * cos + up * sin_up + dn * sin_dn


def _split_bf16(x):
    hi = x.astype(BF16)
    return hi, (x - hi.astype(F32)).astype(BF16)


def _adaln_kernel(cond_ref, w_ref, b_ref, o_ref):
    rows = cond_ref.shape[0]

    @pl.when(pl.program_id(0) == 0)
    def _():
        o_ref[...] = jnp.broadcast_to(b_ref[...], o_ref.shape)

    s_hi, s_lo = _split_bf16(_silu(cond_ref[...]))
    w_hi, w_lo = _split_bf16(w_ref[...])
    both = _dot(jnp.concatenate([s_hi, s_lo], axis=0), w_hi)
    o_ref[...] += both[:rows] + both[rows:] + _dot(s_hi, w_lo)


def _adaln(cond, w_mod, b_mod):
    rows, d = cond.shape
    n = w_mod.shape[1]
    tk = d // 8
    assert rows % BF16_SUBLANES == 0 and tk % LANES == 0
    return pl.pallas_call(
        _adaln_kernel,
        grid=(d // tk,),
        in_specs=[pl.BlockSpec((rows, tk), lambda j: (0, j)),
                  pl.BlockSpec((tk, n), lambda j: (j, 0)),
                  pl.BlockSpec((1, n), lambda j: (0, 0))],
        out_specs=pl.BlockSpec((rows, n), lambda j: (0, 0)),
        out_shape=jax.ShapeDtypeStruct((rows, n), F32),
        compiler_params=pltpu.CompilerParams(
            dimension_semantics=("arbitrary",), vmem_limit_bytes=VMEM_LIMIT),
        name="adaln",
    )(cond, w_mod, b_mod)


def _ctx_kernel(x_ref, mod_ref, gain_ref, w_ref, kgain_ref, bd_ref, kc_ref, vct_ref):
    hn = _mod_norm(x_ref[...], gain_ref[...], mod_ref[0:1, :], mod_ref[1:2, :])
    kv = _dot(hn.astype(BF16), w_ref[...])
    k = kv[:, :LANES]
    k = k * lax.rsqrt(_head_ms(k, bd_ref[:LANES, :LANES]) + EPS) * kgain_ref[...]
    kc_ref[...] = k.astype(BF16)
    vct_ref[...] = kv[:, LANES:].T.astype(BF16)


def _ctx_kv(ctx, mod3, ctx_row, norm_gain, w_in_bf, kgain2, bd):
    b, c, d = ctx.shape
    kvw = 2 * ATTN_KV_HEADS * HEAD_DIM
    return pl.pallas_call(
        _ctx_kernel,
        grid=(b,),
        in_specs=[pl.BlockSpec((None, c, d), lambda i: (i, 0, 0)),
                  pl.BlockSpec((None, N_MOD, d), lambda i: (ctx_row, 0, 0)),
                  pl.BlockSpec((1, d), lambda i: (0, 0)),
                  pl.BlockSpec((d, kvw), lambda i: (0, 0)),
                  pl.BlockSpec((1, LANES), lambda i: (0, 0)),
                  pl.BlockSpec((MXU_DIM, MXU_DIM), lambda i: (0, 0))],
        out_specs=[pl.BlockSpec((None, c, LANES), lambda i: (i, 0, 0)),
                   pl.BlockSpec((None, LANES, c), lambda i: (i, 0, 0))],
        out_shape=[jax.ShapeDtypeStruct((b, c, LANES), BF16),
                   jax.ShapeDtypeStruct((b, LANES, c), BF16)],
        compiler_params=pltpu.CompilerParams(
            dimension_semantics=("arbitrary",), vmem_limit_bytes=VMEM_LIMIT),
        name="ctx_kv",
    )(ctx, mod3, norm_gain, w_in_bf, kgain2, bd)


def _inproj_kernel(x_ref, mod_ref, gain_ref, w_ref, cos_ref, sup_ref, sdn_ref,
                   qgain_ref, kgain_ref, ggain_ref, ws_ref, bs_ref, ogain_ref, bd_ref,
                   wo_f32_ref, wgu_f32_ref, wd_f32_ref,
                   k_ref, vt_ref, qt_ref, mlp_ref, wo_bf_ref, wgu_bf_ref, wd_bf_ref):
    wo_bf_ref[...] = wo_f32_ref[...].astype(BF16)
    wgu_bf_ref[...] = wgu_f32_ref[...].astype(BF16)
    wd_bf_ref[...] = wd_f32_ref[...].astype(BF16)

    t = x_ref.shape[0]
    nblk = t // BLOCK
    kvw = ATTN_KV_HEADS * HEAD_DIM
    aw = ATTN_HEADS * HEAD_DIM
    mw = MLP_HEADS * HEAD_DIM
    q_cols = slice(2 * kvw, 2 * kvw + aw)
    kv_cols = slice(0, 2 * kvw)
    u_cols = slice(2 * kvw + aw, 2 * kvw + aw + mw)
    g_cols = slice(2 * kvw + aw + mw, 2 * kvw + aw + 2 * mw)
    halves = range(aw // MXU_DIM)
    bd = bd_ref[...]

    def project(cols):
        return _dot(hn, w_ref[:, cols])

    def finish_qkv(q, kv, q_ms, k_ms):
        cos, sup, sdn = cos_ref[...], sup_ref[...], sdn_ref[...]
        k = kv[:, :kvw] * lax.rsqrt(k_ms + EPS) * kgain_ref[...]
        k_ref[...] = _rope(k, cos, sup, sdn).astype(BF16)
        vt_ref[...] = kv[:, kvw:].T.astype(BF16)
        for half in halves:
            qh = q[:, half * MXU_DIM:(half + 1) * MXU_DIM] * lax.rsqrt(q_ms[half] + EPS)
            qh = qh * qgain_ref[:, half * MXU_DIM:(half + 1) * MXU_DIM]
            for sl in range(MXU_DIM // LANES):
                qs = _rope(qh[:, sl * LANES:(sl + 1) * LANES], cos, sup, sdn)
                qst = qs.T
                for hh in range(LANES // HEAD_DIM):
                    head = (half * MXU_DIM + sl * LANES) // HEAD_DIM + hh
                    kvh, grp = head // ATTN_GROUP, head % ATTN_GROUP
                    for jb in range(nblk):
                        col = (jb * ATTN_GROUP + grp) * BLOCK
                        qt_ref[kvh, :, col:col + BLOCK] = qst[
                            hh * HEAD_DIM:(hh + 1) * HEAD_DIM, jb * BLOCK:(jb + 1) * BLOCK].astype(BF16)

    def gate_mix(g_raw):
        g = _gelu2_tanh(g_raw)
        g_ms = [_head_ms(g[:, h * MXU_DIM:(h + 1) * MXU_DIM], bd) for h in halves]
        gn = jnp.concatenate(
            [g[:, h * MXU_DIM:(h + 1) * MXU_DIM] * lax.rsqrt(g_ms[h] + 4.0 * EPS) for h in halves],
            axis=1) * ggain_ref[...]
        gnb = gn.astype(BF16)
        low_head = lax.broadcasted_iota(jnp.int32, (BLOCK, t), 1) % LANES < HEAD_DIM
        mixed_slabs = []
        for p in range(MLP_HEADS // 2):
            rhs = jnp.concatenate(
                [gnb[c * BLOCK:(c + 1) * BLOCK, p * LANES:(p + 1) * LANES] for c in range(nblk)], axis=1)
            a = _dot(ws_ref[2 * p], rhs)
            b = _dot(ws_ref[2 * p + 1], rhs)
            mixed_slabs.append(jnp.where(low_head, a, b))
        return mixed_slabs

    def finish_mlp(u_raw, mixed_slabs):
        u = _gelu2_tanh(u_raw)
        rows = []
        for c in range(nblk):
            mixed_c = jnp.concatenate(
                [m[:, c * LANES:(c + 1) * LANES] for m in mixed_slabs], axis=1) + bs_ref[...]
            rows.append(u[c * BLOCK:(c + 1) * BLOCK, :] * mixed_c)
        o = jnp.concatenate(rows, axis=0)
        o = o * lax.rsqrt(jnp.mean(o * o, axis=-1, keepdims=True) + 4.0 * EPS) * ogain_ref[...]
        mlp_ref[...] = o.astype(BF16)

    hn = _mod_norm(x_ref[...], gain_ref[...], mod_ref[0:1, :], mod_ref[1:2, :]).astype(BF16)
    q, kv = project(q_cols), project(kv_cols)
    q_ms = [_head_ms(q[:, h * MXU_DIM:(h + 1) * MXU_DIM], bd) for h in halves]
    k_ms = _head_ms(kv[:, :kvw], bd[:kvw, :kvw])
    g_raw = project(g_cols)
    finish_qkv(q, kv, q_ms, k_ms)
    u_raw = project(u_cols)
    finish_mlp(u_raw, gate_mix(g_raw))


def _inproj(x, mod3, norm_gain, w_in_bf, cos, sup, sdn, qgain, kgain2, ggain, ws_bf, bs_full,
            ogain, bd, later_weights, tile):
    b, s, d = x.shape
    inw = w_in_bf.shape[1]
    aw = ATTN_HEADS * HEAD_DIM
    mw = MLP_HEADS * HEAD_DIM
    n_steps = (s // tile) * b
    const = lambda shape: pl.BlockSpec(shape, lambda i, bb: (0,) * len(shape))

    def slab_spec(w):
        rows = next(r for r in range(BF16_SUBLANES, w.shape[0] + 1, BF16_SUBLANES)
                    if w.shape[0] % r == 0 and w.shape[0] // r <= n_steps)
        n_slabs = w.shape[0] // rows
        return pl.BlockSpec((rows, w.shape[1]), lambda i, bb: (jnp.minimum(i * b + bb, n_slabs - 1), 0))

    slab_specs = [slab_spec(w) for w in later_weights]
    table_spec = pl.BlockSpec((tile, LANES), lambda i, bb: (i, 0))
    return pl.pallas_call(
        _inproj_kernel,
        grid=(s // tile, b),
        in_specs=[pl.BlockSpec((None, tile, d), lambda i, bb: (bb, i, 0)),
                  pl.BlockSpec((None, N_MOD, d), lambda i, bb: (bb, 0, 0)),
                  const((1, d)),
                  const((d, inw)),
                  table_spec, table_spec, table_spec,
                  const((1, aw)), const((1, LANES)), const((1, mw)),
                  const((MLP_HEADS, BLOCK, BLOCK)), const((BLOCK, mw)), const((1, mw)),
                  const((MXU_DIM, MXU_DIM))] + slab_specs,
        out_specs=[pl.BlockSpec((None, tile, LANES), lambda i, bb: (bb, i, 0)),
                   pl.BlockSpec((None, LANES, tile), lambda i, bb: (bb, 0, i)),
                   pl.BlockSpec((None, ATTN_KV_HEADS, HEAD_DIM, ATTN_GROUP * tile),
                                lambda i, bb: (bb, 0, 0, i)),
                   pl.BlockSpec((None, tile, mw), lambda i, bb: (bb, i, 0))] + slab_specs,
        out_shape=[jax.ShapeDtypeStruct((b, s, LANES), BF16),
                   jax.ShapeDtypeStruct((b, LANES, s), BF16),
                   jax.ShapeDtypeStruct((b, ATTN_KV_HEADS, HEAD_DIM, ATTN_GROUP * s), BF16),
                   jax.ShapeDtypeStruct((b, s, mw), BF16)]
        + [jax.ShapeDtypeStruct(w.shape, BF16) for w in later_weights],
        compiler_params=pltpu.CompilerParams(
            dimension_semantics=("arbitrary", "arbitrary"), vmem_limit_bytes=VMEM_LIMIT),
        name="inproj",
    )(x, mod3, norm_gain, w_in_bf, cos, sup, sdn, qgain, kgain2, ggain, ws_bf, bs_full, ogain, bd,
      *later_weights)


def _mix_ffn_kernel(tiles_per_seq, bounded,
                    x_ref, mod_ref, kp_ref, km_ref, kn_ref, vp_ref, vm_ref, vn_ref,
                    kc_ref, vct_ref, qt_ref, mlp_ref, sink_ref, bias_ref, again_ref, wo_ref,
                    fmod_ref, fgain_ref, wgu_ref, wd_ref,
                    o_ref, h_ref, hid_ref):
    tq = x_ref.shape[0]
    nblk = tq // BLOCK
    step_id = pl.program_id(0)
    n_tiles = pl.num_programs(0) - 1
    i = lax.rem(jnp.minimum(step_id, n_tiles - 1), tiles_per_seq)
    last = tiles_per_seq - 1
    gq = ATTN_GROUP * BLOCK
    slot_w = lax.rem(step_id, 2)
    slot_r = 1 - slot_w

    @pl.when(step_id == 0)
    def _():
        h_ref[1] = jnp.zeros(h_ref.shape[1:], F32)

    ff = wd_ref.shape[0]
    n_chunks = ff // MXU_DIM
    h_in = h_ref[slot_r]
    hn = _mod_norm(h_in, fgain_ref[...], fmod_ref[3:4, :], fmod_ref[4:5, :]).astype(BF16)

    def ffn_matmuls(c):
        return (_dot(hn, wgu_ref[:, c * MXU_DIM:(c + 1) * MXU_DIM]),
                _dot(hn, wgu_ref[:, ff + c * MXU_DIM:ff + (c + 1) * MXU_DIM]))

    def ffn_activation(c, gate_up):
        a, b = gate_up
        hid_ref[:, c * MXU_DIM:(c + 1) * MXU_DIM] = (_silu(a) * b).astype(BF16)

    k_ext = jnp.concatenate([kp_ref[...], km_ref[...], kn_ref[...]], axis=0)
    vt_ext = jnp.concatenate([vp_ref[...], vm_ref[...], vn_ref[...]], axis=1)
    kc = kc_ref[...]
    vct = vct_ref[...]
    cw = HEADS_PER_STEP * BLOCK
    zeros_q = jnp.zeros((HEAD_DIM, cw), BF16)
    ones_rows = jnp.ones((BF16_SUBLANES, 3 * BLOCK + kc.shape[0]), BF16)
    bias_prev = bias_ref[0:BLOCK, :cw]
    bias_next = bias_ref[BLOCK:2 * BLOCK, :cw]

    def score_matmul(jb, kvh, part):
        qt = qt_ref[kvh, :, jb * gq + part * cw:jb * gq + (part + 1) * cw]
        rhs = jnp.concatenate([qt, zeros_q] if kvh == 0 else [zeros_q, qt], axis=0)
        return _dot(kc, rhs), _dot(k_ext[jb * BLOCK:(jb + 3) * BLOCK, :], rhs)

    def mask_and_max(jb, kvh, part, s):
        pen_prev = jnp.where(i == 0, MASKED, 0.0) if jb == 0 else 0.0
        pen_next = jnp.where(i == last, MASKED, 0.0) if jb == nblk - 1 else 0.0
        s_ctx, s = s
        parts = [s[0:BLOCK] + (bias_prev + pen_prev), s[BLOCK:2 * BLOCK],
                 s[2 * BLOCK:3 * BLOCK] + (bias_next + pen_next), s_ctx]
        if bounded:
            return jnp.concatenate([jnp.exp2(p_).astype(BF16) for p_ in parts], axis=0), None
        m = jnp.maximum(
            jnp.maximum(jnp.max(parts[0], axis=0, keepdims=True), jnp.max(parts[1], axis=0, keepdims=True)),
            jnp.maximum(jnp.max(parts[2], axis=0, keepdims=True), jnp.max(parts[3], axis=0, keepdims=True)))
        return parts, jnp.maximum(m, sink_ref[kvh, :, part * cw:(part + 1) * cw])

    def exp_weights(parts, m):
        return jnp.concatenate([jnp.exp2(p_ - m).astype(BF16) for p_ in parts], axis=0)

    def value_matmul(jb, kvh, part, p):
        v_all = jnp.concatenate(
            [vt_ext[kvh * HEAD_DIM:(kvh + 1) * HEAD_DIM, jb * BLOCK:(jb + 3) * BLOCK],
             vct[kvh * HEAD_DIM:(kvh + 1) * HEAD_DIM, :]], axis=1)
        return _dot(jnp.concatenate([v_all, ones_rows], axis=0), p)

    def normalize(jb, kvh, part, pv, m):
        sink = sink_ref[kvh, :, part * cw:(part + 1) * cw]
        denom = pv[HEAD_DIM:HEAD_DIM + 1, :] + jnp.exp2(sink if bounded else sink - m)
        o_t = pv[:HEAD_DIM, :] * (1.0 / denom)
        return [o_t[:, g * BLOCK:(g + 1) * BLOCK] for g in range(HEADS_PER_STEP)]

    def block_norm(out_t):
        o_all = jnp.concatenate(out_t, axis=0)
        ms = jnp.mean(o_all * o_all, axis=0, keepdims=True)
        y = o_all * lax.rsqrt(ms + EPS) * again_ref[...]
        return y.T.astype(BF16)

    aw = ATTN_HEADS * HEAD_DIM
    steps = [(jb, kvh, part) for jb in range(nblk) for kvh in range(ATTN_KV_HEADS)
             for part in range(ATTN_GROUP // HEADS_PER_STEP)]
    assert n_chunks >= len(steps)
    attn_rows, out_t, unnormalized, gate_up = [], [], None, {}

    def collect(step, pv, m):
        out_t.extend(normalize(*step, pv, m))
        if len(out_t) == ATTN_HEADS:
            attn_rows.append(block_norm(out_t))
            out_t.clear()

    if bounded:
        proj_mlp = _dot(mlp_ref[...], wo_ref[aw:, :])
        for n, step in enumerate(steps):
            scores = score_matmul(*step)
            gate_up[n] = ffn_matmuls(n)
            weights, _ = mask_and_max(*step, scores)
            if unnormalized is not None:
                collect(*unnormalized)
                ffn_activation(n - 1, gate_up.pop(n - 1))
            unnormalized = (step, value_matmul(*step, weights), None)
    else:
        raw = {n: score_matmul(*steps[n]) for n in range(2)}
        proj_mlp = _dot(mlp_ref[...], wo_ref[aw:, :])
        ready = {0: mask_and_max(*steps[0], raw.pop(0))}
        for n, step in enumerate(steps):
            if n + 1 < len(steps):
                ready[n + 1] = mask_and_max(*steps[n + 1], raw.pop(n + 1))
            if n + 2 < len(steps):
                raw[n + 2] = score_matmul(*steps[n + 2])
            gate_up[n] = ffn_matmuls(n)
            if unnormalized is not None:
                collect(*unnormalized)
                ffn_activation(n - 1, gate_up.pop(n - 1))
            parts, m = ready.pop(n)
            unnormalized = (step, value_matmul(*step, exp_weights(parts, m)), m)

    c_next = len(steps)
    gate_up[c_next] = ffn_matmuls(c_next)
    collect(*unnormalized)
    ffn_activation(c_next - 1, gate_up.pop(c_next - 1))
    for c in range(c_next + 1, n_chunks):
        gate_up[c] = ffn_matmuls(c)
        ffn_activation(c - 1, gate_up.pop(c - 1))
    proj_attn = _dot(jnp.concatenate(attn_rows, axis=0), wo_ref[:aw, :])
    ffn_activation(n_chunks - 1, gate_up.pop(n_chunks - 1))
    h_ref[slot_w] = x_ref[...] + mod_ref[2:3, :] * (proj_attn + proj_mlp)
    o_ref[...] = h_in + fmod_ref[5:6, :] * _dot(hid_ref[...], wd_ref[...])


def _mix_ffn(x, mod3, k, vt, kc, vct, qt, mlpn, sink_rows, bias, again_b, wo_bf,
             ffn_gain, wgu_bf, wd_bf, tile, bounded):
    b, s, d = x.shape
    c = kc.shape[1]
    aw = ATTN_HEADS * HEAD_DIM
    mw = mlpn.shape[2]
    ff = wd_bf.shape[0]
    assert ff % MXU_DIM == 0
    r = tile // BLOCK
    nb = s // BLOCK
    nt = s // tile
    n_tiles = b * nt
    gq = ATTN_GROUP * BLOCK

    def mix_tile(g):
        t = jnp.minimum(g, n_tiles - 1)
        return t // nt, t % nt

    def ffn_tile(g):
        t = jnp.maximum(g - 1, 0)
        return t // nt, t % nt

    def at_mix(fn):
        return lambda g: fn(*mix_tile(g))

    const = lambda shape: pl.BlockSpec(shape, lambda g: (0,) * len(shape))
    return pl.pallas_call(
        functools.partial(_mix_ffn_kernel, nt, bounded),
        grid=(n_tiles + 1,),
        in_specs=[pl.BlockSpec((None, tile, d), at_mix(lambda bb, i: (bb, i, 0))),
                  pl.BlockSpec((None, N_MOD, d), at_mix(lambda bb, i: (bb, 0, 0))),
                  pl.BlockSpec((None, BLOCK, LANES), at_mix(lambda bb, i: (bb, jnp.maximum(i * r - 1, 0), 0))),
                  pl.BlockSpec((None, tile, LANES), at_mix(lambda bb, i: (bb, i, 0))),
                  pl.BlockSpec((None, BLOCK, LANES),
                               at_mix(lambda bb, i: (bb, jnp.minimum((i + 1) * r, nb - 1), 0))),
                  pl.BlockSpec((None, LANES, BLOCK), at_mix(lambda bb, i: (bb, 0, jnp.maximum(i * r - 1, 0)))),
                  pl.BlockSpec((None, LANES, tile), at_mix(lambda bb, i: (bb, 0, i))),
                  pl.BlockSpec((None, LANES, BLOCK),
                               at_mix(lambda bb, i: (bb, 0, jnp.minimum((i + 1) * r, nb - 1)))),
                  pl.BlockSpec((None, c, LANES), at_mix(lambda bb, i: (bb, 0, 0))),
                  pl.BlockSpec((None, LANES, c), at_mix(lambda bb, i: (bb, 0, 0))),
                  pl.BlockSpec((None, ATTN_KV_HEADS, HEAD_DIM, ATTN_GROUP * tile),
                               at_mix(lambda bb, i: (bb, 0, 0, i))),
                  pl.BlockSpec((None, tile, mw), at_mix(lambda bb, i: (bb, i, 0))),
                  const((ATTN_KV_HEADS, 1, gq)),
                  const((2 * BLOCK, gq)),
                  const((aw, BLOCK)),
                  const((aw + mw, d)),
                  pl.BlockSpec((None, N_MOD, d), lambda g: (ffn_tile(g)[0], 0, 0)),
                  const((1, d)), const((d, 2 * ff)), const((ff, d))],
        out_specs=pl.BlockSpec((None, tile, d), lambda g: (*ffn_tile(g), 0)),
        out_shape=jax.ShapeDtypeStruct((b, s, d), F32),
        scratch_shapes=[pltpu.VMEM((2, tile, d), F32), pltpu.VMEM((tile, ff), BF16)],
        compiler_params=pltpu.CompilerParams(
            dimension_semantics=("arbitrary",), vmem_limit_bytes=VMEM_LIMIT),
        name="mix_ffn",
    )(x, mod3, k, k, k, vt, vt, vt, kc, vct, qt, mlpn, sink_rows, bias, again_b, wo_bf,
      mod3, ffn_gain, wgu_bf, wd_bf)


def _rope_tables(s):
    axis_dim = HEAD_DIM // 2
    pos = np.arange(s)
    inv_freq = (ROPE_THETA ** (-np.arange(0, axis_dim, 2, dtype=np.float32) / axis_dim)).astype(np.float32)
    ang_r = (pos // GRID_W).astype(np.float32)[:, None] * inv_freq[None, :]
    ang_c = (pos % GRID_W).astype(np.float32)[:, None] * inv_freq[None, :]
    cr, sr, cc, sc = np.cos(ang_r), np.sin(ang_r), np.cos(ang_c), np.sin(ang_c)
    z = np.zeros_like(sr)
    reps = LANES // HEAD_DIM
    cos = np.tile(np.concatenate([cr, cr, cc, cc], axis=1), (1, reps))
    sin_up = np.tile(np.concatenate([-sr, z, -sc, z], axis=1), (1, reps))
    sin_dn = np.tile(np.concatenate([z, sr, z, sc], axis=1), (1, reps))
    return jnp.asarray(cos, F32), jnp.asarray(sin_up, F32), jnp.asarray(sin_dn, F32)


def _window_bias():
    c = np.arange(BLOCK)[:, None]
    r = np.arange(BLOCK)[None, :]
    prev = np.where(c >= r, 0.0, MASKED).astype(np.float32)
    nxt = np.where(c <= r, 0.0, MASKED).astype(np.float32)
    return jnp.asarray(np.tile(np.concatenate([prev, nxt], axis=0), (1, ATTN_GROUP)), F32)


def _head_mean_matrix():
    bd = np.kron(np.eye(MXU_DIM // HEAD_DIM, dtype=np.float32),
                 np.full((HEAD_DIM, HEAD_DIM), 1.0 / HEAD_DIM, np.float32))
    return jnp.asarray(bd, BF16)


def kernel(x, c, ctx, c_ctx, w_mod, b_mod, norm_mix, norm_ffn, w_in, q_gain, k_gain, attn_sink,
           gate_gain, w_spatial, b_spatial, attn_out_gain, mlp_out_gain, w_out, w_gate_up, w_down):
    b, s, d = x.shape
    assert w_mod.shape[0] == 1, "single-layer problem"
    assert s % 1024 == 0 and d % LANES == 0
    aw = ATTN_HEADS * HEAD_DIM
    mw = MLP_HEADS * HEAD_DIM

    rows = -(-(b + 1) // BF16_SUBLANES) * BF16_SUBLANES
    cond = jnp.concatenate([c, c_ctx[None, :], jnp.zeros((rows - b - 1, d), F32)], axis=0)
    mod3 = _adaln(cond, w_mod[0], b_mod[0][None, :]).reshape(rows, N_MOD, d)

    w_in_bf = w_in[0].astype(BF16)
    bd = _head_mean_matrix()
    kgain2 = jnp.tile(k_gain[0], ATTN_KV_HEADS)[None, :]
    qgain = (jnp.tile(q_gain[0], ATTN_HEADS) * (HEAD_DIM ** -0.5 * LOG2E))[None, :]
    norm_mix_g = norm_mix[0][None, :]

    kc, vct = _ctx_kv(ctx, mod3, b, norm_mix_g, w_in_bf, kgain2, bd)

    cos, sup, sdn = _rope_tables(s)
    bs_full = jnp.repeat(b_spatial[0].T, HEAD_DIM, axis=1)
    k, vt, qt, mlpn, wo_bf, wgu_bf, wd_bf = _inproj(
        x, mod3, norm_mix_g, w_in_bf, cos, sup, sdn, qgain, kgain2,
        gate_gain[0].reshape(1, mw), w_spatial[0].astype(BF16), bs_full,
        mlp_out_gain[0][None, :], bd, (w_out[0], w_gate_up[0], w_down[0]), tile=1024)

    sink_rows = jnp.repeat(attn_sink[0].reshape(ATTN_KV_HEADS, ATTN_GROUP) * LOG2E,
                           BLOCK, axis=1)[:, None, :]
    again_b = jnp.broadcast_to(attn_out_gain[0][:, None], (aw, BLOCK))
    logit_bound = (BOUND_MARGIN * HEAD_DIM ** 0.5 * LOG2E
                   * jnp.max(jnp.abs(q_gain[0])) * jnp.max(jnp.abs(k_gain[0])))
    sink_max = jnp.max(jnp.abs(attn_sink[0])) * LOG2E
    use_bound = (logit_bound <= MAX_SAFE_LOGIT) & (sink_max <= MAX_SAFE_LOGIT)
    operands = (x, mod3, k, vt, kc, vct, qt, mlpn, sink_rows, _window_bias(), again_b,
                wo_bf, norm_ffn[0][None, :], wgu_bf, wd_bf)
    return lax.cond(use_bound,
                    lambda ops: _mix_ffn(*ops, tile=512, bounded=True),
                    lambda ops: _mix_ffn(*ops, tile=512, bounded=False),
                    operands)
```

```python
import functools
import math

import jax
import jax.numpy as jnp
import numpy as np
from jax import lax
from jax.experimental import pallas as pl
from jax.experimental.pallas import tpu as pltpu

F32 = jnp.float32
BF16 = jnp.bfloat16

HEAD_DIM = 64
ATTN_HEADS = 8
ATTN_KV_HEADS = 2
ATTN_GROUP = ATTN_HEADS // ATTN_KV_HEADS
MLP_HEADS = 8
N_MOD = 6
BLOCK = 128
GRID_W = 64
ROPE_THETA = 10000.0
EPS = 1e-6
MASKED = -1e30
BOUND_MARGIN = 1.05
MAX_SAFE_LOGIT = 40.0
LOG2E = math.log2(math.e)

LANES = 128
BF16_SUBLANES = 16
MXU_DIM = 256
VMEM_LIMIT = 56 * 1024 * 1024
HEADS_PER_STEP = 4


def _dot(a, b):
    return jnp.dot(a, b, preferred_element_type=F32)


def _silu(x):
    return x * (1.0 / (1.0 + jnp.exp(-x)))


def _gelu2_tanh(x):
    c = math.sqrt(2.0 / math.pi)
    return x * (1.0 + jnp.tanh(x * (c + (c * 0.044715) * (x * x))))


def _mod_norm(x, gain, shift, scale):
    y = x * lax.rsqrt(jnp.mean(x * x, axis=-1, keepdims=True) + EPS)
    return y * (gain * (1.0 + scale)) + shift


def _head_ms(x, bd):
    return _dot((x * x).astype(BF16), bd)


def _rope(x, cos, sin_up, sin_dn):
    up = pltpu.roll(x, LANES - 16, 1)
    dn = pltpu.roll(x, 16, 1)
    return x * cos + up * sin_up + dn * sin_dn


def _split_bf16(x):
    hi = x.astype(BF16)
    return hi, (x - hi.astype(F32)).astype(BF16)


def _adaln_kernel(cond_ref, w_ref, b_ref, o_ref):
    rows = cond_ref.shape[0]

    @pl.when(pl.program_id(0) == 0)
    def _():
        o_ref[...] = jnp.broadcast_to(b_ref[...], o_ref.shape)

    s_hi, s_lo = _split_bf16(_silu(cond_ref[...]))
    w_hi, w_lo = _split_bf16(w_ref[...])
    both = _dot(jnp.concatenate([s_hi, s_lo], axis=0), w_hi)
    o_ref[...] += both[:rows] + both[rows:] + _dot(s_hi, w_lo)


def _adaln(cond, w_mod, b_mod):
    rows, d = cond.shape
    n = w_mod.shape[1]
    tk = d // 8
    assert rows % BF16_SUBLANES == 0 and tk % LANES == 0
    return pl.pallas_call(
        _adaln_kernel,
        grid=(d // tk,),
        in_specs=[pl.BlockSpec((rows, tk), lambda j: (0, j)),
                  pl.BlockSpec((tk, n), lambda j: (j, 0)),
                  pl.BlockSpec((1, n), lambda j: (0, 0))],
        out_specs=pl.BlockSpec((rows, n), lambda j: (0, 0)),
        out_shape=jax.ShapeDtypeStruct((rows, n), F32),
        compiler_params=pltpu.CompilerParams(
            dimension_semantics=("arbitrary",), vmem_limit_bytes=VMEM_LIMIT),
        name="adaln",
    )(cond, w_mod, b_mod)


def _ctx_kernel(x_ref, mod_ref, gain_ref, w_ref, kgain_ref, bd_ref, kc_ref, vct_ref):
    hn = _mod_norm(x_ref[...], gain_ref[...], mod_ref[0:1, :], mod_ref[1:2, :])
    kv = _dot(hn.astype(BF16), w_ref[...])
    k = kv[:, :LANES]
    k = k * lax.rsqrt(_head_ms(k, bd_ref[:LANES, :LANES]) + EPS) * kgain_ref[...]
    kc_ref[...] = k.astype(BF16)
    vct_ref[...] = kv[:, LANES:].T.astype(BF16)


def _ctx_kv(ctx, mod3, ctx_row, norm_gain, w_in_bf, kgain2, bd):
    b, c, d = ctx.shape
    kvw = 2 * ATTN_KV_HEADS * HEAD_DIM
    return pl.pallas_call(
        _ctx_kernel,
        grid=(b,),
        in_specs=[pl.BlockSpec((None, c, d), lambda i: (i, 0, 0)),
                  pl.BlockSpec((None, N_MOD, d), lambda i: (ctx_row, 0, 0)),
                  pl.BlockSpec((1, d), lambda i: (0, 0)),
                  pl.BlockSpec((d, kvw), lambda i: (0, 0)),
                  pl.BlockSpec((1, LANES), lambda i: (0, 0)),
                  pl.BlockSpec((MXU_DIM, MXU_DIM), lambda i: (0, 0))],
        out_specs=[pl.BlockSpec((None, c, LANES), lambda i: (i, 0, 0)),
                   pl.BlockSpec((None, LANES, c), lambda i: (i, 0, 0))],
        out_shape=[jax.ShapeDtypeStruct((b, c, LANES), BF16),
                   jax.ShapeDtypeStruct((b, LANES, c), BF16)],
        compiler_params=pltpu.CompilerParams(
            dimension_semantics=("arbitrary",), vmem_limit_bytes=VMEM_LIMIT),
        name="ctx_kv",
    )(ctx, mod3, norm_gain, w_in_bf, kgain2, bd)


def _inproj_kernel(x_ref, mod_ref, gain_ref, w_ref, cos_ref, sup_ref, sdn_ref,
                   qgain_ref, kgain_ref, ggain_ref, ws_ref, bs_ref, ogain_ref, bd_ref,
                   wo_f32_ref, wgu_f32_ref, wd_f32_ref,
                   k_ref, vt_ref, qt_ref, mlp_ref, wo_bf_ref, wgu_bf_ref, wd_bf_ref):
    wo_bf_ref[...] = wo_f32_ref[...].astype(BF16)
    wgu_bf_ref[...] = wgu_f32_ref[...].astype(BF16)
    wd_bf_ref[...] = wd_f32_ref[...].astype(BF16)

    t = x_ref.shape[0]
    nblk = t // BLOCK
    kvw = ATTN_KV_HEADS * HEAD_DIM
    aw = ATTN_HEADS * HEAD_DIM
    mw = MLP_HEADS * HEAD_DIM
    q_cols = slice(2 * kvw, 2 * kvw + aw)
    kv_cols = slice(0, 2 * kvw)
    u_cols = slice(2 * kvw + aw, 2 * kvw + aw + mw)
    g_cols = slice(2 * kvw + aw + mw, 2 * kvw + aw + 2 * mw)
    halves = range(aw // MXU_DIM)
    bd = bd_ref[...]

    def project(cols):
        return _dot(hn, w_ref[:, cols])

    def finish_qkv(q, kv, q_ms, k_ms):
        cos, sup, sdn = cos_ref[...], sup_ref[...], sdn_ref[...]
        k = kv[:, :kvw] * lax.rsqrt(k_ms + EPS) * kgain_ref[...]
        k_ref[...] = _rope(k, cos, sup, sdn).astype(BF16)
        vt_ref[...] = kv[:, kvw:].T.astype(BF16)
        for half in halves:
            qh = q[:, half * MXU_DIM:(half + 1) * MXU_DIM] * lax.rsqrt(q_ms[half] + EPS)
            qh = qh * qgain_ref[:, half * MXU_DIM:(half + 1) * MXU_DIM]
            for sl in range(MXU_DIM // LANES):
                qs = _rope(qh[:, sl * LANES:(sl + 1) * LANES], cos, sup, sdn)
                qst = qs.T
                for hh in range(LANES // HEAD_DIM):
                    head = (half * MXU_DIM + sl * LANES) // HEAD_DIM + hh
                    kvh, grp = head // ATTN_GROUP, head % ATTN_GROUP
                    for jb in range(nblk):
                        col = (jb * ATTN_GROUP + grp) * BLOCK
                        qt_ref[kvh, :, col:col + BLOCK] = qst[
                            hh * HEAD_DIM:(hh + 1) * HEAD_DIM, jb * BLOCK:(jb + 1) * BLOCK].astype(BF16)

    def gate_mix(g_raw):
        g = _gelu2_tanh(g_raw)
        g_ms = [_head_ms(g[:, h * MXU_DIM:(h + 1) * MXU_DIM], bd) for h in halves]
        gn = jnp.concatenate(
            [g[:, h * MXU_DIM:(h + 1) * MXU_DIM] * lax.rsqrt(g_ms[h] + 4.0 * EPS) for h in halves],
            axis=1) * ggain_ref[...]
        gnb = gn.astype(BF16)
        low_head = lax.broadcasted_iota(jnp.int32, (BLOCK, t), 1) % LANES < HEAD_DIM
        mixed_slabs = []
        for p in range(MLP_HEADS // 2):
            rhs = jnp.concatenate(
                [gnb[c * BLOCK:(c + 1) * BLOCK, p * LANES:(p + 1) * LANES] for c in range(nblk)], axis=1)
            a = _dot(ws_ref[2 * p], rhs)
            b = _dot(ws_ref[2 * p + 1], rhs)
            mixed_slabs.append(jnp.where(low_head, a, b))
        return mixed_slabs

    def finish_mlp(u_raw, mixed_slabs):
        u = _gelu2_tanh(u_raw)
        rows = []
        for c in range(nblk):
            mixed_c = jnp.concatenate(
                [m[:, c * LANES:(c + 1) * LANES] for m in mixed_slabs], axis=1) + bs_ref[...]
            rows.append(u[c * BLOCK:(c + 1) * BLOCK, :] * mixed_c)
        o = jnp.concatenate(rows, axis=0)
        o = o * lax.rsqrt(jnp.mean(o * o, axis=-1, keepdims=True) + 4.0 * EPS) * ogain_ref[...]
        mlp_ref[...] = o.astype(BF16)

    hn = _mod_norm(x_ref[...], gain_ref[...], mod_ref[0:1, :], mod_ref[1:2, :]).astype(BF16)
    q, kv = project(q_cols), project(kv_cols)
    q_ms = [_head_ms(q[:, h * MXU_DIM:(h + 1) * MXU_DIM], bd) for h in halves]
    k_ms = _head_ms(kv[:, :kvw], bd[:kvw, :kvw])
    g_raw = project(g_cols)
    finish_qkv(q, kv, q_ms, k_ms)
    u_raw = project(u_cols)
    finish_mlp(u_raw, gate_mix(g_raw))


def _inproj(x, mod3, norm_gain, w_in_bf, cos, sup, sdn, qgain, kgain2, ggain, ws_bf, bs_full,
            ogain, bd, later_weights, tile):
    b, s, d = x.shape
    inw = w_in_bf.shape[1]
    aw = ATTN_HEADS * HEAD_DIM
    mw = MLP_HEADS * HEAD_DIM
    n_steps = (s // tile) * b
    const = lambda shape: pl.BlockSpec(shape, lambda i, bb: (0,) * len(shape))

    def slab_spec(w):
        rows = next(r for r in range(BF16_SUBLANES, w.shape[0] + 1, BF16_SUBLANES)
                    if w.shape[0] % r == 0 and w.shape[0] // r <= n_steps)
        n_slabs = w.shape[0] // rows
        return pl.BlockSpec((rows, w.shape[1]), lambda i, bb: (jnp.minimum(i * b + bb, n_slabs - 1), 0))

    slab_specs = [slab_spec(w) for w in later_weights]
    table_spec = pl.BlockSpec((tile, LANES), lambda i, bb: (i, 0))
    return pl.pallas_call(
        _inproj_kernel,
        grid=(s // tile, b),
        in_specs=[pl.BlockSpec((None, tile, d), lambda i, bb: (bb, i, 0)),
                  pl.BlockSpec((None, N_MOD, d), lambda i, bb: (bb, 0, 0)),
                  const((1, d)),
                  const((d, inw)),
                  table_spec, table_spec, table_spec,
                  const((1, aw)), const((1, LANES)), const((1, mw)),
                  const((MLP_HEADS, BLOCK, BLOCK)), const((BLOCK, mw)), const((1, mw)),
                  const((MXU_DIM, MXU_DIM))] + slab_specs,
        out_specs=[pl.BlockSpec((None, tile, LANES), lambda i, bb: (bb, i, 0)),
                   pl.BlockSpec((None, LANES, tile), lambda i, bb: (bb, 0, i)),
                   pl.BlockSpec((None, ATTN_KV_HEADS, HEAD_DIM, ATTN_GROUP * tile),
                                lambda i, bb: (bb, 0, 0, i)),
                   pl.BlockSpec((None, tile, mw), lambda i, bb: (bb, i, 0))] + slab_specs,
        out_shape=[jax.ShapeDtypeStruct((b, s, LANES), BF16),
                   jax.ShapeDtypeStruct((b, LANES, s), BF16),
                   jax.ShapeDtypeStruct((b, ATTN_KV_HEADS, HEAD_DIM, ATTN_GROUP * s), BF16),
                   jax.ShapeDtypeStruct((b, s, mw), BF16)]
        + [jax.ShapeDtypeStruct(w.shape, BF16) for w in later_weights],
        compiler_params=pltpu.CompilerParams(
            dimension_semantics=("arbitrary", "arbitrary"), vmem_limit_bytes=VMEM_LIMIT),
        name="inproj",
    )(x, mod3, norm_gain, w_in_bf, cos, sup, sdn, qgain, kgain2, ggain, ws_bf, bs_full, ogain, bd,
      *later_weights)


def _mix_ffn_kernel(tiles_per_seq, bounded,
                    x_ref, mod_ref, kp_ref, km_ref, kn_ref, vp_ref, vm_ref, vn_ref,
                    kc_ref, vct_ref, qt_ref, mlp_ref, sink_ref, bias_ref, again_ref, wo_ref,
                    fmod_ref, fgain_ref, wgu_ref, wd_ref,
                    o_ref, h_ref, hn_ref, hid_ref):
    tq = x_ref.shape[0]
    nblk = tq // BLOCK
    step_id = pl.program_id(0)
    n_tiles = pl.num_programs(0) - 1
    i = lax.rem(jnp.minimum(step_id, n_tiles - 1), tiles_per_seq)
    last = tiles_per_seq - 1
    gq = ATTN_GROUP * BLOCK
    slot_w = lax.rem(step_id, 2)
    slot_r = 1 - slot_w

    @pl.when(step_id == 0)
    def _():
        h_ref[1] = jnp.zeros(h_ref.shape[1:], F32)
        hn_ref[...] = jnp.zeros(hn_ref.shape, BF16)

    ff = wd_ref.shape[0]
    n_chunks = ff // MXU_DIM
    def ffn_matmuls(c):
        hn = hn_ref[...]
        return (_dot(hn, wgu_ref[:, c * MXU_DIM:(c + 1) * MXU_DIM]),
                _dot(hn, wgu_ref[:, ff + c * MXU_DIM:ff + (c + 1) * MXU_DIM]))

    def ffn_activation(c, gate_up):
        a, b = gate_up
        hid_ref[:, c * MXU_DIM:(c + 1) * MXU_DIM] = (_silu(a) * b).astype(BF16)

    k_ext = jnp.concatenate([kp_ref[...], km_ref[...], kn_ref[...]], axis=0)
    vt_ext = jnp.concatenate([vp_ref[...], vm_ref[...], vn_ref[...]], axis=1)
    kc = kc_ref[...]
    vct = vct_ref[...]
    cw = HEADS_PER_STEP * BLOCK
    zeros_q = jnp.zeros((HEAD_DIM, cw), BF16)
    ones_rows = jnp.ones((BF16_SUBLANES, 3 * BLOCK + kc.shape[0]), BF16)
    bias_prev = bias_ref[0:BLOCK, :cw]
    bias_next = bias_ref[BLOCK:2 * BLOCK, :cw]

    def score_matmul(jb, kvh, part):
        qt = qt_ref[kvh, :, jb * gq + part * cw:jb * gq + (part + 1) * cw]
        rhs = jnp.concatenate([qt, zeros_q] if kvh == 0 else [zeros_q, qt], axis=0)
        return _dot(kc, rhs), _dot(k_ext[jb * BLOCK:(jb + 3) * BLOCK, :], rhs)

    def mask_and_max(jb, kvh, part, s):
        pen_prev = jnp.where(i == 0, MASKED, 0.0) if jb == 0 else 0.0
        pen_next = jnp.where(i == last, MASKED, 0.0) if jb == nblk - 1 else 0.0
        s_ctx, s = s
        parts = [s[0:BLOCK] + (bias_prev + pen_prev), s[BLOCK:2 * BLOCK],
                 s[2 * BLOCK:3 * BLOCK] + (bias_next + pen_next), s_ctx]
        if bounded:
            return jnp.concatenate([jnp.exp2(p_).astype(BF16) for p_ in parts], axis=0), None
        m = jnp.maximum(
            jnp.maximum(jnp.max(parts[0], axis=0, keepdims=True), jnp.max(parts[1], axis=0, keepdims=True)),
            jnp.maximum(jnp.max(parts[2], axis=0, keepdims=True), jnp.max(parts[3], axis=0, keepdims=True)))
        return parts, jnp.maximum(m, sink_ref[kvh, :, part * cw:(part + 1) * cw])

    def exp_weights(parts, m):
        return jnp.concatenate([jnp.exp2(p_ - m).astype(BF16) for p_ in parts], axis=0)

    def value_matmul(jb, kvh, part, p):
        v_all = jnp.concatenate(
            [vt_ext[kvh * HEAD_DIM:(kvh + 1) * HEAD_DIM, jb * BLOCK:(jb + 3) * BLOCK],
             vct[kvh * HEAD_DIM:(kvh + 1) * HEAD_DIM, :]], axis=1)
        return _dot(jnp.concatenate([v_all, ones_rows], axis=0), p)

    def normalize(jb, kvh, part, pv, m):
        sink = sink_ref[kvh, :, part * cw:(part + 1) * cw]
        denom = pv[HEAD_DIM:HEAD_DIM + 1, :] + jnp.exp2(sink if bounded else sink - m)
        o_t = pv[:HEAD_DIM, :] * (1.0 / denom)
        return [o_t[:, g * BLOCK:(g + 1) * BLOCK] for g in range(HEADS_PER_STEP)]

    def block_norm(out_t):
        o_all = jnp.concatenate(out_t, axis=0)
        ms = jnp.mean(o_all * o_all, axis=0, keepdims=True)
        y = o_all * lax.rsqrt(ms + EPS) * again_ref[...]
        return y.T.astype(BF16)

    aw = ATTN_HEADS * HEAD_DIM
    steps = [(jb, kvh, part) for jb in range(nblk) for kvh in range(ATTN_KV_HEADS)
             for part in range(ATTN_GROUP // HEADS_PER_STEP)]
    assert n_chunks >= len(steps)
    attn_rows, out_t, unnormalized, gate_up = [], [], None, {}

    def collect(step, pv, m):
        out_t.extend(normalize(*step, pv, m))
        if len(out_t) == ATTN_HEADS:
            attn_rows.append(block_norm(out_t))
            out_t.clear()

    if bounded:
        proj_mlp = _dot(mlp_ref[...], wo_ref[aw:, :])
        for n, step in enumerate(steps):
            scores = score_matmul(*step)
            gate_up[n] = ffn_matmuls(n)
            weights, _ = mask_and_max(*step, scores)
            if unnormalized is not None:
                collect(*unnormalized)
                ffn_activation(n - 1, gate_up.pop(n - 1))
            unnormalized = (step, value_matmul(*step, weights), None)
    else:
        raw = {n: score_matmul(*steps[n]) for n in range(2)}
        proj_mlp = _dot(mlp_ref[...], wo_ref[aw:, :])
        ready = {0: mask_and_max(*steps[0], raw.pop(0))}
        for n, step in enumerate(steps):
            if n + 1 < len(steps):
                ready[n + 1] = mask_and_max(*steps[n + 1], raw.pop(n + 1))
            if n + 2 < len(steps):
                raw[n + 2] = score_matmul(*steps[n + 2])
            gate_up[n] = ffn_matmuls(n)
            if unnormalized is not None:
                collect(*unnormalized)
                ffn_activation(n - 1, gate_up.pop(n - 1))
            parts, m = ready.pop(n)
            unnormalized = (step, value_matmul(*step, exp_weights(parts, m)), m)

    c_next = len(steps)
    gate_up[c_next] = ffn_matmuls(c_next)
    collect(*unnormalized)
    ffn_activation(c_next - 1, gate_up.pop(c_next - 1))
    for c in range(c_next + 1, n_chunks):
        gate_up[c] = ffn_matmuls(c)
        ffn_activation(c - 1, gate_up.pop(c - 1))
    proj_attn = _dot(jnp.concatenate(attn_rows, axis=0), wo_ref[:aw, :])
    ffn_activation(n_chunks - 1, gate_up.pop(n_chunks - 1))
    h_new = x_ref[...] + mod_ref[2:3, :] * (proj_attn + proj_mlp)
    h_ref[slot_w] = h_new
    hn_ref[...] = _mod_norm(h_new, fgain_ref[...], mod_ref[3:4, :], mod_ref[4:5, :]).astype(BF16)
    o_ref[...] = h_ref[slot_r] + fmod_ref[5:6, :] * _dot(hid_ref[...], wd_ref[...])


def _mix_ffn(x, mod3, k, vt, kc, vct, qt, mlpn, sink_rows, bias, again_b, wo_bf,
             ffn_gain, wgu_bf, wd_bf, tile, bounded):
    b, s, d = x.shape
    c = kc.shape[1]
    aw = ATTN_HEADS * HEAD_DIM
    mw = mlpn.shape[2]
    ff = wd_bf.shape[0]
    assert ff % MXU_DIM == 0
    r = tile // BLOCK
    nb = s // BLOCK
    nt = s // tile
    n_tiles = b * nt
    gq = ATTN_GROUP * BLOCK

    def mix_tile(g):
        t = jnp.minimum(g, n_tiles - 1)
        return t // nt, t % nt

    def ffn_tile(g):
        t = jnp.maximum(g - 1, 0)
        return t // nt, t % nt

    def at_mix(fn):
        return lambda g: fn(*mix_tile(g))

    const = lambda shape: pl.BlockSpec(shape, lambda g: (0,) * len(shape))
    return pl.pallas_call(
        functools.partial(_mix_ffn_kernel, nt, bounded),
        grid=(n_tiles + 1,),
        in_specs=[pl.BlockSpec((None, tile, d), at_mix(lambda bb, i: (bb, i, 0))),
                  pl.BlockSpec((None, N_MOD, d), at_mix(lambda bb, i: (bb, 0, 0))),
                  pl.BlockSpec((None, BLOCK, LANES), at_mix(lambda bb, i: (bb, jnp.maximum(i * r - 1, 0), 0))),
                  pl.BlockSpec((None, tile, LANES), at_mix(lambda bb, i: (bb, i, 0))),
                  pl.BlockSpec((None, BLOCK, LANES),
                               at_mix(lambda bb, i: (bb, jnp.minimum((i + 1) * r, nb - 1), 0))),
                  pl.BlockSpec((None, LANES, BLOCK), at_mix(lambda bb, i: (bb, 0, jnp.maximum(i * r - 1, 0)))),
                  pl.BlockSpec((None, LANES, tile), at_mix(lambda bb, i: (bb, 0, i))),
                  pl.BlockSpec((None, LANES, BLOCK),
                               at_mix(lambda bb, i: (bb, 0, jnp.minimum((i + 1) * r, nb - 1)))),
                  pl.BlockSpec((None, c, LANES), at_mix(lambda bb, i: (bb, 0, 0))),
                  pl.BlockSpec((None, LANES, c), at_mix(lambda bb, i: (bb, 0, 0))),
                  pl.BlockSpec((None, ATTN_KV_HEADS, HEAD_DIM, ATTN_GROUP * tile),
                               at_mix(lambda bb, i: (bb, 0, 0, i))),
                  pl.BlockSpec((None, tile, mw), at_mix(lambda bb, i: (bb, i, 0))),
                  const((ATTN_KV_HEADS, 1, gq)),
                  const((2 * BLOCK, gq)),
                  const((aw, BLOCK)),
                  const((aw + mw, d)),
                  pl.BlockSpec((None, N_MOD, d), lambda g: (ffn_tile(g)[0], 0, 0)),
                  const((1, d)), const((d, 2 * ff)), const((ff, d))],
        out_specs=pl.BlockSpec((None, tile, d), lambda g: (*ffn_tile(g), 0)),
        out_shape=jax.ShapeDtypeStruct((b, s, d), F32),
        scratch_shapes=[pltpu.VMEM((2, tile, d), F32), pltpu.VMEM((tile, d), BF16),
                        pltpu.VMEM((tile, ff), BF16)],
        compiler_params=pltpu.CompilerParams(
            dimension_semantics=("arbitrary",), vmem_limit_bytes=VMEM_LIMIT),
        name="mix_ffn",
    )(x, mod3, k, k, k, vt, vt, vt, kc, vct, qt, mlpn, sink_rows, bias, again_b, wo_bf,
      mod3, ffn_gain, wgu_bf, wd_bf)


def _rope_tables(s):
    axis_dim = HEAD_DIM // 2
    pos = np.arange(s)
    inv_freq = (ROPE_THETA ** (-np.arange(0, axis_dim, 2, dtype=np.float32) / axis_dim)).astype(np.float32)
    ang_r = (pos // GRID_W).astype(np.float32)[:, None] * inv_freq[None, :]
    ang_c = (pos % GRID_W).astype(np.float32)[:, None] * inv_freq[None, :]
    cr, sr, cc, sc = np.cos(ang_r), np.sin(ang_r), np.cos(ang_c), np.sin(ang_c)
    z = np.zeros_like(sr)
    reps = LANES // HEAD_DIM
    cos = np.tile(np.concatenate([cr, cr, cc, cc], axis=1), (1, reps))
    sin_up = np.tile(np.concatenate([-sr, z, -sc, z], axis=1), (1, reps))
    sin_dn = np.tile(np.concatenate([z, sr, z, sc], axis=1), (1, reps))
    return jnp.asarray(cos, F32), jnp.asarray(sin_up, F32), jnp.asarray(sin_dn, F32)


def _window_bias():
    c = np.arange(BLOCK)[:, None]
    r = np.arange(BLOCK)[None, :]
    prev = np.where(c >= r, 0.0, MASKED).astype(np.float32)
    nxt = np.where(c <= r, 0.0, MASKED).astype(np.float32)
    return jnp.asarray(np.tile(np.concatenate([prev, nxt], axis=0), (1, ATTN_GROUP)), F32)


def _head_mean_matrix():
    bd = np.kron(np.eye(MXU_DIM // HEAD_DIM, dtype=np.float32),
                 np.full((HEAD_DIM, HEAD_DIM), 1.0 / HEAD_DIM, np.float32))
    return jnp.asarray(bd, BF16)


def kernel(x, c, ctx, c_ctx, w_mod, b_mod, norm_mix, norm_ffn, w_in, q_gain, k_gain, attn_sink,
           gate_gain, w_spatial, b_spatial, attn_out_gain, mlp_out_gain, w_out, w_gate_up, w_down):
    b, s, d = x.shape
    assert w_mod.shape[0] == 1, "single-layer problem"
    assert s % 1024 == 0 and d % LANES == 0
    aw = ATTN_HEADS * HEAD_DIM
    mw = MLP_HEADS * HEAD_DIM

    rows = -(-(b + 1) // BF16_SUBLANES) * BF16_SUBLANES
    cond = jnp.concatenate([c, c_ctx[None, :], jnp.zeros((rows - b - 1, d), F32)], axis=0)
    mod3 = _adaln(cond, w_mod[0], b_mod[0][None, :]).reshape(rows, N_MOD, d)

    w_in_bf = w_in[0].astype(BF16)
    bd = _head_mean_matrix()
    kgain2 = jnp.tile(k_gain[0], ATTN_KV_HEADS)[None, :]
    qgain = (jnp.tile(q_gain[0], ATTN_HEADS) * (HEAD_DIM ** -0.5 * LOG2E))[None, :]
    norm_mix_g = norm_mix[0][None, :]

    kc, vct = _ctx_kv(ctx, mod3, b, norm_mix_g, w_in_bf, kgain2, bd)

    cos, sup, sdn = _rope_tables(s)
    bs_full = jnp.repeat(b_spatial[0].T, HEAD_DIM, axis=1)
    k, vt, qt, mlpn, wo_bf, wgu_bf, wd_bf = _inproj(
        x, mod3, norm_mix_g, w_in_bf, cos, sup, sdn, qgain, kgain2,
        gate_gain[0].reshape(1, mw), w_spatial[0].astype(BF16), bs_full,
        mlp_out_gain[0][None, :], bd, (w_out[0], w_gate_up[0], w_down[0]), tile=1024)

    sink_rows = jnp.repeat(attn_sink[0].reshape(ATTN_KV_HEADS, ATTN_GROUP) * LOG2E,
                           BLOCK, axis=1)[:, None, :]
    again_b = jnp.broadcast_to(attn_out_gain[0][:, None], (aw, BLOCK))
    logit_bound = (BOUND_MARGIN * HEAD_DIM ** 0.5 * LOG2E
                   * jnp.max(jnp.abs(q_gain[0])) * jnp.max(jnp.abs(k_gain[0])))
    sink_max = jnp.max(jnp.abs(attn_sink[0])) * LOG2E
    use_bound = (logit_bound <= MAX_SAFE_LOGIT) & (sink_max <= MAX_SAFE_LOGIT)
    operands = (x, mod3, k, vt, kc, vct, qt, mlpn, sink_rows, _window_bias(), again_b,
                wo_bf, norm_ffn[0][None, :], wgu_bf, wd_bf)
    return lax.cond(use_bound,
                    lambda ops: _mix_ffn(*ops, tile=512, bounded=True),
                    lambda ops: _mix_ffn(*ops, tile=512, bounded=False),
                    operands)
```

```python
import functools
import math

import jax
import jax.numpy as jnp
import numpy as np
from jax import lax
from jax.experimental import pallas as pl
from jax.experimental.pallas import tpu as pltpu

F32 = jnp.float32
BF16 = jnp.bfloat16

HEAD_DIM = 64
ATTN_HEADS = 8
ATTN_KV_HEADS = 2
ATTN_GROUP = ATTN_HEADS // ATTN_KV_HEADS
MLP_HEADS = 8
N_MOD = 6
BLOCK = 128
GRID_W = 64
ROPE_THETA = 10000.0
EPS = 1e-6
MASKED = -1e30
BOUND_MARGIN = 1.05
MAX_SAFE_LOGIT = 40.0
LOG2E = math.log2(math.e)

LANES = 128
BF16_SUBLANES = 16
MXU_DIM = 256
VMEM_LIMIT = 56 * 1024 * 1024
HEADS_PER_STEP = 4


def _dot(a, b):
    return jnp.dot(a, b, preferred_element_type=F32)


def _silu(x):
    return x * (1.0 / (1.0 + jnp.exp(-x)))


def _gelu2_tanh(x):
    c = math.sqrt(2.0 / math.pi)
    return x * (1.0 + jnp.tanh(x * (c + (c * 0.044715) * (x * x))))


def _mod_norm(x, gain, shift, scale):
    y = x * lax.rsqrt(jnp.mean(x * x, axis=-1, keepdims=True) + EPS)
    return y * (gain * (1.0 + scale)) + shift


def _head_ms(x, bd):
    return _dot((x * x).astype(BF16), bd)


def _rope(x, cos, sin_up, sin_dn):
    up = pltpu.roll(x, LANES - 16, 1)
    dn = pltpu.roll(x, 16, 1)
    return x * cos + up * sin_up + dn * sin_dn


def _split_bf16(x):
    hi = x.astype(BF16)
    return hi, (x - hi.astype(F32)).astype(BF16)


def _adaln_kernel(cond_ref, wa_ref, wb_ref, b_ref, o_ref):
    rows = cond_ref.shape[0]
    half = wa_ref.shape[1]

    @pl.when(pl.program_id(0) == 0)
    def _():
        o_ref[...] = jnp.broadcast_to(b_ref[...], o_ref.shape)

    s_hi, s_lo = _split_bf16(_silu(cond_ref[...]))
    s_both = jnp.concatenate([s_hi, s_lo], axis=0)
    for part, w_ref in enumerate((wa_ref, wb_ref)):
        w_hi, w_lo = _split_bf16(w_ref[...])
        both = _dot(s_both, w_hi)
        o_ref[:, part * half:(part + 1) * half] += both[:rows] + both[rows:] + _dot(s_hi, w_lo)


def _adaln(cond, w_mod, b_mod):
    rows, d = cond.shape
    n = w_mod.shape[1]
    tk = d // 8
    assert rows % BF16_SUBLANES == 0 and tk % LANES == 0 and (n // 2) % LANES == 0
    return pl.pallas_call(
        _adaln_kernel,
        grid=(d // tk,),
        in_specs=[pl.BlockSpec((rows, tk), lambda j: (0, j)),
                  pl.BlockSpec((tk, n // 2), lambda j: (j, 0)),
                  pl.BlockSpec((tk, n // 2), lambda j: (j, 1)),
                  pl.BlockSpec((1, n), lambda j: (0, 0))],
        out_specs=pl.BlockSpec((rows, n), lambda j: (0, 0)),
        out_shape=jax.ShapeDtypeStruct((rows, n), F32),
        compiler_params=pltpu.CompilerParams(
            dimension_semantics=("arbitrary",), vmem_limit_bytes=VMEM_LIMIT),
        name="adaln",
    )(cond, w_mod, w_mod, b_mod)


def _ctx_kernel(x_ref, mod_ref, gain_ref, w_ref, kgain_ref, bd_ref, kc_ref, vct_ref):
    per_step, c, d = x_ref.shape
    x = x_ref[...].reshape(per_step * c, d)
    hn = _mod_norm(x, gain_ref[...], mod_ref[0:1, :], mod_ref[1:2, :])
    kv = _dot(hn.astype(BF16), w_ref[...])
    k = kv[:, :LANES]
    k = k * lax.rsqrt(_head_ms(k, bd_ref[:LANES, :LANES]) + EPS) * kgain_ref[...]
    for j in range(per_step):
        kc_ref[j] = k[j * c:(j + 1) * c, :].astype(BF16)
        vct_ref[j] = kv[j * c:(j + 1) * c, LANES:].T.astype(BF16)


def _ctx_kv(ctx, mod3, ctx_row, norm_gain, w_in_bf, kgain2, bd):
    b, c, d = ctx.shape
    kvw = 2 * ATTN_KV_HEADS * HEAD_DIM
    per_step = 2 if b % 2 == 0 else 1
    return pl.pallas_call(
        _ctx_kernel,
        grid=(b // per_step,),
        in_specs=[pl.BlockSpec((per_step, c, d), lambda i: (i, 0, 0)),
                  pl.BlockSpec((None, N_MOD, d), lambda i: (ctx_row, 0, 0)),
                  pl.BlockSpec((1, d), lambda i: (0, 0)),
                  pl.BlockSpec((d, kvw), lambda i: (0, 0)),
                  pl.BlockSpec((1, LANES), lambda i: (0, 0)),
                  pl.BlockSpec((MXU_DIM, MXU_DIM), lambda i: (0, 0))],
        out_specs=[pl.BlockSpec((per_step, c, LANES), lambda i: (i, 0, 0)),
                   pl.BlockSpec((per_step, LANES, c), lambda i: (i, 0, 0))],
        out_shape=[jax.ShapeDtypeStruct((b, c, LANES), BF16),
                   jax.ShapeDtypeStruct((b, LANES, c), BF16)],
        compiler_params=pltpu.CompilerParams(
            dimension_semantics=("arbitrary",), vmem_limit_bytes=VMEM_LIMIT),
        name="ctx_kv",
    )(ctx, mod3, norm_gain, w_in_bf, kgain2, bd)


def _inproj_kernel(x_ref, mod_ref, gain_ref, w_ref, cos_ref, sup_ref, sdn_ref,
                   qgain_ref, kgain_ref, ggain_ref, ws_ref, bs_ref, ogain_ref, bd_ref,
                   wo_f32_ref, wgu_f32_ref, wd_f32_ref,
                   k_ref, vt_ref, qt_ref, mlp_ref, wo_bf_ref, wgu_bf_ref, wd_bf_ref):
    wo_bf_ref[...] = wo_f32_ref[...].astype(BF16)
    wgu_bf_ref[...] = wgu_f32_ref[...].astype(BF16)
    wd_bf_ref[...] = wd_f32_ref[...].astype(BF16)

    t = x_ref.shape[0]
    nblk = t // BLOCK
    kvw = ATTN_KV_HEADS * HEAD_DIM
    aw = ATTN_HEADS * HEAD_DIM
    mw = MLP_HEADS * HEAD_DIM
    q_cols = slice(2 * kvw, 2 * kvw + aw)
    kv_cols = slice(0, 2 * kvw)
    u_cols = slice(2 * kvw + aw, 2 * kvw + aw + mw)
    g_cols = slice(2 * kvw + aw + mw, 2 * kvw + aw + 2 * mw)
    halves = range(aw // MXU_DIM)
    bd = bd_ref[...]

    def project(cols):
        return _dot(hn, w_ref[:, cols])

    def finish_qkv(q, kv, q_ms, k_ms):
        cos, sup, sdn = cos_ref[...], sup_ref[...], sdn_ref[...]
        k = kv[:, :kvw] * lax.rsqrt(k_ms + EPS) * kgain_ref[...]
        k_ref[...] = _rope(k, cos, sup, sdn).astype(BF16)
        vt_ref[...] = kv[:, kvw:].T.astype(BF16)
        for half in halves:
            qh = q[:, half * MXU_DIM:(half + 1) * MXU_DIM] * lax.rsqrt(q_ms[half] + EPS)
            qh = qh * qgain_ref[:, half * MXU_DIM:(half + 1) * MXU_DIM]
            for sl in range(MXU_DIM // LANES):
                qs = _rope(qh[:, sl * LANES:(sl + 1) * LANES], cos, sup, sdn)
                qst = qs.T
                for hh in range(LANES // HEAD_DIM):
                    head = (half * MXU_DIM + sl * LANES) // HEAD_DIM + hh
                    kvh, grp = head // ATTN_GROUP, head % ATTN_GROUP
                    for jb in range(nblk):
                        col = (jb * ATTN_GROUP + grp) * BLOCK
                        qt_ref[kvh, :, col:col + BLOCK] = qst[
                            hh * HEAD_DIM:(hh + 1) * HEAD_DIM, jb * BLOCK:(jb + 1) * BLOCK].astype(BF16)

    def gate_mix(g_raw):
        g = _gelu2_tanh(g_raw)
        g_ms = [_head_ms(g[:, h * MXU_DIM:(h + 1) * MXU_DIM], bd) for h in halves]
        gn = jnp.concatenate(
            [g[:, h * MXU_DIM:(h + 1) * MXU_DIM] * lax.rsqrt(g_ms[h] + 4.0 * EPS) for h in halves],
            axis=1) * ggain_ref[...]
        gnb = gn.astype(BF16)
        low_head = lax.broadcasted_iota(jnp.int32, (BLOCK, t), 1) % LANES < HEAD_DIM
        mixed_slabs = []
        for p in range(MLP_HEADS // 2):
            rhs = jnp.concatenate(
                [gnb[c * BLOCK:(c + 1) * BLOCK, p * LANES:(p + 1) * LANES] for c in range(nblk)], axis=1)
            a = _dot(ws_ref[2 * p], rhs)
            b = _dot(ws_ref[2 * p + 1], rhs)
            mixed_slabs.append(jnp.where(low_head, a, b))
        return mixed_slabs

    def finish_mlp(u_raw, mixed_slabs):
        u = _gelu2_tanh(u_raw)
        rows = []
        for c in range(nblk):
            mixed_c = jnp.concatenate(
                [m[:, c * LANES:(c + 1) * LANES] for m in mixed_slabs], axis=1) + bs_ref[...]
            rows.append(u[c * BLOCK:(c + 1) * BLOCK, :] * mixed_c)
        o = jnp.concatenate(rows, axis=0)
        o = o * lax.rsqrt(jnp.mean(o * o, axis=-1, keepdims=True) + 4.0 * EPS) * ogain_ref[...]
        mlp_ref[...] = o.astype(BF16)

    hn = _mod_norm(x_ref[...], gain_ref[...], mod_ref[0:1, :], mod_ref[1:2, :]).astype(BF16)
    q, kv = project(q_cols), project(kv_cols)
    q_ms = [_head_ms(q[:, h * MXU_DIM:(h + 1) * MXU_DIM], bd) for h in halves]
    k_ms = _head_ms(kv[:, :kvw], bd[:kvw, :kvw])
    g_raw = project(g_cols)
    finish_qkv(q, kv, q_ms, k_ms)
    u_raw = project(u_cols)
    finish_mlp(u_raw, gate_mix(g_raw))


def _inproj(x, mod3, norm_gain, w_in_bf, cos, sup, sdn, qgain, kgain2, ggain, ws_bf, bs_full,
            ogain, bd, later_weights, tile):
    b, s, d = x.shape
    inw = w_in_bf.shape[1]
    aw = ATTN_HEADS * HEAD_DIM
    mw = MLP_HEADS * HEAD_DIM
    n_steps = (s // tile) * b
    const = lambda shape: pl.BlockSpec(shape, lambda i, bb: (0,) * len(shape))

    def slab_spec(w):
        rows = next(r for r in range(BF16_SUBLANES, w.shape[0] + 1, BF16_SUBLANES)
                    if w.shape[0] % r == 0 and w.shape[0] // r <= n_steps)
        n_slabs = w.shape[0] // rows
        return pl.BlockSpec((rows, w.shape[1]), lambda i, bb: (jnp.minimum(i * b + bb, n_slabs - 1), 0))

    slab_specs = [slab_spec(w) for w in later_weights]
    table_spec = pl.BlockSpec((tile, LANES), lambda i, bb: (i, 0))
    return pl.pallas_call(
        _inproj_kernel,
        grid=(s // tile, b),
        in_specs=[pl.BlockSpec((None, tile, d), lambda i, bb: (bb, i, 0)),
                  pl.BlockSpec((None, N_MOD, d), lambda i, bb: (bb, 0, 0)),
                  const((1, d)),
                  const((d, inw)),
                  table_spec, table_spec, table_spec,
                  const((1, aw)), const((1, LANES)), const((1, mw)),
                  const((MLP_HEADS, BLOCK, BLOCK)), const((BLOCK, mw)), const((1, mw)),
                  const((MXU_DIM, MXU_DIM))] + slab_specs,
        out_specs=[pl.BlockSpec((None, tile, LANES), lambda i, bb: (bb, i, 0)),
                   pl.BlockSpec((None, LANES, tile), lambda i, bb: (bb, 0, i)),
                   pl.BlockSpec((None, ATTN_KV_HEADS, HEAD_DIM, ATTN_GROUP * tile),
                                lambda i, bb: (bb, 0, 0, i)),
                   pl.BlockSpec((None, tile, mw), lambda i, bb: (bb, i, 0))] + slab_specs,
        out_shape=[jax.ShapeDtypeStruct((b, s, LANES), BF16),
                   jax.ShapeDtypeStruct((b, LANES, s), BF16),
                   jax.ShapeDtypeStruct((b, ATTN_KV_HEADS, HEAD_DIM, ATTN_GROUP * s), BF16),
                   jax.ShapeDtypeStruct((b, s, mw), BF16)]
        + [jax.ShapeDtypeStruct(w.shape, BF16) for w in later_weights],
        compiler_params=pltpu.CompilerParams(
            dimension_semantics=("arbitrary", "arbitrary"), vmem_limit_bytes=VMEM_LIMIT),
        name="inproj",
    )(x, mod3, norm_gain, w_in_bf, cos, sup, sdn, qgain, kgain2, ggain, ws_bf, bs_full, ogain, bd,
      *later_weights)


def _mix_ffn_kernel(tiles_per_seq, bounded,
                    x_ref, mod_ref, kp_ref, km_ref, kn_ref, vp_ref, vm_ref, vn_ref,
                    kc_ref, vct_ref, qt_ref, mlp_ref, sink_ref, bias_ref, again_ref, wo_ref,
                    fmod_ref, fgain_ref, wgu_ref, wd_ref,
                    o_ref, h_ref, hn_ref, hid_ref):
    tq = x_ref.shape[0]
    nblk = tq // BLOCK
    step_id = pl.program_id(0)
    n_tiles = pl.num_programs(0) - 1
    i = lax.rem(jnp.minimum(step_id, n_tiles - 1), tiles_per_seq)
    last = tiles_per_seq - 1
    gq = ATTN_GROUP * BLOCK
    slot_w = lax.rem(step_id, 2)
    slot_r = 1 - slot_w

    @pl.when(step_id == 0)
    def _():
        h_ref[1] = jnp.zeros(h_ref.shape[1:], F32)
        hn_ref[...] = jnp.zeros(hn_ref.shape, BF16)

    ff = wd_ref.shape[0]
    n_chunks = ff // MXU_DIM
    def ffn_matmuls(c):
        hn = hn_ref[...]
        return (_dot(hn, wgu_ref[:, c * MXU_DIM:(c + 1) * MXU_DIM]),
                _dot(hn, wgu_ref[:, ff + c * MXU_DIM:ff + (c + 1) * MXU_DIM]))

    def ffn_activation(c, gate_up):
        a, b = gate_up
        hid_ref[:, c * MXU_DIM:(c + 1) * MXU_DIM] = (_silu(a) * b).astype(BF16)

    k_ext = jnp.concatenate([kp_ref[...], km_ref[...], kn_ref[...]], axis=0)
    vt_ext = jnp.concatenate([vp_ref[...], vm_ref[...], vn_ref[...]], axis=1)
    kc = kc_ref[...]
    vct = vct_ref[...]
    cw = HEADS_PER_STEP * BLOCK
    zeros_q = jnp.zeros((HEAD_DIM, cw), BF16)
    ones_rows = jnp.ones((BF16_SUBLANES, 3 * BLOCK + kc.shape[0]), BF16)
    bias_prev = bias_ref[0:BLOCK, :cw]
    bias_next = bias_ref[BLOCK:2 * BLOCK, :cw]

    def score_matmul(jb, kvh, part):
        qt = qt_ref[kvh, :, jb * gq + part * cw:jb * gq + (part + 1) * cw]
        rhs = jnp.concatenate([qt, zeros_q] if kvh == 0 else [zeros_q, qt], axis=0)
        return _dot(kc, rhs), _dot(k_ext[jb * BLOCK:(jb + 3) * BLOCK, :], rhs)

    def mask_and_max(jb, kvh, part, s):
        pen_prev = jnp.where(i == 0, MASKED, 0.0) if jb == 0 else 0.0
        pen_next = jnp.where(i == last, MASKED, 0.0) if jb == nblk - 1 else 0.0
        s_ctx, s = s
        parts = [s[0:BLOCK] + (bias_prev + pen_prev), s[BLOCK:2 * BLOCK],
                 s[2 * BLOCK:3 * BLOCK] + (bias_next + pen_next), s_ctx]
        if bounded:
            return jnp.concatenate([jnp.exp2(p_).astype(BF16) for p_ in parts], axis=0), None
        m = jnp.maximum(
            jnp.maximum(jnp.max(parts[0], axis=0, keepdims=True), jnp.max(parts[1], axis=0, keepdims=True)),
            jnp.maximum(jnp.max(parts[2], axis=0, keepdims=True), jnp.max(parts[3], axis=0, keepdims=True)))
        return parts, jnp.maximum(m, sink_ref[kvh, :, part * cw:(part + 1) * cw])

    def exp_weights(parts, m):
        return jnp.concatenate([jnp.exp2(p_ - m).astype(BF16) for p_ in parts], axis=0)

    def value_matmul(jb, kvh, part, p):
        v_all = jnp.concatenate(
            [vt_ext[kvh * HEAD_DIM:(kvh + 1) * HEAD_DIM, jb * BLOCK:(jb + 3) * BLOCK],
             vct[kvh * HEAD_DIM:(kvh + 1) * HEAD_DIM, :]], axis=1)
        return _dot(jnp.concatenate([v_all, ones_rows], axis=0), p)

    def normalize(jb, kvh, part, pv, m):
        sink = sink_ref[kvh, :, part * cw:(part + 1) * cw]
        denom = pv[HEAD_DIM:HEAD_DIM + 1, :] + jnp.exp2(sink if bounded else sink - m)
        o_t = pv[:HEAD_DIM, :] * (1.0 / denom)
        return [o_t[:, g * BLOCK:(g + 1) * BLOCK] for g in range(HEADS_PER_STEP)]

    def block_norm(out_t):
        o_all = jnp.concatenate(out_t, axis=0)
        ms = jnp.mean(o_all * o_all, axis=0, keepdims=True)
        y = o_all * lax.rsqrt(ms + EPS) * again_ref[...]
        return y.T.astype(BF16)

    aw = ATTN_HEADS * HEAD_DIM
    steps = [(jb, kvh, part) for jb in range(nblk) for kvh in range(ATTN_KV_HEADS)
             for part in range(ATTN_GROUP // HEADS_PER_STEP)]
    assert n_chunks >= len(steps)
    attn_rows, out_t, unnormalized, gate_up = [], [], None, {}

    def collect(step, pv, m):
        out_t.extend(normalize(*step, pv, m))
        if len(out_t) == ATTN_HEADS:
            attn_rows.append(block_norm(out_t))
            out_t.clear()

    if bounded:
        proj_mlp = _dot(mlp_ref[...], wo_ref[aw:, :])
        for n, step in enumerate(steps):
            scores = score_matmul(*step)
            gate_up[n] = ffn_matmuls(n)
            weights, _ = mask_and_max(*step, scores)
            if unnormalized is not None:
                collect(*unnormalized)
                ffn_activation(n - 1, gate_up.pop(n - 1))
            unnormalized = (step, value_matmul(*step, weights), None)
    else:
        raw = {n: score_matmul(*steps[n]) for n in range(2)}
        proj_mlp = _dot(mlp_ref[...], wo_ref[aw:, :])
        ready = {0: mask_and_max(*steps[0], raw.pop(0))}
        for n, step in enumerate(steps):
            if n + 1 < len(steps):
                ready[n + 1] = mask_and_max(*steps[n + 1], raw.pop(n + 1))
            if n + 2 < len(steps):
                raw[n + 2] = score_matmul(*steps[n + 2])
            gate_up[n] = ffn_matmuls(n)
            if unnormalized is not None:
                collect(*unnormalized)
                ffn_activation(n - 1, gate_up.pop(n - 1))
            parts, m = ready.pop(n)
            unnormalized = (step, value_matmul(*step, exp_weights(parts, m)), m)

    c_next = len(steps)
    gate_up[c_next] = ffn_matmuls(c_next)
    collect(*unnormalized)
    ffn_activation(c_next - 1, gate_up.pop(c_next - 1))
    for c in range(c_next + 1, n_chunks):
        gate_up[c] = ffn_matmuls(c)
        ffn_activation(c - 1, gate_up.pop(c - 1))
    proj_attn = _dot(jnp.concatenate(attn_rows, axis=0), wo_ref[:aw, :])
    ffn_activation(n_chunks - 1, gate_up.pop(n_chunks - 1))
    h_new = x_ref[...] + mod_ref[2:3, :] * (proj_attn + proj_mlp)
    h_ref[slot_w] = h_new
    hn_ref[...] = _mod_norm(h_new, fgain_ref[...], mod_ref[3:4, :], mod_ref[4:5, :]).astype(BF16)
    o_ref[...] = h_ref[slot_r] + fmod_ref[5:6, :] * _dot(hid_ref[...], wd_ref[...])


def _mix_ffn(x, mod3, k, vt, kc, vct, qt, mlpn, sink_rows, bias, again_b, wo_bf,
             ffn_gain, wgu_bf, wd_bf, tile, bounded):
    b, s, d = x.shape
    c = kc.shape[1]
    aw = ATTN_HEADS * HEAD_DIM
    mw = mlpn.shape[2]
    ff = wd_bf.shape[0]
    assert ff % MXU_DIM == 0
    r = tile // BLOCK
    nb = s // BLOCK
    nt = s // tile
    n_tiles = b * nt
    gq = ATTN_GROUP * BLOCK

    def mix_tile(g):
        t = jnp.minimum(g, n_tiles - 1)
        return t // nt, t % nt

    def ffn_tile(g):
        t = jnp.maximum(g - 1, 0)
        return t // nt, t % nt

    def at_mix(fn):
        return lambda g: fn(*mix_tile(g))

    const = lambda shape: pl.BlockSpec(shape, lambda g: (0,) * len(shape))
    return pl.pallas_call(
        functools.partial(_mix_ffn_kernel, nt, bounded),
        grid=(n_tiles + 1,),
        in_specs=[pl.BlockSpec((None, tile, d), at_mix(lambda bb, i: (bb, i, 0))),
                  pl.BlockSpec((None, N_MOD, d), at_mix(lambda bb, i: (bb, 0, 0))),
                  pl.BlockSpec((None, BLOCK, LANES), at_mix(lambda bb, i: (bb, jnp.maximum(i * r - 1, 0), 0))),
                  pl.BlockSpec((None, tile, LANES), at_mix(lambda bb, i: (bb, i, 0))),
                  pl.BlockSpec((None, BLOCK, LANES),
                               at_mix(lambda bb, i: (bb, jnp.minimum((i + 1) * r, nb - 1), 0))),
                  pl.BlockSpec((None, LANES, BLOCK), at_mix(lambda bb, i: (bb, 0, jnp.maximum(i * r - 1, 0)))),
                  pl.BlockSpec((None, LANES, tile), at_mix(lambda bb, i: (bb, 0, i))),
                  pl.BlockSpec((None, LANES, BLOCK),
                               at_mix(lambda bb, i: (bb, 0, jnp.minimum((i + 1) * r, nb - 1)))),
                  pl.BlockSpec((None, c, LANES), at_mix(lambda bb, i: (bb, 0, 0))),
                  pl.BlockSpec((None, LANES, c), at_mix(lambda bb, i: (bb, 0, 0))),
                  pl.BlockSpec((None, ATTN_KV_HEADS, HEAD_DIM, ATTN_GROUP * tile),
                               at_mix(lambda bb, i: (bb, 0, 0, i))),
                  pl.BlockSpec((None, tile, mw), at_mix(lambda bb, i: (bb, i, 0))),
                  const((ATTN_KV_HEADS, 1, gq)),
                  const((2 * BLOCK, gq)),
                  const((aw, BLOCK)),
                  const((aw + mw, d)),
                  pl.BlockSpec((None, N_MOD, d), lambda g: (ffn_tile(g)[0], 0, 0)),
                  const((1, d)), const((d, 2 * ff)), const((ff, d))],
        out_specs=pl.BlockSpec((None, tile, d), lambda g: (*ffn_tile(g), 0)),
        out_shape=jax.ShapeDtypeStruct((b, s, d), F32),
        scratch_shapes=[pltpu.VMEM((2, tile, d), F32), pltpu.VMEM((tile, d), BF16),
                        pltpu.VMEM((tile, ff), BF16)],
        compiler_params=pltpu.CompilerParams(
            dimension_semantics=("arbitrary",), vmem_limit_bytes=VMEM_LIMIT),
        name="mix_ffn",
    )(x, mod3, k, k, k, vt, vt, vt, kc, vct, qt, mlpn, sink_rows, bias, again_b, wo_bf,
      mod3, ffn_gain, wgu_bf, wd_bf)


def _rope_tables(s):
    axis_dim = HEAD_DIM // 2
    pos = np.arange(s)
    inv_freq = (ROPE_THETA ** (-np.arange(0, axis_dim, 2, dtype=np.float32) / axis_dim)).astype(np.float32)
    ang_r = (pos // GRID_W).astype(np.float32)[:, None] * inv_freq[None, :]
    ang_c = (pos % GRID_W).astype(np.float32)[:, None] * inv_freq[None, :]
    cr, sr, cc, sc = np.cos(ang_r), np.sin(ang_r), np.cos(ang_c), np.sin(ang_c)
    z = np.zeros_like(sr)
    reps = LANES // HEAD_DIM
    cos = np.tile(np.concatenate([cr, cr, cc, cc], axis=1), (1, reps))
    sin_up = np.tile(np.concatenate([-sr, z, -sc, z], axis=1), (1, reps))
    sin_dn = np.tile(np.concatenate([z, sr, z, sc], axis=1), (1, reps))
    return jnp.asarray(cos, F32), jnp.asarray(sin_up, F32), jnp.asarray(sin_dn, F32)


def _window_bias():
    c = np.arange(BLOCK)[:, None]
    r = np.arange(BLOCK)[None, :]
    prev = np.where(c >= r, 0.0, MASKED).astype(np.float32)
    nxt = np.where(c <= r, 0.0, MASKED).astype(np.float32)
    return jnp.asarray(np.tile(np.concatenate([prev, nxt], axis=0), (1, ATTN_GROUP)), F32)


def _head_mean_matrix():
    bd = np.kron(np.eye(MXU_DIM // HEAD_DIM, dtype=np.float32),
                 np.full((HEAD_DIM, HEAD_DIM), 1.0 / HEAD_DIM, np.float32))
    return jnp.asarray(bd, BF16)


def kernel(x, c, ctx, c_ctx, w_mod, b_mod, norm_mix, norm_ffn, w_in, q_gain, k_gain, attn_sink,
           gate_gain, w_spatial, b_spatial, attn_out_gain, mlp_out_gain, w_out, w_gate_up, w_down):
    b, s, d = x.shape
    assert w_mod.shape[0] == 1, "single-layer problem"
    assert s % 1024 == 0 and d % LANES == 0
    aw = ATTN_HEADS * HEAD_DIM
    mw = MLP_HEADS * HEAD_DIM

    rows = -(-(b + 1) // BF16_SUBLANES) * BF16_SUBLANES
    cond = jnp.concatenate([c, c_ctx[None, :], jnp.zeros((rows - b - 1, d), F32)], axis=0)
    mod3 = _adaln(cond, w_mod[0], b_mod[0][None, :]).reshape(rows, N_MOD, d)

    w_in_bf = w_in[0].astype(BF16)
    bd = _head_mean_matrix()
    kgain2 = jnp.tile(k_gain[0], ATTN_KV_HEADS)[None, :]
    qgain = (jnp.tile(q_gain[0], ATTN_HEADS) * (HEAD_DIM ** -0.5 * LOG2E))[None, :]
    norm_mix_g = norm_mix[0][None, :]

    kc, vct = _ctx_kv(ctx, mod3, b, norm_mix_g, w_in_bf, kgain2, bd)

    cos, sup, sdn = _rope_tables(s)
    bs_full = jnp.repeat(b_spatial[0].T, HEAD_DIM, axis=1)
    k, vt, qt, mlpn, wo_bf, wgu_bf, wd_bf = _inproj(
        x, mod3, norm_mix_g, w_in_bf, cos, sup, sdn, qgain, kgain2,
        gate_gain[0].reshape(1, mw), w_spatial[0].astype(BF16), bs_full,
        mlp_out_gain[0][None, :], bd, (w_out[0], w_gate_up[0], w_down[0]), tile=1024)

    sink_rows = jnp.repeat(attn_sink[0].reshape(ATTN_KV_HEADS, ATTN_GROUP) * LOG2E,
                           BLOCK, axis=1)[:, None, :]
    again_b = jnp.broadcast_to(attn_out_gain[0][:, None], (aw, BLOCK))
    logit_bound = (BOUND_MARGIN * HEAD_DIM ** 0.5 * LOG2E
                   * jnp.max(jnp.abs(q_gain[0])) * jnp.max(jnp.abs(k_gain[0])))
    sink_max = jnp.max(jnp.abs(attn_sink[0])) * LOG2E
    use_bound = (logit_bound <= MAX_SAFE_LOGIT) & (sink_max <= MAX_SAFE_LOGIT)
    operands = (x, mod3, k, vt, kc, vct, qt, mlpn, sink_rows, _window_bias(), again_b,
                wo_bf, norm_ffn[0][None, :], wgu_bf, wd_bf)
    return lax.cond(use_bound,
                    lambda ops: _mix_ffn(*ops, tile=512, bounded=True),
                    lambda ops: _mix_ffn(*ops, tile=512, bounded=False),
                    operands)
```

```python
import functools
import math

import jax
import jax.numpy as jnp
import numpy as np
from jax import lax
from jax.experimental import pallas as pl
from jax.experimental.pallas import tpu as pltpu

F32 = jnp.float32
BF16 = jnp.bfloat16

HEAD_DIM = 64
ATTN_HEADS = 8
ATTN_KV_HEADS = 2
ATTN_GROUP = ATTN_HEADS // ATTN_KV_HEADS
MLP_HEADS = 8
N_MOD = 6
BLOCK = 128
GRID_W = 64
ROPE_THETA = 10000.0
EPS = 1e-6
MASKED = -1e30
BOUND_MARGIN = 1.05
MAX_SAFE_LOGIT = 40.0
LOG2E = math.log2(math.e)

LANES = 128
BF16_SUBLANES = 16
MXU_DIM = 256
VMEM_LIMIT = 56 * 1024 * 1024
HEADS_PER_STEP = 4


def _dot(a, b):
    return jnp.dot(a, b, preferred_element_type=F32)


def _silu(x):
    return x * (1.0 / (1.0 + jnp.exp(-x)))


def _gelu2_tanh(x):
    c = math.sqrt(2.0 / math.pi)
    return x * (1.0 + jnp.tanh(x * (c + (c * 0.044715) * (x * x))))


def _mod_norm(x, gain, shift, scale):
    y = x * lax.rsqrt(jnp.mean(x * x, axis=-1, keepdims=True) + EPS)
    return y * (gain * (1.0 + scale)) + shift


def _head_ms(x, bd):
    return _dot((x * x).astype(BF16), bd)


def _rope(x, cos, sin_up, sin_dn):
    up = pltpu.roll(x, LANES - 16, 1)
    dn = pltpu.roll(x, 16, 1)
    return x * cos + up * sin_up + dn * sin_dn


def _split_bf16(x):
    hi = x.astype(BF16)
    return hi, (x - hi.astype(F32)).astype(BF16)


def _adaln_kernel(cond_ref, wa_ref, wb_ref, b_ref, o_ref):
    rows = cond_ref.shape[0]
    half = wa_ref.shape[1]

    @pl.when(pl.program_id(0) == 0)
    def _():
        o_ref[...] = jnp.broadcast_to(b_ref[...], o_ref.shape)

    s_hi, s_lo = _split_bf16(_silu(cond_ref[...]))
    s_both = jnp.concatenate([s_hi, s_lo], axis=0)
    for part, w_ref in enumerate((wa_ref, wb_ref)):
        w_hi, w_lo = _split_bf16(w_ref[...])
        both = _dot(s_both, w_hi)
        o_ref[:, part * half:(part + 1) * half] += both[:rows] + both[rows:] + _dot(s_hi, w_lo)


def _adaln(cond, w_mod, b_mod):
    rows, d = cond.shape
    n = w_mod.shape[1]
    tk = d // 4
    assert rows % BF16_SUBLANES == 0 and tk % LANES == 0 and (n // 2) % LANES == 0
    return pl.pallas_call(
        _adaln_kernel,
        grid=(d // tk,),
        in_specs=[pl.BlockSpec((rows, tk), lambda j: (0, j)),
                  pl.BlockSpec((tk, n // 2), lambda j: (j, 0)),
                  pl.BlockSpec((tk, n // 2), lambda j: (j, 1)),
                  pl.BlockSpec((1, n), lambda j: (0, 0))],
        out_specs=pl.BlockSpec((rows, n), lambda j: (0, 0)),
        out_shape=jax.ShapeDtypeStruct((rows, n), F32),
        compiler_params=pltpu.CompilerParams(
            dimension_semantics=("arbitrary",), vmem_limit_bytes=VMEM_LIMIT),
        name="adaln",
    )(cond, w_mod, w_mod, b_mod)


def _ctx_kernel(x_ref, mod_ref, gain_ref, w_ref, kgain_ref, bd_ref, kc_ref, vct_ref):
    per_step, c, d = x_ref.shape
    x = x_ref[...].reshape(per_step * c, d)
    hn = _mod_norm(x, gain_ref[...], mod_ref[0:1, :], mod_ref[1:2, :])
    kv = _dot(hn.astype(BF16), w_ref[...])
    k = kv[:, :LANES]
    k = k * lax.rsqrt(_head_ms(k, bd_ref[:LANES, :LANES]) + EPS) * kgain_ref[...]
    for j in range(per_step):
        kc_ref[j] = k[j * c:(j + 1) * c, :].astype(BF16)
        vct_ref[j] = kv[j * c:(j + 1) * c, LANES:].T.astype(BF16)


def _ctx_kv(ctx, mod3, ctx_row, norm_gain, w_in_bf, kgain2, bd):
    b, c, d = ctx.shape
    kvw = 2 * ATTN_KV_HEADS * HEAD_DIM
    per_step = 2 if b % 2 == 0 else 1
    return pl.pallas_call(
        _ctx_kernel,
        grid=(b // per_step,),
        in_specs=[pl.BlockSpec((per_step, c, d), lambda i: (i, 0, 0)),
                  pl.BlockSpec((None, N_MOD, d), lambda i: (ctx_row, 0, 0)),
                  pl.BlockSpec((1, d), lambda i: (0, 0)),
                  pl.BlockSpec((d, kvw), lambda i: (0, 0)),
                  pl.BlockSpec((1, LANES), lambda i: (0, 0)),
                  pl.BlockSpec((MXU_DIM, MXU_DIM), lambda i: (0, 0))],
        out_specs=[pl.BlockSpec((per_step, c, LANES), lambda i: (i, 0, 0)),
                   pl.BlockSpec((per_step, LANES, c), lambda i: (i, 0, 0))],
        out_shape=[jax.ShapeDtypeStruct((b, c, LANES), BF16),
                   jax.ShapeDtypeStruct((b, LANES, c), BF16)],
        compiler_params=pltpu.CompilerParams(
            dimension_semantics=("arbitrary",), vmem_limit_bytes=VMEM_LIMIT),
        name="ctx_kv",
    )(ctx, mod3, norm_gain, w_in_bf, kgain2, bd)


def _inproj_kernel(x_ref, mod_ref, gain_ref, w_ref, cos_ref, sup_ref, sdn_ref,
                   qgain_ref, kgain_ref, ggain_ref, ws_ref, bs_ref, ogain_ref, bd_ref,
                   wo_f32_ref, wgu_f32_ref, wd_f32_ref,
                   k_ref, vt_ref, qt_ref, mlp_ref, wo_bf_ref, wgu_bf_ref, wd_bf_ref):
    wo_bf_ref[...] = wo_f32_ref[...].astype(BF16)
    wgu_bf_ref[...] = wgu_f32_ref[...].astype(BF16)
    wd_bf_ref[...] = wd_f32_ref[...].astype(BF16)

    t = x_ref.shape[0]
    nblk = t // BLOCK
    kvw = ATTN_KV_HEADS * HEAD_DIM
    aw = ATTN_HEADS * HEAD_DIM
    mw = MLP_HEADS * HEAD_DIM
    q_cols = slice(2 * kvw, 2 * kvw + aw)
    kv_cols = slice(0, 2 * kvw)
    u_cols = slice(2 * kvw + aw, 2 * kvw + aw + mw)
    g_cols = slice(2 * kvw + aw + mw, 2 * kvw + aw + 2 * mw)
    halves = range(aw // MXU_DIM)
    bd = bd_ref[...]

    def project(cols):
        return _dot(hn, w_ref[:, cols])

    def finish_qkv(q, kv, q_ms, k_ms):
        cos, sup, sdn = cos_ref[...], sup_ref[...], sdn_ref[...]
        k = kv[:, :kvw] * lax.rsqrt(k_ms + EPS) * kgain_ref[...]
        k_ref[...] = _rope(k, cos, sup, sdn).astype(BF16)
        vt_ref[...] = kv[:, kvw:].T.astype(BF16)
        for half in halves:
            qh = q[:, half * MXU_DIM:(half + 1) * MXU_DIM] * lax.rsqrt(q_ms[half] + EPS)
            qh = qh * qgain_ref[:, half * MXU_DIM:(half + 1) * MXU_DIM]
            for sl in range(MXU_DIM // LANES):
                qs = _rope(qh[:, sl * LANES:(sl + 1) * LANES], cos, sup, sdn)
                qst = qs.T
                for hh in range(LANES // HEAD_DIM):
                    head = (half * MXU_DIM + sl * LANES) // HEAD_DIM + hh
                    kvh, grp = head // ATTN_GROUP, head % ATTN_GROUP
                    for jb in range(nblk):
                        col = (jb * ATTN_GROUP + grp) * BLOCK
                        qt_ref[kvh, :, col:col + BLOCK] = qst[
                            hh * HEAD_DIM:(hh + 1) * HEAD_DIM, jb * BLOCK:(jb + 1) * BLOCK].astype(BF16)

    def gate_mix(g_raw):
        g = _gelu2_tanh(g_raw)
        g_ms = [_head_ms(g[:, h * MXU_DIM:(h + 1) * MXU_DIM], bd) for h in halves]
        gn = jnp.concatenate(
            [g[:, h * MXU_DIM:(h + 1) * MXU_DIM] * lax.rsqrt(g_ms[h] + 4.0 * EPS) for h in halves],
            axis=1) * ggain_ref[...]
        gnb = gn.astype(BF16)
        low_head = lax.broadcasted_iota(jnp.int32, (BLOCK, t), 1) % LANES < HEAD_DIM
        mixed_slabs = []
        for p in range(MLP_HEADS // 2):
            rhs = jnp.concatenate(
                [gnb[c * BLOCK:(c + 1) * BLOCK, p * LANES:(p + 1) * LANES] for c in range(nblk)], axis=1)
            a = _dot(ws_ref[2 * p], rhs)
            b = _dot(ws_ref[2 * p + 1], rhs)
            mixed_slabs.append(jnp.where(low_head, a, b))
        return mixed_slabs

    def finish_mlp(u_raw, mixed_slabs):
        u = _gelu2_tanh(u_raw)
        rows = []
        for c in range(nblk):
            mixed_c = jnp.concatenate(
                [m[:, c * LANES:(c + 1) * LANES] for m in mixed_slabs], axis=1) + bs_ref[...]
            rows.append(u[c * BLOCK:(c + 1) * BLOCK, :] * mixed_c)
        o = jnp.concatenate(rows, axis=0)
        o = o * lax.rsqrt(jnp.mean(o * o, axis=-1, keepdims=True) + 4.0 * EPS) * ogain_ref[...]
        mlp_ref[...] = o.astype(BF16)

    hn = _mod_norm(x_ref[...], gain_ref[...], mod_ref[0:1, :], mod_ref[1:2, :]).astype(BF16)
    q, kv = project(q_cols), project(kv_cols)
    q_ms = [_head_ms(q[:, h * MXU_DIM:(h + 1) * MXU_DIM], bd) for h in halves]
    k_ms = _head_ms(kv[:, :kvw], bd[:kvw, :kvw])
    g_raw = project(g_cols)
    finish_qkv(q, kv, q_ms, k_ms)
    u_raw = project(u_cols)
    finish_mlp(u_raw, gate_mix(g_raw))


def _inproj(x, mod3, norm_gain, w_in_bf, cos, sup, sdn, qgain, kgain2, ggain, ws_bf, bs_full,
            ogain, bd, later_weights, tile):
    b, s, d = x.shape
    inw = w_in_bf.shape[1]
    aw = ATTN_HEADS * HEAD_DIM
    mw = MLP_HEADS * HEAD_DIM
    n_steps = (s // tile) * b
    const = lambda shape: pl.BlockSpec(shape, lambda i, bb: (0,) * len(shape))

    def slab_spec(w):
        rows = next(r for r in range(BF16_SUBLANES, w.shape[0] + 1, BF16_SUBLANES)
                    if w.shape[0] % r == 0 and w.shape[0] // r <= n_steps)
        n_slabs = w.shape[0] // rows
        return pl.BlockSpec((rows, w.shape[1]), lambda i, bb: (jnp.minimum(i * b + bb, n_slabs - 1), 0))

    slab_specs = [slab_spec(w) for w in later_weights]
    table_spec = pl.BlockSpec((tile, LANES), lambda i, bb: (i, 0))
    return pl.pallas_call(
        _inproj_kernel,
        grid=(s // tile, b),
        in_specs=[pl.BlockSpec((None, tile, d), lambda i, bb: (bb, i, 0)),
                  pl.BlockSpec((None, N_MOD, d), lambda i, bb: (bb, 0, 0)),
                  const((1, d)),
                  const((d, inw)),
                  table_spec, table_spec, table_spec,
                  const((1, aw)), const((1, LANES)), const((1, mw)),
                  const((MLP_HEADS, BLOCK, BLOCK)), const((BLOCK, mw)), const((1, mw)),
                  const((MXU_DIM, MXU_DIM))] + slab_specs,
        out_specs=[pl.BlockSpec((None, tile, LANES), lambda i, bb: (bb, i, 0)),
                   pl.BlockSpec((None, LANES, tile), lambda i, bb: (bb, 0, i)),
                   pl.BlockSpec((None, ATTN_KV_HEADS, HEAD_DIM, ATTN_GROUP * tile),
                                lambda i, bb: (bb, 0, 0, i)),
                   pl.BlockSpec((None, tile, mw), lambda i, bb: (bb, i, 0))] + slab_specs,
        out_shape=[jax.ShapeDtypeStruct((b, s, LANES), BF16),
                   jax.ShapeDtypeStruct((b, LANES, s), BF16),
                   jax.ShapeDtypeStruct((b, ATTN_KV_HEADS, HEAD_DIM, ATTN_GROUP * s), BF16),
                   jax.ShapeDtypeStruct((b, s, mw), BF16)]
        + [jax.ShapeDtypeStruct(w.shape, BF16) for w in later_weights],
        compiler_params=pltpu.CompilerParams(
            dimension_semantics=("arbitrary", "arbitrary"), vmem_limit_bytes=VMEM_LIMIT),
        name="inproj",
    )(x, mod3, norm_gain, w_in_bf, cos, sup, sdn, qgain, kgain2, ggain, ws_bf, bs_full, ogain, bd,
      *later_weights)


def _mix_ffn_kernel(tiles_per_seq, bounded,
                    x_ref, mod_ref, kp_ref, km_ref, kn_ref, vp_ref, vm_ref, vn_ref,
                    kc_ref, vct_ref, qt_ref, mlp_ref, sink_ref, bias_ref, again_ref, wo_ref,
                    fmod_ref, fgain_ref, wgu_ref, wd_ref,
                    o_ref, h_ref, hn_ref, hid_ref):
    tq = x_ref.shape[0]
    nblk = tq // BLOCK
    step_id = pl.program_id(0)
    n_tiles = pl.num_programs(0) - 1
    i = lax.rem(jnp.minimum(step_id, n_tiles - 1), tiles_per_seq)
    last = tiles_per_seq - 1
    gq = ATTN_GROUP * BLOCK
    slot_w = lax.rem(step_id, 2)
    slot_r = 1 - slot_w

    @pl.when(step_id == 0)
    def _():
        h_ref[1] = jnp.zeros(h_ref.shape[1:], F32)
        hn_ref[...] = jnp.zeros(hn_ref.shape, BF16)

    ff = wd_ref.shape[0]
    n_chunks = ff // MXU_DIM
    def ffn_matmuls(c):
        hn = hn_ref[...]
        return (_dot(hn, wgu_ref[:, c * MXU_DIM:(c + 1) * MXU_DIM]),
                _dot(hn, wgu_ref[:, ff + c * MXU_DIM:ff + (c + 1) * MXU_DIM]))

    def ffn_activation(c, gate_up):
        a, b = gate_up
        hid_ref[:, c * MXU_DIM:(c + 1) * MXU_DIM] = (_silu(a) * b).astype(BF16)

    k_ext = jnp.concatenate([kp_ref[...], km_ref[...], kn_ref[...]], axis=0)
    vt_ext = jnp.concatenate([vp_ref[...], vm_ref[...], vn_ref[...]], axis=1)
    kc = kc_ref[...]
    vct = vct_ref[...]
    cw = HEADS_PER_STEP * BLOCK
    zeros_q = jnp.zeros((HEAD_DIM, cw), BF16)
    ones_rows = jnp.ones((BF16_SUBLANES, 3 * BLOCK + kc.shape[0]), BF16)
    bias_prev = bias_ref[0:BLOCK, :cw]
    bias_next = bias_ref[BLOCK:2 * BLOCK, :cw]

    def score_matmul(jb, kvh, part):
        qt = qt_ref[kvh, :, jb * gq + part * cw:jb * gq + (part + 1) * cw]
        rhs = jnp.concatenate([qt, zeros_q] if kvh == 0 else [zeros_q, qt], axis=0)
        return _dot(kc, rhs), _dot(k_ext[jb * BLOCK:(jb + 3) * BLOCK, :], rhs)

    def mask_and_max(jb, kvh, part, s):
        pen_prev = jnp.where(i == 0, MASKED, 0.0) if jb == 0 else 0.0
        pen_next = jnp.where(i == last, MASKED, 0.0) if jb == nblk - 1 else 0.0
        s_ctx, s = s
        parts = [s[0:BLOCK] + (bias_prev + pen_prev), s[BLOCK:2 * BLOCK],
                 s[2 * BLOCK:3 * BLOCK] + (bias_next + pen_next), s_ctx]
        if bounded:
            return jnp.concatenate([jnp.exp2(p_).astype(BF16) for p_ in parts], axis=0), None
        m = jnp.maximum(
            jnp.maximum(jnp.max(parts[0], axis=0, keepdims=True), jnp.max(parts[1], axis=0, keepdims=True)),
            jnp.maximum(jnp.max(parts[2], axis=0, keepdims=True), jnp.max(parts[3], axis=0, keepdims=True)))
        return parts, jnp.maximum(m, sink_ref[kvh, :, part * cw:(part + 1) * cw])

    def exp_weights(parts, m):
        return jnp.concatenate([jnp.exp2(p_ - m).astype(BF16) for p_ in parts], axis=0)

    def value_matmul(jb, kvh, part, p):
        v_all = jnp.concatenate(
            [vt_ext[kvh * HEAD_DIM:(kvh + 1) * HEAD_DIM, jb * BLOCK:(jb + 3) * BLOCK],
             vct[kvh * HEAD_DIM:(kvh + 1) * HEAD_DIM, :]], axis=1)
        return _dot(jnp.concatenate([v_all, ones_rows], axis=0), p)

    def normalize(jb, kvh, part, pv, m):
        sink = sink_ref[kvh, :, part * cw:(part + 1) * cw]
        denom = pv[HEAD_DIM:HEAD_DIM + 1, :] + jnp.exp2(sink if bounded else sink - m)
        o_t = pv[:HEAD_DIM, :] * (1.0 / denom)
        return [o_t[:, g * BLOCK:(g + 1) * BLOCK] for g in range(HEADS_PER_STEP)]

    def block_norm(out_t):
        o_all = jnp.concatenate(out_t, axis=0)
        ms = jnp.mean(o_all * o_all, axis=0, keepdims=True)
        y = o_all * lax.rsqrt(ms + EPS) * again_ref[...]
        return y.T.astype(BF16)

    aw = ATTN_HEADS * HEAD_DIM
    steps = [(jb, kvh, part) for jb in range(nblk) for kvh in range(ATTN_KV_HEADS)
             for part in range(ATTN_GROUP // HEADS_PER_STEP)]
    assert n_chunks >= len(steps)
    attn_rows, out_t, unnormalized, gate_up = [], [], None, {}

    def collect(step, pv, m):
        out_t.extend(normalize(*step, pv, m))
        if len(out_t) == ATTN_HEADS:
            attn_rows.append(block_norm(out_t))
            out_t.clear()

    if bounded:
        proj_mlp = _dot(mlp_ref[...], wo_ref[aw:, :])
        for n, step in enumerate(steps):
            scores = score_matmul(*step)
            gate_up[n] = ffn_matmuls(n)
            weights, _ = mask_and_max(*step, scores)
            if unnormalized is not None:
                collect(*unnormalized)
                ffn_activation(n - 1, gate_up.pop(n - 1))
            unnormalized = (step, value_matmul(*step, weights), None)
    else:
        raw = {n: score_matmul(*steps[n]) for n in range(2)}
        proj_mlp = _dot(mlp_ref[...], wo_ref[aw:, :])
        ready = {0: mask_and_max(*steps[0], raw.pop(0))}
        for n, step in enumerate(steps):
            if n + 1 < len(steps):
                ready[n + 1] = mask_and_max(*steps[n + 1], raw.pop(n + 1))
            if n + 2 < len(steps):
                raw[n + 2] = score_matmul(*steps[n + 2])
            gate_up[n] = ffn_matmuls(n)
            if unnormalized is not None:
                collect(*unnormalized)
                ffn_activation(n - 1, gate_up.pop(n - 1))
            parts, m = ready.pop(n)
            unnormalized = (step, value_matmul(*step, exp_weights(parts, m)), m)

    c_next = len(steps)
    gate_up[c_next] = ffn_matmuls(c_next)
    collect(*unnormalized)
    ffn_activation(c_next - 1, gate_up.pop(c_next - 1))
    for c in range(c_next + 1, n_chunks):
        gate_up[c] = ffn_matmuls(c)
        ffn_activation(c - 1, gate_up.pop(c - 1))
    proj_attn = _dot(jnp.concatenate(attn_rows, axis=0), wo_ref[:aw, :])
    ffn_activation(n_chunks - 1, gate_up.pop(n_chunks - 1))
    h_new = x_ref[...] + mod_ref[2:3, :] * (proj_attn + proj_mlp)
    h_ref[slot_w] = h_new
    hn_ref[...] = _mod_norm(h_new, fgain_ref[...], mod_ref[3:4, :], mod_ref[4:5, :]).astype(BF16)
    o_ref[...] = h_ref[slot_r] + fmod_ref[5:6, :] * _dot(hid_ref[...], wd_ref[...])


def _mix_ffn(x, mod3, k, vt, kc, vct, qt, mlpn, sink_rows, bias, again_b, wo_bf,
             ffn_gain, wgu_bf, wd_bf, tile, bounded):
    b, s, d = x.shape
    c = kc.shape[1]
    aw = ATTN_HEADS * HEAD_DIM
    mw = mlpn.shape[2]
    ff = wd_bf.shape[0]
    assert ff % MXU_DIM == 0
    r = tile // BLOCK
    nb = s // BLOCK
    nt = s // tile
    n_tiles = b * nt
    gq = ATTN_GROUP * BLOCK

    def mix_tile(g):
        t = jnp.minimum(g, n_tiles - 1)
        return t // nt, t % nt

    def ffn_tile(g):
        t = jnp.maximum(g - 1, 0)
        return t // nt, t % nt

    def at_mix(fn):
        return lambda g: fn(*mix_tile(g))

    const = lambda shape: pl.BlockSpec(shape, lambda g: (0,) * len(shape))
    return pl.pallas_call(
        functools.partial(_mix_ffn_kernel, nt, bounded),
        grid=(n_tiles + 1,),
        in_specs=[pl.BlockSpec((None, tile, d), at_mix(lambda bb, i: (bb, i, 0))),
                  pl.BlockSpec((None, N_MOD, d), at_mix(lambda bb, i: (bb, 0, 0))),
                  pl.BlockSpec((None, BLOCK, LANES), at_mix(lambda bb, i: (bb, jnp.maximum(i * r - 1, 0), 0))),
                  pl.BlockSpec((None, tile, LANES), at_mix(lambda bb, i: (bb, i, 0))),
                  pl.BlockSpec((None, BLOCK, LANES),
                               at_mix(lambda bb, i: (bb, jnp.minimum((i + 1) * r, nb - 1), 0))),
                  pl.BlockSpec((None, LANES, BLOCK), at_mix(lambda bb, i: (bb, 0, jnp.maximum(i * r - 1, 0)))),
                  pl.BlockSpec((None, LANES, tile), at_mix(lambda bb, i: (bb, 0, i))),
                  pl.BlockSpec((None, LANES, BLOCK),
                               at_mix(lambda bb, i: (bb, 0, jnp.minimum((i + 1) * r, nb - 1)))),
                  pl.BlockSpec((None, c, LANES), at_mix(lambda bb, i: (bb, 0, 0))),
                  pl.BlockSpec((None, LANES, c), at_mix(lambda bb, i: (bb, 0, 0))),
                  pl.BlockSpec((None, ATTN_KV_HEADS, HEAD_DIM, ATTN_GROUP * tile),
                               at_mix(lambda bb, i: (bb, 0, 0, i))),
                  pl.BlockSpec((None, tile, mw), at_mix(lambda bb, i: (bb, i, 0))),
                  const((ATTN_KV_HEADS, 1, gq)),
                  const((2 * BLOCK, gq)),
                  const((aw, BLOCK)),
                  const((aw + mw, d)),
                  pl.BlockSpec((None, N_MOD, d), lambda g: (ffn_tile(g)[0], 0, 0)),
                  const((1, d)), const((d, 2 * ff)), const((ff, d))],
        out_specs=pl.BlockSpec((None, tile, d), lambda g: (*ffn_tile(g), 0)),
        out_shape=jax.ShapeDtypeStruct((b, s, d), F32),
        scratch_shapes=[pltpu.VMEM((2, tile, d), F32), pltpu.VMEM((tile, d), BF16),
                        pltpu.VMEM((tile, ff), BF16)],
        compiler_params=pltpu.CompilerParams(
            dimension_semantics=("arbitrary",), vmem_limit_bytes=VMEM_LIMIT),
        name="mix_ffn",
    )(x, mod3, k, k, k, vt, vt, vt, kc, vct, qt, mlpn, sink_rows, bias, again_b, wo_bf,
      mod3, ffn_gain, wgu_bf, wd_bf)


def _rope_tables(s):
    axis_dim = HEAD_DIM // 2
    pos = np.arange(s)
    inv_freq = (ROPE_THETA ** (-np.arange(0, axis_dim, 2, dtype=np.float32) / axis_dim)).astype(np.float32)
    ang_r = (pos // GRID_W).astype(np.float32)[:, None] * inv_freq[None, :]
    ang_c = (pos % GRID_W).astype(np.float32)[:, None] * inv_freq[None, :]
    cr, sr, cc, sc = np.cos(ang_r), np.sin(ang_r), np.cos(ang_c), np.sin(ang_c)
    z = np.zeros_like(sr)
    reps = LANES // HEAD_DIM
    cos = np.tile(np.concatenate([cr, cr, cc, cc], axis=1), (1, reps))
    sin_up = np.tile(np.concatenate([-sr, z, -sc, z], axis=1), (1, reps))
    sin_dn = np.tile(np.concatenate([z, sr, z, sc], axis=1), (1, reps))
    return jnp.asarray(cos, F32), jnp.asarray(sin_up, F32), jnp.asarray(sin_dn, F32)


def _window_bias():
    c = np.arange(BLOCK)[:, None]
    r = np.arange(BLOCK)[None, :]
    prev = np.where(c >= r, 0.0, MASKED).astype(np.float32)
    nxt = np.where(c <= r, 0.0, MASKED).astype(np.float32)
    return jnp.asarray(np.tile(np.concatenate([prev, nxt], axis=0), (1, ATTN_GROUP)), F32)


def _head_mean_matrix():
    bd = np.kron(np.eye(MXU_DIM // HEAD_DIM, dtype=np.float32),
                 np.full((HEAD_DIM, HEAD_DIM), 1.0 / HEAD_DIM, np.float32))
    return jnp.asarray(bd, BF16)


def kernel(x, c, ctx, c_ctx, w_mod, b_mod, norm_mix, norm_ffn, w_in, q_gain, k_gain, attn_sink,
           gate_gain, w_spatial, b_spatial, attn_out_gain, mlp_out_gain, w_out, w_gate_up, w_down):
    b, s, d = x.shape
    assert w_mod.shape[0] == 1, "single-layer problem"
    assert s % 1024 == 0 and d % LANES == 0
    aw = ATTN_HEADS * HEAD_DIM
    mw = MLP_HEADS * HEAD_DIM

    rows = -(-(b + 1) // BF16_SUBLANES) * BF16_SUBLANES
    cond = jnp.concatenate([c, c_ctx[None, :], jnp.zeros((rows - b - 1, d), F32)], axis=0)
    mod3 = _adaln(cond, w_mod[0], b_mod[0][None, :]).reshape(rows, N_MOD, d)

    w_in_bf = w_in[0].astype(BF16)
    bd = _head_mean_matrix()
    kgain2 = jnp.tile(k_gain[0], ATTN_KV_HEADS)[None, :]
    qgain = (jnp.tile(q_gain[0], ATTN_HEADS) * (HEAD_DIM ** -0.5 * LOG2E))[None, :]
    norm_mix_g = norm_mix[0][None, :]

    kc, vct = _ctx_kv(ctx, mod3, b, norm_mix_g, w_in_bf, kgain2, bd)

    cos, sup, sdn = _rope_tables(s)
    bs_full = jnp.repeat(b_spatial[0].T, HEAD_DIM, axis=1)
    k, vt, qt, mlpn, wo_bf, wgu_bf, wd_bf = _inproj(
        x, mod3, norm_mix_g, w_in_bf, cos, sup, sdn, qgain, kgain2,
        gate_gain[0].reshape(1, mw), w_spatial[0].astype(BF16), bs_full,
        mlp_out_gain[0][None, :], bd, (w_out[0], w_gate_up[0], w_down[0]), tile=1024)

    sink_rows = jnp.repeat(attn_sink[0].reshape(ATTN_KV_HEADS, ATTN_GROUP) * LOG2E,
                           BLOCK, axis=1)[:, None, :]
    again_b = jnp.broadcast_to(attn_out_gain[0][:, None], (aw, BLOCK))
    logit_bound = (BOUND_MARGIN * HEAD_DIM ** 0.5 * LOG2E
                   * jnp.max(jnp.abs(q_gain[0])) * jnp.max(jnp.abs(k_gain[0])))
    sink_max = jnp.max(jnp.abs(attn_sink[0])) * LOG2E
    use_bound = (logit_bound <= MAX_SAFE_LOGIT) & (sink_max <= MAX_SAFE_LOGIT)
    operands = (x, mod3, k, vt, kc, vct, qt, mlpn, sink_rows, _window_bias(), again_b,
                wo_bf, norm_ffn[0][None, :], wgu_bf, wd_bf)
    return lax.cond(use_bound,
                    lambda ops: _mix_ffn(*ops, tile=512, bounded=True),
                    lambda ops: _mix_ffn(*ops, tile=512, bounded=False),
                    operands)
```

```python
import functools
import math

import jax
import jax.numpy as jnp
import numpy as np
from jax import lax
from jax.experimental import pallas as pl
from jax.experimental.pallas import tpu as pltpu

F32 = jnp.float32
BF16 = jnp.bfloat16

HEAD_DIM = 64
ATTN_HEADS = 8
ATTN_KV_HEADS = 2
ATTN_GROUP = ATTN_HEADS // ATTN_KV_HEADS
MLP_HEADS = 8
N_MOD = 6
BLOCK = 128
GRID_W = 64
ROPE_THETA = 10000.0
EPS = 1e-6
MASKED = -1e30
BOUND_MARGIN = 1.05
MAX_SAFE_LOGIT = 40.0
LOG2E = math.log2(math.e)

LANES = 128
BF16_SUBLANES = 16
MXU_DIM = 256
VMEM_LIMIT = 56 * 1024 * 1024
HEADS_PER_STEP = 4


def _dot(a, b):
    return jnp.dot(a, b, preferred_element_type=F32)


def _silu(x):
    return x * (1.0 / (1.0 + jnp.exp(-x)))


def _gelu2_tanh(x):
    c = math.sqrt(2.0 / math.pi)
    return x * (1.0 + jnp.tanh(x * (c + (c * 0.044715) * (x * x))))


def _mod_norm(x, gain, shift, scale):
    y = x * lax.rsqrt(jnp.mean(x * x, axis=-1, keepdims=True) + EPS)
    return y * (gain * (1.0 + scale)) + shift


def _head_ms(x, bd):
    return _dot((x * x).astype(BF16), bd)


def _rope(x, cos, sin_up, sin_dn):
    up = pltpu.roll(x, LANES - 16, 1)
    dn = pltpu.roll(x, 16, 1)
    return x * cos + up * sin_up + dn * sin_dn


def _split_bf16(x):
    hi = x.astype(BF16)
    return hi, (x - hi.astype(F32)).astype(BF16)


def _adaln_kernel(cond_ref, wa_ref, wb_ref, b_ref, win_ref, o_ref, win_bf_ref):
    win_bf_ref[...] = win_ref[...].astype(BF16)
    rows = cond_ref.shape[0]
    half = wa_ref.shape[1]

    @pl.when(pl.program_id(0) == 0)
    def _():
        o_ref[...] = jnp.broadcast_to(b_ref[...], o_ref.shape)

    s_hi, s_lo = _split_bf16(_silu(cond_ref[...]))
    s_both = jnp.concatenate([s_hi, s_lo], axis=0)
    for part, w_ref in enumerate((wa_ref, wb_ref)):
        w_hi, w_lo = _split_bf16(w_ref[...])
        both = _dot(s_both, w_hi)
        o_ref[:, part * half:(part + 1) * half] += both[:rows] + both[rows:] + _dot(s_hi, w_lo)


def _adaln(cond, w_mod, b_mod, w_in):
    rows, d = cond.shape
    n = w_mod.shape[1]
    steps = 4
    tk = d // steps
    win_rows = w_in.shape[0] // steps
    assert rows % BF16_SUBLANES == 0 and tk % LANES == 0 and (n // 2) % LANES == 0
    assert win_rows % BF16_SUBLANES == 0
    win_spec = pl.BlockSpec((win_rows, w_in.shape[1]), lambda j: (j, 0))
    return pl.pallas_call(
        _adaln_kernel,
        grid=(steps,),
        in_specs=[pl.BlockSpec((rows, tk), lambda j: (0, j)),
                  pl.BlockSpec((tk, n // 2), lambda j: (j, 0)),
                  pl.BlockSpec((tk, n // 2), lambda j: (j, 1)),
                  pl.BlockSpec((1, n), lambda j: (0, 0)),
                  win_spec],
        out_specs=[pl.BlockSpec((rows, n), lambda j: (0, 0)), win_spec],
        out_shape=[jax.ShapeDtypeStruct((rows, n), F32), jax.ShapeDtypeStruct(w_in.shape, BF16)],
        compiler_params=pltpu.CompilerParams(
            dimension_semantics=("arbitrary",), vmem_limit_bytes=VMEM_LIMIT),
        name="adaln",
    )(cond, w_mod, w_mod, b_mod, w_in)


def _ctx_kernel(x_ref, mod_ref, gain_ref, w_ref, kgain_ref, bd_ref, kc_ref, vct_ref):
    per_step, c, d = x_ref.shape
    x = x_ref[...].reshape(per_step * c, d)
    hn = _mod_norm(x, gain_ref[...], mod_ref[0:1, :], mod_ref[1:2, :])
    kv = _dot(hn.astype(BF16), w_ref[...])
    k = kv[:, :LANES]
    k = k * lax.rsqrt(_head_ms(k, bd_ref[:LANES, :LANES]) + EPS) * kgain_ref[...]
    for j in range(per_step):
        kc_ref[j] = k[j * c:(j + 1) * c, :].astype(BF16)
        vct_ref[j] = kv[j * c:(j + 1) * c, LANES:].T.astype(BF16)


def _ctx_kv(ctx, mod3, ctx_row, norm_gain, w_in_bf, kgain2, bd):
    b, c, d = ctx.shape
    kvw = 2 * ATTN_KV_HEADS * HEAD_DIM
    per_step = 2 if b % 2 == 0 else 1
    return pl.pallas_call(
        _ctx_kernel,
        grid=(b // per_step,),
        in_specs=[pl.BlockSpec((per_step, c, d), lambda i: (i, 0, 0)),
                  pl.BlockSpec((None, N_MOD, d), lambda i: (ctx_row, 0, 0)),
                  pl.BlockSpec((1, d), lambda i: (0, 0)),
                  pl.BlockSpec((d, kvw), lambda i: (0, 0)),
                  pl.BlockSpec((1, LANES), lambda i: (0, 0)),
                  pl.BlockSpec((MXU_DIM, MXU_DIM), lambda i: (0, 0))],
        out_specs=[pl.BlockSpec((per_step, c, LANES), lambda i: (i, 0, 0)),
                   pl.BlockSpec((per_step, LANES, c), lambda i: (i, 0, 0))],
        out_shape=[jax.ShapeDtypeStruct((b, c, LANES), BF16),
                   jax.ShapeDtypeStruct((b, LANES, c), BF16)],
        compiler_params=pltpu.CompilerParams(
            dimension_semantics=("arbitrary",), vmem_limit_bytes=VMEM_LIMIT),
        name="ctx_kv",
    )(ctx, mod3, norm_gain, w_in_bf, kgain2, bd)


def _inproj_kernel(x_ref, mod_ref, gain_ref, w_ref, cos_ref, sup_ref, sdn_ref,
                   qgain_ref, kgain_ref, ggain_ref, ws_ref, bs_ref, ogain_ref, bd_ref,
                   wo_f32_ref, wgu_f32_ref, wd_f32_ref,
                   k_ref, vt_ref, qt_ref, mlp_ref, wo_bf_ref, wgu_bf_ref, wd_bf_ref):
    wo_bf_ref[...] = wo_f32_ref[...].astype(BF16)
    wgu_bf_ref[...] = wgu_f32_ref[...].astype(BF16)
    wd_bf_ref[...] = wd_f32_ref[...].astype(BF16)

    t = x_ref.shape[0]
    nblk = t // BLOCK
    kvw = ATTN_KV_HEADS * HEAD_DIM
    aw = ATTN_HEADS * HEAD_DIM
    mw = MLP_HEADS * HEAD_DIM
    q_cols = slice(2 * kvw, 2 * kvw + aw)
    kv_cols = slice(0, 2 * kvw)
    u_cols = slice(2 * kvw + aw, 2 * kvw + aw + mw)
    g_cols = slice(2 * kvw + aw + mw, 2 * kvw + aw + 2 * mw)
    halves = range(aw // MXU_DIM)
    bd = bd_ref[...]

    def project(cols):
        return _dot(hn, w_ref[:, cols])

    def finish_qkv(q, kv, q_ms, k_ms):
        cos, sup, sdn = cos_ref[...], sup_ref[...], sdn_ref[...]
        k = kv[:, :kvw] * lax.rsqrt(k_ms + EPS) * kgain_ref[...]
        k_ref[...] = _rope(k, cos, sup, sdn).astype(BF16)
        vt_ref[...] = kv[:, kvw:].T.astype(BF16)
        for half in halves:
            qh = q[:, half * MXU_DIM:(half + 1) * MXU_DIM] * lax.rsqrt(q_ms[half] + EPS)
            qh = qh * qgain_ref[:, half * MXU_DIM:(half + 1) * MXU_DIM]
            for sl in range(MXU_DIM // LANES):
                qs = _rope(qh[:, sl * LANES:(sl + 1) * LANES], cos, sup, sdn)
                qst = qs.T
                for hh in range(LANES // HEAD_DIM):
                    head = (half * MXU_DIM + sl * LANES) // HEAD_DIM + hh
                    kvh, grp = head // ATTN_GROUP, head % ATTN_GROUP
                    for jb in range(nblk):
                        col = (jb * ATTN_GROUP + grp) * BLOCK
                        qt_ref[kvh, :, col:col + BLOCK] = qst[
                            hh * HEAD_DIM:(hh + 1) * HEAD_DIM, jb * BLOCK:(jb + 1) * BLOCK].astype(BF16)

    def gate_mix(g_raw):
        g = _gelu2_tanh(g_raw)
        g_ms = [_head_ms(g[:, h * MXU_DIM:(h + 1) * MXU_DIM], bd) for h in halves]
        gn = jnp.concatenate(
            [g[:, h * MXU_DIM:(h + 1) * MXU_DIM] * lax.rsqrt(g_ms[h] + 4.0 * EPS) for h in halves],
            axis=1) * ggain_ref[...]
        gnb = gn.astype(BF16)
        low_head = lax.broadcasted_iota(jnp.int32, (BLOCK, t), 1) % LANES < HEAD_DIM
        mixed_slabs = []
        for p in range(MLP_HEADS // 2):
            rhs = jnp.concatenate(
                [gnb[c * BLOCK:(c + 1) * BLOCK, p * LANES:(p + 1) * LANES] for c in range(nblk)], axis=1)
            a = _dot(ws_ref[2 * p].astype(BF16), rhs)
            b = _dot(ws_ref[2 * p + 1].astype(BF16), rhs)
            mixed_slabs.append(jnp.where(low_head, a, b))
        return mixed_slabs

    def finish_mlp(u_raw, mixed_slabs):
        u = _gelu2_tanh(u_raw)
        rows = []
        for c in range(nblk):
            mixed_c = jnp.concatenate(
                [m[:, c * LANES:(c + 1) * LANES] for m in mixed_slabs], axis=1) + bs_ref[...]
            rows.append(u[c * BLOCK:(c + 1) * BLOCK, :] * mixed_c)
        o = jnp.concatenate(rows, axis=0)
        o = o * lax.rsqrt(jnp.mean(o * o, axis=-1, keepdims=True) + 4.0 * EPS) * ogain_ref[...]
        mlp_ref[...] = o.astype(BF16)

    hn = _mod_norm(x_ref[...], gain_ref[...], mod_ref[0:1, :], mod_ref[1:2, :]).astype(BF16)
    q, kv = project(q_cols), project(kv_cols)
    q_ms = [_head_ms(q[:, h * MXU_DIM:(h + 1) * MXU_DIM], bd) for h in halves]
    k_ms = _head_ms(kv[:, :kvw], bd[:kvw, :kvw])
    g_raw = project(g_cols)
    finish_qkv(q, kv, q_ms, k_ms)
    u_raw = project(u_cols)
    finish_mlp(u_raw, gate_mix(g_raw))


def _inproj(x, mod3, norm_gain, w_in_bf, cos, sup, sdn, qgain, kgain2, ggain, w_spatial, bs_full,
            ogain, bd, later_weights, tile):
    b, s, d = x.shape
    inw = w_in_bf.shape[1]
    aw = ATTN_HEADS * HEAD_DIM
    mw = MLP_HEADS * HEAD_DIM
    n_steps = (s // tile) * b
    const = lambda shape: pl.BlockSpec(shape, lambda i, bb: (0,) * len(shape))

    def slab_spec(w):
        rows = next(r for r in range(BF16_SUBLANES, w.shape[0] + 1, BF16_SUBLANES)
                    if w.shape[0] % r == 0 and w.shape[0] // r <= n_steps)
        n_slabs = w.shape[0] // rows
        return pl.BlockSpec((rows, w.shape[1]), lambda i, bb: (jnp.minimum(i * b + bb, n_slabs - 1), 0))

    slab_specs = [slab_spec(w) for w in later_weights]
    table_spec = pl.BlockSpec((tile, LANES), lambda i, bb: (i, 0))
    return pl.pallas_call(
        _inproj_kernel,
        grid=(s // tile, b),
        in_specs=[pl.BlockSpec((None, tile, d), lambda i, bb: (bb, i, 0)),
                  pl.BlockSpec((None, N_MOD, d), lambda i, bb: (bb, 0, 0)),
                  const((1, d)),
                  const((d, inw)),
                  table_spec, table_spec, table_spec,
                  const((1, aw)), const((1, LANES)), const((1, mw)),
                  const((MLP_HEADS, BLOCK, BLOCK)), const((BLOCK, mw)), const((1, mw)),
                  const((MXU_DIM, MXU_DIM))] + slab_specs,
        out_specs=[pl.BlockSpec((None, tile, LANES), lambda i, bb: (bb, i, 0)),
                   pl.BlockSpec((None, LANES, tile), lambda i, bb: (bb, 0, i)),
                   pl.BlockSpec((None, ATTN_KV_HEADS, HEAD_DIM, ATTN_GROUP * tile),
                                lambda i, bb: (bb, 0, 0, i)),
                   pl.BlockSpec((None, tile, mw), lambda i, bb: (bb, i, 0))] + slab_specs,
        out_shape=[jax.ShapeDtypeStruct((b, s, LANES), BF16),
                   jax.ShapeDtypeStruct((b, LANES, s), BF16),
                   jax.ShapeDtypeStruct((b, ATTN_KV_HEADS, HEAD_DIM, ATTN_GROUP * s), BF16),
                   jax.ShapeDtypeStruct((b, s, mw), BF16)]
        + [jax.ShapeDtypeStruct(w.shape, BF16) for w in later_weights],
        compiler_params=pltpu.CompilerParams(
            dimension_semantics=("arbitrary", "arbitrary"), vmem_limit_bytes=VMEM_LIMIT),
        name="inproj",
    )(x, mod3, norm_gain, w_in_bf, cos, sup, sdn, qgain, kgain2, ggain, w_spatial, bs_full, ogain, bd,
      *later_weights)


def _mix_ffn_kernel(tiles_per_seq, bounded,
                    x_ref, mod_ref, kp_ref, km_ref, kn_ref, vp_ref, vm_ref, vn_ref,
                    kc_ref, vct_ref, qt_ref, mlp_ref, sink_ref, bias_ref, again_ref, wo_ref,
                    fmod_ref, fgain_ref, wgu_ref, wd_ref,
                    o_ref, h_ref, hn_ref, hid_ref):
    tq = x_ref.shape[0]
    nblk = tq // BLOCK
    step_id = pl.program_id(0)
    n_tiles = pl.num_programs(0) - 1
    i = lax.rem(jnp.minimum(step_id, n_tiles - 1), tiles_per_seq)
    last = tiles_per_seq - 1
    gq = ATTN_GROUP * BLOCK
    slot_w = lax.rem(step_id, 2)
    slot_r = 1 - slot_w

    @pl.when(step_id == 0)
    def _():
        h_ref[1] = jnp.zeros(h_ref.shape[1:], F32)
        hn_ref[...] = jnp.zeros(hn_ref.shape, BF16)

    ff = wd_ref.shape[0]
    n_chunks = ff // MXU_DIM
    def ffn_matmuls(c):
        hn = hn_ref[...]
        return (_dot(hn, wgu_ref[:, c * MXU_DIM:(c + 1) * MXU_DIM]),
                _dot(hn, wgu_ref[:, ff + c * MXU_DIM:ff + (c + 1) * MXU_DIM]))

    def ffn_activation(c, gate_up):
        a, b = gate_up
        hid_ref[:, c * MXU_DIM:(c + 1) * MXU_DIM] = (_silu(a) * b).astype(BF16)

    k_ext = jnp.concatenate([kp_ref[...], km_ref[...], kn_ref[...]], axis=0)
    vt_ext = jnp.concatenate([vp_ref[...], vm_ref[...], vn_ref[...]], axis=1)
    kc = kc_ref[...]
    vct = vct_ref[...]
    cw = HEADS_PER_STEP * BLOCK
    zeros_q = jnp.zeros((HEAD_DIM, cw), BF16)
    ones_rows = jnp.ones((BF16_SUBLANES, 3 * BLOCK + kc.shape[0]), BF16)
    bias_prev = bias_ref[0:BLOCK, :cw]
    bias_next = bias_ref[BLOCK:2 * BLOCK, :cw]

    def score_matmul(jb, kvh, part):
        qt = qt_ref[kvh, :, jb * gq + part * cw:jb * gq + (part + 1) * cw]
        rhs = jnp.concatenate([qt, zeros_q] if kvh == 0 else [zeros_q, qt], axis=0)
        return _dot(kc, rhs), _dot(k_ext[jb * BLOCK:(jb + 3) * BLOCK, :], rhs)

    def mask_and_max(jb, kvh, part, s):
        pen_prev = jnp.where(i == 0, MASKED, 0.0) if jb == 0 else 0.0
        pen_next = jnp.where(i == last, MASKED, 0.0) if jb == nblk - 1 else 0.0
        s_ctx, s = s
        parts = [s[0:BLOCK] + (bias_prev + pen_prev), s[BLOCK:2 * BLOCK],
                 s[2 * BLOCK:3 * BLOCK] + (bias_next + pen_next), s_ctx]
        if bounded:
            return jnp.concatenate([jnp.exp2(p_).astype(BF16) for p_ in parts], axis=0), None
        m = jnp.maximum(
            jnp.maximum(jnp.max(parts[0], axis=0, keepdims=True), jnp.max(parts[1], axis=0, keepdims=True)),
            jnp.maximum(jnp.max(parts[2], axis=0, keepdims=True), jnp.max(parts[3], axis=0, keepdims=True)))
        return parts, jnp.maximum(m, sink_ref[kvh, :, part * cw:(part + 1) * cw])

    def exp_weights(parts, m):
        return jnp.concatenate([jnp.exp2(p_ - m).astype(BF16) for p_ in parts], axis=0)

    def value_matmul(jb, kvh, part, p):
        v_all = jnp.concatenate(
            [vt_ext[kvh * HEAD_DIM:(kvh + 1) * HEAD_DIM, jb * BLOCK:(jb + 3) * BLOCK],
             vct[kvh * HEAD_DIM:(kvh + 1) * HEAD_DIM, :]], axis=1)
        return _dot(jnp.concatenate([v_all, ones_rows], axis=0), p)

    def normalize(jb, kvh, part, pv, m):
        sink = sink_ref[kvh, :, part * cw:(part + 1) * cw]
        denom = pv[HEAD_DIM:HEAD_DIM + 1, :] + jnp.exp2(sink if bounded else sink - m)
        o_t = pv[:HEAD_DIM, :] * (1.0 / denom)
        return [o_t[:, g * BLOCK:(g + 1) * BLOCK] for g in range(HEADS_PER_STEP)]

    def block_norm(out_t):
        o_all = jnp.concatenate(out_t, axis=0)
        ms = jnp.mean(o_all * o_all, axis=0, keepdims=True)
        y = o_all * lax.rsqrt(ms + EPS) * again_ref[...]
        return y.T.astype(BF16)

    aw = ATTN_HEADS * HEAD_DIM
    steps = [(jb, kvh, part) for jb in range(nblk) for kvh in range(ATTN_KV_HEADS)
             for part in range(ATTN_GROUP // HEADS_PER_STEP)]
    assert n_chunks >= len(steps)
    attn_rows, out_t, unnormalized, gate_up = [], [], None, {}

    def collect(step, pv, m):
        out_t.extend(normalize(*step, pv, m))
        if len(out_t) == ATTN_HEADS:
            attn_rows.append(block_norm(out_t))
            out_t.clear()

    if bounded:
        proj_mlp = _dot(mlp_ref[...], wo_ref[aw:, :])
        for n, step in enumerate(steps):
            scores = score_matmul(*step)
            gate_up[n] = ffn_matmuls(n)
            weights, _ = mask_and_max(*step, scores)
            if unnormalized is not None:
                collect(*unnormalized)
                ffn_activation(n - 1, gate_up.pop(n - 1))
            unnormalized = (step, value_matmul(*step, weights), None)
    else:
        raw = {n: score_matmul(*steps[n]) for n in range(2)}
        proj_mlp = _dot(mlp_ref[...], wo_ref[aw:, :])
        ready = {0: mask_and_max(*steps[0], raw.pop(0))}
        for n, step in enumerate(steps):
            if n + 1 < len(steps):
                ready[n + 1] = mask_and_max(*steps[n + 1], raw.pop(n + 1))
            if n + 2 < len(steps):
                raw[n + 2] = score_matmul(*steps[n + 2])
            gate_up[n] = ffn_matmuls(n)
            if unnormalized is not None:
                collect(*unnormalized)
                ffn_activation(n - 1, gate_up.pop(n - 1))
            parts, m = ready.pop(n)
            unnormalized = (step, value_matmul(*step, exp_weights(parts, m)), m)

    c_next = len(steps)
    gate_up[c_next] = ffn_matmuls(c_next)
    collect(*unnormalized)
    ffn_activation(c_next - 1, gate_up.pop(c_next - 1))
    for c in range(c_next + 1, n_chunks):
        gate_up[c] = ffn_matmuls(c)
        ffn_activation(c - 1, gate_up.pop(c - 1))
    proj_attn = _dot(jnp.concatenate(attn_rows, axis=0), wo_ref[:aw, :])
    ffn_activation(n_chunks - 1, gate_up.pop(n_chunks - 1))
    h_new = x_ref[...] + mod_ref[2:3, :] * (proj_attn + proj_mlp)
    h_ref[slot_w] = h_new
    hn_ref[...] = _mod_norm(h_new, fgain_ref[...], mod_ref[3:4, :], mod_ref[4:5, :]).astype(BF16)
    o_ref[...] = h_ref[slot_r] + fmod_ref[5:6, :] * _dot(hid_ref[...], wd_ref[...])


def _mix_ffn(x, mod3, k, vt, kc, vct, qt, mlpn, sink_rows, bias, again_b, wo_bf,
             ffn_gain, wgu_bf, wd_bf, tile, bounded):
    b, s, d = x.shape
    c = kc.shape[1]
    aw = ATTN_HEADS * HEAD_DIM
    mw = mlpn.shape[2]
    ff = wd_bf.shape[0]
    assert ff % MXU_DIM == 0
    r = tile // BLOCK
    nb = s // BLOCK
    nt = s // tile
    n_tiles = b * nt
    gq = ATTN_GROUP * BLOCK

    def mix_tile(g):
        t = jnp.minimum(g, n_tiles - 1)
        return t // nt, t % nt

    def ffn_tile(g):
        t = jnp.maximum(g - 1, 0)
        return t // nt, t % nt

    def at_mix(fn):
        return lambda g: fn(*mix_tile(g))

    const = lambda shape: pl.BlockSpec(shape, lambda g: (0,) * len(shape))
    return pl.pallas_call(
        functools.partial(_mix_ffn_kernel, nt, bounded),
        grid=(n_tiles + 1,),
        in_specs=[pl.BlockSpec((None, tile, d), at_mix(lambda bb, i: (bb, i, 0))),
                  pl.BlockSpec((None, N_MOD, d), at_mix(lambda bb, i: (bb, 0, 0))),
                  pl.BlockSpec((None, BLOCK, LANES), at_mix(lambda bb, i: (bb, jnp.maximum(i * r - 1, 0), 0))),
                  pl.BlockSpec((None, tile, LANES), at_mix(lambda bb, i: (bb, i, 0))),
                  pl.BlockSpec((None, BLOCK, LANES),
                               at_mix(lambda bb, i: (bb, jnp.minimum((i + 1) * r, nb - 1), 0))),
                  pl.BlockSpec((None, LANES, BLOCK), at_mix(lambda bb, i: (bb, 0, jnp.maximum(i * r - 1, 0)))),
                  pl.BlockSpec((None, LANES, tile), at_mix(lambda bb, i: (bb, 0, i))),
                  pl.BlockSpec((None, LANES, BLOCK),
                               at_mix(lambda bb, i: (bb, 0, jnp.minimum((i + 1) * r, nb - 1)))),
                  pl.BlockSpec((None, c, LANES), at_mix(lambda bb, i: (bb, 0, 0))),
                  pl.BlockSpec((None, LANES, c), at_mix(lambda bb, i: (bb, 0, 0))),
                  pl.BlockSpec((None, ATTN_KV_HEADS, HEAD_DIM, ATTN_GROUP * tile),
                               at_mix(lambda bb, i: (bb, 0, 0, i))),
                  pl.BlockSpec((None, tile, mw), at_mix(lambda bb, i: (bb, i, 0))),
                  const((ATTN_KV_HEADS, 1, gq)),
                  const((2 * BLOCK, gq)),
                  const((aw, BLOCK)),
                  const((aw + mw, d)),
                  pl.BlockSpec((None, N_MOD, d), lambda g: (ffn_tile(g)[0], 0, 0)),
                  const((1, d)), const((d, 2 * ff)), const((ff, d))],
        out_specs=pl.BlockSpec((None, tile, d), lambda g: (*ffn_tile(g), 0)),
        out_shape=jax.ShapeDtypeStruct((b, s, d), F32),
        scratch_shapes=[pltpu.VMEM((2, tile, d), F32), pltpu.VMEM((tile, d), BF16),
                        pltpu.VMEM((tile, ff), BF16)],
        compiler_params=pltpu.CompilerParams(
            dimension_semantics=("arbitrary",), vmem_limit_bytes=VMEM_LIMIT),
        name="mix_ffn",
    )(x, mod3, k, k, k, vt, vt, vt, kc, vct, qt, mlpn, sink_rows, bias, again_b, wo_bf,
      mod3, ffn_gain, wgu_bf, wd_bf)


def _rope_tables(s):
    axis_dim = HEAD_DIM // 2
    pos = np.arange(s)
    inv_freq = (ROPE_THETA ** (-np.arange(0, axis_dim, 2, dtype=np.float32) / axis_dim)).astype(np.float32)
    ang_r = (pos // GRID_W).astype(np.float32)[:, None] * inv_freq[None, :]
    ang_c = (pos % GRID_W).astype(np.float32)[:, None] * inv_freq[None, :]
    cr, sr, cc, sc = np.cos(ang_r), np.sin(ang_r), np.cos(ang_c), np.sin(ang_c)
    z = np.zeros_like(sr)
    reps = LANES // HEAD_DIM
    cos = np.tile(np.concatenate([cr, cr, cc, cc], axis=1), (1, reps))
    sin_up = np.tile(np.concatenate([-sr, z, -sc, z], axis=1), (1, reps))
    sin_dn = np.tile(np.concatenate([z, sr, z, sc], axis=1), (1, reps))
    return jnp.asarray(cos, F32), jnp.asarray(sin_up, F32), jnp.asarray(sin_dn, F32)


def _window_bias():
    c = np.arange(BLOCK)[:, None]
    r = np.arange(BLOCK)[None, :]
    prev = np.where(c >= r, 0.0, MASKED).astype(np.float32)
    nxt = np.where(c <= r, 0.0, MASKED).astype(np.float32)
    return jnp.asarray(np.tile(np.concatenate([prev, nxt], axis=0), (1, ATTN_GROUP)), F32)


def _head_mean_matrix():
    bd = np.kron(np.eye(MXU_DIM // HEAD_DIM, dtype=np.float32),
                 np.full((HEAD_DIM, HEAD_DIM), 1.0 / HEAD_DIM, np.float32))
    return jnp.asarray(bd, BF16)


def kernel(x, c, ctx, c_ctx, w_mod, b_mod, norm_mix, norm_ffn, w_in, q_gain, k_gain, attn_sink,
           gate_gain, w_spatial, b_spatial, attn_out_gain, mlp_out_gain, w_out, w_gate_up, w_down):
    b, s, d = x.shape
    assert w_mod.shape[0] == 1, "single-layer problem"
    assert s % 1024 == 0 and d % LANES == 0
    aw = ATTN_HEADS * HEAD_DIM
    mw = MLP_HEADS * HEAD_DIM

    rows = -(-(b + 1) // BF16_SUBLANES) * BF16_SUBLANES
    cond = jnp.concatenate([c, c_ctx[None, :], jnp.zeros((rows - b - 1, d), F32)], axis=0)
    mod, w_in_bf = _adaln(cond, w_mod[0], b_mod[0][None, :], w_in[0])
    mod3 = mod.reshape(rows, N_MOD, d)
    bd = _head_mean_matrix()
    kgain2 = jnp.tile(k_gain[0], ATTN_KV_HEADS)[None, :]
    qgain = (jnp.tile(q_gain[0], ATTN_HEADS) * (HEAD_DIM ** -0.5 * LOG2E))[None, :]
    norm_mix_g = norm_mix[0][None, :]

    kc, vct = _ctx_kv(ctx, mod3, b, norm_mix_g, w_in_bf, kgain2, bd)

    cos, sup, sdn = _rope_tables(s)
    bs_full = jnp.repeat(b_spatial[0].T, HEAD_DIM, axis=1)
    k, vt, qt, mlpn, wo_bf, wgu_bf, wd_bf = _inproj(
        x, mod3, norm_mix_g, w_in_bf, cos, sup, sdn, qgain, kgain2,
        gate_gain[0].reshape(1, mw), w_spatial[0], bs_full,
        mlp_out_gain[0][None, :], bd, (w_out[0], w_gate_up[0], w_down[0]), tile=1024)

    sink_rows = jnp.repeat(attn_sink[0].reshape(ATTN_KV_HEADS, ATTN_GROUP) * LOG2E,
                           BLOCK, axis=1)[:, None, :]
    again_b = jnp.broadcast_to(attn_out_gain[0][:, None], (aw, BLOCK))
    logit_bound = (BOUND_MARGIN * HEAD_DIM ** 0.5 * LOG2E
                   * jnp.max(jnp.abs(q_gain[0])) * jnp.max(jnp.abs(k_gain[0])))
    sink_max = jnp.max(jnp.abs(attn_sink[0])) * LOG2E
    use_bound = (logit_bound <= MAX_SAFE_LOGIT) & (sink_max <= MAX_SAFE_LOGIT)
    operands = (x, mod3, k, vt, kc, vct, qt, mlpn, sink_rows, _window_bias(), again_b,
                wo_bf, norm_ffn[0][None, :], wgu_bf, wd_bf)
    return lax.cond(use_bound,
                    lambda ops: _mix_ffn(*ops, tile=512, bounded=True),
                    lambda ops: _mix_ffn(*ops, tile=512, bounded=False),
                    operands)
```

```python
import functools
import math

import jax
import jax.numpy as jnp
import numpy as np
from jax import lax
from jax.experimental import pallas as pl
from jax.experimental.pallas import tpu as pltpu

F32 = jnp.float32
BF16 = jnp.bfloat16

HEAD_DIM = 64
ATTN_HEADS = 8
ATTN_KV_HEADS = 2
ATTN_GROUP = ATTN_HEADS // ATTN_KV_HEADS
MLP_HEADS = 8
N_MOD = 6
BLOCK = 128
GRID_W = 64
ROPE_THETA = 10000.0
EPS = 1e-6
MASKED = -1e30
BOUND_MARGIN = 1.05
MAX_SAFE_LOGIT = 40.0
LOG2E = math.log2(math.e)

LANES = 128
BF16_SUBLANES = 16
MXU_DIM = 256
VMEM_LIMIT = 56 * 1024 * 1024
HEADS_PER_STEP = 4


def _dot(a, b):
    return jnp.dot(a, b, preferred_element_type=F32)


def _silu(x):
    return x * (1.0 / (1.0 + jnp.exp(-x)))


def _gelu2_tanh(x):
    c = math.sqrt(2.0 / math.pi)
    return x * (1.0 + jnp.tanh(x * (c + (c * 0.044715) * (x * x))))


def _mod_norm(x, gain, shift, scale):
    y = x * lax.rsqrt(jnp.mean(x * x, axis=-1, keepdims=True) + EPS)
    return y * (gain * (1.0 + scale)) + shift


def _head_ms(x, bd):
    return _dot((x * x).astype(BF16), bd)


def _rope(x, cos, sin_up, sin_dn):
    up = pltpu.roll(x, LANES - 16, 1)
    dn = pltpu.roll(x, 16, 1)
    return x * cos + up * sin_up + dn * sin_dn


def _split_bf16(x):
    hi = x.astype(BF16)
    return hi, (x - hi.astype(F32)).astype(BF16)


def _adaln_kernel(cond_ref, wa_ref, wb_ref, b_ref, win_ref, o_ref, win_bf_ref):
    win_bf_ref[...] = win_ref[...].astype(BF16)
    rows = cond_ref.shape[0]
    half = wa_ref.shape[1]

    @pl.when(pl.program_id(0) == 0)
    def _():
        o_ref[...] = jnp.broadcast_to(b_ref[...], o_ref.shape)

    s_hi, s_lo = _split_bf16(_silu(cond_ref[...]))
    s_both = jnp.concatenate([s_hi, s_lo], axis=0)
    for part, w_ref in enumerate((wa_ref, wb_ref)):
        w_hi, w_lo = _split_bf16(w_ref[...])
        both = _dot(s_both, w_hi)
        o_ref[:, part * half:(part + 1) * half] += both[:rows] + both[rows:] + _dot(s_hi, w_lo)


def _adaln(cond, w_mod, b_mod, w_in):
    rows, d = cond.shape
    n = w_mod.shape[1]
    steps = 4
    tk = d // steps
    win_rows = w_in.shape[0] // steps
    assert rows % BF16_SUBLANES == 0 and tk % LANES == 0 and (n // 2) % LANES == 0
    assert win_rows % BF16_SUBLANES == 0
    win_spec = pl.BlockSpec((win_rows, w_in.shape[1]), lambda j: (j, 0))
    return pl.pallas_call(
        _adaln_kernel,
        grid=(steps,),
        in_specs=[pl.BlockSpec((rows, tk), lambda j: (0, j)),
                  pl.BlockSpec((tk, n // 2), lambda j: (j, 0)),
                  pl.BlockSpec((tk, n // 2), lambda j: (j, 1)),
                  pl.BlockSpec((1, n), lambda j: (0, 0)),
                  win_spec],
        out_specs=[pl.BlockSpec((rows, n), lambda j: (0, 0)), win_spec],
        out_shape=[jax.ShapeDtypeStruct((rows, n), F32), jax.ShapeDtypeStruct(w_in.shape, BF16)],
        compiler_params=pltpu.CompilerParams(
            dimension_semantics=("arbitrary",), vmem_limit_bytes=VMEM_LIMIT),
        name="adaln",
    )(cond, w_mod, w_mod, b_mod, w_in)


def _ctx_kernel(x_ref, mod_ref, gain_ref, w_ref, kgain_ref, bd_ref, kc_ref, vct_ref):
    per_step, c, d = x_ref.shape
    x = x_ref[...].reshape(per_step * c, d)
    hn = _mod_norm(x, gain_ref[...], mod_ref[0:1, :], mod_ref[1:2, :])
    kv = _dot(hn.astype(BF16), w_ref[...])
    k = kv[:, :LANES]
    k = k * lax.rsqrt(_head_ms(k, bd_ref[:LANES, :LANES]) + EPS) * kgain_ref[...]
    for j in range(per_step):
        kc_ref[j] = k[j * c:(j + 1) * c, :].astype(BF16)
        vct_ref[j] = kv[j * c:(j + 1) * c, LANES:].T.astype(BF16)


def _ctx_kv(ctx, mod3, ctx_row, norm_gain, w_in_bf, kgain2, bd):
    b, c, d = ctx.shape
    kvw = 2 * ATTN_KV_HEADS * HEAD_DIM
    per_step = 2 if b % 2 == 0 else 1
    return pl.pallas_call(
        _ctx_kernel,
        grid=(b // per_step,),
        in_specs=[pl.BlockSpec((per_step, c, d), lambda i: (i, 0, 0)),
                  pl.BlockSpec((None, N_MOD, d), lambda i: (ctx_row, 0, 0)),
                  pl.BlockSpec((1, d), lambda i: (0, 0)),
                  pl.BlockSpec((d, kvw), lambda i: (0, 0)),
                  pl.BlockSpec((1, LANES), lambda i: (0, 0)),
                  pl.BlockSpec((MXU_DIM, MXU_DIM), lambda i: (0, 0))],
        out_specs=[pl.BlockSpec((per_step, c, LANES), lambda i: (i, 0, 0)),
                   pl.BlockSpec((per_step, LANES, c), lambda i: (i, 0, 0))],
        out_shape=[jax.ShapeDtypeStruct((b, c, LANES), BF16),
                   jax.ShapeDtypeStruct((b, LANES, c), BF16)],
        compiler_params=pltpu.CompilerParams(
            dimension_semantics=("arbitrary",), vmem_limit_bytes=VMEM_LIMIT),
        name="ctx_kv",
    )(ctx, mod3, norm_gain, w_in_bf, kgain2, bd)


def _inproj_kernel(x_ref, mod_ref, gain_ref, w_ref, cos_ref, sup_ref, sdn_ref,
                   qgain_ref, kgain_ref, ggain_ref, ws_ref, bs_ref, ogain_ref, bd_ref,
                   wo_f32_ref, wgu_f32_ref, wd_f32_ref,
                   k_ref, vt_ref, qt_ref, mlp_ref, wo_bf_ref, wgu_bf_ref, wd_bf_ref):
    wo_bf_ref[...] = wo_f32_ref[...].astype(BF16)
    wgu_bf_ref[...] = wgu_f32_ref[...].astype(BF16)
    wd_bf_ref[...] = wd_f32_ref[...].astype(BF16)

    t = x_ref.shape[0]
    nblk = t // BLOCK
    kvw = ATTN_KV_HEADS * HEAD_DIM
    aw = ATTN_HEADS * HEAD_DIM
    mw = MLP_HEADS * HEAD_DIM
    q_cols = slice(2 * kvw, 2 * kvw + aw)
    kv_cols = slice(0, 2 * kvw)
    u_cols = slice(2 * kvw + aw, 2 * kvw + aw + mw)
    g_cols = slice(2 * kvw + aw + mw, 2 * kvw + aw + 2 * mw)
    halves = range(aw // MXU_DIM)
    bd = bd_ref[...]

    def project(cols):
        return _dot(hn, w_ref[:, cols])

    def finish_qkv(q, kv, q_ms, k_ms):
        cos, sup, sdn = cos_ref[...], sup_ref[...], sdn_ref[...]
        k = kv[:, :kvw] * lax.rsqrt(k_ms + EPS) * kgain_ref[...]
        k_ref[...] = _rope(k, cos, sup, sdn).astype(BF16)
        vt_ref[...] = kv[:, kvw:].T.astype(BF16)
        for half in halves:
            qh = q[:, half * MXU_DIM:(half + 1) * MXU_DIM] * lax.rsqrt(q_ms[half] + EPS)
            qh = qh * qgain_ref[:, half * MXU_DIM:(half + 1) * MXU_DIM]
            for sl in range(MXU_DIM // LANES):
                qs = _rope(qh[:, sl * LANES:(sl + 1) * LANES], cos, sup, sdn)
                qst = qs.T
                for hh in range(LANES // HEAD_DIM):
                    head = (half * MXU_DIM + sl * LANES) // HEAD_DIM + hh
                    kvh, grp = head // ATTN_GROUP, head % ATTN_GROUP
                    for jb in range(nblk):
                        col = (jb * ATTN_GROUP + grp) * BLOCK
                        qt_ref[kvh, :, col:col + BLOCK] = qst[
                            hh * HEAD_DIM:(hh + 1) * HEAD_DIM, jb * BLOCK:(jb + 1) * BLOCK].astype(BF16)

    def gate_mix(g_raw):
        g = _gelu2_tanh(g_raw)
        g_ms = [_head_ms(g[:, h * MXU_DIM:(h + 1) * MXU_DIM], bd) for h in halves]
        gn = jnp.concatenate(
            [g[:, h * MXU_DIM:(h + 1) * MXU_DIM] * lax.rsqrt(g_ms[h] + 4.0 * EPS) for h in halves],
            axis=1) * ggain_ref[...]
        gnb = gn.astype(BF16)
        low_head = lax.broadcasted_iota(jnp.int32, (BLOCK, t), 1) % LANES < HEAD_DIM
        mixed_slabs = []
        for p in range(MLP_HEADS // 2):
            rhs = jnp.concatenate(
                [gnb[c * BLOCK:(c + 1) * BLOCK, p * LANES:(p + 1) * LANES] for c in range(nblk)], axis=1)
            a = _dot(ws_ref[2 * p].astype(BF16), rhs)
            b = _dot(ws_ref[2 * p + 1].astype(BF16), rhs)
            mixed_slabs.append(jnp.where(low_head, a, b))
        return mixed_slabs

    def finish_mlp(u_raw, mixed_slabs):
        u = _gelu2_tanh(u_raw)
        rows = []
        for c in range(nblk):
            mixed_c = jnp.concatenate(
                [m[:, c * LANES:(c + 1) * LANES] for m in mixed_slabs], axis=1) + bs_ref[...]
            rows.append(u[c * BLOCK:(c + 1) * BLOCK, :] * mixed_c)
        o = jnp.concatenate(rows, axis=0)
        o = o * lax.rsqrt(jnp.mean(o * o, axis=-1, keepdims=True) + 4.0 * EPS) * ogain_ref[...]
        mlp_ref[...] = o.astype(BF16)

    hn = _mod_norm(x_ref[...], gain_ref[...], mod_ref[0:1, :], mod_ref[1:2, :]).astype(BF16)
    q, kv = project(q_cols), project(kv_cols)
    q_ms = [_head_ms(q[:, h * MXU_DIM:(h + 1) * MXU_DIM], bd) for h in halves]
    k_ms = _head_ms(kv[:, :kvw], bd[:kvw, :kvw])
    g_raw = project(g_cols)
    finish_qkv(q, kv, q_ms, k_ms)
    u_raw = project(u_cols)
    finish_mlp(u_raw, gate_mix(g_raw))


def _inproj(x, mod3, norm_gain, w_in_bf, cos, sup, sdn, qgain, kgain2, ggain, w_spatial, bs_full,
            ogain, bd, later_weights, tile):
    b, s, d = x.shape
    inw = w_in_bf.shape[1]
    aw = ATTN_HEADS * HEAD_DIM
    mw = MLP_HEADS * HEAD_DIM
    n_steps = (s // tile) * b
    const = lambda shape: pl.BlockSpec(shape, lambda i, bb: (0,) * len(shape))

    def slab_spec(w):
        rows = next(r for r in range(BF16_SUBLANES, w.shape[0] + 1, BF16_SUBLANES)
                    if w.shape[0] % r == 0 and w.shape[0] // r <= n_steps)
        n_slabs = w.shape[0] // rows
        return pl.BlockSpec((rows, w.shape[1]), lambda i, bb: (jnp.minimum(i * b + bb, n_slabs - 1), 0))

    slab_specs = [slab_spec(w) for w in later_weights]
    table_spec = pl.BlockSpec((tile, LANES), lambda i, bb: (i, 0))
    return pl.pallas_call(
        _inproj_kernel,
        grid=(s // tile, b),
        in_specs=[pl.BlockSpec((None, tile, d), lambda i, bb: (bb, i, 0)),
                  pl.BlockSpec((None, N_MOD, d), lambda i, bb: (bb, 0, 0)),
                  const((1, d)),
                  const((d, inw)),
                  table_spec, table_spec, table_spec,
                  const((1, aw)), const((1, LANES)), const((1, mw)),
                  const((MLP_HEADS, BLOCK, BLOCK)), const((BLOCK, mw)), const((1, mw)),
                  const((MXU_DIM, MXU_DIM))] + slab_specs,
        out_specs=[pl.BlockSpec((None, tile, LANES), lambda i, bb: (bb, i, 0)),
                   pl.BlockSpec((None, LANES, tile), lambda i, bb: (bb, 0, i)),
                   pl.BlockSpec((None, ATTN_KV_HEADS, HEAD_DIM, ATTN_GROUP * tile),
                                lambda i, bb: (bb, 0, 0, i)),
                   pl.BlockSpec((None, tile, mw), lambda i, bb: (bb, i, 0))] + slab_specs,
        out_shape=[jax.ShapeDtypeStruct((b, s, LANES), BF16),
                   jax.ShapeDtypeStruct((b, LANES, s), BF16),
                   jax.ShapeDtypeStruct((b, ATTN_KV_HEADS, HEAD_DIM, ATTN_GROUP * s), BF16),
                   jax.ShapeDtypeStruct((b, s, mw), BF16)]
        + [jax.ShapeDtypeStruct(w.shape, BF16) for w in later_weights],
        compiler_params=pltpu.CompilerParams(
            dimension_semantics=("arbitrary", "arbitrary"), vmem_limit_bytes=VMEM_LIMIT),
        name="inproj",
    )(x, mod3, norm_gain, w_in_bf, cos, sup, sdn, qgain, kgain2, ggain, w_spatial, bs_full, ogain, bd,
      *later_weights)


def _mix_ffn_kernel(tiles_per_seq, bounded,
                    x_ref, mod_ref, kp_ref, km_ref, kn_ref, vp_ref, vm_ref, vn_ref,
                    kc_ref, vct_ref, qt_ref, mlp_ref, sink_ref, bias_ref, again_ref, wo_ref,
                    fmod_ref, fgain_ref, wgu_ref, wd_ref,
                    o_ref, h_ref, hn_ref, hid_ref):
    tq = x_ref.shape[0]
    nblk = tq // BLOCK
    step_id = pl.program_id(0)
    n_tiles = pl.num_programs(0) - 1
    i = lax.rem(jnp.minimum(step_id, n_tiles - 1), tiles_per_seq)
    last = tiles_per_seq - 1
    gq = ATTN_GROUP * BLOCK
    slot_w = lax.rem(step_id, 2)
    slot_r = 1 - slot_w

    @pl.when(step_id == 0)
    def _():
        h_ref[1] = jnp.zeros(h_ref.shape[1:], F32)
        hn_ref[...] = jnp.zeros(hn_ref.shape, BF16)

    ff = wd_ref.shape[0]
    n_chunks = ff // MXU_DIM
    def ffn_matmuls(c):
        hn = hn_ref[...]
        return (_dot(hn, wgu_ref[:, c * MXU_DIM:(c + 1) * MXU_DIM]),
                _dot(hn, wgu_ref[:, ff + c * MXU_DIM:ff + (c + 1) * MXU_DIM]))

    def ffn_activation(c, gate_up):
        a, b = gate_up
        hid_ref[:, c * MXU_DIM:(c + 1) * MXU_DIM] = (_silu(a) * b).astype(BF16)

    k_ext = jnp.concatenate([kp_ref[...], km_ref[...], kn_ref[...]], axis=0)
    vt_ext = jnp.concatenate([vp_ref[...], vm_ref[...], vn_ref[...]], axis=1)
    kc = kc_ref[...]
    vct = vct_ref[...]
    cw = HEADS_PER_STEP * BLOCK
    zeros_q = jnp.zeros((HEAD_DIM, cw), BF16)
    ones_rows = jnp.ones((BF16_SUBLANES, 3 * BLOCK + kc.shape[0]), BF16)
    bias_prev = bias_ref[0:BLOCK, :cw]
    bias_next = bias_ref[BLOCK:2 * BLOCK, :cw]

    def sink_row(kvh, part):
        first = kvh * ATTN_GROUP + part * HEADS_PER_STEP
        return jnp.concatenate([jnp.full((1, BLOCK), sink_ref[first + g] * LOG2E, F32)
                                for g in range(HEADS_PER_STEP)], axis=1)

    def score_matmul(jb, kvh, part):
        qt = qt_ref[kvh, :, jb * gq + part * cw:jb * gq + (part + 1) * cw]
        rhs = jnp.concatenate([qt, zeros_q] if kvh == 0 else [zeros_q, qt], axis=0)
        return _dot(kc, rhs), _dot(k_ext[jb * BLOCK:(jb + 3) * BLOCK, :], rhs)

    def mask_and_max(jb, kvh, part, s):
        pen_prev = jnp.where(i == 0, MASKED, 0.0) if jb == 0 else 0.0
        pen_next = jnp.where(i == last, MASKED, 0.0) if jb == nblk - 1 else 0.0
        s_ctx, s = s
        parts = [s[0:BLOCK] + (bias_prev + pen_prev), s[BLOCK:2 * BLOCK],
                 s[2 * BLOCK:3 * BLOCK] + (bias_next + pen_next), s_ctx]
        if bounded:
            return jnp.concatenate([jnp.exp2(p_).astype(BF16) for p_ in parts], axis=0), None
        m = jnp.maximum(
            jnp.maximum(jnp.max(parts[0], axis=0, keepdims=True), jnp.max(parts[1], axis=0, keepdims=True)),
            jnp.maximum(jnp.max(parts[2], axis=0, keepdims=True), jnp.max(parts[3], axis=0, keepdims=True)))
        return parts, jnp.maximum(m, sink_row(kvh, part))

    def exp_weights(parts, m):
        return jnp.concatenate([jnp.exp2(p_ - m).astype(BF16) for p_ in parts], axis=0)

    def value_matmul(jb, kvh, part, p):
        v_all = jnp.concatenate(
            [vt_ext[kvh * HEAD_DIM:(kvh + 1) * HEAD_DIM, jb * BLOCK:(jb + 3) * BLOCK],
             vct[kvh * HEAD_DIM:(kvh + 1) * HEAD_DIM, :]], axis=1)
        return _dot(jnp.concatenate([v_all, ones_rows], axis=0), p)

    def normalize(jb, kvh, part, pv, m):
        sink = sink_row(kvh, part)
        denom = pv[HEAD_DIM:HEAD_DIM + 1, :] + jnp.exp2(sink if bounded else sink - m)
        o_t = pv[:HEAD_DIM, :] * (1.0 / denom)
        return [o_t[:, g * BLOCK:(g + 1) * BLOCK] for g in range(HEADS_PER_STEP)]

    def block_norm(out_t):
        o_all = jnp.concatenate(out_t, axis=0)
        ms = jnp.mean(o_all * o_all, axis=0, keepdims=True)
        y = o_all * lax.rsqrt(ms + EPS)
        return (y.T * again_ref[...]).astype(BF16)

    aw = ATTN_HEADS * HEAD_DIM
    steps = [(jb, kvh, part) for jb in range(nblk) for kvh in range(ATTN_KV_HEADS)
             for part in range(ATTN_GROUP // HEADS_PER_STEP)]
    assert n_chunks >= len(steps)
    attn_rows, out_t, unnormalized, gate_up = [], [], None, {}

    def collect(step, pv, m):
        out_t.extend(normalize(*step, pv, m))
        if len(out_t) == ATTN_HEADS:
            attn_rows.append(block_norm(out_t))
            out_t.clear()

    if bounded:
        proj_mlp = _dot(mlp_ref[...], wo_ref[aw:, :])
        for n, step in enumerate(steps):
            scores = score_matmul(*step)
            gate_up[n] = ffn_matmuls(n)
            weights, _ = mask_and_max(*step, scores)
            if unnormalized is not None:
                collect(*unnormalized)
                ffn_activation(n - 1, gate_up.pop(n - 1))
            unnormalized = (step, value_matmul(*step, weights), None)
    else:
        raw = {n: score_matmul(*steps[n]) for n in range(2)}
        proj_mlp = _dot(mlp_ref[...], wo_ref[aw:, :])
        ready = {0: mask_and_max(*steps[0], raw.pop(0))}
        for n, step in enumerate(steps):
            if n + 1 < len(steps):
                ready[n + 1] = mask_and_max(*steps[n + 1], raw.pop(n + 1))
            if n + 2 < len(steps):
                raw[n + 2] = score_matmul(*steps[n + 2])
            gate_up[n] = ffn_matmuls(n)
            if unnormalized is not None:
                collect(*unnormalized)
                ffn_activation(n - 1, gate_up.pop(n - 1))
            parts, m = ready.pop(n)
            unnormalized = (step, value_matmul(*step, exp_weights(parts, m)), m)

    c_next = len(steps)
    gate_up[c_next] = ffn_matmuls(c_next)
    collect(*unnormalized)
    ffn_activation(c_next - 1, gate_up.pop(c_next - 1))
    for c in range(c_next + 1, n_chunks):
        gate_up[c] = ffn_matmuls(c)
        ffn_activation(c - 1, gate_up.pop(c - 1))
    proj_attn = _dot(jnp.concatenate(attn_rows, axis=0), wo_ref[:aw, :])
    ffn_activation(n_chunks - 1, gate_up.pop(n_chunks - 1))
    h_new = x_ref[...] + mod_ref[2:3, :] * (proj_attn + proj_mlp)
    h_ref[slot_w] = h_new
    hn_ref[...] = _mod_norm(h_new, fgain_ref[...], mod_ref[3:4, :], mod_ref[4:5, :]).astype(BF16)
    o_ref[...] = h_ref[slot_r] + fmod_ref[5:6, :] * _dot(hid_ref[...], wd_ref[...])


def _mix_ffn(x, mod3, k, vt, kc, vct, qt, mlpn, sinks, bias, again_row, wo_bf,
             ffn_gain, wgu_bf, wd_bf, tile, bounded):
    b, s, d = x.shape
    c = kc.shape[1]
    aw = ATTN_HEADS * HEAD_DIM
    mw = mlpn.shape[2]
    ff = wd_bf.shape[0]
    assert ff % MXU_DIM == 0
    r = tile // BLOCK
    nb = s // BLOCK
    nt = s // tile
    n_tiles = b * nt
    gq = ATTN_GROUP * BLOCK

    def mix_tile(g):
        t = jnp.minimum(g, n_tiles - 1)
        return t // nt, t % nt

    def ffn_tile(g):
        t = jnp.maximum(g - 1, 0)
        return t // nt, t % nt

    def at_mix(fn):
        return lambda g: fn(*mix_tile(g))

    const = lambda shape: pl.BlockSpec(shape, lambda g: (0,) * len(shape))
    return pl.pallas_call(
        functools.partial(_mix_ffn_kernel, nt, bounded),
        grid=(n_tiles + 1,),
        in_specs=[pl.BlockSpec((None, tile, d), at_mix(lambda bb, i: (bb, i, 0))),
                  pl.BlockSpec((None, N_MOD, d), at_mix(lambda bb, i: (bb, 0, 0))),
                  pl.BlockSpec((None, BLOCK, LANES), at_mix(lambda bb, i: (bb, jnp.maximum(i * r - 1, 0), 0))),
                  pl.BlockSpec((None, tile, LANES), at_mix(lambda bb, i: (bb, i, 0))),
                  pl.BlockSpec((None, BLOCK, LANES),
                               at_mix(lambda bb, i: (bb, jnp.minimum((i + 1) * r, nb - 1), 0))),
                  pl.BlockSpec((None, LANES, BLOCK), at_mix(lambda bb, i: (bb, 0, jnp.maximum(i * r - 1, 0)))),
                  pl.BlockSpec((None, LANES, tile), at_mix(lambda bb, i: (bb, 0, i))),
                  pl.BlockSpec((None, LANES, BLOCK),
                               at_mix(lambda bb, i: (bb, 0, jnp.minimum((i + 1) * r, nb - 1)))),
                  pl.BlockSpec((None, c, LANES), at_mix(lambda bb, i: (bb, 0, 0))),
                  pl.BlockSpec((None, LANES, c), at_mix(lambda bb, i: (bb, 0, 0))),
                  pl.BlockSpec((None, ATTN_KV_HEADS, HEAD_DIM, ATTN_GROUP * tile),
                               at_mix(lambda bb, i: (bb, 0, 0, i))),
                  pl.BlockSpec((None, tile, mw), at_mix(lambda bb, i: (bb, i, 0))),
                  pl.BlockSpec(memory_space=pltpu.SMEM),
                  const((2 * BLOCK, gq)),
                  const((1, aw)),
                  const((aw + mw, d)),
                  pl.BlockSpec((None, N_MOD, d), lambda g: (ffn_tile(g)[0], 0, 0)),
                  const((1, d)), const((d, 2 * ff)), const((ff, d))],
        out_specs=pl.BlockSpec((None, tile, d), lambda g: (*ffn_tile(g), 0)),
        out_shape=jax.ShapeDtypeStruct((b, s, d), F32),
        scratch_shapes=[pltpu.VMEM((2, tile, d), F32), pltpu.VMEM((tile, d), BF16),
                        pltpu.VMEM((tile, ff), BF16)],
        compiler_params=pltpu.CompilerParams(
            dimension_semantics=("arbitrary",), vmem_limit_bytes=VMEM_LIMIT),
        name="mix_ffn",
    )(x, mod3, k, k, k, vt, vt, vt, kc, vct, qt, mlpn, sinks, bias, again_row, wo_bf,
      mod3, ffn_gain, wgu_bf, wd_bf)


def _rope_tables(s):
    axis_dim = HEAD_DIM // 2
    pos = np.arange(s)
    inv_freq = (ROPE_THETA ** (-np.arange(0, axis_dim, 2, dtype=np.float32) / axis_dim)).astype(np.float32)
    ang_r = (pos // GRID_W).astype(np.float32)[:, None] * inv_freq[None, :]
    ang_c = (pos % GRID_W).astype(np.float32)[:, None] * inv_freq[None, :]
    cr, sr, cc, sc = np.cos(ang_r), np.sin(ang_r), np.cos(ang_c), np.sin(ang_c)
    z = np.zeros_like(sr)
    reps = LANES // HEAD_DIM
    cos = np.tile(np.concatenate([cr, cr, cc, cc], axis=1), (1, reps))
    sin_up = np.tile(np.concatenate([-sr, z, -sc, z], axis=1), (1, reps))
    sin_dn = np.tile(np.concatenate([z, sr, z, sc], axis=1), (1, reps))
    return jnp.asarray(cos, F32), jnp.asarray(sin_up, F32), jnp.asarray(sin_dn, F32)


def _window_bias():
    c = np.arange(BLOCK)[:, None]
    r = np.arange(BLOCK)[None, :]
    prev = np.where(c >= r, 0.0, MASKED).astype(np.float32)
    nxt = np.where(c <= r, 0.0, MASKED).astype(np.float32)
    return jnp.asarray(np.tile(np.concatenate([prev, nxt], axis=0), (1, ATTN_GROUP)), F32)


def _head_mean_matrix():
    bd = np.kron(np.eye(MXU_DIM // HEAD_DIM, dtype=np.float32),
                 np.full((HEAD_DIM, HEAD_DIM), 1.0 / HEAD_DIM, np.float32))
    return jnp.asarray(bd, BF16)


def kernel(x, c, ctx, c_ctx, w_mod, b_mod, norm_mix, norm_ffn, w_in, q_gain, k_gain, attn_sink,
           gate_gain, w_spatial, b_spatial, attn_out_gain, mlp_out_gain, w_out, w_gate_up, w_down):
    b, s, d = x.shape
    assert w_mod.shape[0] == 1, "single-layer problem"
    assert s % 1024 == 0 and d % LANES == 0
    aw = ATTN_HEADS * HEAD_DIM
    mw = MLP_HEADS * HEAD_DIM

    rows = -(-(b + 1) // BF16_SUBLANES) * BF16_SUBLANES
    cond = jnp.concatenate([c, c_ctx[None, :], jnp.zeros((rows - b - 1, d), F32)], axis=0)
    mod, w_in_bf = _adaln(cond, w_mod[0], b_mod[0][None, :], w_in[0])
    mod3 = mod.reshape(rows, N_MOD, d)
    bd = _head_mean_matrix()
    kgain2 = jnp.tile(k_gain[0], ATTN_KV_HEADS)[None, :]
    qgain = (jnp.tile(q_gain[0], ATTN_HEADS) * (HEAD_DIM ** -0.5 * LOG2E))[None, :]
    norm_mix_g = norm_mix[0][None, :]

    kc, vct = _ctx_kv(ctx, mod3, b, norm_mix_g, w_in_bf, kgain2, bd)

    cos, sup, sdn = _rope_tables(s)
    bs_full = jnp.repeat(b_spatial[0].T, HEAD_DIM, axis=1)
    k, vt, qt, mlpn, wo_bf, wgu_bf, wd_bf = _inproj(
        x, mod3, norm_mix_g, w_in_bf, cos, sup, sdn, qgain, kgain2,
        gate_gain[0].reshape(1, mw), w_spatial[0], bs_full,
        mlp_out_gain[0][None, :], bd, (w_out[0], w_gate_up[0], w_down[0]), tile=1024)

    again_row = attn_out_gain[0][None, :]
    logit_bound = (BOUND_MARGIN * HEAD_DIM ** 0.5 * LOG2E
                   * jnp.max(jnp.abs(q_gain[0])) * jnp.max(jnp.abs(k_gain[0])))
    sink_max = jnp.max(jnp.abs(attn_sink[0])) * LOG2E
    use_bound = (logit_bound <= MAX_SAFE_LOGIT) & (sink_max <= MAX_SAFE_LOGIT)
    operands = (x, mod3, k, vt, kc, vct, qt, mlpn, attn_sink[0], _window_bias(), again_row,
                wo_bf, norm_ffn[0][None, :], wgu_bf, wd_bf)
    return lax.cond(use_bound,
                    lambda ops: _mix_ffn(*ops, tile=512, bounded=True),
                    lambda ops: _mix_ffn(*ops, tile=512, bounded=False),
                    operands)
```

```python
import functools
import math

import jax
import jax.numpy as jnp
import numpy as np
from jax import lax
from jax.experimental import pallas as pl
from jax.experimental.pallas import tpu as pltpu

F32 = jnp.float32
BF16 = jnp.bfloat16

HEAD_DIM = 64
ATTN_HEADS = 8
ATTN_KV_HEADS = 2
ATTN_GROUP = ATTN_HEADS // ATTN_KV_HEADS
MLP_HEADS = 8
N_MOD = 6
BLOCK = 128
GRID_W = 64
ROPE_THETA = 10000.0
EPS = 1e-6
MASKED = -1e30
BOUND_MARGIN = 1.05
MAX_SAFE_LOGIT = 40.0
LOG2E = math.log2(math.e)

LANES = 128
BF16_SUBLANES = 16
MXU_DIM = 256
VMEM_LIMIT = 56 * 1024 * 1024
HEADS_PER_STEP = 4


def _dot(a, b):
    return jnp.dot(a, b, preferred_element_type=F32)


def _silu(x):
    return x * (1.0 / (1.0 + jnp.exp(-x)))


def _gelu2_tanh(x):
    c = math.sqrt(2.0 / math.pi)
    return x * (1.0 + jnp.tanh(x * (c + (c * 0.044715) * (x * x))))


def _mod_norm(x, gain, shift, scale):
    y = x * lax.rsqrt(jnp.mean(x * x, axis=-1, keepdims=True) + EPS)
    return y * (gain * (1.0 + scale)) + shift


def _head_ms(x, bd):
    return _dot((x * x).astype(BF16), bd)


def _rope(x, cos, sin_up, sin_dn):
    up = pltpu.roll(x, LANES - 16, 1)
    dn = pltpu.roll(x, 16, 1)
    return x * cos + up * sin_up + dn * sin_dn


def _split_bf16(x):
    hi = x.astype(BF16)
    return hi, (x - hi.astype(F32)).astype(BF16)


def _adaln_kernel(cond_ref, wa_ref, wb_ref, b_ref, win_ref, o_ref, win_bf_ref):
    win_bf_ref[...] = win_ref[...].astype(BF16)
    rows = cond_ref.shape[0]
    half = wa_ref.shape[1]

    @pl.when(pl.program_id(0) == 0)
    def _():
        o_ref[...] = jnp.broadcast_to(b_ref[...], o_ref.shape)

    s_hi, s_lo = _split_bf16(_silu(cond_ref[...]))
    s_both = jnp.concatenate([s_hi, s_lo], axis=0)
    for part, w_ref in enumerate((wa_ref, wb_ref)):
        w_hi, w_lo = _split_bf16(w_ref[...])
        both = _dot(s_both, w_hi)
        o_ref[:, part * half:(part + 1) * half] += both[:rows] + both[rows:] + _dot(s_hi, w_lo)


def _adaln(cond, w_mod, b_mod, w_in):
    rows, d = cond.shape
    n = w_mod.shape[1]
    steps = 4
    tk = d // steps
    win_rows = w_in.shape[0] // steps
    assert rows % BF16_SUBLANES == 0 and tk % LANES == 0 and (n // 2) % LANES == 0
    assert win_rows % BF16_SUBLANES == 0
    win_spec = pl.BlockSpec((win_rows, w_in.shape[1]), lambda j: (j, 0))
    return pl.pallas_call(
        _adaln_kernel,
        grid=(steps,),
        in_specs=[pl.BlockSpec((rows, tk), lambda j: (0, j)),
                  pl.BlockSpec((tk, n // 2), lambda j: (j, 0)),
                  pl.BlockSpec((tk, n // 2), lambda j: (j, 1)),
                  pl.BlockSpec((1, n), lambda j: (0, 0)),
                  win_spec],
        out_specs=[pl.BlockSpec((rows, n), lambda j: (0, 0)), win_spec],
        out_shape=[jax.ShapeDtypeStruct((rows, n), F32), jax.ShapeDtypeStruct(w_in.shape, BF16)],
        compiler_params=pltpu.CompilerParams(
            dimension_semantics=("arbitrary",), vmem_limit_bytes=VMEM_LIMIT),
        name="adaln",
    )(cond, w_mod, w_mod, b_mod, w_in)


def _ctx_kernel(x_ref, mod_ref, gain_ref, w_ref, kgain_ref, bd_ref, kc_ref, vct_ref):
    per_step, c, d = x_ref.shape
    x = x_ref[...].reshape(per_step * c, d)
    hn = _mod_norm(x, gain_ref[...], mod_ref[0:1, :], mod_ref[1:2, :])
    kv = _dot(hn.astype(BF16), w_ref[...])
    k = kv[:, :LANES]
    k = k * lax.rsqrt(_head_ms(k, bd_ref[:LANES, :LANES]) + EPS) * kgain_ref[...]
    for j in range(per_step):
        kc_ref[j] = k[j * c:(j + 1) * c, :].astype(BF16)
        vct_ref[j] = kv[j * c:(j + 1) * c, LANES:].T.astype(BF16)


def _ctx_kv(ctx, mod3, ctx_row, norm_gain, w_in_bf, kgain2, bd):
    b, c, d = ctx.shape
    kvw = 2 * ATTN_KV_HEADS * HEAD_DIM
    per_step = 2 if b % 2 == 0 else 1
    return pl.pallas_call(
        _ctx_kernel,
        grid=(b // per_step,),
        in_specs=[pl.BlockSpec((per_step, c, d), lambda i: (i, 0, 0)),
                  pl.BlockSpec((None, N_MOD, d), lambda i: (ctx_row, 0, 0)),
                  pl.BlockSpec((1, d), lambda i: (0, 0)),
                  pl.BlockSpec((d, kvw), lambda i: (0, 0)),
                  pl.BlockSpec((1, LANES), lambda i: (0, 0)),
                  pl.BlockSpec((MXU_DIM, MXU_DIM), lambda i: (0, 0))],
        out_specs=[pl.BlockSpec((per_step, c, LANES), lambda i: (i, 0, 0)),
                   pl.BlockSpec((per_step, LANES, c), lambda i: (i, 0, 0))],
        out_shape=[jax.ShapeDtypeStruct((b, c, LANES), BF16),
                   jax.ShapeDtypeStruct((b, LANES, c), BF16)],
        compiler_params=pltpu.CompilerParams(
            dimension_semantics=("arbitrary",), vmem_limit_bytes=VMEM_LIMIT),
        name="ctx_kv",
    )(ctx, mod3, norm_gain, w_in_bf, kgain2, bd)


def _inproj_kernel(x_ref, mod_ref, gain_ref, w_ref, cos_ref, sup_ref, sdn_ref,
                   qgain_ref, kgain_ref, ggain_ref, ws_ref, bs_ref, ogain_ref, bd_ref,
                   wo_f32_ref, wgu_f32_ref, wd_f32_ref,
                   k_ref, vt_ref, qt_ref, mlp_ref, wo_bf_ref, wgu_bf_ref, wd_bf_ref):
    wo_bf_ref[...] = wo_f32_ref[...].astype(BF16)
    wgu_bf_ref[...] = wgu_f32_ref[...].astype(BF16)
    wd_bf_ref[...] = wd_f32_ref[...].astype(BF16)

    t = x_ref.shape[0]
    nblk = t // BLOCK
    kvw = ATTN_KV_HEADS * HEAD_DIM
    aw = ATTN_HEADS * HEAD_DIM
    mw = MLP_HEADS * HEAD_DIM
    q_cols = slice(2 * kvw, 2 * kvw + aw)
    kv_cols = slice(0, 2 * kvw)
    u_cols = slice(2 * kvw + aw, 2 * kvw + aw + mw)
    g_cols = slice(2 * kvw + aw + mw, 2 * kvw + aw + 2 * mw)
    halves = range(aw // MXU_DIM)
    bd = bd_ref[...]

    def project(cols):
        return _dot(hn, w_ref[:, cols])

    def finish_qkv(q, kv, q_ms, k_ms):
        cos, sup, sdn = cos_ref[...], sup_ref[...], sdn_ref[...]
        k = kv[:, :kvw] * lax.rsqrt(k_ms + EPS) * kgain_ref[...]
        k_ref[...] = _rope(k, cos, sup, sdn).astype(BF16)
        vt_ref[...] = kv[:, kvw:].T.astype(BF16)
        for half in halves:
            qh = q[:, half * MXU_DIM:(half + 1) * MXU_DIM] * lax.rsqrt(q_ms[half] + EPS)
            qh = qh * qgain_ref[:, half * MXU_DIM:(half + 1) * MXU_DIM]
            for sl in range(MXU_DIM // LANES):
                qs = _rope(qh[:, sl * LANES:(sl + 1) * LANES], cos, sup, sdn)
                qst = qs.T
                for hh in range(LANES // HEAD_DIM):
                    head = (half * MXU_DIM + sl * LANES) // HEAD_DIM + hh
                    kvh, grp = head // ATTN_GROUP, head % ATTN_GROUP
                    for jb in range(nblk):
                        col = (jb * ATTN_GROUP + grp) * BLOCK
                        qt_ref[kvh, :, col:col + BLOCK] = qst[
                            hh * HEAD_DIM:(hh + 1) * HEAD_DIM, jb * BLOCK:(jb + 1) * BLOCK].astype(BF16)

    def gate_mix(g_raw):
        g = _gelu2_tanh(g_raw)
        g_ms = [_head_ms(g[:, h * MXU_DIM:(h + 1) * MXU_DIM], bd) for h in halves]
        gn = jnp.concatenate(
            [g[:, h * MXU_DIM:(h + 1) * MXU_DIM] * lax.rsqrt(g_ms[h] + 4.0 * EPS) for h in halves],
            axis=1) * ggain_ref[...]
        gnb = gn.astype(BF16)
        low_head = lax.broadcasted_iota(jnp.int32, (BLOCK, t), 1) % LANES < HEAD_DIM
        mixed_slabs = []
        for p in range(MLP_HEADS // 2):
            rhs = jnp.concatenate(
                [gnb[c * BLOCK:(c + 1) * BLOCK, p * LANES:(p + 1) * LANES] for c in range(nblk)], axis=1)
            a = _dot(ws_ref[2 * p].astype(BF16), rhs)
            b = _dot(ws_ref[2 * p + 1].astype(BF16), rhs)
            mixed_slabs.append(jnp.where(low_head, a, b))
        return mixed_slabs

    def finish_mlp(u_raw, mixed_slabs):
        u = _gelu2_tanh(u_raw)
        rows = []
        for c in range(nblk):
            mixed_c = jnp.concatenate(
                [m[:, c * LANES:(c + 1) * LANES] for m in mixed_slabs], axis=1) + bs_ref[...]
            rows.append(u[c * BLOCK:(c + 1) * BLOCK, :] * mixed_c)
        o = jnp.concatenate(rows, axis=0)
        o = o * lax.rsqrt(jnp.mean(o * o, axis=-1, keepdims=True) + 4.0 * EPS) * ogain_ref[...]
        mlp_ref[...] = o.astype(BF16)

    hn = _mod_norm(x_ref[...], gain_ref[...], mod_ref[0:1, :], mod_ref[1:2, :]).astype(BF16)
    q, kv = project(q_cols), project(kv_cols)
    q_ms = [_head_ms(q[:, h * MXU_DIM:(h + 1) * MXU_DIM], bd) for h in halves]
    k_ms = _head_ms(kv[:, :kvw], bd[:kvw, :kvw])
    g_raw = project(g_cols)
    finish_qkv(q, kv, q_ms, k_ms)
    u_raw = project(u_cols)
    finish_mlp(u_raw, gate_mix(g_raw))


def _inproj(x, mod3, norm_gain, w_in_bf, cos, sup, sdn, qgain, kgain2, ggain, w_spatial, bs_full,
            ogain, bd, later_weights, tile):
    b, s, d = x.shape
    inw = w_in_bf.shape[1]
    aw = ATTN_HEADS * HEAD_DIM
    mw = MLP_HEADS * HEAD_DIM
    n_steps = (s // tile) * b
    const = lambda shape: pl.BlockSpec(shape, lambda i, bb: (0,) * len(shape))

    def slab_spec(w):
        rows = next(r for r in range(BF16_SUBLANES, w.shape[0] + 1, BF16_SUBLANES)
                    if w.shape[0] % r == 0 and w.shape[0] // r <= n_steps)
        n_slabs = w.shape[0] // rows
        return pl.BlockSpec((rows, w.shape[1]), lambda i, bb: (jnp.minimum(i * b + bb, n_slabs - 1), 0))

    slab_specs = [slab_spec(w) for w in later_weights]
    table_spec = pl.BlockSpec((tile, LANES), lambda i, bb: (i, 0))
    return pl.pallas_call(
        _inproj_kernel,
        grid=(s // tile, b),
        in_specs=[pl.BlockSpec((None, tile, d), lambda i, bb: (bb, i, 0)),
                  pl.BlockSpec((None, N_MOD, d), lambda i, bb: (bb, 0, 0)),
                  const((1, d)),
                  const((d, inw)),
                  table_spec, table_spec, table_spec,
                  const((1, aw)), const((1, LANES)), const((1, mw)),
                  const((MLP_HEADS, BLOCK, BLOCK)), const((BLOCK, mw)), const((1, mw)),
                  const((MXU_DIM, MXU_DIM))] + slab_specs,
        out_specs=[pl.BlockSpec((None, tile, LANES), lambda i, bb: (bb, i, 0)),
                   pl.BlockSpec((None, LANES, tile), lambda i, bb: (bb, 0, i)),
                   pl.BlockSpec((None, ATTN_KV_HEADS, HEAD_DIM, ATTN_GROUP * tile),
                                lambda i, bb: (bb, 0, 0, i)),
                   pl.BlockSpec((None, tile, mw), lambda i, bb: (bb, i, 0))] + slab_specs,
        out_shape=[jax.ShapeDtypeStruct((b, s, LANES), BF16),
                   jax.ShapeDtypeStruct((b, LANES, s), BF16),
                   jax.ShapeDtypeStruct((b, ATTN_KV_HEADS, HEAD_DIM, ATTN_GROUP * s), BF16),
                   jax.ShapeDtypeStruct((b, s, mw), BF16)]
        + [jax.ShapeDtypeStruct(w.shape, BF16) for w in later_weights],
        compiler_params=pltpu.CompilerParams(
            dimension_semantics=("arbitrary", "arbitrary"), vmem_limit_bytes=VMEM_LIMIT),
        name="inproj",
    )(x, mod3, norm_gain, w_in_bf, cos, sup, sdn, qgain, kgain2, ggain, w_spatial, bs_full, ogain, bd,
      *later_weights)


def _mix_ffn_kernel(tiles_per_seq, bounded,
                    x_ref, mod_ref, kp_ref, km_ref, kn_ref, vp_ref, vm_ref, vn_ref,
                    kc_ref, vct_ref, qt_ref, mlp_ref, sink_ref, bias_ref, again_ref, wo_ref,
                    fmod_ref, fgain_ref, wgu_ref, wd_ref,
                    o_ref, h_ref, hn_ref, hid_ref):
    tq = x_ref.shape[0]
    nblk = tq // BLOCK
    step_id = pl.program_id(0)
    n_tiles = pl.num_programs(0) - 1
    i = lax.rem(jnp.minimum(step_id, n_tiles - 1), tiles_per_seq)
    last = tiles_per_seq - 1
    gq = ATTN_GROUP * BLOCK
    slot_w = lax.rem(step_id, 2)
    slot_r = 1 - slot_w

    @pl.when(step_id == 0)
    def _():
        h_ref[1] = jnp.zeros(h_ref.shape[1:], F32)
        hn_ref[...] = jnp.zeros(hn_ref.shape, BF16)

    ff = wd_ref.shape[0]
    n_chunks = ff // MXU_DIM
    def ffn_matmuls(c):
        hn = hn_ref[...]
        return (_dot(hn, wgu_ref[:, c * MXU_DIM:(c + 1) * MXU_DIM]),
                _dot(hn, wgu_ref[:, ff + c * MXU_DIM:ff + (c + 1) * MXU_DIM]))

    def ffn_activation(c, gate_up):
        a, b = gate_up
        hid_ref[:, c * MXU_DIM:(c + 1) * MXU_DIM] = (_silu(a) * b).astype(BF16)

    k_ext = jnp.concatenate([kp_ref[...], km_ref[...], kn_ref[...]], axis=0)
    vt_ext = jnp.concatenate([vp_ref[...], vm_ref[...], vn_ref[...]], axis=1)
    kc = kc_ref[...]
    vct = vct_ref[...]
    cw = HEADS_PER_STEP * BLOCK
    zeros_q = jnp.zeros((HEAD_DIM, cw), BF16)
    ones_rows = jnp.ones((BF16_SUBLANES, 3 * BLOCK + kc.shape[0]), BF16)
    bias_prev = bias_ref[0:BLOCK, :cw]
    bias_next = bias_ref[BLOCK:2 * BLOCK, :cw]

    def sink_row(kvh, part):
        first = kvh * ATTN_GROUP + part * HEADS_PER_STEP
        return jnp.concatenate([jnp.full((1, BLOCK), sink_ref[first + g] * LOG2E, F32)
                                for g in range(HEADS_PER_STEP)], axis=1)

    def score_matmul(jb, kvh, part):
        qt = qt_ref[kvh, :, jb * gq + part * cw:jb * gq + (part + 1) * cw]
        rhs = jnp.concatenate([qt, zeros_q] if kvh == 0 else [zeros_q, qt], axis=0)
        return _dot(kc, rhs), _dot(k_ext[jb * BLOCK:(jb + 3) * BLOCK, :], rhs)

    def mask_and_max(jb, kvh, part, s):
        pen_prev = jnp.where(i == 0, MASKED, 0.0) if jb == 0 else 0.0
        pen_next = jnp.where(i == last, MASKED, 0.0) if jb == nblk - 1 else 0.0
        s_ctx, s = s
        parts = [s[0:BLOCK] + (bias_prev + pen_prev), s[BLOCK:2 * BLOCK],
                 s[2 * BLOCK:3 * BLOCK] + (bias_next + pen_next), s_ctx]
        if bounded:
            return jnp.concatenate([jnp.exp2(p_).astype(BF16) for p_ in parts], axis=0), None
        m = jnp.maximum(
            jnp.maximum(jnp.max(parts[0], axis=0, keepdims=True), jnp.max(parts[1], axis=0, keepdims=True)),
            jnp.maximum(jnp.max(parts[2], axis=0, keepdims=True), jnp.max(parts[3], axis=0, keepdims=True)))
        return parts, jnp.maximum(m, sink_row(kvh, part))

    def exp_weights(parts, m):
        return jnp.concatenate([jnp.exp2(p_ - m).astype(BF16) for p_ in parts], axis=0)

    def value_matmul(jb, kvh, part, p):
        v_all = jnp.concatenate(
            [vt_ext[kvh * HEAD_DIM:(kvh + 1) * HEAD_DIM, jb * BLOCK:(jb + 3) * BLOCK],
             vct[kvh * HEAD_DIM:(kvh + 1) * HEAD_DIM, :]], axis=1)
        return _dot(jnp.concatenate([v_all, ones_rows], axis=0), p)

    def normalize(jb, kvh, part, pv, m):
        sink = sink_row(kvh, part)
        denom = pv[HEAD_DIM:HEAD_DIM + 1, :] + jnp.exp2(sink if bounded else sink - m)
        o_t = pv[:HEAD_DIM, :] * (1.0 / denom)
        return [o_t[:, g * BLOCK:(g + 1) * BLOCK] for g in range(HEADS_PER_STEP)]

    def block_norm(out_t):
        o_all = jnp.concatenate(out_t, axis=0)
        ms = jnp.mean(o_all * o_all, axis=0, keepdims=True)
        y = o_all * lax.rsqrt(ms + EPS)
        return (y.T * again_ref[...]).astype(BF16)

    aw = ATTN_HEADS * HEAD_DIM
    steps = [(jb, kvh, part) for jb in range(nblk) for kvh in range(ATTN_KV_HEADS)
             for part in range(ATTN_GROUP // HEADS_PER_STEP)]
    assert n_chunks >= len(steps)
    attn_rows, out_t, unnormalized, gate_up = [], [], None, {}

    def collect(step, pv, m):
        out_t.extend(normalize(*step, pv, m))
        if len(out_t) == ATTN_HEADS:
            attn_rows.append(block_norm(out_t))
            out_t.clear()

    if bounded:
        proj_mlp = None
        for n, step in enumerate(steps):
            scores = score_matmul(*step)
            gate_up[n] = ffn_matmuls(n)
            weights, _ = mask_and_max(*step, scores)
            if unnormalized is not None:
                collect(*unnormalized)
                ffn_activation(n - 1, gate_up.pop(n - 1))
            unnormalized = (step, value_matmul(*step, weights), None)
    else:
        raw = {n: score_matmul(*steps[n]) for n in range(2)}
        proj_mlp = _dot(mlp_ref[...], wo_ref[aw:, :])
        ready = {0: mask_and_max(*steps[0], raw.pop(0))}
        for n, step in enumerate(steps):
            if n + 1 < len(steps):
                ready[n + 1] = mask_and_max(*steps[n + 1], raw.pop(n + 1))
            if n + 2 < len(steps):
                raw[n + 2] = score_matmul(*steps[n + 2])
            gate_up[n] = ffn_matmuls(n)
            if unnormalized is not None:
                collect(*unnormalized)
                ffn_activation(n - 1, gate_up.pop(n - 1))
            parts, m = ready.pop(n)
            unnormalized = (step, value_matmul(*step, exp_weights(parts, m)), m)

    c_next = len(steps)
    gate_up[c_next] = ffn_matmuls(c_next)
    collect(*unnormalized)
    ffn_activation(c_next - 1, gate_up.pop(c_next - 1))
    for c in range(c_next + 1, n_chunks):
        gate_up[c] = ffn_matmuls(c)
        ffn_activation(c - 1, gate_up.pop(c - 1))
    if proj_mlp is None:
        proj_mlp = _dot(mlp_ref[...], wo_ref[aw:, :])
    proj_attn = _dot(jnp.concatenate(attn_rows, axis=0), wo_ref[:aw, :])
    ffn_activation(n_chunks - 1, gate_up.pop(n_chunks - 1))
    h_new = x_ref[...] + mod_ref[2:3, :] * (proj_attn + proj_mlp)
    h_ref[slot_w] = h_new
    hn_ref[...] = _mod_norm(h_new, fgain_ref[...], mod_ref[3:4, :], mod_ref[4:5, :]).astype(BF16)
    o_ref[...] = h_ref[slot_r] + fmod_ref[5:6, :] * _dot(hid_ref[...], wd_ref[...])


def _mix_ffn(x, mod3, k, vt, kc, vct, qt, mlpn, sinks, bias, again_row, wo_bf,
             ffn_gain, wgu_bf, wd_bf, tile, bounded):
    b, s, d = x.shape
    c = kc.shape[1]
    aw = ATTN_HEADS * HEAD_DIM
    mw = mlpn.shape[2]
    ff = wd_bf.shape[0]
    assert ff % MXU_DIM == 0
    r = tile // BLOCK
    nb = s // BLOCK
    nt = s // tile
    n_tiles = b * nt
    gq = ATTN_GROUP * BLOCK

    def mix_tile(g):
        t = jnp.minimum(g, n_tiles - 1)
        return t // nt, t % nt

    def ffn_tile(g):
        t = jnp.maximum(g - 1, 0)
        return t // nt, t % nt

    def at_mix(fn):
        return lambda g: fn(*mix_tile(g))

    const = lambda shape: pl.BlockSpec(shape, lambda g: (0,) * len(shape))
    return pl.pallas_call(
        functools.partial(_mix_ffn_kernel, nt, bounded),
        grid=(n_tiles + 1,),
        in_specs=[pl.BlockSpec((None, tile, d), at_mix(lambda bb, i: (bb, i, 0))),
                  pl.BlockSpec((None, N_MOD, d), at_mix(lambda bb, i: (bb, 0, 0))),
                  pl.BlockSpec((None, BLOCK, LANES), at_mix(lambda bb, i: (bb, jnp.maximum(i * r - 1, 0), 0))),
                  pl.BlockSpec((None, tile, LANES), at_mix(lambda bb, i: (bb, i, 0))),
                  pl.BlockSpec((None, BLOCK, LANES),
                               at_mix(lambda bb, i: (bb, jnp.minimum((i + 1) * r, nb - 1), 0))),
                  pl.BlockSpec((None, LANES, BLOCK), at_mix(lambda bb, i: (bb, 0, jnp.maximum(i * r - 1, 0)))),
                  pl.BlockSpec((None, LANES, tile), at_mix(lambda bb, i: (bb, 0, i))),
                  pl.BlockSpec((None, LANES, BLOCK),
                               at_mix(lambda bb, i: (bb, 0, jnp.minimum((i + 1) * r, nb - 1)))),
                  pl.BlockSpec((None, c, LANES), at_mix(lambda bb, i: (bb, 0, 0))),
                  pl.BlockSpec((None, LANES, c), at_mix(lambda bb, i: (bb, 0, 0))),
                  pl.BlockSpec((None, ATTN_KV_HEADS, HEAD_DIM, ATTN_GROUP * tile),
                               at_mix(lambda bb, i: (bb, 0, 0, i))),
                  pl.BlockSpec((None, tile, mw), at_mix(lambda bb, i: (bb, i, 0))),
                  pl.BlockSpec(memory_space=pltpu.SMEM),
                  const((2 * BLOCK, gq)),
                  const((1, aw)),
                  const((aw + mw, d)),
                  pl.BlockSpec((None, N_MOD, d), lambda g: (ffn_tile(g)[0], 0, 0)),
                  const((1, d)), const((d, 2 * ff)), const((ff, d))],
        out_specs=pl.BlockSpec((None, tile, d), lambda g: (*ffn_tile(g), 0)),
        out_shape=jax.ShapeDtypeStruct((b, s, d), F32),
        scratch_shapes=[pltpu.VMEM((2, tile, d), F32), pltpu.VMEM((tile, d), BF16),
                        pltpu.VMEM((tile, ff), BF16)],
        compiler_params=pltpu.CompilerParams(
            dimension_semantics=("arbitrary",), vmem_limit_bytes=VMEM_LIMIT),
        name="mix_ffn",
    )(x, mod3, k, k, k, vt, vt, vt, kc, vct, qt, mlpn, sinks, bias, again_row, wo_bf,
      mod3, ffn_gain, wgu_bf, wd_bf)


def _rope_tables(s):
    axis_dim = HEAD_DIM // 2
    pos = np.arange(s)
    inv_freq = (ROPE_THETA ** (-np.arange(0, axis_dim, 2, dtype=np.float32) / axis_dim)).astype(np.float32)
    ang_r = (pos // GRID_W).astype(np.float32)[:, None] * inv_freq[None, :]
    ang_c = (pos % GRID_W).astype(np.float32)[:, None] * inv_freq[None, :]
    cr, sr, cc, sc = np.cos(ang_r), np.sin(ang_r), np.cos(ang_c), np.sin(ang_c)
    z = np.zeros_like(sr)
    reps = LANES // HEAD_DIM
    cos = np.tile(np.concatenate([cr, cr, cc, cc], axis=1), (1, reps))
    sin_up = np.tile(np.concatenate([-sr, z, -sc, z], axis=1), (1, reps))
    sin_dn = np.tile(np.concatenate([z, sr, z, sc], axis=1), (1, reps))
    return jnp.asarray(cos, F32), jnp.asarray(sin_up, F32), jnp.asarray(sin_dn, F32)


def _window_bias():
    c = np.arange(BLOCK)[:, None]
    r = np.arange(BLOCK)[None, :]
    prev = np.where(c >= r, 0.0, MASKED).astype(np.float32)
    nxt = np.where(c <= r, 0.0, MASKED).astype(np.float32)
    return jnp.asarray(np.tile(np.concatenate([prev, nxt], axis=0), (1, ATTN_GROUP)), F32)


def _head_mean_matrix():
    bd = np.kron(np.eye(MXU_DIM // HEAD_DIM, dtype=np.float32),
                 np.full((HEAD_DIM, HEAD_DIM), 1.0 / HEAD_DIM, np.float32))
    return jnp.asarray(bd, BF16)


def kernel(x, c, ctx, c_ctx, w_mod, b_mod, norm_mix, norm_ffn, w_in, q_gain, k_gain, attn_sink,
           gate_gain, w_spatial, b_spatial, attn_out_gain, mlp_out_gain, w_out, w_gate_up, w_down):
    b, s, d = x.shape
    assert w_mod.shape[0] == 1, "single-layer problem"
    assert s % 1024 == 0 and d % LANES == 0
    aw = ATTN_HEADS * HEAD_DIM
    mw = MLP_HEADS * HEAD_DIM

    rows = -(-(b + 1) // BF16_SUBLANES) * BF16_SUBLANES
    cond = jnp.concatenate([c, c_ctx[None, :], jnp.zeros((rows - b - 1, d), F32)], axis=0)
    mod, w_in_bf = _adaln(cond, w_mod[0], b_mod[0][None, :], w_in[0])
    mod3 = mod.reshape(rows, N_MOD, d)
    bd = _head_mean_matrix()
    kgain2 = jnp.tile(k_gain[0], ATTN_KV_HEADS)[None, :]
    qgain = (jnp.tile(q_gain[0], ATTN_HEADS) * (HEAD_DIM ** -0.5 * LOG2E))[None, :]
    norm_mix_g = norm_mix[0][None, :]

    kc, vct = _ctx_kv(ctx, mod3, b, norm_mix_g, w_in_bf, kgain2, bd)

    cos, sup, sdn = _rope_tables(s)
    bs_full = jnp.repeat(b_spatial[0].T, HEAD_DIM, axis=1)
    k, vt, qt, mlpn, wo_bf, wgu_bf, wd_bf = _inproj(
        x, mod3, norm_mix_g, w_in_bf, cos, sup, sdn, qgain, kgain2,
        gate_gain[0].reshape(1, mw), w_spatial[0], bs_full,
        mlp_out_gain[0][None, :], bd, (w_out[0], w_gate_up[0], w_down[0]), tile=1024)

    again_row = attn_out_gain[0][None, :]
    logit_bound = (BOUND_MARGIN * HEAD_DIM ** 0.5 * LOG2E
                   * jnp.max(jnp.abs(q_gain[0])) * jnp.max(jnp.abs(k_gain[0])))
    sink_max = jnp.max(jnp.abs(attn_sink[0])) * LOG2E
    use_bound = (logit_bound <= MAX_SAFE_LOGIT) & (sink_max <= MAX_SAFE_LOGIT)
    operands = (x, mod3, k, vt, kc, vct, qt, mlpn, attn_sink[0], _window_bias(), again_row,
                wo_bf, norm_ffn[0][None, :], wgu_bf, wd_bf)
    return lax.cond(use_bound,
                    lambda ops: _mix_ffn(*ops, tile=512, bounded=True),
                    lambda ops: _mix_ffn(*ops, tile=512, bounded=False),
                    operands)
```

```python
import functools
import math

import jax
import jax.numpy as jnp
import numpy as np
from jax import lax
from jax.experimental import pallas as pl
from jax.experimental.pallas import tpu as pltpu

F32 = jnp.float32
BF16 = jnp.bfloat16

HEAD_DIM = 64
ATTN_HEADS = 8
ATTN_KV_HEADS = 2
ATTN_GROUP = ATTN_HEADS // ATTN_KV_HEADS
MLP_HEADS = 8
N_MOD = 6
BLOCK = 128
GRID_W = 64
ROPE_THETA = 10000.0
EPS = 1e-6
MASKED = -1e30
BOUND_MARGIN = 1.05
MAX_SAFE_LOGIT = 40.0
LOG2E = math.log2(math.e)

LANES = 128
BF16_SUBLANES = 16
MXU_DIM = 256
VMEM_LIMIT = 56 * 1024 * 1024
HEADS_PER_STEP = 4


def _dot(a, b):
    return jnp.dot(a, b, preferred_element_type=F32)


def _silu(x):
    return x * (1.0 / (1.0 + jnp.exp(-x)))


def _gelu2_tanh(x):
    c = math.sqrt(2.0 / math.pi)
    return x * (1.0 + jnp.tanh(x * (c + (c * 0.044715) * (x * x))))


def _mod_norm(x, gain, shift, scale):
    y = x * lax.rsqrt(jnp.mean(x * x, axis=-1, keepdims=True) + EPS)
    return y * (gain * (1.0 + scale)) + shift


def _head_ms(x, bd):
    return _dot((x * x).astype(BF16), bd)


def _rope(x, cos, sin_up, sin_dn):
    up = pltpu.roll(x, LANES - 16, 1)
    dn = pltpu.roll(x, 16, 1)
    return x * cos + up * sin_up + dn * sin_dn


def _split_bf16(x):
    hi = x.astype(BF16)
    return hi, (x - hi.astype(F32)).astype(BF16)


def _adaln_kernel(cond_ref, wa_ref, wb_ref, b_ref, win_ref, o_ref, win_bf_ref):
    win_bf_ref[...] = win_ref[...].astype(BF16)
    rows = cond_ref.shape[0]
    half = wa_ref.shape[1]

    @pl.when(pl.program_id(0) == 0)
    def _():
        o_ref[...] = jnp.broadcast_to(b_ref[...], o_ref.shape)

    s_hi, s_lo = _split_bf16(_silu(cond_ref[...]))
    s_both = jnp.concatenate([s_hi, s_lo], axis=0)
    for part, w_ref in enumerate((wa_ref, wb_ref)):
        w_hi, w_lo = _split_bf16(w_ref[...])
        both = _dot(s_both, w_hi)
        o_ref[:, part * half:(part + 1) * half] += both[:rows] + both[rows:] + _dot(s_hi, w_lo)


def _adaln(cond, w_mod, b_mod, w_in):
    rows, d = cond.shape
    n = w_mod.shape[1]
    steps = 4
    tk = d // steps
    win_rows = w_in.shape[0] // steps
    assert rows % BF16_SUBLANES == 0 and tk % LANES == 0 and (n // 2) % LANES == 0
    assert win_rows % BF16_SUBLANES == 0
    win_spec = pl.BlockSpec((win_rows, w_in.shape[1]), lambda j: (j, 0))
    return pl.pallas_call(
        _adaln_kernel,
        grid=(steps,),
        in_specs=[pl.BlockSpec((rows, tk), lambda j: (0, j)),
                  pl.BlockSpec((tk, n // 2), lambda j: (j, 0)),
                  pl.BlockSpec((tk, n // 2), lambda j: (j, 1)),
                  pl.BlockSpec((1, n), lambda j: (0, 0)),
                  win_spec],
        out_specs=[pl.BlockSpec((rows, n), lambda j: (0, 0)), win_spec],
        out_shape=[jax.ShapeDtypeStruct((rows, n), F32), jax.ShapeDtypeStruct(w_in.shape, BF16)],
        compiler_params=pltpu.CompilerParams(
            dimension_semantics=("arbitrary",), vmem_limit_bytes=VMEM_LIMIT),
        name="adaln",
    )(cond, w_mod, w_mod, b_mod, w_in)


def _ctx_kernel(x_ref, mod_ref, gain_ref, w_ref, kgain_ref, bd_ref, kc_ref, vct_ref):
    per_step, c, d = x_ref.shape
    x = x_ref[...].reshape(per_step * c, d)
    hn = _mod_norm(x, gain_ref[...], mod_ref[0:1, :], mod_ref[1:2, :])
    kv = _dot(hn.astype(BF16), w_ref[...])
    k = kv[:, :LANES]
    k = k * lax.rsqrt(_head_ms(k, bd_ref[:LANES, :LANES]) + EPS) * kgain_ref[...]
    for j in range(per_step):
        kc_ref[j] = k[j * c:(j + 1) * c, :].astype(BF16)
        vct_ref[j] = kv[j * c:(j + 1) * c, LANES:].T.astype(BF16)


def _ctx_kv(ctx, mod3, ctx_row, norm_gain, w_in_bf, gains, kgain_block, bd):
    b, c, d = ctx.shape
    kvw = 2 * ATTN_KV_HEADS * HEAD_DIM
    per_step = 2 if b % 2 == 0 else 1
    return pl.pallas_call(
        _ctx_kernel,
        grid=(b // per_step,),
        in_specs=[pl.BlockSpec((per_step, c, d), lambda i: (i, 0, 0)),
                  pl.BlockSpec((None, N_MOD, d), lambda i: (ctx_row, 0, 0)),
                  pl.BlockSpec((1, d), lambda i: (0, 0)),
                  pl.BlockSpec((d, kvw), lambda i: (0, 0)),
                  pl.BlockSpec((1, LANES), lambda i: (0, kgain_block)),
                  pl.BlockSpec((MXU_DIM, MXU_DIM), lambda i: (0, 0))],
        out_specs=[pl.BlockSpec((per_step, c, LANES), lambda i: (i, 0, 0)),
                   pl.BlockSpec((per_step, LANES, c), lambda i: (i, 0, 0))],
        out_shape=[jax.ShapeDtypeStruct((b, c, LANES), BF16),
                   jax.ShapeDtypeStruct((b, LANES, c), BF16)],
        compiler_params=pltpu.CompilerParams(
            dimension_semantics=("arbitrary",), vmem_limit_bytes=VMEM_LIMIT),
        name="ctx_kv",
    )(ctx, mod3, norm_gain, w_in_bf, gains, bd)


def _inproj_kernel(x_ref, mod_ref, gain_ref, w_ref, cos_ref, sup_ref, sdn_ref,
                   qgain_ref, kgain_ref, ggain_ref, ws_ref, bs_ref, ogain_ref, bd_ref,
                   wo_f32_ref, wgu_f32_ref, wd_f32_ref,
                   k_ref, vt_ref, qt_ref, mlp_ref, wo_bf_ref, wgu_bf_ref, wd_bf_ref):
    wo_bf_ref[...] = wo_f32_ref[...].astype(BF16)
    wgu_bf_ref[...] = wgu_f32_ref[...].astype(BF16)
    wd_bf_ref[...] = wd_f32_ref[...].astype(BF16)

    t = x_ref.shape[0]
    nblk = t // BLOCK
    kvw = ATTN_KV_HEADS * HEAD_DIM
    aw = ATTN_HEADS * HEAD_DIM
    mw = MLP_HEADS * HEAD_DIM
    q_cols = slice(2 * kvw, 2 * kvw + aw)
    kv_cols = slice(0, 2 * kvw)
    u_cols = slice(2 * kvw + aw, 2 * kvw + aw + mw)
    g_cols = slice(2 * kvw + aw + mw, 2 * kvw + aw + 2 * mw)
    halves = range(aw // MXU_DIM)
    bd = bd_ref[...]

    def project(cols):
        return _dot(hn, w_ref[:, cols])

    def finish_qkv(q, kv, q_ms, k_ms):
        cos, sup, sdn = cos_ref[...], sup_ref[...], sdn_ref[...]
        k = kv[:, :kvw] * lax.rsqrt(k_ms + EPS) * kgain_ref[...]
        k_ref[...] = _rope(k, cos, sup, sdn).astype(BF16)
        vt_ref[...] = kv[:, kvw:].T.astype(BF16)
        for half in halves:
            qh = q[:, half * MXU_DIM:(half + 1) * MXU_DIM] * lax.rsqrt(q_ms[half] + EPS)
            qh = qh * qgain_ref[:, half * MXU_DIM:(half + 1) * MXU_DIM]
            for sl in range(MXU_DIM // LANES):
                qs = _rope(qh[:, sl * LANES:(sl + 1) * LANES], cos, sup, sdn)
                qst = qs.T
                for hh in range(LANES // HEAD_DIM):
                    head = (half * MXU_DIM + sl * LANES) // HEAD_DIM + hh
                    kvh, grp = head // ATTN_GROUP, head % ATTN_GROUP
                    for jb in range(nblk):
                        col = (jb * ATTN_GROUP + grp) * BLOCK
                        qt_ref[kvh, :, col:col + BLOCK] = qst[
                            hh * HEAD_DIM:(hh + 1) * HEAD_DIM, jb * BLOCK:(jb + 1) * BLOCK].astype(BF16)

    def gate_mix(g_raw):
        g = _gelu2_tanh(g_raw)
        g_ms = [_head_ms(g[:, h * MXU_DIM:(h + 1) * MXU_DIM], bd) for h in halves]
        gn = jnp.concatenate(
            [g[:, h * MXU_DIM:(h + 1) * MXU_DIM] * lax.rsqrt(g_ms[h] + 4.0 * EPS) for h in halves],
            axis=1) * ggain_ref[...]
        gnb = gn.astype(BF16)
        low_head = lax.broadcasted_iota(jnp.int32, (BLOCK, t), 1) % LANES < HEAD_DIM
        mixed_slabs = []
        for p in range(MLP_HEADS // 2):
            rhs = jnp.concatenate(
                [gnb[c * BLOCK:(c + 1) * BLOCK, p * LANES:(p + 1) * LANES] for c in range(nblk)], axis=1)
            a = _dot(ws_ref[2 * p].astype(BF16), rhs)
            b = _dot(ws_ref[2 * p + 1].astype(BF16), rhs)
            mixed_slabs.append(jnp.where(low_head, a, b))
        return mixed_slabs

    def finish_mlp(u_raw, mixed_slabs):
        u = _gelu2_tanh(u_raw)
        rows = []
        for c in range(nblk):
            mixed_c = jnp.concatenate(
                [m[:, c * LANES:(c + 1) * LANES] for m in mixed_slabs], axis=1) + bs_ref[...]
            rows.append(u[c * BLOCK:(c + 1) * BLOCK, :] * mixed_c)
        o = jnp.concatenate(rows, axis=0)
        o = o * lax.rsqrt(jnp.mean(o * o, axis=-1, keepdims=True) + 4.0 * EPS) * ogain_ref[...]
        mlp_ref[...] = o.astype(BF16)

    hn = _mod_norm(x_ref[...], gain_ref[...], mod_ref[0:1, :], mod_ref[1:2, :]).astype(BF16)
    q, kv = project(q_cols), project(kv_cols)
    q_ms = [_head_ms(q[:, h * MXU_DIM:(h + 1) * MXU_DIM], bd) for h in halves]
    k_ms = _head_ms(kv[:, :kvw], bd[:kvw, :kvw])
    g_raw = project(g_cols)
    finish_qkv(q, kv, q_ms, k_ms)
    u_raw = project(u_cols)
    finish_mlp(u_raw, gate_mix(g_raw))


def _inproj(x, mod3, norm_gain, w_in_bf, cos, sup, sdn, gains, kgain_block, w_spatial, bs_full,
            ogain, bd, later_weights, tile):
    b, s, d = x.shape
    inw = w_in_bf.shape[1]
    aw = ATTN_HEADS * HEAD_DIM
    mw = MLP_HEADS * HEAD_DIM
    n_steps = (s // tile) * b
    const = lambda shape: pl.BlockSpec(shape, lambda i, bb: (0,) * len(shape))

    def slab_spec(w):
        rows = next(r for r in range(BF16_SUBLANES, w.shape[0] + 1, BF16_SUBLANES)
                    if w.shape[0] % r == 0 and w.shape[0] // r <= n_steps)
        n_slabs = w.shape[0] // rows
        return pl.BlockSpec((rows, w.shape[1]), lambda i, bb: (jnp.minimum(i * b + bb, n_slabs - 1), 0))

    slab_specs = [slab_spec(w) for w in later_weights]
    table_spec = pl.BlockSpec((tile, LANES), lambda i, bb: (i, 0))
    return pl.pallas_call(
        _inproj_kernel,
        grid=(s // tile, b),
        in_specs=[pl.BlockSpec((None, tile, d), lambda i, bb: (bb, i, 0)),
                  pl.BlockSpec((None, N_MOD, d), lambda i, bb: (bb, 0, 0)),
                  const((1, d)),
                  const((d, inw)),
                  table_spec, table_spec, table_spec,
                  const((1, aw)), pl.BlockSpec((1, LANES), lambda i, bb: (0, kgain_block)),
                  pl.BlockSpec((1, mw), lambda i, bb: (0, aw // mw)),
                  const((MLP_HEADS, BLOCK, BLOCK)), const((BLOCK, mw)), const((1, mw)),
                  const((MXU_DIM, MXU_DIM))] + slab_specs,
        out_specs=[pl.BlockSpec((None, tile, LANES), lambda i, bb: (bb, i, 0)),
                   pl.BlockSpec((None, LANES, tile), lambda i, bb: (bb, 0, i)),
                   pl.BlockSpec((None, ATTN_KV_HEADS, HEAD_DIM, ATTN_GROUP * tile),
                                lambda i, bb: (bb, 0, 0, i)),
                   pl.BlockSpec((None, tile, mw), lambda i, bb: (bb, i, 0))] + slab_specs,
        out_shape=[jax.ShapeDtypeStruct((b, s, LANES), BF16),
                   jax.ShapeDtypeStruct((b, LANES, s), BF16),
                   jax.ShapeDtypeStruct((b, ATTN_KV_HEADS, HEAD_DIM, ATTN_GROUP * s), BF16),
                   jax.ShapeDtypeStruct((b, s, mw), BF16)]
        + [jax.ShapeDtypeStruct(w.shape, BF16) for w in later_weights],
        compiler_params=pltpu.CompilerParams(
            dimension_semantics=("arbitrary", "arbitrary"), vmem_limit_bytes=VMEM_LIMIT),
        name="inproj",
    )(x, mod3, norm_gain, w_in_bf, cos, sup, sdn, gains, gains, gains, w_spatial, bs_full, ogain, bd,
      *later_weights)


def _mix_ffn_kernel(tiles_per_seq, bounded,
                    x_ref, mod_ref, kp_ref, km_ref, kn_ref, vp_ref, vm_ref, vn_ref,
                    kc_ref, vct_ref, qt_ref, mlp_ref, sink_ref, bias_ref, again_ref, wo_ref,
                    fmod_ref, fgain_ref, wgu_ref, wd_ref,
                    o_ref, h_ref, hn_ref, hid_ref):
    tq = x_ref.shape[0]
    nblk = tq // BLOCK
    step_id = pl.program_id(0)
    n_tiles = pl.num_programs(0) - 1
    i = lax.rem(jnp.minimum(step_id, n_tiles - 1), tiles_per_seq)
    last = tiles_per_seq - 1
    gq = ATTN_GROUP * BLOCK
    slot_w = lax.rem(step_id, 2)
    slot_r = 1 - slot_w

    @pl.when(step_id == 0)
    def _():
        h_ref[1] = jnp.zeros(h_ref.shape[1:], F32)
        hn_ref[...] = jnp.zeros(hn_ref.shape, BF16)

    ff = wd_ref.shape[0]
    n_chunks = ff // MXU_DIM
    def ffn_matmuls(c):
        hn = hn_ref[...]
        return (_dot(hn, wgu_ref[:, c * MXU_DIM:(c + 1) * MXU_DIM]),
                _dot(hn, wgu_ref[:, ff + c * MXU_DIM:ff + (c + 1) * MXU_DIM]))

    def ffn_activation(c, gate_up):
        a, b = gate_up
        hid_ref[:, c * MXU_DIM:(c + 1) * MXU_DIM] = (_silu(a) * b).astype(BF16)

    k_ext = jnp.concatenate([kp_ref[...], km_ref[...], kn_ref[...]], axis=0)
    vt_ext = jnp.concatenate([vp_ref[...], vm_ref[...], vn_ref[...]], axis=1)
    kc = kc_ref[...]
    vct = vct_ref[...]
    cw = HEADS_PER_STEP * BLOCK
    zeros_q = jnp.zeros((HEAD_DIM, cw), BF16)
    ones_rows = jnp.ones((BF16_SUBLANES, 3 * BLOCK + kc.shape[0]), BF16)
    bias_prev = bias_ref[0:BLOCK, :cw]
    bias_next = bias_ref[BLOCK:2 * BLOCK, :cw]

    def sink_row(kvh, part):
        first = kvh * ATTN_GROUP + part * HEADS_PER_STEP
        return jnp.concatenate([jnp.full((1, BLOCK), sink_ref[first + g] * LOG2E, F32)
                                for g in range(HEADS_PER_STEP)], axis=1)

    def score_matmul(jb, kvh, part):
        qt = qt_ref[kvh, :, jb * gq + part * cw:jb * gq + (part + 1) * cw]
        rhs = jnp.concatenate([qt, zeros_q] if kvh == 0 else [zeros_q, qt], axis=0)
        return _dot(kc, rhs), _dot(k_ext[jb * BLOCK:(jb + 3) * BLOCK, :], rhs)

    def mask_and_max(jb, kvh, part, s):
        pen_prev = jnp.where(i == 0, MASKED, 0.0) if jb == 0 else 0.0
        pen_next = jnp.where(i == last, MASKED, 0.0) if jb == nblk - 1 else 0.0
        s_ctx, s = s
        parts = [s[0:BLOCK] + (bias_prev + pen_prev), s[BLOCK:2 * BLOCK],
                 s[2 * BLOCK:3 * BLOCK] + (bias_next + pen_next), s_ctx]
        if bounded:
            return jnp.concatenate([jnp.exp2(p_).astype(BF16) for p_ in parts], axis=0), None
        m = jnp.maximum(
            jnp.maximum(jnp.max(parts[0], axis=0, keepdims=True), jnp.max(parts[1], axis=0, keepdims=True)),
            jnp.maximum(jnp.max(parts[2], axis=0, keepdims=True), jnp.max(parts[3], axis=0, keepdims=True)))
        return parts, jnp.maximum(m, sink_row(kvh, part))

    def exp_weights(parts, m):
        return jnp.concatenate([jnp.exp2(p_ - m).astype(BF16) for p_ in parts], axis=0)

    def value_matmul(jb, kvh, part, p):
        v_all = jnp.concatenate(
            [vt_ext[kvh * HEAD_DIM:(kvh + 1) * HEAD_DIM, jb * BLOCK:(jb + 3) * BLOCK],
             vct[kvh * HEAD_DIM:(kvh + 1) * HEAD_DIM, :]], axis=1)
        return _dot(jnp.concatenate([v_all, ones_rows], axis=0), p)

    def normalize(jb, kvh, part, pv, m):
        sink = sink_row(kvh, part)
        denom = pv[HEAD_DIM:HEAD_DIM + 1, :] + jnp.exp2(sink if bounded else sink - m)
        o_t = pv[:HEAD_DIM, :] * (1.0 / denom)
        return [o_t[:, g * BLOCK:(g + 1) * BLOCK] for g in range(HEADS_PER_STEP)]

    def block_norm(out_t):
        o_all = jnp.concatenate(out_t, axis=0)
        ms = jnp.mean(o_all * o_all, axis=0, keepdims=True)
        y = o_all * lax.rsqrt(ms + EPS)
        return (y.T * again_ref[...]).astype(BF16)

    aw = ATTN_HEADS * HEAD_DIM
    steps = [(jb, kvh, part) for jb in range(nblk) for kvh in range(ATTN_KV_HEADS)
             for part in range(ATTN_GROUP // HEADS_PER_STEP)]
    assert n_chunks >= len(steps)
    attn_rows, out_t, unnormalized, gate_up = [], [], None, {}

    def collect(step, pv, m):
        out_t.extend(normalize(*step, pv, m))
        if len(out_t) == ATTN_HEADS:
            attn_rows.append(block_norm(out_t))
            out_t.clear()

    if bounded:
        proj_mlp = None
        for n, step in enumerate(steps):
            scores = score_matmul(*step)
            gate_up[n] = ffn_matmuls(n)
            weights, _ = mask_and_max(*step, scores)
            if unnormalized is not None:
                collect(*unnormalized)
                ffn_activation(n - 1, gate_up.pop(n - 1))
            unnormalized = (step, value_matmul(*step, weights), None)
    else:
        raw = {n: score_matmul(*steps[n]) for n in range(2)}
        proj_mlp = _dot(mlp_ref[...], wo_ref[aw:, :])
        ready = {0: mask_and_max(*steps[0], raw.pop(0))}
        for n, step in enumerate(steps):
            if n + 1 < len(steps):
                ready[n + 1] = mask_and_max(*steps[n + 1], raw.pop(n + 1))
            if n + 2 < len(steps):
                raw[n + 2] = score_matmul(*steps[n + 2])
            gate_up[n] = ffn_matmuls(n)
            if unnormalized is not None:
                collect(*unnormalized)
                ffn_activation(n - 1, gate_up.pop(n - 1))
            parts, m = ready.pop(n)
            unnormalized = (step, value_matmul(*step, exp_weights(parts, m)), m)

    c_next = len(steps)
    gate_up[c_next] = ffn_matmuls(c_next)
    collect(*unnormalized)
    ffn_activation(c_next - 1, gate_up.pop(c_next - 1))
    for c in range(c_next + 1, n_chunks):
        gate_up[c] = ffn_matmuls(c)
        ffn_activation(c - 1, gate_up.pop(c - 1))
    if proj_mlp is None:
        proj_mlp = _dot(mlp_ref[...], wo_ref[aw:, :])
    proj_attn = _dot(jnp.concatenate(attn_rows, axis=0), wo_ref[:aw, :])
    ffn_activation(n_chunks - 1, gate_up.pop(n_chunks - 1))
    h_new = x_ref[...] + mod_ref[2:3, :] * (proj_attn + proj_mlp)
    h_ref[slot_w] = h_new
    hn_ref[...] = _mod_norm(h_new, fgain_ref[...], mod_ref[3:4, :], mod_ref[4:5, :]).astype(BF16)
    o_ref[...] = h_ref[slot_r] + fmod_ref[5:6, :] * _dot(hid_ref[...], wd_ref[...])


def _mix_ffn(x, mod3, k, vt, kc, vct, qt, mlpn, sinks, bias, again_row, wo_bf,
             ffn_gain, wgu_bf, wd_bf, tile, bounded):
    b, s, d = x.shape
    c = kc.shape[1]
    aw = ATTN_HEADS * HEAD_DIM
    mw = mlpn.shape[2]
    ff = wd_bf.shape[0]
    assert ff % MXU_DIM == 0
    r = tile // BLOCK
    nb = s // BLOCK
    nt = s // tile
    n_tiles = b * nt
    gq = ATTN_GROUP * BLOCK

    def mix_tile(g):
        t = jnp.minimum(g, n_tiles - 1)
        return t // nt, t % nt

    def ffn_tile(g):
        t = jnp.maximum(g - 1, 0)
        return t // nt, t % nt

    def at_mix(fn):
        return lambda g: fn(*mix_tile(g))

    const = lambda shape: pl.BlockSpec(shape, lambda g: (0,) * len(shape))
    return pl.pallas_call(
        functools.partial(_mix_ffn_kernel, nt, bounded),
        grid=(n_tiles + 1,),
        in_specs=[pl.BlockSpec((None, tile, d), at_mix(lambda bb, i: (bb, i, 0))),
                  pl.BlockSpec((None, N_MOD, d), at_mix(lambda bb, i: (bb, 0, 0))),
                  pl.BlockSpec((None, BLOCK, LANES), at_mix(lambda bb, i: (bb, jnp.maximum(i * r - 1, 0), 0))),
                  pl.BlockSpec((None, tile, LANES), at_mix(lambda bb, i: (bb, i, 0))),
                  pl.BlockSpec((None, BLOCK, LANES),
                               at_mix(lambda bb, i: (bb, jnp.minimum((i + 1) * r, nb - 1), 0))),
                  pl.BlockSpec((None, LANES, BLOCK), at_mix(lambda bb, i: (bb, 0, jnp.maximum(i * r - 1, 0)))),
                  pl.BlockSpec((None, LANES, tile), at_mix(lambda bb, i: (bb, 0, i))),
                  pl.BlockSpec((None, LANES, BLOCK),
                               at_mix(lambda bb, i: (bb, 0, jnp.minimum((i + 1) * r, nb - 1)))),
                  pl.BlockSpec((None, c, LANES), at_mix(lambda bb, i: (bb, 0, 0))),
                  pl.BlockSpec((None, LANES, c), at_mix(lambda bb, i: (bb, 0, 0))),
                  pl.BlockSpec((None, ATTN_KV_HEADS, HEAD_DIM, ATTN_GROUP * tile),
                               at_mix(lambda bb, i: (bb, 0, 0, i))),
                  pl.BlockSpec((None, tile, mw), at_mix(lambda bb, i: (bb, i, 0))),
                  pl.BlockSpec(memory_space=pltpu.SMEM),
                  const((2 * BLOCK, gq)),
                  const((1, aw)),
                  const((aw + mw, d)),
                  pl.BlockSpec((None, N_MOD, d), lambda g: (ffn_tile(g)[0], 0, 0)),
                  const((1, d)), const((d, 2 * ff)), const((ff, d))],
        out_specs=pl.BlockSpec((None, tile, d), lambda g: (*ffn_tile(g), 0)),
        out_shape=jax.ShapeDtypeStruct((b, s, d), F32),
        scratch_shapes=[pltpu.VMEM((2, tile, d), F32), pltpu.VMEM((tile, d), BF16),
                        pltpu.VMEM((tile, ff), BF16)],
        compiler_params=pltpu.CompilerParams(
            dimension_semantics=("arbitrary",), vmem_limit_bytes=VMEM_LIMIT),
        name="mix_ffn",
    )(x, mod3, k, k, k, vt, vt, vt, kc, vct, qt, mlpn, sinks, bias, again_row, wo_bf,
      mod3, ffn_gain, wgu_bf, wd_bf)


def _rope_tables(s):
    axis_dim = HEAD_DIM // 2
    pos = np.arange(s)
    inv_freq = (ROPE_THETA ** (-np.arange(0, axis_dim, 2, dtype=np.float32) / axis_dim)).astype(np.float32)
    ang_r = (pos // GRID_W).astype(np.float32)[:, None] * inv_freq[None, :]
    ang_c = (pos % GRID_W).astype(np.float32)[:, None] * inv_freq[None, :]
    cr, sr, cc, sc = np.cos(ang_r), np.sin(ang_r), np.cos(ang_c), np.sin(ang_c)
    z = np.zeros_like(sr)
    reps = LANES // HEAD_DIM
    cos = np.tile(np.concatenate([cr, cr, cc, cc], axis=1), (1, reps))
    sin_up = np.tile(np.concatenate([-sr, z, -sc, z], axis=1), (1, reps))
    sin_dn = np.tile(np.concatenate([z, sr, z, sc], axis=1), (1, reps))
    return jnp.asarray(cos, F32), jnp.asarray(sin_up, F32), jnp.asarray(sin_dn, F32)


def _window_bias():
    c = np.arange(BLOCK)[:, None]
    r = np.arange(BLOCK)[None, :]
    prev = np.where(c >= r, 0.0, MASKED).astype(np.float32)
    nxt = np.where(c <= r, 0.0, MASKED).astype(np.float32)
    return jnp.asarray(np.tile(np.concatenate([prev, nxt], axis=0), (1, ATTN_GROUP)), F32)


def _head_mean_matrix():
    bd = np.kron(np.eye(MXU_DIM // HEAD_DIM, dtype=np.float32),
                 np.full((HEAD_DIM, HEAD_DIM), 1.0 / HEAD_DIM, np.float32))
    return jnp.asarray(bd, BF16)


def kernel(x, c, ctx, c_ctx, w_mod, b_mod, norm_mix, norm_ffn, w_in, q_gain, k_gain, attn_sink,
           gate_gain, w_spatial, b_spatial, attn_out_gain, mlp_out_gain, w_out, w_gate_up, w_down):
    b, s, d = x.shape
    assert w_mod.shape[0] == 1, "single-layer problem"
    assert s % 1024 == 0 and d % LANES == 0
    aw = ATTN_HEADS * HEAD_DIM
    mw = MLP_HEADS * HEAD_DIM

    rows = -(-(b + 1) // BF16_SUBLANES) * BF16_SUBLANES
    cond = jnp.concatenate([c, c_ctx[None, :], jnp.zeros((rows - b - 1, d), F32)], axis=0)
    mod, w_in_bf = _adaln(cond, w_mod[0], b_mod[0][None, :], w_in[0])
    mod3 = mod.reshape(rows, N_MOD, d)
    bd = _head_mean_matrix()
    assert aw == mw and (aw + mw) % LANES == 0
    gains = jnp.concatenate([jnp.tile(q_gain[0], ATTN_HEADS) * (HEAD_DIM ** -0.5 * LOG2E),
                             gate_gain[0].reshape(mw),
                             jnp.tile(k_gain[0], ATTN_KV_HEADS)])[None, :]
    kgain_block = (aw + mw) // LANES
    norm_mix_g = norm_mix[0][None, :]

    kc, vct = _ctx_kv(ctx, mod3, b, norm_mix_g, w_in_bf, gains, kgain_block, bd)

    cos, sup, sdn = _rope_tables(s)
    bs_full = jnp.repeat(b_spatial[0].T, HEAD_DIM, axis=1)
    k, vt, qt, mlpn, wo_bf, wgu_bf, wd_bf = _inproj(
        x, mod3, norm_mix_g, w_in_bf, cos, sup, sdn, gains, kgain_block, w_spatial[0], bs_full,
        mlp_out_gain[0][None, :], bd, (w_out[0], w_gate_up[0], w_down[0]), tile=1024)

    again_row = attn_out_gain[0][None, :]
    sinks_padded = jnp.pad(attn_sink[0], (0, HEAD_DIM - attn_sink.shape[1]))
    q_max, k_max, sink_max = jnp.max(jnp.abs(jnp.stack([q_gain[0], k_gain[0], sinks_padded])), axis=1)
    logit_bound = BOUND_MARGIN * HEAD_DIM ** 0.5 * LOG2E * q_max * k_max
    sink_max = sink_max * LOG2E
    use_bound = (logit_bound <= MAX_SAFE_LOGIT) & (sink_max <= MAX_SAFE_LOGIT)
    operands = (x, mod3, k, vt, kc, vct, qt, mlpn, attn_sink[0], _window_bias(), again_row,
                wo_bf, norm_ffn[0][None, :], wgu_bf, wd_bf)
    return lax.cond(use_bound,
                    lambda ops: _mix_ffn(*ops, tile=512, bounded=True),
                    lambda ops: _mix_ffn(*ops, tile=512, bounded=False),
                    operands)
```

```python
import functools
import math

import jax
import jax.numpy as jnp
import numpy as np
from jax import lax
from jax.experimental import pallas as pl
from jax.experimental.pallas import tpu as pltpu

F32 = jnp.float32
BF16 = jnp.bfloat16

HEAD_DIM = 64
ATTN_HEADS = 8
ATTN_KV_HEADS = 2
ATTN_GROUP = ATTN_HEADS // ATTN_KV_HEADS
MLP_HEADS = 8
N_MOD = 6
BLOCK = 128
GRID_W = 64
ROPE_THETA = 10000.0
EPS = 1e-6
MASKED = -1e30
BOUND_MARGIN = 1.05
MAX_SAFE_LOGIT = 40.0
LOG2E = math.log2(math.e)

LANES = 128
BF16_SUBLANES = 16
MXU_DIM = 256
VMEM_LIMIT = 56 * 1024 * 1024
HEADS_PER_STEP = 4


def _dot(a, b):
    return jnp.dot(a, b, preferred_element_type=F32)


def _silu(x):
    return x * (1.0 / (1.0 + jnp.exp(-x)))


def _gelu2_tanh(x):
    c = math.sqrt(2.0 / math.pi)
    return x * (1.0 + jnp.tanh(x * (c + (c * 0.044715) * (x * x))))


def _mod_norm(x, gain, shift, scale):
    y = x * lax.rsqrt(jnp.mean(x * x, axis=-1, keepdims=True) + EPS)
    return y * (gain * (1.0 + scale)) + shift


def _head_ms(x, bd):
    return _dot((x * x).astype(BF16), bd)


def _rope(x, cos, sin_up, sin_dn):
    up = pltpu.roll(x, LANES - 16, 1)
    dn = pltpu.roll(x, 16, 1)
    return x * cos + up * sin_up + dn * sin_dn


def _split_bf16(x):
    hi = x.astype(BF16)
    return hi, (x - hi.astype(F32)).astype(BF16)


def _adaln_kernel(cond_ref, wa_ref, wb_ref, b_ref, win_ref, o_ref, win_bf_ref):
    win_bf_ref[...] = win_ref[...].astype(BF16)
    rows = cond_ref.shape[0]
    half = wa_ref.shape[1]

    @pl.when(pl.program_id(0) == 0)
    def _():
        o_ref[...] = jnp.broadcast_to(b_ref[...], o_ref.shape)

    s_hi, s_lo = _split_bf16(_silu(cond_ref[...]))
    s_both = jnp.concatenate([s_hi, s_lo], axis=0)
    for part, w_ref in enumerate((wa_ref, wb_ref)):
        w_hi, w_lo = _split_bf16(w_ref[...])
        both = _dot(s_both, w_hi)
        o_ref[:, part * half:(part + 1) * half] += both[:rows] + both[rows:] + _dot(s_hi, w_lo)


def _adaln(cond, w_mod, b_mod, w_in):
    rows, d = cond.shape
    n = w_mod.shape[1]
    steps = 4
    tk = d // steps
    win_rows = w_in.shape[0] // steps
    assert rows % BF16_SUBLANES == 0 and tk % LANES == 0 and (n // 2) % LANES == 0
    assert win_rows % BF16_SUBLANES == 0
    win_spec = pl.BlockSpec((win_rows, w_in.shape[1]), lambda j: (j, 0))
    return pl.pallas_call(
        _adaln_kernel,
        grid=(steps,),
        in_specs=[pl.BlockSpec((rows, tk), lambda j: (0, j)),
                  pl.BlockSpec((tk, n // 2), lambda j: (j, 0)),
                  pl.BlockSpec((tk, n // 2), lambda j: (j, 1)),
                  pl.BlockSpec((1, n), lambda j: (0, 0)),
                  win_spec],
        out_specs=[pl.BlockSpec((rows, n), lambda j: (0, 0)), win_spec],
        out_shape=[jax.ShapeDtypeStruct((rows, n), F32), jax.ShapeDtypeStruct(w_in.shape, BF16)],
        compiler_params=pltpu.CompilerParams(
            dimension_semantics=("arbitrary",), vmem_limit_bytes=VMEM_LIMIT),
        name="adaln",
    )(cond, w_mod, w_mod, b_mod, w_in)


def _ctx_kernel(x_ref, mod_ref, gain_ref, w_ref, kgain_ref, bd_ref, kc_ref, vct_ref):
    per_step, c, d = x_ref.shape
    x = x_ref[...].reshape(per_step * c, d)
    hn = _mod_norm(x, gain_ref[...], mod_ref[0:1, :], mod_ref[1:2, :])
    kv = _dot(hn.astype(BF16), w_ref[...])
    k = kv[:, :LANES]
    k = k * lax.rsqrt(_head_ms(k, bd_ref[:LANES, :LANES]) + EPS) * kgain_ref[...]
    for j in range(per_step):
        kc_ref[j] = k[j * c:(j + 1) * c, :].astype(BF16)
        vct_ref[j] = kv[j * c:(j + 1) * c, LANES:].T.astype(BF16)


def _ctx_kv(ctx, mod3, ctx_row, norm_gain, w_in_bf, gains, kgain_block, bd):
    b, c, d = ctx.shape
    kvw = 2 * ATTN_KV_HEADS * HEAD_DIM
    per_step = 2 if b % 2 == 0 else 1
    return pl.pallas_call(
        _ctx_kernel,
        grid=(b // per_step,),
        in_specs=[pl.BlockSpec((per_step, c, d), lambda i: (i, 0, 0)),
                  pl.BlockSpec((None, N_MOD, d), lambda i: (ctx_row, 0, 0)),
                  pl.BlockSpec((1, d), lambda i: (0, 0)),
                  pl.BlockSpec((d, kvw), lambda i: (0, 0)),
                  pl.BlockSpec((1, LANES), lambda i: (0, kgain_block)),
                  pl.BlockSpec((MXU_DIM, MXU_DIM), lambda i: (0, 0))],
        out_specs=[pl.BlockSpec((per_step, c, LANES), lambda i: (i, 0, 0)),
                   pl.BlockSpec((per_step, LANES, c), lambda i: (i, 0, 0))],
        out_shape=[jax.ShapeDtypeStruct((b, c, LANES), BF16),
                   jax.ShapeDtypeStruct((b, LANES, c), BF16)],
        compiler_params=pltpu.CompilerParams(
            dimension_semantics=("arbitrary",), vmem_limit_bytes=VMEM_LIMIT),
        name="ctx_kv",
    )(ctx, mod3, norm_gain, w_in_bf, gains, bd)


def _inproj_kernel(x_ref, mod_ref, gain_ref, w_ref, cos_ref, sup_ref, sdn_ref,
                   qgain_ref, kgain_ref, ggain_ref, ws_ref, bs_ref, ogain_ref, bd_ref,
                   wo_f32_ref, wgu_f32_ref, wd_f32_ref,
                   k_ref, vt_ref, qt_ref, mlp_ref, wo_bf_ref, wgu_bf_ref, wd_bf_ref):
    wo_bf_ref[...] = wo_f32_ref[...].astype(BF16)
    wgu_bf_ref[...] = wgu_f32_ref[...].astype(BF16)
    wd_bf_ref[...] = wd_f32_ref[...].astype(BF16)

    t = x_ref.shape[0]
    nblk = t // BLOCK
    kvw = ATTN_KV_HEADS * HEAD_DIM
    aw = ATTN_HEADS * HEAD_DIM
    mw = MLP_HEADS * HEAD_DIM
    q_cols = slice(2 * kvw, 2 * kvw + aw)
    kv_cols = slice(0, 2 * kvw)
    u_cols = slice(2 * kvw + aw, 2 * kvw + aw + mw)
    g_cols = slice(2 * kvw + aw + mw, 2 * kvw + aw + 2 * mw)
    halves = range(aw // MXU_DIM)
    bd = bd_ref[...]

    def project(cols):
        return _dot(hn, w_ref[:, cols])

    def finish_qkv(q, kv, q_ms, k_ms):
        cos, sup, sdn = cos_ref[...], sup_ref[...], sdn_ref[...]
        k = kv[:, :kvw] * lax.rsqrt(k_ms + EPS) * kgain_ref[...]
        k_ref[...] = _rope(k, cos, sup, sdn).astype(BF16)
        vt_ref[...] = kv[:, kvw:].T.astype(BF16)
        for half in halves:
            qh = q[:, half * MXU_DIM:(half + 1) * MXU_DIM] * lax.rsqrt(q_ms[half] + EPS)
            qh = qh * qgain_ref[:, half * MXU_DIM:(half + 1) * MXU_DIM]
            for sl in range(MXU_DIM // LANES):
                qs = _rope(qh[:, sl * LANES:(sl + 1) * LANES], cos, sup, sdn)
                qst = qs.T
                for hh in range(LANES // HEAD_DIM):
                    head = (half * MXU_DIM + sl * LANES) // HEAD_DIM + hh
                    kvh, grp = head // ATTN_GROUP, head % ATTN_GROUP
                    for jb in range(nblk):
                        col = (jb * ATTN_GROUP + grp) * BLOCK
                        qt_ref[kvh, :, col:col + BLOCK] = qst[
                            hh * HEAD_DIM:(hh + 1) * HEAD_DIM, jb * BLOCK:(jb + 1) * BLOCK].astype(BF16)

    def gate_mix(g_raw):
        g = _gelu2_tanh(g_raw)
        g_ms = [_head_ms(g[:, h * MXU_DIM:(h + 1) * MXU_DIM], bd) for h in halves]
        gn = jnp.concatenate(
            [g[:, h * MXU_DIM:(h + 1) * MXU_DIM] * lax.rsqrt(g_ms[h] + 4.0 * EPS) for h in halves],
            axis=1) * ggain_ref[...]
        gnb = gn.astype(BF16)
        low_head = lax.broadcasted_iota(jnp.int32, (BLOCK, t), 1) % LANES < HEAD_DIM
        mixed_slabs = []
        for p in range(MLP_HEADS // 2):
            rhs = jnp.concatenate(
                [gnb[c * BLOCK:(c + 1) * BLOCK, p * LANES:(p + 1) * LANES] for c in range(nblk)], axis=1)
            a = _dot(ws_ref[2 * p].astype(BF16), rhs)
            b = _dot(ws_ref[2 * p + 1].astype(BF16), rhs)
            mixed_slabs.append(jnp.where(low_head, a, b))
        return mixed_slabs

    def finish_mlp(u_raw, mixed_slabs):
        u = _gelu2_tanh(u_raw)
        rows = []
        for c in range(nblk):
            mixed_c = jnp.concatenate(
                [m[:, c * LANES:(c + 1) * LANES] for m in mixed_slabs], axis=1) + bs_ref[...]
            rows.append(u[c * BLOCK:(c + 1) * BLOCK, :] * mixed_c)
        o = jnp.concatenate(rows, axis=0)
        o = o * lax.rsqrt(jnp.mean(o * o, axis=-1, keepdims=True) + 4.0 * EPS) * ogain_ref[...]
        mlp_ref[...] = o.astype(BF16)

    hn = _mod_norm(x_ref[...], gain_ref[...], mod_ref[0:1, :], mod_ref[1:2, :]).astype(BF16)
    q, kv = project(q_cols), project(kv_cols)
    q_ms = [_head_ms(q[:, h * MXU_DIM:(h + 1) * MXU_DIM], bd) for h in halves]
    k_ms = _head_ms(kv[:, :kvw], bd[:kvw, :kvw])
    g_raw = project(g_cols)
    finish_qkv(q, kv, q_ms, k_ms)
    u_raw = project(u_cols)
    finish_mlp(u_raw, gate_mix(g_raw))


def _inproj(x, mod3, norm_gain, w_in_bf, cos, sup, sdn, gains, kgain_block, w_spatial, bs_full,
            ogain, bd, later_weights, tile):
    b, s, d = x.shape
    inw = w_in_bf.shape[1]
    aw = ATTN_HEADS * HEAD_DIM
    mw = MLP_HEADS * HEAD_DIM
    n_steps = (s // tile) * b
    const = lambda shape: pl.BlockSpec(shape, lambda i, bb: (0,) * len(shape))

    def slab_spec(w):
        rows = next(r for r in range(BF16_SUBLANES, w.shape[0] + 1, BF16_SUBLANES)
                    if w.shape[0] % r == 0 and w.shape[0] // r <= n_steps)
        n_slabs = w.shape[0] // rows
        return pl.BlockSpec((rows, w.shape[1]), lambda i, bb: (jnp.minimum(i * b + bb, n_slabs - 1), 0))

    slab_specs = [slab_spec(w) for w in later_weights]
    table_spec = pl.BlockSpec((tile, LANES), lambda i, bb: (i, 0))
    return pl.pallas_call(
        _inproj_kernel,
        grid=(s // tile, b),
        in_specs=[pl.BlockSpec((None, tile, d), lambda i, bb: (bb, i, 0)),
                  pl.BlockSpec((None, N_MOD, d), lambda i, bb: (bb, 0, 0)),
                  const((1, d)),
                  const((d, inw)),
                  table_spec, table_spec, table_spec,
                  const((1, aw)), pl.BlockSpec((1, LANES), lambda i, bb: (0, kgain_block)),
                  pl.BlockSpec((1, mw), lambda i, bb: (0, aw // mw)),
                  const((MLP_HEADS, BLOCK, BLOCK)), const((BLOCK, mw)), const((1, mw)),
                  const((MXU_DIM, MXU_DIM))] + slab_specs,
        out_specs=[pl.BlockSpec((None, tile, LANES), lambda i, bb: (bb, i, 0)),
                   pl.BlockSpec((None, LANES, tile), lambda i, bb: (bb, 0, i)),
                   pl.BlockSpec((None, ATTN_KV_HEADS, HEAD_DIM, ATTN_GROUP * tile),
                                lambda i, bb: (bb, 0, 0, i)),
                   pl.BlockSpec((None, tile, mw), lambda i, bb: (bb, i, 0))] + slab_specs,
        out_shape=[jax.ShapeDtypeStruct((b, s, LANES), BF16),
                   jax.ShapeDtypeStruct((b, LANES, s), BF16),
                   jax.ShapeDtypeStruct((b, ATTN_KV_HEADS, HEAD_DIM, ATTN_GROUP * s), BF16),
                   jax.ShapeDtypeStruct((b, s, mw), BF16)]
        + [jax.ShapeDtypeStruct(w.shape, BF16) for w in later_weights],
        compiler_params=pltpu.CompilerParams(
            dimension_semantics=("arbitrary", "arbitrary"), vmem_limit_bytes=VMEM_LIMIT),
        name="inproj",
    )(x, mod3, norm_gain, w_in_bf, cos, sup, sdn, gains, gains, gains, w_spatial, bs_full, ogain, bd,
      *later_weights)


def _mix_ffn_kernel(tiles_per_seq, bounded,
                    x_ref, mod_ref, kp_ref, km_ref, kn_ref, vp_ref, vm_ref, vn_ref,
                    kc_ref, vct_ref, qt_ref, mlp_ref, sink_ref, bias_ref, again_ref, wo_ref,
                    fmod_ref, fgain_ref, wgu_ref, wd_ref,
                    o_ref, h_ref, hn_ref, hid_ref):
    tq = x_ref.shape[0]
    nblk = tq // BLOCK
    step_id = pl.program_id(0)
    n_tiles = pl.num_programs(0) - 1
    i = lax.rem(jnp.minimum(step_id, n_tiles - 1), tiles_per_seq)
    last = tiles_per_seq - 1
    gq = ATTN_GROUP * BLOCK
    slot_w = lax.rem(step_id, 2)
    slot_r = 1 - slot_w

    @pl.when(step_id == 0)
    def _():
        h_ref[1] = jnp.zeros(h_ref.shape[1:], F32)
        hn_ref[...] = jnp.zeros(hn_ref.shape, BF16)

    ff = wd_ref.shape[0]
    n_chunks = ff // MXU_DIM
    def ffn_matmuls(c):
        hn = hn_ref[...]
        return (_dot(hn, wgu_ref[:, c * MXU_DIM:(c + 1) * MXU_DIM]),
                _dot(hn, wgu_ref[:, ff + c * MXU_DIM:ff + (c + 1) * MXU_DIM]))

    def ffn_activation(c, gate_up):
        a, b = gate_up
        hid_ref[:, c * MXU_DIM:(c + 1) * MXU_DIM] = (_silu(a) * b).astype(BF16)

    k_ext = jnp.concatenate([kp_ref[...], km_ref[...], kn_ref[...]], axis=0)
    vt_ext = jnp.concatenate([vp_ref[...], vm_ref[...], vn_ref[...]], axis=1)
    kc = kc_ref[...]
    vct = vct_ref[...]
    cw = HEADS_PER_STEP * BLOCK
    zeros_q = jnp.zeros((HEAD_DIM, cw), BF16)
    ones_rows = jnp.ones((BF16_SUBLANES, 3 * BLOCK + kc.shape[0]), BF16)
    bias_prev = bias_ref[0:BLOCK, :cw]
    bias_next = bias_ref[BLOCK:2 * BLOCK, :cw]

    def sink_row(kvh, part):
        first = kvh * ATTN_GROUP + part * HEADS_PER_STEP
        return jnp.concatenate([jnp.full((1, BLOCK), sink_ref[first + g] * LOG2E, F32)
                                for g in range(HEADS_PER_STEP)], axis=1)

    def score_matmul(jb, kvh, part):
        qt = qt_ref[kvh, :, jb * gq + part * cw:jb * gq + (part + 1) * cw]
        rhs = jnp.concatenate([qt, zeros_q] if kvh == 0 else [zeros_q, qt], axis=0)
        return _dot(kc, rhs), _dot(k_ext[jb * BLOCK:(jb + 3) * BLOCK, :], rhs)

    def mask_and_max(jb, kvh, part, s):
        pen_prev = jnp.where(i == 0, MASKED, 0.0) if jb == 0 else 0.0
        pen_next = jnp.where(i == last, MASKED, 0.0) if jb == nblk - 1 else 0.0
        s_ctx, s = s
        parts = [s[0:BLOCK] + (bias_prev + pen_prev), s[BLOCK:2 * BLOCK],
                 s[2 * BLOCK:3 * BLOCK] + (bias_next + pen_next), s_ctx]
        if bounded:
            return jnp.concatenate([jnp.exp2(p_).astype(BF16) for p_ in parts], axis=0), None
        m = jnp.maximum(
            jnp.maximum(jnp.max(parts[0], axis=0, keepdims=True), jnp.max(parts[1], axis=0, keepdims=True)),
            jnp.maximum(jnp.max(parts[2], axis=0, keepdims=True), jnp.max(parts[3], axis=0, keepdims=True)))
        return parts, jnp.maximum(m, sink_row(kvh, part))

    def exp_weights(parts, m):
        return jnp.concatenate([jnp.exp2(p_ - m).astype(BF16) for p_ in parts], axis=0)

    def value_matmul(jb, kvh, part, p):
        v_all = jnp.concatenate(
            [vt_ext[kvh * HEAD_DIM:(kvh + 1) * HEAD_DIM, jb * BLOCK:(jb + 3) * BLOCK],
             vct[kvh * HEAD_DIM:(kvh + 1) * HEAD_DIM, :]], axis=1)
        return _dot(jnp.concatenate([v_all, ones_rows], axis=0), p)

    def normalize(jb, kvh, part, pv, m):
        sink = sink_row(kvh, part)
        denom = pv[HEAD_DIM:HEAD_DIM + 1, :] + jnp.exp2(sink if bounded else sink - m)
        o_t = pv[:HEAD_DIM, :] * (1.0 / denom)
        return [o_t[:, g * BLOCK:(g + 1) * BLOCK] for g in range(HEADS_PER_STEP)]

    def block_norm(out_t):
        o_all = jnp.concatenate(out_t, axis=0)
        ms = jnp.mean(o_all * o_all, axis=0, keepdims=True)
        y = o_all * lax.rsqrt(ms + EPS)
        return (y.T * again_ref[...]).astype(BF16)

    aw = ATTN_HEADS * HEAD_DIM
    steps = [(jb, kvh, part) for jb in range(nblk) for kvh in range(ATTN_KV_HEADS)
             for part in range(ATTN_GROUP // HEADS_PER_STEP)]
    assert n_chunks >= len(steps)
    attn_rows, out_t, unnormalized, gate_up = [], [], None, {}

    def collect(step, pv, m):
        out_t.extend(normalize(*step, pv, m))
        if len(out_t) == ATTN_HEADS:
            attn_rows.append(block_norm(out_t))
            out_t.clear()

    if bounded:
        proj_mlp = None
        for n, step in enumerate(steps):
            scores = score_matmul(*step)
            gate_up[n] = ffn_matmuls(n)
            weights, _ = mask_and_max(*step, scores)
            if unnormalized is not None:
                collect(*unnormalized)
                ffn_activation(n - 1, gate_up.pop(n - 1))
            unnormalized = (step, value_matmul(*step, weights), None)
    else:
        raw = {n: score_matmul(*steps[n]) for n in range(2)}
        proj_mlp = _dot(mlp_ref[...], wo_ref[aw:, :])
        ready = {0: mask_and_max(*steps[0], raw.pop(0))}
        for n, step in enumerate(steps):
            if n + 1 < len(steps):
                ready[n + 1] = mask_and_max(*steps[n + 1], raw.pop(n + 1))
            if n + 2 < len(steps):
                raw[n + 2] = score_matmul(*steps[n + 2])
            gate_up[n] = ffn_matmuls(n)
            if unnormalized is not None:
                collect(*unnormalized)
                ffn_activation(n - 1, gate_up.pop(n - 1))
            parts, m = ready.pop(n)
            unnormalized = (step, value_matmul(*step, exp_weights(parts, m)), m)

    c_next = len(steps)
    gate_up[c_next] = ffn_matmuls(c_next)
    collect(*unnormalized)
    ffn_activation(c_next - 1, gate_up.pop(c_next - 1))
    for c in range(c_next + 1, n_chunks):
        gate_up[c] = ffn_matmuls(c)
        ffn_activation(c - 1, gate_up.pop(c - 1))
    proj_top = _dot(jnp.concatenate(attn_rows[:nblk // 2], axis=0), wo_ref[:aw, :])
    if proj_mlp is None:
        proj_mlp = _dot(mlp_ref[...], wo_ref[aw:, :])
    proj_bot = _dot(jnp.concatenate(attn_rows[nblk // 2:], axis=0), wo_ref[:aw, :])
    proj_attn = jnp.concatenate([proj_top, proj_bot], axis=0)
    ffn_activation(n_chunks - 1, gate_up.pop(n_chunks - 1))
    h_new = x_ref[...] + mod_ref[2:3, :] * (proj_attn + proj_mlp)
    h_ref[slot_w] = h_new
    hn_ref[...] = _mod_norm(h_new, fgain_ref[...], mod_ref[3:4, :], mod_ref[4:5, :]).astype(BF16)
    o_ref[...] = h_ref[slot_r] + fmod_ref[5:6, :] * _dot(hid_ref[...], wd_ref[...])


def _mix_ffn(x, mod3, k, vt, kc, vct, qt, mlpn, sinks, bias, again_row, wo_bf,
             ffn_gain, wgu_bf, wd_bf, tile, bounded):
    b, s, d = x.shape
    c = kc.shape[1]
    aw = ATTN_HEADS * HEAD_DIM
    mw = mlpn.shape[2]
    ff = wd_bf.shape[0]
    assert ff % MXU_DIM == 0
    r = tile // BLOCK
    nb = s // BLOCK
    nt = s // tile
    n_tiles = b * nt
    gq = ATTN_GROUP * BLOCK

    def mix_tile(g):
        t = jnp.minimum(g, n_tiles - 1)
        return t // nt, t % nt

    def ffn_tile(g):
        t = jnp.maximum(g - 1, 0)
        return t // nt, t % nt

    def at_mix(fn):
        return lambda g: fn(*mix_tile(g))

    const = lambda shape: pl.BlockSpec(shape, lambda g: (0,) * len(shape))
    return pl.pallas_call(
        functools.partial(_mix_ffn_kernel, nt, bounded),
        grid=(n_tiles + 1,),
        in_specs=[pl.BlockSpec((None, tile, d), at_mix(lambda bb, i: (bb, i, 0))),
                  pl.BlockSpec((None, N_MOD, d), at_mix(lambda bb, i: (bb, 0, 0))),
                  pl.BlockSpec((None, BLOCK, LANES), at_mix(lambda bb, i: (bb, jnp.maximum(i * r - 1, 0), 0))),
                  pl.BlockSpec((None, tile, LANES), at_mix(lambda bb, i: (bb, i, 0))),
                  pl.BlockSpec((None, BLOCK, LANES),
                               at_mix(lambda bb, i: (bb, jnp.minimum((i + 1) * r, nb - 1), 0))),
                  pl.BlockSpec((None, LANES, BLOCK), at_mix(lambda bb, i: (bb, 0, jnp.maximum(i * r - 1, 0)))),
                  pl.BlockSpec((None, LANES, tile), at_mix(lambda bb, i: (bb, 0, i))),
                  pl.BlockSpec((None, LANES, BLOCK),
                               at_mix(lambda bb, i: (bb, 0, jnp.minimum((i + 1) * r, nb - 1)))),
                  pl.BlockSpec((None, c, LANES), at_mix(lambda bb, i: (bb, 0, 0))),
                  pl.BlockSpec((None, LANES, c), at_mix(lambda bb, i: (bb, 0, 0))),
                  pl.BlockSpec((None, ATTN_KV_HEADS, HEAD_DIM, ATTN_GROUP * tile),
                               at_mix(lambda bb, i: (bb, 0, 0, i))),
                  pl.BlockSpec((None, tile, mw), at_mix(lambda bb, i: (bb, i, 0))),
                  pl.BlockSpec(memory_space=pltpu.SMEM),
                  const((2 * BLOCK, gq)),
                  const((1, aw)),
                  const((aw + mw, d)),
                  pl.BlockSpec((None, N_MOD, d), lambda g: (ffn_tile(g)[0], 0, 0)),
                  const((1, d)), const((d, 2 * ff)), const((ff, d))],
        out_specs=pl.BlockSpec((None, tile, d), lambda g: (*ffn_tile(g), 0)),
        out_shape=jax.ShapeDtypeStruct((b, s, d), F32),
        scratch_shapes=[pltpu.VMEM((2, tile, d), F32), pltpu.VMEM((tile, d), BF16),
                        pltpu.VMEM((tile, ff), BF16)],
        compiler_params=pltpu.CompilerParams(
            dimension_semantics=("arbitrary",), vmem_limit_bytes=VMEM_LIMIT),
        name="mix_ffn",
    )(x, mod3, k, k, k, vt, vt, vt, kc, vct, qt, mlpn, sinks, bias, again_row, wo_bf,
      mod3, ffn_gain, wgu_bf, wd_bf)


def _rope_tables(s):
    axis_dim = HEAD_DIM // 2
    pos = np.arange(s)
    inv_freq = (ROPE_THETA ** (-np.arange(0, axis_dim, 2, dtype=np.float32) / axis_dim)).astype(np.float32)
    ang_r = (pos // GRID_W).astype(np.float32)[:, None] * inv_freq[None, :]
    ang_c = (pos % GRID_W).astype(np.float32)[:, None] * inv_freq[None, :]
    cr, sr, cc, sc = np.cos(ang_r), np.sin(ang_r), np.cos(ang_c), np.sin(ang_c)
    z = np.zeros_like(sr)
    reps = LANES // HEAD_DIM
    cos = np.tile(np.concatenate([cr, cr, cc, cc], axis=1), (1, reps))
    sin_up = np.tile(np.concatenate([-sr, z, -sc, z], axis=1), (1, reps))
    sin_dn = np.tile(np.concatenate([z, sr, z, sc], axis=1), (1, reps))
    return jnp.asarray(cos, F32), jnp.asarray(sin_up, F32), jnp.asarray(sin_dn, F32)


def _window_bias():
    c = np.arange(BLOCK)[:, None]
    r = np.arange(BLOCK)[None, :]
    prev = np.where(c >= r, 0.0, MASKED).astype(np.float32)
    nxt = np.where(c <= r, 0.0, MASKED).astype(np.float32)
    return jnp.asarray(np.tile(np.concatenate([prev, nxt], axis=0), (1, ATTN_GROUP)), F32)


def _head_mean_matrix():
    bd = np.kron(np.eye(MXU_DIM // HEAD_DIM, dtype=np.float32),
                 np.full((HEAD_DIM, HEAD_DIM), 1.0 / HEAD_DIM, np.float32))
    return jnp.asarray(bd, BF16)


def kernel(x, c, ctx, c_ctx, w_mod, b_mod, norm_mix, norm_ffn, w_in, q_gain, k_gain, attn_sink,
           gate_gain, w_spatial, b_spatial, attn_out_gain, mlp_out_gain, w_out, w_gate_up, w_down):
    b, s, d = x.shape
    assert w_mod.shape[0] == 1, "single-layer problem"
    assert s % 1024 == 0 and d % LANES == 0
    aw = ATTN_HEADS * HEAD_DIM
    mw = MLP_HEADS * HEAD_DIM

    rows = -(-(b + 1) // BF16_SUBLANES) * BF16_SUBLANES
    cond = jnp.concatenate([c, c_ctx[None, :], jnp.zeros((rows - b - 1, d), F32)], axis=0)
    mod, w_in_bf = _adaln(cond, w_mod[0], b_mod[0][None, :], w_in[0])
    mod3 = mod.reshape(rows, N_MOD, d)
    bd = _head_mean_matrix()
    assert aw == mw and (aw + mw) % LANES == 0
    gains = jnp.concatenate([jnp.tile(q_gain[0], ATTN_HEADS) * (HEAD_DIM ** -0.5 * LOG2E),
                             gate_gain[0].reshape(mw),
                             jnp.tile(k_gain[0], ATTN_KV_HEADS)])[None, :]
    kgain_block = (aw + mw) // LANES
    norm_mix_g = norm_mix[0][None, :]

    kc, vct = _ctx_kv(ctx, mod3, b, norm_mix_g, w_in_bf, gains, kgain_block, bd)

    cos, sup, sdn = _rope_tables(s)
    bs_full = jnp.repeat(b_spatial[0].T, HEAD_DIM, axis=1)
    k, vt, qt, mlpn, wo_bf, wgu_bf, wd_bf = _inproj(
        x, mod3, norm_mix_g, w_in_bf, cos, sup, sdn, gains, kgain_block, w_spatial[0], bs_full,
        mlp_out_gain[0][None, :], bd, (w_out[0], w_gate_up[0], w_down[0]), tile=1024)

    again_row = attn_out_gain[0][None, :]
    sinks_padded = jnp.pad(attn_sink[0], (0, HEAD_DIM - attn_sink.shape[1]))
    q_max, k_max, sink_max = jnp.max(jnp.abs(jnp.stack([q_gain[0], k_gain[0], sinks_padded])), axis=1)
    logit_bound = BOUND_MARGIN * HEAD_DIM ** 0.5 * LOG2E * q_max * k_max
    sink_max = sink_max * LOG2E
    use_bound = (logit_bound <= MAX_SAFE_LOGIT) & (sink_max <= MAX_SAFE_LOGIT)
    operands = (x, mod3, k, vt, kc, vct, qt, mlpn, attn_sink[0], _window_bias(), again_row,
                wo_bf, norm_ffn[0][None, :], wgu_bf, wd_bf)
    return lax.cond(use_bound,
                    lambda ops: _mix_ffn(*ops, tile=512, bounded=True),
                    lambda ops: _mix_ffn(*ops, tile=512, bounded=False),
                    operands)
```

```python
import functools
import math

import jax
import jax.numpy as jnp
import numpy as np
from jax import lax
from jax.experimental import pallas as pl
from jax.experimental.pallas import tpu as pltpu

F32 = jnp.float32
BF16 = jnp.bfloat16

HEAD_DIM = 64
ATTN_HEADS = 8
ATTN_KV_HEADS = 2
ATTN_GROUP = ATTN_HEADS // ATTN_KV_HEADS
MLP_HEADS = 8
N_MOD = 6
BLOCK = 128
GRID_W = 64
ROPE_THETA = 10000.0
EPS = 1e-6
MASKED = -1e30
BOUND_MARGIN = 1.05
MAX_SAFE_LOGIT = 40.0
LOG2E = math.log2(math.e)

LANES = 128
BF16_SUBLANES = 16
MXU_DIM = 256
VMEM_LIMIT = 56 * 1024 * 1024
HEADS_PER_STEP = 2


def _dot(a, b):
    return jnp.dot(a, b, preferred_element_type=F32)


def _silu(x):
    return x * (1.0 / (1.0 + jnp.exp(-x)))


def _gelu2_tanh(x):
    c = math.sqrt(2.0 / math.pi)
    return x * (1.0 + jnp.tanh(x * (c + (c * 0.044715) * (x * x))))


def _mod_norm(x, gain, shift, scale):
    y = x * lax.rsqrt(jnp.mean(x * x, axis=-1, keepdims=True) + EPS)
    return y * (gain * (1.0 + scale)) + shift


def _head_ms(x, bd):
    return _dot((x * x).astype(BF16), bd)


def _rope(x, cos, sin_up, sin_dn):
    up = pltpu.roll(x, LANES - 16, 1)
    dn = pltpu.roll(x, 16, 1)
    return x * cos + up * sin_up + dn * sin_dn


def _split_bf16(x):
    hi = x.astype(BF16)
    return hi, (x - hi.astype(F32)).astype(BF16)


def _adaln_kernel(cond_ref, wa_ref, wb_ref, b_ref, win_ref, o_ref, win_bf_ref):
    win_bf_ref[...] = win_ref[...].astype(BF16)
    rows = cond_ref.shape[0]
    half = wa_ref.shape[1]

    @pl.when(pl.program_id(0) == 0)
    def _():
        o_ref[...] = jnp.broadcast_to(b_ref[...], o_ref.shape)

    s_hi, s_lo = _split_bf16(_silu(cond_ref[...]))
    s_both = jnp.concatenate([s_hi, s_lo], axis=0)
    for part, w_ref in enumerate((wa_ref, wb_ref)):
        w_hi, w_lo = _split_bf16(w_ref[...])
        both = _dot(s_both, w_hi)
        o_ref[:, part * half:(part + 1) * half] += both[:rows] + both[rows:] + _dot(s_hi, w_lo)


def _adaln(cond, w_mod, b_mod, w_in):
    rows, d = cond.shape
    n = w_mod.shape[1]
    steps = 4
    tk = d // steps
    win_rows = w_in.shape[0] // steps
    assert rows % BF16_SUBLANES == 0 and tk % LANES == 0 and (n // 2) % LANES == 0
    assert win_rows % BF16_SUBLANES == 0
    win_spec = pl.BlockSpec((win_rows, w_in.shape[1]), lambda j: (j, 0))
    return pl.pallas_call(
        _adaln_kernel,
        grid=(steps,),
        in_specs=[pl.BlockSpec((rows, tk), lambda j: (0, j)),
                  pl.BlockSpec((tk, n // 2), lambda j: (j, 0)),
                  pl.BlockSpec((tk, n // 2), lambda j: (j, 1)),
                  pl.BlockSpec((1, n), lambda j: (0, 0)),
                  win_spec],
        out_specs=[pl.BlockSpec((rows, n), lambda j: (0, 0)), win_spec],
        out_shape=[jax.ShapeDtypeStruct((rows, n), F32), jax.ShapeDtypeStruct(w_in.shape, BF16)],
        compiler_params=pltpu.CompilerParams(
            dimension_semantics=("arbitrary",), vmem_limit_bytes=VMEM_LIMIT),
        name="adaln",
    )(cond, w_mod, w_mod, b_mod, w_in)


def _ctx_kernel(x_ref, mod_ref, gain_ref, w_ref, kgain_ref, bd_ref, kc_ref, vct_ref):
    per_step, c, d = x_ref.shape
    x = x_ref[...].reshape(per_step * c, d)
    hn = _mod_norm(x, gain_ref[...], mod_ref[0:1, :], mod_ref[1:2, :])
    kv = _dot(hn.astype(BF16), w_ref[...])
    k = kv[:, :LANES]
    k = k * lax.rsqrt(_head_ms(k, bd_ref[:LANES, :LANES]) + EPS) * kgain_ref[...]
    for j in range(per_step):
        kc_ref[j] = k[j * c:(j + 1) * c, :].astype(BF16)
        vct_ref[j] = kv[j * c:(j + 1) * c, LANES:].T.astype(BF16)


def _ctx_kv(ctx, mod3, ctx_row, norm_gain, w_in_bf, gains, kgain_block, bd):
    b, c, d = ctx.shape
    kvw = 2 * ATTN_KV_HEADS * HEAD_DIM
    per_step = 2 if b % 2 == 0 else 1
    return pl.pallas_call(
        _ctx_kernel,
        grid=(b // per_step,),
        in_specs=[pl.BlockSpec((per_step, c, d), lambda i: (i, 0, 0)),
                  pl.BlockSpec((None, N_MOD, d), lambda i: (ctx_row, 0, 0)),
                  pl.BlockSpec((1, d), lambda i: (0, 0)),
                  pl.BlockSpec((d, kvw), lambda i: (0, 0)),
                  pl.BlockSpec((1, LANES), lambda i: (0, kgain_block)),
                  pl.BlockSpec((MXU_DIM, MXU_DIM), lambda i: (0, 0))],
        out_specs=[pl.BlockSpec((per_step, c, LANES), lambda i: (i, 0, 0)),
                   pl.BlockSpec((per_step, LANES, c), lambda i: (i, 0, 0))],
        out_shape=[jax.ShapeDtypeStruct((b, c, LANES), BF16),
                   jax.ShapeDtypeStruct((b, LANES, c), BF16)],
        compiler_params=pltpu.CompilerParams(
            dimension_semantics=("arbitrary",), vmem_limit_bytes=VMEM_LIMIT),
        name="ctx_kv",
    )(ctx, mod3, norm_gain, w_in_bf, gains, bd)


def _inproj_kernel(x_ref, mod_ref, gain_ref, w_ref, cos_ref, sup_ref, sdn_ref,
                   qgain_ref, kgain_ref, ggain_ref, ws_ref, bs_ref, ogain_ref, bd_ref,
                   wo_f32_ref, wgu_f32_ref, wd_f32_ref,
                   k_ref, vt_ref, qt_ref, mlp_ref, wo_bf_ref, wgu_bf_ref, wd_bf_ref):
    wo_bf_ref[...] = wo_f32_ref[...].astype(BF16)
    wgu_bf_ref[...] = wgu_f32_ref[...].astype(BF16)
    wd_bf_ref[...] = wd_f32_ref[...].astype(BF16)

    t = x_ref.shape[0]
    nblk = t // BLOCK
    kvw = ATTN_KV_HEADS * HEAD_DIM
    aw = ATTN_HEADS * HEAD_DIM
    mw = MLP_HEADS * HEAD_DIM
    q_cols = slice(2 * kvw, 2 * kvw + aw)
    kv_cols = slice(0, 2 * kvw)
    u_cols = slice(2 * kvw + aw, 2 * kvw + aw + mw)
    g_cols = slice(2 * kvw + aw + mw, 2 * kvw + aw + 2 * mw)
    halves = range(aw // MXU_DIM)
    bd = bd_ref[...]

    def project(cols):
        return _dot(hn, w_ref[:, cols])

    def finish_qkv(q, kv, q_ms, k_ms):
        cos, sup, sdn = cos_ref[...], sup_ref[...], sdn_ref[...]
        k = kv[:, :kvw] * lax.rsqrt(k_ms + EPS) * kgain_ref[...]
        k_ref[...] = _rope(k, cos, sup, sdn).astype(BF16)
        vt_ref[...] = kv[:, kvw:].T.astype(BF16)
        for half in halves:
            qh = q[:, half * MXU_DIM:(half + 1) * MXU_DIM] * lax.rsqrt(q_ms[half] + EPS)
            qh = qh * qgain_ref[:, half * MXU_DIM:(half + 1) * MXU_DIM]
            for sl in range(MXU_DIM // LANES):
                qs = _rope(qh[:, sl * LANES:(sl + 1) * LANES], cos, sup, sdn)
                qst = qs.T
                for hh in range(LANES // HEAD_DIM):
                    head = (half * MXU_DIM + sl * LANES) // HEAD_DIM + hh
                    kvh, grp = head // ATTN_GROUP, head % ATTN_GROUP
                    for jb in range(nblk):
                        col = (jb * ATTN_GROUP + grp) * BLOCK
                        qt_ref[kvh, :, col:col + BLOCK] = qst[
                            hh * HEAD_DIM:(hh + 1) * HEAD_DIM, jb * BLOCK:(jb + 1) * BLOCK].astype(BF16)

    def gate_mix(g_raw):
        g = _gelu2_tanh(g_raw)
        g_ms = [_head_ms(g[:, h * MXU_DIM:(h + 1) * MXU_DIM], bd) for h in halves]
        gn = jnp.concatenate(
            [g[:, h * MXU_DIM:(h + 1) * MXU_DIM] * lax.rsqrt(g_ms[h] + 4.0 * EPS) for h in halves],
            axis=1) * ggain_ref[...]
        gnb = gn.astype(BF16)
        low_head = lax.broadcasted_iota(jnp.int32, (BLOCK, t), 1) % LANES < HEAD_DIM
        mixed_slabs = []
        for p in range(MLP_HEADS // 2):
            rhs = jnp.concatenate(
                [gnb[c * BLOCK:(c + 1) * BLOCK, p * LANES:(p + 1) * LANES] for c in range(nblk)], axis=1)
            a = _dot(ws_ref[2 * p].astype(BF16), rhs)
            b = _dot(ws_ref[2 * p + 1].astype(BF16), rhs)
            mixed_slabs.append(jnp.where(low_head, a, b))
        return mixed_slabs

    def finish_mlp(u_raw, mixed_slabs):
        u = _gelu2_tanh(u_raw)
        rows = []
        for c in range(nblk):
            mixed_c = jnp.concatenate(
                [m[:, c * LANES:(c + 1) * LANES] for m in mixed_slabs], axis=1) + bs_ref[...]
            rows.append(u[c * BLOCK:(c + 1) * BLOCK, :] * mixed_c)
        o = jnp.concatenate(rows, axis=0)
        o = o * lax.rsqrt(jnp.mean(o * o, axis=-1, keepdims=True) + 4.0 * EPS) * ogain_ref[...]
        mlp_ref[...] = o.astype(BF16)

    hn = _mod_norm(x_ref[...], gain_ref[...], mod_ref[0:1, :], mod_ref[1:2, :]).astype(BF16)
    q, kv = project(q_cols), project(kv_cols)
    q_ms = [_head_ms(q[:, h * MXU_DIM:(h + 1) * MXU_DIM], bd) for h in halves]
    k_ms = _head_ms(kv[:, :kvw], bd[:kvw, :kvw])
    g_raw = project(g_cols)
    finish_qkv(q, kv, q_ms, k_ms)
    u_raw = project(u_cols)
    finish_mlp(u_raw, gate_mix(g_raw))


def _inproj(x, mod3, norm_gain, w_in_bf, cos, sup, sdn, gains, kgain_block, w_spatial, bs_full,
            ogain, bd, later_weights, tile):
    b, s, d = x.shape
    inw = w_in_bf.shape[1]
    aw = ATTN_HEADS * HEAD_DIM
    mw = MLP_HEADS * HEAD_DIM
    n_steps = (s // tile) * b
    const = lambda shape: pl.BlockSpec(shape, lambda i, bb: (0,) * len(shape))

    def slab_spec(w):
        rows = next(r for r in range(BF16_SUBLANES, w.shape[0] + 1, BF16_SUBLANES)
                    if w.shape[0] % r == 0 and w.shape[0] // r <= n_steps)
        n_slabs = w.shape[0] // rows
        return pl.BlockSpec((rows, w.shape[1]), lambda i, bb: (jnp.minimum(i * b + bb, n_slabs - 1), 0))

    slab_specs = [slab_spec(w) for w in later_weights]
    table_spec = pl.BlockSpec((tile, LANES), lambda i, bb: (i, 0))
    return pl.pallas_call(
        _inproj_kernel,
        grid=(s // tile, b),
        in_specs=[pl.BlockSpec((None, tile, d), lambda i, bb: (bb, i, 0)),
                  pl.BlockSpec((None, N_MOD, d), lambda i, bb: (bb, 0, 0)),
                  const((1, d)),
                  const((d, inw)),
                  table_spec, table_spec, table_spec,
                  const((1, aw)), pl.BlockSpec((1, LANES), lambda i, bb: (0, kgain_block)),
                  pl.BlockSpec((1, mw), lambda i, bb: (0, aw // mw)),
                  const((MLP_HEADS, BLOCK, BLOCK)), const((BLOCK, mw)), const((1, mw)),
                  const((MXU_DIM, MXU_DIM))] + slab_specs,
        out_specs=[pl.BlockSpec((None, tile, LANES), lambda i, bb: (bb, i, 0)),
                   pl.BlockSpec((None, LANES, tile), lambda i, bb: (bb, 0, i)),
                   pl.BlockSpec((None, ATTN_KV_HEADS, HEAD_DIM, ATTN_GROUP * tile),
                                lambda i, bb: (bb, 0, 0, i)),
                   pl.BlockSpec((None, tile, mw), lambda i, bb: (bb, i, 0))] + slab_specs,
        out_shape=[jax.ShapeDtypeStruct((b, s, LANES), BF16),
                   jax.ShapeDtypeStruct((b, LANES, s), BF16),
                   jax.ShapeDtypeStruct((b, ATTN_KV_HEADS, HEAD_DIM, ATTN_GROUP * s), BF16),
                   jax.ShapeDtypeStruct((b, s, mw), BF16)]
        + [jax.ShapeDtypeStruct(w.shape, BF16) for w in later_weights],
        compiler_params=pltpu.CompilerParams(
            dimension_semantics=("arbitrary", "arbitrary"), vmem_limit_bytes=VMEM_LIMIT),
        name="inproj",
    )(x, mod3, norm_gain, w_in_bf, cos, sup, sdn, gains, gains, gains, w_spatial, bs_full, ogain, bd,
      *later_weights)


def _mix_ffn_kernel(tiles_per_seq, bounded,
                    x_ref, mod_ref, kp_ref, km_ref, kn_ref, vp_ref, vm_ref, vn_ref,
                    kc_ref, vct_ref, qt_ref, mlp_ref, sink_ref, bias_ref, again_ref, wo_ref,
                    fmod_ref, fgain_ref, wgu_ref, wd_ref,
                    o_ref, h_ref, hn_ref, hid_ref):
    tq = x_ref.shape[0]
    nblk = tq // BLOCK
    step_id = pl.program_id(0)
    n_tiles = pl.num_programs(0) - 1
    i = lax.rem(jnp.minimum(step_id, n_tiles - 1), tiles_per_seq)
    last = tiles_per_seq - 1
    gq = ATTN_GROUP * BLOCK
    slot_w = lax.rem(step_id, 2)
    slot_r = 1 - slot_w

    @pl.when(step_id == 0)
    def _():
        h_ref[1] = jnp.zeros(h_ref.shape[1:], F32)
        hn_ref[...] = jnp.zeros(hn_ref.shape, BF16)

    ff = wd_ref.shape[0]
    n_chunks = ff // MXU_DIM
    def ffn_matmuls(c):
        hn = hn_ref[...]
        return (_dot(hn, wgu_ref[:, c * MXU_DIM:(c + 1) * MXU_DIM]),
                _dot(hn, wgu_ref[:, ff + c * MXU_DIM:ff + (c + 1) * MXU_DIM]))

    def ffn_activation(c, gate_up):
        a, b = gate_up
        hid_ref[:, c * MXU_DIM:(c + 1) * MXU_DIM] = (_silu(a) * b).astype(BF16)

    k_ext = jnp.concatenate([kp_ref[...], km_ref[...], kn_ref[...]], axis=0)
    vt_ext = jnp.concatenate([vp_ref[...], vm_ref[...], vn_ref[...]], axis=1)
    kc = kc_ref[...]
    vct = vct_ref[...]
    cw = HEADS_PER_STEP * BLOCK
    zeros_q = jnp.zeros((HEAD_DIM, cw), BF16)
    ones_rows = jnp.ones((BF16_SUBLANES, 3 * BLOCK + kc.shape[0]), BF16)
    bias_prev = bias_ref[0:BLOCK, :cw]
    bias_next = bias_ref[BLOCK:2 * BLOCK, :cw]

    def sink_row(kvh, part):
        first = kvh * ATTN_GROUP + part * HEADS_PER_STEP
        return jnp.concatenate([jnp.full((1, BLOCK), sink_ref[first + g] * LOG2E, F32)
                                for g in range(HEADS_PER_STEP)], axis=1)

    def score_matmul(jb, kvh, part):
        qt = qt_ref[kvh, :, jb * gq + part * cw:jb * gq + (part + 1) * cw]
        rhs = jnp.concatenate([qt, zeros_q] if kvh == 0 else [zeros_q, qt], axis=0)
        return _dot(kc, rhs), _dot(k_ext[jb * BLOCK:(jb + 3) * BLOCK, :], rhs)

    def mask_and_max(jb, kvh, part, s):
        pen_prev = jnp.where(i == 0, MASKED, 0.0) if jb == 0 else 0.0
        pen_next = jnp.where(i == last, MASKED, 0.0) if jb == nblk - 1 else 0.0
        s_ctx, s = s
        parts = [s[0:BLOCK] + (bias_prev + pen_prev), s[BLOCK:2 * BLOCK],
                 s[2 * BLOCK:3 * BLOCK] + (bias_next + pen_next), s_ctx]
        if bounded:
            return jnp.concatenate([jnp.exp2(p_).astype(BF16) for p_ in parts], axis=0), None
        m = jnp.maximum(
            jnp.maximum(jnp.max(parts[0], axis=0, keepdims=True), jnp.max(parts[1], axis=0, keepdims=True)),
            jnp.maximum(jnp.max(parts[2], axis=0, keepdims=True), jnp.max(parts[3], axis=0, keepdims=True)))
        return parts, jnp.maximum(m, sink_row(kvh, part))

    def exp_weights(parts, m):
        return jnp.concatenate([jnp.exp2(p_ - m).astype(BF16) for p_ in parts], axis=0)

    def value_matmul(jb, kvh, part, p):
        v_all = jnp.concatenate(
            [vt_ext[kvh * HEAD_DIM:(kvh + 1) * HEAD_DIM, jb * BLOCK:(jb + 3) * BLOCK],
             vct[kvh * HEAD_DIM:(kvh + 1) * HEAD_DIM, :]], axis=1)
        return _dot(jnp.concatenate([v_all, ones_rows], axis=0), p)

    def normalize(jb, kvh, part, pv, m):
        sink = sink_row(kvh, part)
        denom = pv[HEAD_DIM:HEAD_DIM + 1, :] + jnp.exp2(sink if bounded else sink - m)
        o_t = pv[:HEAD_DIM, :] * (1.0 / denom)
        return [o_t[:, g * BLOCK:(g + 1) * BLOCK] for g in range(HEADS_PER_STEP)]

    def block_norm(out_t):
        o_all = jnp.concatenate(out_t, axis=0)
        ms = jnp.mean(o_all * o_all, axis=0, keepdims=True)
        y = o_all * lax.rsqrt(ms + EPS)
        return (y.T * again_ref[...]).astype(BF16)

    aw = ATTN_HEADS * HEAD_DIM
    steps = [(jb, kvh, part) for jb in range(nblk) for kvh in range(ATTN_KV_HEADS)
             for part in range(ATTN_GROUP // HEADS_PER_STEP)]
    parts = ATTN_GROUP // HEADS_PER_STEP
    n_loop = len(steps) // parts
    assert n_chunks > n_loop
    attn_rows, out_t, unnormalized, pending = [], [], None, []

    def collect(step, pv, m):
        out_t.extend(normalize(*step, pv, m))
        if len(out_t) == ATTN_HEADS:
            attn_rows.append(block_norm(out_t))
            out_t.clear()

    def issue_chunk(c):
        return [(c, ffn_matmuls(c))]

    def activate(issued):
        for c, gate_up in issued:
            ffn_activation(c, gate_up)

    if bounded:
        proj_mlp = None
        for n, step in enumerate(steps):
            scores = score_matmul(*step)
            issued = issue_chunk(n // parts) if n % parts == 0 else []
            weights, _ = mask_and_max(*step, scores)
            if unnormalized is not None:
                collect(*unnormalized)
            activate(pending)
            pending = issued
            unnormalized = (step, value_matmul(*step, weights), None)
    else:
        raw = {n: score_matmul(*steps[n]) for n in range(2)}
        proj_mlp = _dot(mlp_ref[...], wo_ref[aw:, :])
        ready = {0: mask_and_max(*steps[0], raw.pop(0))}
        for n, step in enumerate(steps):
            if n + 1 < len(steps):
                ready[n + 1] = mask_and_max(*steps[n + 1], raw.pop(n + 1))
            if n + 2 < len(steps):
                raw[n + 2] = score_matmul(*steps[n + 2])
            issued = issue_chunk(n // parts) if n % parts == 0 else []
            if unnormalized is not None:
                collect(*unnormalized)
            activate(pending)
            pending = issued
            parts_n, m = ready.pop(n)
            unnormalized = (step, value_matmul(*step, exp_weights(parts_n, m)), m)

    for c in range(n_loop, n_chunks):
        issued = issue_chunk(c)
        if unnormalized is not None:
            collect(*unnormalized)
            unnormalized = None
        activate(pending)
        pending = issued
    proj_top = _dot(jnp.concatenate(attn_rows[:nblk // 2], axis=0), wo_ref[:aw, :])
    if proj_mlp is None:
        proj_mlp = _dot(mlp_ref[...], wo_ref[aw:, :])
    proj_bot = _dot(jnp.concatenate(attn_rows[nblk // 2:], axis=0), wo_ref[:aw, :])
    proj_attn = jnp.concatenate([proj_top, proj_bot], axis=0)
    activate(pending)
    h_new = x_ref[...] + mod_ref[2:3, :] * (proj_attn + proj_mlp)
    h_ref[slot_w] = h_new
    hn_ref[...] = _mod_norm(h_new, fgain_ref[...], mod_ref[3:4, :], mod_ref[4:5, :]).astype(BF16)
    o_ref[...] = h_ref[slot_r] + fmod_ref[5:6, :] * _dot(hid_ref[...], wd_ref[...])


def _mix_ffn(x, mod3, k, vt, kc, vct, qt, mlpn, sinks, bias, again_row, wo_bf,
             ffn_gain, wgu_bf, wd_bf, tile, bounded):
    b, s, d = x.shape
    c = kc.shape[1]
    aw = ATTN_HEADS * HEAD_DIM
    mw = mlpn.shape[2]
    ff = wd_bf.shape[0]
    assert ff % MXU_DIM == 0
    r = tile // BLOCK
    nb = s // BLOCK
    nt = s // tile
    n_tiles = b * nt
    gq = ATTN_GROUP * BLOCK

    def mix_tile(g):
        t = jnp.minimum(g, n_tiles - 1)
        return t // nt, t % nt

    def ffn_tile(g):
        t = jnp.maximum(g - 1, 0)
        return t // nt, t % nt

    def at_mix(fn):
        return lambda g: fn(*mix_tile(g))

    const = lambda shape: pl.BlockSpec(shape, lambda g: (0,) * len(shape))
    return pl.pallas_call(
        functools.partial(_mix_ffn_kernel, nt, bounded),
        grid=(n_tiles + 1,),
        in_specs=[pl.BlockSpec((None, tile, d), at_mix(lambda bb, i: (bb, i, 0))),
                  pl.BlockSpec((None, N_MOD, d), at_mix(lambda bb, i: (bb, 0, 0))),
                  pl.BlockSpec((None, BLOCK, LANES), at_mix(lambda bb, i: (bb, jnp.maximum(i * r - 1, 0), 0))),
                  pl.BlockSpec((None, tile, LANES), at_mix(lambda bb, i: (bb, i, 0))),
                  pl.BlockSpec((None, BLOCK, LANES),
                               at_mix(lambda bb, i: (bb, jnp.minimum((i + 1) * r, nb - 1), 0))),
                  pl.BlockSpec((None, LANES, BLOCK), at_mix(lambda bb, i: (bb, 0, jnp.maximum(i * r - 1, 0)))),
                  pl.BlockSpec((None, LANES, tile), at_mix(lambda bb, i: (bb, 0, i))),
                  pl.BlockSpec((None, LANES, BLOCK),
                               at_mix(lambda bb, i: (bb, 0, jnp.minimum((i + 1) * r, nb - 1)))),
                  pl.BlockSpec((None, c, LANES), at_mix(lambda bb, i: (bb, 0, 0))),
                  pl.BlockSpec((None, LANES, c), at_mix(lambda bb, i: (bb, 0, 0))),
                  pl.BlockSpec((None, ATTN_KV_HEADS, HEAD_DIM, ATTN_GROUP * tile),
                               at_mix(lambda bb, i: (bb, 0, 0, i))),
                  pl.BlockSpec((None, tile, mw), at_mix(lambda bb, i: (bb, i, 0))),
                  pl.BlockSpec(memory_space=pltpu.SMEM),
                  const((2 * BLOCK, gq)),
                  const((1, aw)),
                  const((aw + mw, d)),
                  pl.BlockSpec((None, N_MOD, d), lambda g: (ffn_tile(g)[0], 0, 0)),
                  const((1, d)), const((d, 2 * ff)), const((ff, d))],
        out_specs=pl.BlockSpec((None, tile, d), lambda g: (*ffn_tile(g), 0)),
        out_shape=jax.ShapeDtypeStruct((b, s, d), F32),
        scratch_shapes=[pltpu.VMEM((2, tile, d), F32), pltpu.VMEM((tile, d), BF16),
                        pltpu.VMEM((tile, ff), BF16)],
        compiler_params=pltpu.CompilerParams(
            dimension_semantics=("arbitrary",), vmem_limit_bytes=VMEM_LIMIT),
        name="mix_ffn",
    )(x, mod3, k, k, k, vt, vt, vt, kc, vct, qt, mlpn, sinks, bias, again_row, wo_bf,
      mod3, ffn_gain, wgu_bf, wd_bf)


def _rope_tables(s):
    axis_dim = HEAD_DIM // 2
    pos = np.arange(s)
    inv_freq = (ROPE_THETA ** (-np.arange(0, axis_dim, 2, dtype=np.float32) / axis_dim)).astype(np.float32)
    ang_r = (pos // GRID_W).astype(np.float32)[:, None] * inv_freq[None, :]
    ang_c = (pos % GRID_W).astype(np.float32)[:, None] * inv_freq[None, :]
    cr, sr, cc, sc = np.cos(ang_r), np.sin(ang_r), np.cos(ang_c), np.sin(ang_c)
    z = np.zeros_like(sr)
    reps = LANES // HEAD_DIM
    cos = np.tile(np.concatenate([cr, cr, cc, cc], axis=1), (1, reps))
    sin_up = np.tile(np.concatenate([-sr, z, -sc, z], axis=1), (1, reps))
    sin_dn = np.tile(np.concatenate([z, sr, z, sc], axis=1), (1, reps))
    return jnp.asarray(cos, F32), jnp.asarray(sin_up, F32), jnp.asarray(sin_dn, F32)


def _window_bias():
    c = np.arange(BLOCK)[:, None]
    r = np.arange(BLOCK)[None, :]
    prev = np.where(c >= r, 0.0, MASKED).astype(np.float32)
    nxt = np.where(c <= r, 0.0, MASKED).astype(np.float32)
    return jnp.asarray(np.tile(np.concatenate([prev, nxt], axis=0), (1, ATTN_GROUP)), F32)


def _head_mean_matrix():
    bd = np.kron(np.eye(MXU_DIM // HEAD_DIM, dtype=np.float32),
                 np.full((HEAD_DIM, HEAD_DIM), 1.0 / HEAD_DIM, np.float32))
    return jnp.asarray(bd, BF16)


def kernel(x, c, ctx, c_ctx, w_mod, b_mod, norm_mix, norm_ffn, w_in, q_gain, k_gain, attn_sink,
           gate_gain, w_spatial, b_spatial, attn_out_gain, mlp_out_gain, w_out, w_gate_up, w_down):
    b, s, d = x.shape
    assert w_mod.shape[0] == 1, "single-layer problem"
    assert s % 1024 == 0 and d % LANES == 0
    aw = ATTN_HEADS * HEAD_DIM
    mw = MLP_HEADS * HEAD_DIM

    rows = -(-(b + 1) // BF16_SUBLANES) * BF16_SUBLANES
    cond = jnp.concatenate([c, c_ctx[None, :], jnp.zeros((rows - b - 1, d), F32)], axis=0)
    mod, w_in_bf = _adaln(cond, w_mod[0], b_mod[0][None, :], w_in[0])
    mod3 = mod.reshape(rows, N_MOD, d)
    bd = _head_mean_matrix()
    assert aw == mw and (aw + mw) % LANES == 0
    gains = jnp.concatenate([jnp.tile(q_gain[0], ATTN_HEADS) * (HEAD_DIM ** -0.5 * LOG2E),
                             gate_gain[0].reshape(mw),
                             jnp.tile(k_gain[0], ATTN_KV_HEADS)])[None, :]
    kgain_block = (aw + mw) // LANES
    norm_mix_g = norm_mix[0][None, :]

    kc, vct = _ctx_kv(ctx, mod3, b, norm_mix_g, w_in_bf, gains, kgain_block, bd)

    cos, sup, sdn = _rope_tables(s)
    bs_full = jnp.repeat(b_spatial[0].T, HEAD_DIM, axis=1)
    k, vt, qt, mlpn, wo_bf, wgu_bf, wd_bf = _inproj(
        x, mod3, norm_mix_g, w_in_bf, cos, sup, sdn, gains, kgain_block, w_spatial[0], bs_full,
        mlp_out_gain[0][None, :], bd, (w_out[0], w_gate_up[0], w_down[0]), tile=1024)

    again_row = attn_out_gain[0][None, :]
    sinks_padded = jnp.pad(attn_sink[0], (0, HEAD_DIM - attn_sink.shape[1]))
    q_max, k_max, sink_max = jnp.max(jnp.abs(jnp.stack([q_gain[0], k_gain[0], sinks_padded])), axis=1)
    logit_bound = BOUND_MARGIN * HEAD_DIM ** 0.5 * LOG2E * q_max * k_max
    sink_max = sink_max * LOG2E
    use_bound = (logit_bound <= MAX_SAFE_LOGIT) & (sink_max <= MAX_SAFE_LOGIT)
    operands = (x, mod3, k, vt, kc, vct, qt, mlpn, attn_sink[0], _window_bias(), again_row,
                wo_bf, norm_ffn[0][None, :], wgu_bf, wd_bf)
    return lax.cond(use_bound,
                    lambda ops: _mix_ffn(*ops, tile=512, bounded=True),
                    lambda ops: _mix_ffn(*ops, tile=512, bounded=False),
                    operands)
```

```python
import functools
import math

import jax
import jax.numpy as jnp
import numpy as np
from jax import lax
from jax.experimental import pallas as pl
from jax.experimental.pallas import tpu as pltpu

F32 = jnp.float32
BF16 = jnp.bfloat16

HEAD_DIM = 64
ATTN_HEADS = 8
ATTN_KV_HEADS = 2
ATTN_GROUP = ATTN_HEADS // ATTN_KV_HEADS
MLP_HEADS = 8
N_MOD = 6
BLOCK = 128
GRID_W = 64
ROPE_THETA = 10000.0
EPS = 1e-6
MASKED = -1e30
BOUND_MARGIN = 1.05
MAX_SAFE_LOGIT = 40.0
LOG2E = math.log2(math.e)

LANES = 128
BF16_SUBLANES = 16
MXU_DIM = 256
VMEM_LIMIT = 56 * 1024 * 1024
HEADS_PER_STEP = 4


def _dot(a, b):
    return jnp.dot(a, b, preferred_element_type=F32)


def _silu(x):
    return x * (1.0 / (1.0 + jnp.exp(-x)))


def _gelu2_tanh(x):
    c = math.sqrt(2.0 / math.pi)
    return x * (1.0 + jnp.tanh(x * (c + (c * 0.044715) * (x * x))))


def _mod_norm(x, gain, shift, scale):
    y = x * lax.rsqrt(jnp.mean(x * x, axis=-1, keepdims=True) + EPS)
    return y * (gain * (1.0 + scale)) + shift


def _head_ms(x, bd):
    return _dot((x * x).astype(BF16), bd)


def _rope(x, cos, sin_up, sin_dn):
    up = pltpu.roll(x, LANES - 16, 1)
    dn = pltpu.roll(x, 16, 1)
    return x * cos + up * sin_up + dn * sin_dn


def _split_bf16(x):
    hi = x.astype(BF16)
    return hi, (x - hi.astype(F32)).astype(BF16)


def _adaln_kernel(cond_ref, wa_ref, wb_ref, b_ref, win_ref, o_ref, win_bf_ref):
    win_bf_ref[...] = win_ref[...].astype(BF16)
    rows = cond_ref.shape[0]
    half = wa_ref.shape[1]

    @pl.when(pl.program_id(0) == 0)
    def _():
        o_ref[...] = jnp.broadcast_to(b_ref[...], o_ref.shape)

    s_hi, s_lo = _split_bf16(_silu(cond_ref[...]))
    s_both = jnp.concatenate([s_hi, s_lo], axis=0)
    for part, w_ref in enumerate((wa_ref, wb_ref)):
        w_hi, w_lo = _split_bf16(w_ref[...])
        both = _dot(s_both, w_hi)
        o_ref[:, part * half:(part + 1) * half] += both[:rows] + both[rows:] + _dot(s_hi, w_lo)


def _adaln(cond, w_mod, b_mod, w_in):
    rows, d = cond.shape
    n = w_mod.shape[1]
    steps = 4
    tk = d // steps
    win_rows = w_in.shape[0] // steps
    assert rows % BF16_SUBLANES == 0 and tk % LANES == 0 and (n // 2) % LANES == 0
    assert win_rows % BF16_SUBLANES == 0
    win_spec = pl.BlockSpec((win_rows, w_in.shape[1]), lambda j: (j, 0))
    return pl.pallas_call(
        _adaln_kernel,
        grid=(steps,),
        in_specs=[pl.BlockSpec((rows, tk), lambda j: (0, j)),
                  pl.BlockSpec((tk, n // 2), lambda j: (j, 0)),
                  pl.BlockSpec((tk, n // 2), lambda j: (j, 1)),
                  pl.BlockSpec((1, n), lambda j: (0, 0)),
                  win_spec],
        out_specs=[pl.BlockSpec((rows, n), lambda j: (0, 0)), win_spec],
        out_shape=[jax.ShapeDtypeStruct((rows, n), F32), jax.ShapeDtypeStruct(w_in.shape, BF16)],
        compiler_params=pltpu.CompilerParams(
            dimension_semantics=("arbitrary",), vmem_limit_bytes=VMEM_LIMIT),
        name="adaln",
    )(cond, w_mod, w_mod, b_mod, w_in)


def _ctx_kernel(x_ref, mod_ref, gain_ref, w_ref, kgain_ref, bd_ref, kc_ref, vct_ref):
    per_step, c, d = x_ref.shape
    x = x_ref[...].reshape(per_step * c, d)
    hn = _mod_norm(x, gain_ref[...], mod_ref[0:1, :], mod_ref[1:2, :])
    kv = _dot(hn.astype(BF16), w_ref[...])
    k = kv[:, :LANES]
    k = k * lax.rsqrt(_head_ms(k, bd_ref[:LANES, :LANES]) + EPS) * kgain_ref[...]
    for j in range(per_step):
        kc_ref[j] = k[j * c:(j + 1) * c, :].astype(BF16)
        vct_ref[j] = kv[j * c:(j + 1) * c, LANES:].T.astype(BF16)


def _ctx_kv(ctx, mod3, ctx_row, norm_gain, w_in_bf, gains, kgain_block, bd):
    b, c, d = ctx.shape
    kvw = 2 * ATTN_KV_HEADS * HEAD_DIM
    per_step = 2 if b % 2 == 0 else 1
    return pl.pallas_call(
        _ctx_kernel,
        grid=(b // per_step,),
        in_specs=[pl.BlockSpec((per_step, c, d), lambda i: (i, 0, 0)),
                  pl.BlockSpec((None, N_MOD, d), lambda i: (ctx_row, 0, 0)),
                  pl.BlockSpec((1, d), lambda i: (0, 0)),
                  pl.BlockSpec((d, kvw), lambda i: (0, 0)),
                  pl.BlockSpec((1, LANES), lambda i: (0, kgain_block)),
                  pl.BlockSpec((MXU_DIM, MXU_DIM), lambda i: (0, 0))],
        out_specs=[pl.BlockSpec((per_step, c, LANES), lambda i: (i, 0, 0)),
                   pl.BlockSpec((per_step, LANES, c), lambda i: (i, 0, 0))],
        out_shape=[jax.ShapeDtypeStruct((b, c, LANES), BF16),
                   jax.ShapeDtypeStruct((b, LANES, c), BF16)],
        compiler_params=pltpu.CompilerParams(
            dimension_semantics=("arbitrary",), vmem_limit_bytes=VMEM_LIMIT),
        name="ctx_kv",
    )(ctx, mod3, norm_gain, w_in_bf, gains, bd)


def _inproj_kernel(x_ref, mod_ref, gain_ref, w_ref, cos_ref, sup_ref, sdn_ref,
                   qgain_ref, kgain_ref, ggain_ref, ws_ref, bs_ref, ogain_ref, bd_ref,
                   wo_f32_ref, wgu_f32_ref, wd_f32_ref,
                   k_ref, vt_ref, qt_ref, mlp_ref, wo_bf_ref, wgu_bf_ref, wd_bf_ref):
    wo_bf_ref[...] = wo_f32_ref[...].astype(BF16)
    wgu_bf_ref[...] = wgu_f32_ref[...].astype(BF16)
    wd_bf_ref[...] = wd_f32_ref[...].astype(BF16)

    t = x_ref.shape[0]
    nblk = t // BLOCK
    kvw = ATTN_KV_HEADS * HEAD_DIM
    aw = ATTN_HEADS * HEAD_DIM
    mw = MLP_HEADS * HEAD_DIM
    q_cols = slice(2 * kvw, 2 * kvw + aw)
    kv_cols = slice(0, 2 * kvw)
    u_cols = slice(2 * kvw + aw, 2 * kvw + aw + mw)
    g_cols = slice(2 * kvw + aw + mw, 2 * kvw + aw + 2 * mw)
    halves = range(aw // MXU_DIM)
    bd = bd_ref[...]

    def project(cols):
        return _dot(hn, w_ref[:, cols])

    def finish_qkv(q, kv, q_ms, k_ms):
        cos, sup, sdn = cos_ref[...], sup_ref[...], sdn_ref[...]
        k = kv[:, :kvw] * lax.rsqrt(k_ms + EPS) * kgain_ref[...]
        k_ref[...] = _rope(k, cos, sup, sdn).astype(BF16)
        vt_ref[...] = kv[:, kvw:].T.astype(BF16)
        for half in halves:
            qh = q[:, half * MXU_DIM:(half + 1) * MXU_DIM] * lax.rsqrt(q_ms[half] + EPS)
            qh = qh * qgain_ref[:, half * MXU_DIM:(half + 1) * MXU_DIM]
            for sl in range(MXU_DIM // LANES):
                qs = _rope(qh[:, sl * LANES:(sl + 1) * LANES], cos, sup, sdn)
                qst = qs.T
                for hh in range(LANES // HEAD_DIM):
                    head = (half * MXU_DIM + sl * LANES) // HEAD_DIM + hh
                    kvh, grp = head // ATTN_GROUP, head % ATTN_GROUP
                    for jb in range(nblk):
                        col = (jb * ATTN_GROUP + grp) * BLOCK
                        qt_ref[kvh, :, col:col + BLOCK] = qst[
                            hh * HEAD_DIM:(hh + 1) * HEAD_DIM, jb * BLOCK:(jb + 1) * BLOCK].astype(BF16)

    def gate_mix(g_raw):
        g = _gelu2_tanh(g_raw)
        g_ms = [_head_ms(g[:, h * MXU_DIM:(h + 1) * MXU_DIM], bd) for h in halves]
        gn = jnp.concatenate(
            [g[:, h * MXU_DIM:(h + 1) * MXU_DIM] * lax.rsqrt(g_ms[h] + 4.0 * EPS) for h in halves],
            axis=1) * ggain_ref[...]
        gnb = gn.astype(BF16)
        low_head = lax.broadcasted_iota(jnp.int32, (BLOCK, t), 1) % LANES < HEAD_DIM
        mixed_slabs = []
        for p in range(MLP_HEADS // 2):
            rhs = jnp.concatenate(
                [gnb[c * BLOCK:(c + 1) * BLOCK, p * LANES:(p + 1) * LANES] for c in range(nblk)], axis=1)
            a = _dot(ws_ref[2 * p].astype(BF16), rhs)
            b = _dot(ws_ref[2 * p + 1].astype(BF16), rhs)
            mixed_slabs.append(jnp.where(low_head, a, b))
        return mixed_slabs

    def finish_mlp(u_raw, mixed_slabs):
        u = _gelu2_tanh(u_raw)
        rows = []
        for c in range(nblk):
            mixed_c = jnp.concatenate(
                [m[:, c * LANES:(c + 1) * LANES] for m in mixed_slabs], axis=1) + bs_ref[...]
            rows.append(u[c * BLOCK:(c + 1) * BLOCK, :] * mixed_c)
        o = jnp.concatenate(rows, axis=0)
        o = o * lax.rsqrt(jnp.mean(o * o, axis=-1, keepdims=True) + 4.0 * EPS) * ogain_ref[...]
        mlp_ref[...] = o.astype(BF16)

    hn = _mod_norm(x_ref[...], gain_ref[...], mod_ref[0:1, :], mod_ref[1:2, :]).astype(BF16)
    q, kv = project(q_cols), project(kv_cols)
    q_ms = [_head_ms(q[:, h * MXU_DIM:(h + 1) * MXU_DIM], bd) for h in halves]
    k_ms = _head_ms(kv[:, :kvw], bd[:kvw, :kvw])
    g_raw = project(g_cols)
    finish_qkv(q, kv, q_ms, k_ms)
    u_raw = project(u_cols)
    finish_mlp(u_raw, gate_mix(g_raw))


def _inproj(x, mod3, norm_gain, w_in_bf, cos, sup, sdn, gains, kgain_block, w_spatial, bs_full,
            ogain, bd, later_weights, tile):
    b, s, d = x.shape
    inw = w_in_bf.shape[1]
    aw = ATTN_HEADS * HEAD_DIM
    mw = MLP_HEADS * HEAD_DIM
    n_steps = (s // tile) * b
    const = lambda shape: pl.BlockSpec(shape, lambda i, bb: (0,) * len(shape))

    def slab_spec(w):
        rows = next(r for r in range(BF16_SUBLANES, w.shape[0] + 1, BF16_SUBLANES)
                    if w.shape[0] % r == 0 and w.shape[0] // r <= n_steps)
        n_slabs = w.shape[0] // rows
        return pl.BlockSpec((rows, w.shape[1]), lambda i, bb: (jnp.minimum(i * b + bb, n_slabs - 1), 0))

    slab_specs = [slab_spec(w) for w in later_weights]
    table_spec = pl.BlockSpec((tile, LANES), lambda i, bb: (i, 0))
    return pl.pallas_call(
        _inproj_kernel,
        grid=(s // tile, b),
        in_specs=[pl.BlockSpec((None, tile, d), lambda i, bb: (bb, i, 0)),
                  pl.BlockSpec((None, N_MOD, d), lambda i, bb: (bb, 0, 0)),
                  const((1, d)),
                  const((d, inw)),
                  table_spec, table_spec, table_spec,
                  const((1, aw)), pl.BlockSpec((1, LANES), lambda i, bb: (0, kgain_block)),
                  pl.BlockSpec((1, mw), lambda i, bb: (0, aw // mw)),
                  const((MLP_HEADS, BLOCK, BLOCK)), const((BLOCK, mw)), const((1, mw)),
                  const((MXU_DIM, MXU_DIM))] + slab_specs,
        out_specs=[pl.BlockSpec((None, tile, LANES), lambda i, bb: (bb, i, 0)),
                   pl.BlockSpec((None, LANES, tile), lambda i, bb: (bb, 0, i)),
                   pl.BlockSpec((None, ATTN_KV_HEADS, HEAD_DIM, ATTN_GROUP * tile),
                                lambda i, bb: (bb, 0, 0, i)),
                   pl.BlockSpec((None, tile, mw), lambda i, bb: (bb, i, 0))] + slab_specs,
        out_shape=[jax.ShapeDtypeStruct((b, s, LANES), BF16),
                   jax.ShapeDtypeStruct((b, LANES, s), BF16),
                   jax.ShapeDtypeStruct((b, ATTN_KV_HEADS, HEAD_DIM, ATTN_GROUP * s), BF16),
                   jax.ShapeDtypeStruct((b, s, mw), BF16)]
        + [jax.ShapeDtypeStruct(w.shape, BF16) for w in later_weights],
        compiler_params=pltpu.CompilerParams(
            dimension_semantics=("arbitrary", "arbitrary"), vmem_limit_bytes=VMEM_LIMIT),
        name="inproj",
    )(x, mod3, norm_gain, w_in_bf, cos, sup, sdn, gains, gains, gains, w_spatial, bs_full, ogain, bd,
      *later_weights)


def _mix_ffn_kernel(tiles_per_seq, bounded,
                    x_ref, mod_ref, kp_ref, km_ref, kn_ref, vp_ref, vm_ref, vn_ref,
                    kc_ref, vct_ref, qt_ref, mlp_ref, sink_ref, bias_ref, again_ref, wo_ref,
                    fmod_ref, fgain_ref, wgu_ref, wd_ref,
                    o_ref, h_ref, hn_ref, hid_ref):
    tq = x_ref.shape[0]
    nblk = tq // BLOCK
    step_id = pl.program_id(0)
    n_tiles = pl.num_programs(0) - 1
    i = lax.rem(jnp.minimum(step_id, n_tiles - 1), tiles_per_seq)
    last = tiles_per_seq - 1
    gq = ATTN_GROUP * BLOCK
    slot_w = lax.rem(step_id, 2)
    slot_r = 1 - slot_w

    @pl.when(step_id == 0)
    def _():
        h_ref[1] = jnp.zeros(h_ref.shape[1:], F32)
        hn_ref[...] = jnp.zeros(hn_ref.shape, BF16)

    ff = wd_ref.shape[0]
    n_chunks = ff // MXU_DIM
    def ffn_matmuls(c):
        hn = hn_ref[...]
        return (_dot(hn, wgu_ref[:, c * MXU_DIM:(c + 1) * MXU_DIM]),
                _dot(hn, wgu_ref[:, ff + c * MXU_DIM:ff + (c + 1) * MXU_DIM]))

    def ffn_activation(c, gate_up):
        a, b = gate_up
        hid_ref[:, c * MXU_DIM:(c + 1) * MXU_DIM] = (_silu(a) * b).astype(BF16)

    k_ext = jnp.concatenate([kp_ref[...], km_ref[...], kn_ref[...]], axis=0)
    vt_ext = jnp.concatenate([vp_ref[...], vm_ref[...], vn_ref[...]], axis=1)
    kc = kc_ref[...]
    vct = vct_ref[...]
    cw = ATTN_GROUP * BLOCK
    zeros_q = jnp.zeros((HEAD_DIM, cw), BF16)
    ones_rows = jnp.ones((BF16_SUBLANES, 3 * BLOCK + kc.shape[0]), BF16)
    bias_prev = jnp.concatenate([bias_ref[0:BLOCK, :cw]] * ATTN_KV_HEADS, axis=1)
    bias_next = jnp.concatenate([bias_ref[BLOCK:2 * BLOCK, :cw]] * ATTN_KV_HEADS, axis=1)

    def sink_rows():
        return jnp.concatenate([jnp.full((1, BLOCK), sink_ref[h] * LOG2E, F32)
                                for h in range(ATTN_HEADS)], axis=1)

    def score_matmul(jb):
        rows = []
        for kvh in range(ATTN_KV_HEADS):
            qt = qt_ref[kvh, :, jb * gq:(jb + 1) * gq]
            rows.append(jnp.concatenate([qt if j == kvh else zeros_q
                                         for j in range(ATTN_KV_HEADS)], axis=1))
        rhs = jnp.concatenate(rows, axis=0)
        return _dot(kc, rhs), _dot(k_ext[jb * BLOCK:(jb + 3) * BLOCK, :], rhs)

    def mask_and_max(jb, s):
        pen_prev = jnp.where(i == 0, MASKED, 0.0) if jb == 0 else 0.0
        pen_next = jnp.where(i == last, MASKED, 0.0) if jb == nblk - 1 else 0.0
        s_ctx, s = s
        parts = [s[0:BLOCK] + (bias_prev + pen_prev), s[BLOCK:2 * BLOCK],
                 s[2 * BLOCK:3 * BLOCK] + (bias_next + pen_next), s_ctx]
        if bounded:
            return jnp.concatenate([jnp.exp2(p_).astype(BF16) for p_ in parts], axis=0), None
        m = jnp.maximum(
            jnp.maximum(jnp.max(parts[0], axis=0, keepdims=True), jnp.max(parts[1], axis=0, keepdims=True)),
            jnp.maximum(jnp.max(parts[2], axis=0, keepdims=True), jnp.max(parts[3], axis=0, keepdims=True)))
        return parts, jnp.maximum(m, sink_rows())

    def exp_weights(parts, m):
        return jnp.concatenate([jnp.exp2(p_ - m).astype(BF16) for p_ in parts], axis=0)

    def value_matmul(jb, p):
        pvs = []
        for kvh in range(ATTN_KV_HEADS):
            v_all = jnp.concatenate(
                [vt_ext[kvh * HEAD_DIM:(kvh + 1) * HEAD_DIM, jb * BLOCK:(jb + 3) * BLOCK],
                 vct[kvh * HEAD_DIM:(kvh + 1) * HEAD_DIM, :]], axis=1)
            pvs.append(_dot(jnp.concatenate([v_all, ones_rows], axis=0), p[:, kvh * cw:(kvh + 1) * cw]))
        return pvs

    def normalize(jb, pvs, m):
        sink = sink_rows()
        heads = []
        for kvh, pv in enumerate(pvs):
            cols = slice(kvh * cw, (kvh + 1) * cw)
            denom = pv[HEAD_DIM:HEAD_DIM + 1, :] + jnp.exp2(sink[:, cols] if bounded else sink[:, cols] - m[:, cols])
            o_t = pv[:HEAD_DIM, :] * (1.0 / denom)
            heads.extend(o_t[:, g * BLOCK:(g + 1) * BLOCK] for g in range(ATTN_GROUP))
        return heads

    def block_norm(out_t):
        o_all = jnp.concatenate(out_t, axis=0)
        ms = jnp.mean(o_all * o_all, axis=0, keepdims=True)
        y = o_all * lax.rsqrt(ms + EPS)
        return (y.T * again_ref[...]).astype(BF16)

    aw = ATTN_HEADS * HEAD_DIM
    steps = [(jb,) for jb in range(nblk)]
    per_step = ATTN_KV_HEADS
    assert n_chunks >= per_step * len(steps)
    attn_rows, unnormalized, pending = [], None, []

    def collect(step, pvs, m):
        attn_rows.append(block_norm(normalize(*step, pvs, m)))

    def issue_chunks(chunks):
        return [(c, ffn_matmuls(c)) for c in chunks]

    def activate(issued):
        for c, gate_up in issued:
            ffn_activation(c, gate_up)

    if bounded:
        proj_mlp = None
        for n, step in enumerate(steps):
            scores = score_matmul(*step)
            issued = issue_chunks(range(n * per_step, (n + 1) * per_step))
            weights, _ = mask_and_max(*step, scores)
            if unnormalized is not None:
                collect(*unnormalized)
            activate(pending)
            pending = issued
            unnormalized = (step, value_matmul(*step, weights), None)
    else:
        raw = {n: score_matmul(*steps[n]) for n in range(2)}
        proj_mlp = _dot(mlp_ref[...], wo_ref[aw:, :])
        ready = {0: mask_and_max(*steps[0], raw.pop(0))}
        for n, step in enumerate(steps):
            if n + 1 < len(steps):
                ready[n + 1] = mask_and_max(*steps[n + 1], raw.pop(n + 1))
            if n + 2 < len(steps):
                raw[n + 2] = score_matmul(*steps[n + 2])
            issued = issue_chunks(range(n * per_step, (n + 1) * per_step))
            if unnormalized is not None:
                collect(*unnormalized)
            activate(pending)
            pending = issued
            parts, m = ready.pop(n)
            unnormalized = (step, value_matmul(*step, exp_weights(parts, m)), m)

    for c in range(per_step * len(steps), n_chunks):
        issued = issue_chunks([c])
        if unnormalized is not None:
            collect(*unnormalized)
            unnormalized = None
        activate(pending)
        pending = issued
    proj_top = _dot(jnp.concatenate(attn_rows[:nblk // 2], axis=0), wo_ref[:aw, :])
    if proj_mlp is None:
        proj_mlp = _dot(mlp_ref[...], wo_ref[aw:, :])
    proj_bot = _dot(jnp.concatenate(attn_rows[nblk // 2:], axis=0), wo_ref[:aw, :])
    proj_attn = jnp.concatenate([proj_top, proj_bot], axis=0)
    activate(pending)
    h_new = x_ref[...] + mod_ref[2:3, :] * (proj_attn + proj_mlp)
    h_ref[slot_w] = h_new
    hn_ref[...] = _mod_norm(h_new, fgain_ref[...], mod_ref[3:4, :], mod_ref[4:5, :]).astype(BF16)
    o_ref[...] = h_ref[slot_r] + fmod_ref[5:6, :] * _dot(hid_ref[...], wd_ref[...])


def _mix_ffn(x, mod3, k, vt, kc, vct, qt, mlpn, sinks, bias, again_row, wo_bf,
             ffn_gain, wgu_bf, wd_bf, tile, bounded):
    b, s, d = x.shape
    c = kc.shape[1]
    aw = ATTN_HEADS * HEAD_DIM
    mw = mlpn.shape[2]
    ff = wd_bf.shape[0]
    assert ff % MXU_DIM == 0
    r = tile // BLOCK
    nb = s // BLOCK
    nt = s // tile
    n_tiles = b * nt
    gq = ATTN_GROUP * BLOCK

    def mix_tile(g):
        t = jnp.minimum(g, n_tiles - 1)
        return t // nt, t % nt

    def ffn_tile(g):
        t = jnp.maximum(g - 1, 0)
        return t // nt, t % nt

    def at_mix(fn):
        return lambda g: fn(*mix_tile(g))

    const = lambda shape: pl.BlockSpec(shape, lambda g: (0,) * len(shape))
    return pl.pallas_call(
        functools.partial(_mix_ffn_kernel, nt, bounded),
        grid=(n_tiles + 1,),
        in_specs=[pl.BlockSpec((None, tile, d), at_mix(lambda bb, i: (bb, i, 0))),
                  pl.BlockSpec((None, N_MOD, d), at_mix(lambda bb, i: (bb, 0, 0))),
                  pl.BlockSpec((None, BLOCK, LANES), at_mix(lambda bb, i: (bb, jnp.maximum(i * r - 1, 0), 0))),
                  pl.BlockSpec((None, tile, LANES), at_mix(lambda bb, i: (bb, i, 0))),
                  pl.BlockSpec((None, BLOCK, LANES),
                               at_mix(lambda bb, i: (bb, jnp.minimum((i + 1) * r, nb - 1), 0))),
                  pl.BlockSpec((None, LANES, BLOCK), at_mix(lambda bb, i: (bb, 0, jnp.maximum(i * r - 1, 0)))),
                  pl.BlockSpec((None, LANES, tile), at_mix(lambda bb, i: (bb, 0, i))),
                  pl.BlockSpec((None, LANES, BLOCK),
                               at_mix(lambda bb, i: (bb, 0, jnp.minimum((i + 1) * r, nb - 1)))),
                  pl.BlockSpec((None, c, LANES), at_mix(lambda bb, i: (bb, 0, 0))),
                  pl.BlockSpec((None, LANES, c), at_mix(lambda bb, i: (bb, 0, 0))),
                  pl.BlockSpec((None, ATTN_KV_HEADS, HEAD_DIM, ATTN_GROUP * tile),
                               at_mix(lambda bb, i: (bb, 0, 0, i))),
                  pl.BlockSpec((None, tile, mw), at_mix(lambda bb, i: (bb, i, 0))),
                  pl.BlockSpec(memory_space=pltpu.SMEM),
                  const((2 * BLOCK, gq)),
                  const((1, aw)),
                  const((aw + mw, d)),
                  pl.BlockSpec((None, N_MOD, d), lambda g: (ffn_tile(g)[0], 0, 0)),
                  const((1, d)), const((d, 2 * ff)), const((ff, d))],
        out_specs=pl.BlockSpec((None, tile, d), lambda g: (*ffn_tile(g), 0)),
        out_shape=jax.ShapeDtypeStruct((b, s, d), F32),
        scratch_shapes=[pltpu.VMEM((2, tile, d), F32), pltpu.VMEM((tile, d), BF16),
                        pltpu.VMEM((tile, ff), BF16)],
        compiler_params=pltpu.CompilerParams(
            dimension_semantics=("arbitrary",), vmem_limit_bytes=VMEM_LIMIT),
        name="mix_ffn",
    )(x, mod3, k, k, k, vt, vt, vt, kc, vct, qt, mlpn, sinks, bias, again_row, wo_bf,
      mod3, ffn_gain, wgu_bf, wd_bf)


def _rope_tables(s):
    axis_dim = HEAD_DIM // 2
    pos = np.arange(s)
    inv_freq = (ROPE_THETA ** (-np.arange(0, axis_dim, 2, dtype=np.float32) / axis_dim)).astype(np.float32)
    ang_r = (pos // GRID_W).astype(np.float32)[:, None] * inv_freq[None, :]
    ang_c = (pos % GRID_W).astype(np.float32)[:, None] * inv_freq[None, :]
    cr, sr, cc, sc = np.cos(ang_r), np.sin(ang_r), np.cos(ang_c), np.sin(ang_c)
    z = np.zeros_like(sr)
    reps = LANES // HEAD_DIM
    cos = np.tile(np.concatenate([cr, cr, cc, cc], axis=1), (1, reps))
    sin_up = np.tile(np.concatenate([-sr, z, -sc, z], axis=1), (1, reps))
    sin_dn = np.tile(np.concatenate([z, sr, z, sc], axis=1), (1, reps))
    return jnp.asarray(cos, F32), jnp.asarray(sin_up, F32), jnp.asarray(sin_dn, F32)


def _window_bias():
    c = np.arange(BLOCK)[:, None]
    r = np.arange(BLOCK)[None, :]
    prev = np.where(c >= r, 0.0, MASKED).astype(np.float32)
    nxt = np.where(c <= r, 0.0, MASKED).astype(np.float32)
    return jnp.asarray(np.tile(np.concatenate([prev, nxt], axis=0), (1, ATTN_GROUP)), F32)


def _head_mean_matrix():
    bd = np.kron(np.eye(MXU_DIM // HEAD_DIM, dtype=np.float32),
                 np.full((HEAD_DIM, HEAD_DIM), 1.0 / HEAD_DIM, np.float32))
    return jnp.asarray(bd, BF16)


def kernel(x, c, ctx, c_ctx, w_mod, b_mod, norm_mix, norm_ffn, w_in, q_gain, k_gain, attn_sink,
           gate_gain, w_spatial, b_spatial, attn_out_gain, mlp_out_gain, w_out, w_gate_up, w_down):
    b, s, d = x.shape
    assert w_mod.shape[0] == 1, "single-layer problem"
    assert s % 1024 == 0 and d % LANES == 0
    aw = ATTN_HEADS * HEAD_DIM
    mw = MLP_HEADS * HEAD_DIM

    rows = -(-(b + 1) // BF16_SUBLANES) * BF16_SUBLANES
    cond = jnp.concatenate([c, c_ctx[None, :], jnp.zeros((rows - b - 1, d), F32)], axis=0)
    mod, w_in_bf = _adaln(cond, w_mod[0], b_mod[0][None, :], w_in[0])
    mod3 = mod.reshape(rows, N_MOD, d)
    bd = _head_mean_matrix()
    assert aw == mw and (aw + mw) % LANES == 0
    gains = jnp.concatenate([jnp.tile(q_gain[0], ATTN_HEADS) * (HEAD_DIM ** -0.5 * LOG2E),
                             gate_gain[0].reshape(mw),
                             jnp.tile(k_gain[0], ATTN_KV_HEADS)])[None, :]
    kgain_block = (aw + mw) // LANES
    norm_mix_g = norm_mix[0][None, :]

    kc, vct = _ctx_kv(ctx, mod3, b, norm_mix_g, w_in_bf, gains, kgain_block, bd)

    cos, sup, sdn = _rope_tables(s)
    bs_full = jnp.repeat(b_spatial[0].T, HEAD_DIM, axis=1)
    k, vt, qt, mlpn, wo_bf, wgu_bf, wd_bf = _inproj(
        x, mod3, norm_mix_g, w_in_bf, cos, sup, sdn, gains, kgain_block, w_spatial[0], bs_full,
        mlp_out_gain[0][None, :], bd, (w_out[0], w_gate_up[0], w_down[0]), tile=1024)

    again_row = attn_out_gain[0][None, :]
    sinks_padded = jnp.pad(attn_sink[0], (0, HEAD_DIM - attn_sink.shape[1]))
    q_max, k_max, sink_max = jnp.max(jnp.abs(jnp.stack([q_gain[0], k_gain[0], sinks_padded])), axis=1)
    logit_bound = BOUND_MARGIN * HEAD_DIM ** 0.5 * LOG2E * q_max * k_max
    sink_max = sink_max * LOG2E
    use_bound = (logit_bound <= MAX_SAFE_LOGIT) & (sink_max <= MAX_SAFE_LOGIT)
    operands = (x, mod3, k, vt, kc, vct, qt, mlpn, attn_sink[0], _window_bias(), again_row,
                wo_bf, norm_ffn[0][None, :], wgu_bf, wd_bf)
    return lax.cond(use_bound,
                    lambda ops: _mix_ffn(*ops, tile=512, bounded=True),
                    lambda ops: _mix_ffn(*ops, tile=512, bounded=False),
                    operands)
```
